```python
import jax, jax.numpy as jnp
from jax import lax
import numpy as np

D_MODEL = 1024
BATCH = 4
SEQ = 8192
DEPTH = 2

CHUNK = 64
N_META = 16
LEAD_PAD = CHUNK - N_META
EPS = 1e-6
MIX_WIDTH = D_MODEL
M_HEADS = 4
M_WIDTH = MIX_WIDTH // 2
M_HEAD_DIM = M_WIDTH // M_HEADS
CONV_K = 4
G_HEADS = 4
G_WIDTH = MIX_WIDTH - M_WIDTH
G_V_DIM = G_WIDTH // G_HEADS
G_K_DIM = G_V_DIM // 2
G_K_WIDTH = G_HEADS * G_K_DIM
G_RANK = 16
G_TAU = 16.0
N_GROUPS = 4
EXPERTS_PER_GROUP = 8
N_EXPERTS = N_GROUPS * EXPERTS_PER_GROUP
TOP_K = 2
D_EXPERT = D_MODEL // 4
EXPERT_BLOCK = 128
IN_SPLITS = (2 * M_WIDTH, M_WIDTH, M_WIDTH, M_HEADS, M_HEADS, G_K_WIDTH, G_K_WIDTH, G_WIDTH, G_WIDTH, G_RANK)
IN_WIDTH = sum(IN_SPLITS)
F32 = jnp.float32

kernel_name = "hybrid_mlstm_gla_hiermoe"


def rmsnorm(x, g):
    xf = x.astype(F32)
    y = xf * lax.rsqrt(jnp.mean(xf * xf, axis=-1, keepdims=True) + EPS)
    return (y * g.astype(F32)).astype(x.dtype)


def head_rmsnorm(h, g):
    y = h * lax.rsqrt(jnp.mean(h * h, axis=-1, keepdims=True) + EPS)
    return y.reshape(h.shape[0], h.shape[1], -1) * g.astype(F32)


def causal_depthwise_conv(x, w, b):
    c = x.shape[-1]
    y = lax.conv_general_dilated(x, w[:, None, :], window_strides=(1,), padding=[(CONV_K - 1, 0)],
                                 dimension_numbers=('NWC', 'WIO', 'NWC'), feature_group_count=c)
    return y + b


def to_chunks(t, fill):
    pad = [(0, 0), (LEAD_PAD, 0)] + [(0, 0)] * (t.ndim - 2)
    t = jnp.pad(t, pad, constant_values=fill)
    bn, lp = t.shape[0], t.shape[1]
    t = t.reshape(bn, lp // CHUNK, CHUNK, *t.shape[2:])
    return jnp.moveaxis(t, (1, 3), (0, 2))


def from_chunks(h):
    h = jnp.moveaxis(h, (0, 2), (1, 3))
    bn, nc, c, nh, d = h.shape
    return h.reshape(bn, nc * c, nh, d)[:, LEAD_PAD:]


def mlstm_heads(q, k, v, ig, lf):
    bn, _, nh, d = q.shape
    causal = jnp.tril(jnp.ones((CHUNK, CHUNK), dtype=bool))
    xs = (to_chunks(q, 0.0), to_chunks(k, 0.0), to_chunks(v, 0.0),
          to_chunks(ig, -jnp.inf), to_chunks(lf, 0.0))

    def step(carry, inp):
        c_state, n_state, m_state = carry
        qc, kc, vc, igc, lfc = inp
        b = jnp.cumsum(lfc, axis=-1)
        dmat = jnp.where(causal, b[..., :, None] - b[..., None, :] + igc[..., None, :], -jnp.inf)
        m_inter = b + m_state[..., None]
        m_t = jnp.maximum(m_inter, jnp.max(dmat, axis=-1))
        s = jnp.einsum('bhtd,bhsd->bhts', qc, kc) * jnp.exp(dmat - m_t[..., None])
        w_inter = jnp.exp(m_inter - m_t)
        num = jnp.einsum('bhts,bhsv->bhtv', s, vc) + w_inter[..., None] * jnp.einsum('bhvk,bhtk->bhtv', c_state, qc)
        den = jnp.sum(s, axis=-1) + w_inter * jnp.einsum('bhk,bhtk->bht', n_state, qc)
        h = num / jnp.maximum(jnp.abs(den), jnp.exp(-m_t))[..., None]
        b_end = b[..., -1]
        g = b_end[..., None] - b + igc
        m_new = jnp.maximum(b_end + m_state, jnp.max(g, axis=-1))
        wk = jnp.exp(g - m_new[..., None])
        decay = jnp.exp(b_end + m_state - m_new)
        c_state = decay[..., None, None] * c_state + jnp.einsum('bhs,bhsv,bhsk->bhvk', wk, vc, kc)
        n_state = decay[..., None] * n_state + jnp.einsum('bhs,bhsk->bhk', wk, kc)
        return (c_state, n_state, m_new), h

    init = (jnp.zeros((bn, nh, d, d), F32), jnp.zeros((bn, nh, d), F32), jnp.zeros((bn, nh), F32))
    _, h = lax.scan(step, init, xs)
    return from_chunks(h)


def gla_heads(q, k, v, la):
    bn, _, nh, dk = q.shape
    dv = v.shape[-1]
    causal = jnp.tril(jnp.ones((CHUNK, CHUNK), dtype=bool))
    xs = (to_chunks(q, 0.0), to_chunks(k, 0.0), to_chunks(v, 0.0), to_chunks(la, 0.0))

    def step(s_state, inp):
        qc, kc, vc, lac = inp
        b = jnp.cumsum(lac, axis=2)
        pair = jnp.where(causal[:, :, None], b[:, :, :, None, :] - b[:, :, None, :, :], -jnp.inf)
        a = jnp.einsum('bhtj,bhsj,bhtsj->bhts', qc, kc, jnp.exp(pair))
        o = jnp.einsum('bhts,bhsv->bhtv', a, vc) + jnp.einsum('bhtj,bhjv->bhtv', qc * jnp.exp(b), s_state)
        b_end = b[:, :, -1]
        s_state = jnp.exp(b_end)[..., None] * s_state + jnp.einsum(
            'bhsj,bhsv->bhjv', kc * jnp.exp(b_end[:, :, None] - b), vc)
        return s_state, o

    _, o = lax.scan(step, jnp.zeros((bn, nh, dk, dv), F32), xs)
    return from_chunks(o)


def hybrid_mixer(xn, w_in, conv_w, conv_b, gate_b, m_norm_g, a_w2, a_b, g_norm_g, w_out):
    bn, seq_len, _ = xn.shape
    z = (xn @ w_in).astype(F32)
    qk_m, v_m, o_m, ig, fg, q_g, k_g, v_g, gate_g, code_g = jnp.split(
        z, np.cumsum(IN_SPLITS)[:-1].tolist(), axis=-1)
    heads = lambda t, d: t.reshape(bn, seq_len, -1, d)
    qk_m = jax.nn.silu(causal_depthwise_conv(qk_m, conv_w.astype(F32), conv_b.astype(F32)))
    q_m, k_m = jnp.split(qk_m, 2, axis=-1)
    h_m = mlstm_heads(heads(q_m, M_HEAD_DIM), heads(k_m, M_HEAD_DIM) * (M_HEAD_DIM ** -0.5),
                      heads(v_m, M_HEAD_DIM), ig + gate_b[0].astype(F32),
                      jax.nn.log_sigmoid(fg + gate_b[1].astype(F32)))
    y_m = head_rmsnorm(h_m, m_norm_g) * jax.nn.sigmoid(o_m)
    la = jax.nn.log_sigmoid(code_g @ a_w2.astype(F32) + a_b.astype(F32)) / G_TAU
    h_g = gla_heads(heads(q_g, G_K_DIM) * (G_K_DIM ** -0.5), heads(k_g, G_K_DIM),
                    heads(v_g, G_V_DIM), heads(la, G_K_DIM))
    y_g = head_rmsnorm(h_g, g_norm_g) * jax.nn.silu(gate_g)
    y = jnp.concatenate([y_m, y_g], axis=-1).astype(xn.dtype)
    return y @ w_out


def hierarchical_moe(xn, wr_g, br_g, wr_e, br_e, w_gate, w_up, w_down):
    bn, seq_len, dm = xn.shape
    xt = xn.reshape(-1, dm)
    n_tok = xt.shape[0]
    p_group = jax.nn.softmax((xt @ wr_g).astype(F32) + br_g.astype(F32), axis=-1)
    g_sel = jnp.argmax(p_group, axis=-1)
    p_sel = jnp.take_along_axis(p_group, g_sel[:, None], axis=-1)
    e_logits = ((xt @ wr_e).astype(F32) + br_e.astype(F32)).reshape(n_tok, N_GROUPS, EXPERTS_PER_GROUP)
    e_logits = jnp.take_along_axis(e_logits, g_sel[:, None, None], axis=1)[:, 0]
    top_v, top_i = lax.top_k(e_logits, TOP_K)
    gate = jax.nn.softmax(top_v, axis=-1) * p_sel
    expert = (g_sel[:, None] * EXPERTS_PER_GROUP + top_i).reshape(-1).astype(jnp.int32)
    token = jnp.repeat(jnp.arange(n_tok, dtype=jnp.int32), TOP_K)
    weight = gate.reshape(-1)
    n_asg = n_tok * TOP_K
    order = jnp.argsort(expert)
    s_exp, s_tok, s_w = expert[order], token[order], weight[order]
    counts = jnp.bincount(expert, length=N_EXPERTS)
    padded = (counts + EXPERT_BLOCK - 1) // EXPERT_BLOCK * EXPERT_BLOCK
    pad_end = jnp.cumsum(padded)
    slot = (pad_end - padded)[s_exp] + jnp.arange(n_asg, dtype=jnp.int32) - (jnp.cumsum(counts) - counts)[s_exp]
    n_blocks = -(-(n_asg + N_EXPERTS * (EXPERT_BLOCK - 1)) // EXPERT_BLOCK)
    slot_tok = jnp.full((n_blocks * EXPERT_BLOCK,), n_tok, jnp.int32).at[slot].set(s_tok)
    block_exp = jnp.minimum(jnp.searchsorted(pad_end, jnp.arange(n_blocks, dtype=jnp.int32) * EXPERT_BLOCK,
                                             side='right'), N_EXPERTS - 1)
    x_pad = jnp.concatenate([xt, jnp.zeros((1, dm), xt.dtype)], axis=0)

    def run_block(args):
        tok, e = args
        xb = x_pad[tok]
        hb = jax.nn.silu(xb @ w_gate[e]) * (xb @ w_up[e])
        return hb @ w_down[e]

    y_blocks = lax.map(run_block, (slot_tok.reshape(n_blocks, EXPERT_BLOCK), block_exp))
    y_slots = y_blocks.reshape(-1, dm)[slot].astype(F32)
    y = jax.ops.segment_sum(y_slots * s_w[:, None], s_tok, num_segments=n_tok)
    return y.astype(xn.dtype).reshape(bn, seq_len, dm)


def setup_inputs(seed: int = 0) -> dict:
    key = jax.random.key(seed)
    ks = jax.random.split(key, 24)
    nrm = lambda k, shape, scale: jax.random.normal(k, shape, F32) * scale
    gains = lambda k, shape: 1.0 + nrm(k, shape, 0.02)
    gate_b = jnp.stack([nrm(ks[6], (DEPTH, M_HEADS), 0.1),
                        jnp.linspace(3.0, 6.0, M_HEADS, dtype=F32) + nrm(ks[7], (DEPTH, M_HEADS), 0.1)], axis=1)
    return {
        "x": nrm(ks[0], (BATCH, SEQ, D_MODEL), 1.0),
        "meta_tokens": nrm(ks[1], (N_META, D_MODEL), 1.0),
        "norm1_g": gains(ks[2], (DEPTH, D_MODEL)),
        "w_in": nrm(ks[3], (DEPTH, D_MODEL, IN_WIDTH), D_MODEL ** -0.5),
        "conv_w": nrm(ks[4], (DEPTH, CONV_K, 2 * M_WIDTH), CONV_K ** -0.5),
        "conv_b": nrm(ks[5], (DEPTH, 2 * M_WIDTH), 0.02),
        "gate_b": gate_b,
        "m_norm_g": gains(ks[8], (DEPTH, M_WIDTH)),
        "a_w2": nrm(ks[9], (DEPTH, G_RANK, G_K_WIDTH), G_RANK ** -0.5),
        "a_b": nrm(ks[10], (DEPTH, G_K_WIDTH), 0.1),
        "g_norm_g": gains(ks[11], (DEPTH, G_WIDTH)),
        "w_out": nrm(ks[12], (DEPTH, MIX_WIDTH, D_MODEL), MIX_WIDTH ** -0.5),
        "norm2_g": gains(ks[13], (DEPTH, D_MODEL)),
        "wr_g": nrm(ks[14], (DEPTH, D_MODEL, N_GROUPS), D_MODEL ** -0.5),
        "br_g": nrm(ks[15], (DEPTH, N_GROUPS), 0.01),
        "wr_e": nrm(ks[16], (DEPTH, D_MODEL, N_EXPERTS), D_MODEL ** -0.5),
        "br_e": nrm(ks[17], (DEPTH, N_EXPERTS), 0.01),
        "w_gate": nrm(ks[18], (DEPTH, N_EXPERTS, D_MODEL, D_EXPERT), D_MODEL ** -0.5),
        "w_up": nrm(ks[19], (DEPTH, N_EXPERTS, D_MODEL, D_EXPERT), D_MODEL ** -0.5),
        "w_down": nrm(ks[20], (DEPTH, N_EXPERTS, D_EXPERT, D_MODEL), D_EXPERT ** -0.5),
        "final_norm_g": gains(ks[21], (D_MODEL,)),
    }


def reference(x, meta_tokens, norm1_g, w_in, conv_w, conv_b, gate_b, m_norm_g, a_w2, a_b, g_norm_g, w_out,
              norm2_g, wr_g, br_g, wr_e, br_e, w_gate, w_up, w_down, final_norm_g):
    bn = x.shape[0]
    meta = jnp.broadcast_to(meta_tokens[None].astype(x.dtype), (bn, N_META, x.shape[-1]))
    h = jnp.concatenate([meta, x], axis=1)
    for l in range(DEPTH):
        h = h + hybrid_mixer(rmsnorm(h, norm1_g[l]), w_in[l], conv_w[l], conv_b[l], gate_b[l], m_norm_g[l],
                             a_w2[l], a_b[l], g_norm_g[l], w_out[l])
        h = h + hierarchical_moe(rmsnorm(h, norm2_g[l]), wr_g[l], br_g[l], wr_e[l], br_e[l],
                                 w_gate[l], w_up[l], w_down[l])
    return rmsnorm(h, final_norm_g)[:, N_META:]
```

```python
import functools

import numpy as np
import jax
import jax.numpy as jnp
from jax import lax
from jax.experimental import pallas as pl
from jax.experimental.pallas import tpu as pltpu

F32 = jnp.float32
BF16 = jnp.bfloat16
HIGHEST = lax.Precision.HIGHEST

D_MODEL = 1024
CHUNK = 64
N_META = 16
LEAD_PAD = CHUNK - N_META
EPS = 1e-6
M_HEADS = 4
M_WIDTH = 512
M_HEAD_DIM = 128
CONV_K = 4
G_HEADS = 4
G_WIDTH = 512
G_V_DIM = 128
G_K_DIM = 64
G_K_WIDTH = 256
G_RANK = 16
G_TAU = 16.0
N_GROUPS = 4
EXPERTS_PER_GROUP = 8
N_EXPERTS = 32
D_EXPERT = 256
N_PAIRS = EXPERTS_PER_GROUP * (EXPERTS_PER_GROUP - 1) // 2
N_CLASSES = N_GROUPS * N_PAIRS

LANES = 128
SUBLANES = 8
EXPERT_ROWS = 128
VMEM_LIMIT = 56 * 1024 * 1024

Z_QM, Z_KM, Z_VM, Z_OM = 0, 512, 1024, 1536
Z_QG, Z_KG, Z_VG, Z_GG = 2048, 2304, 2560, 3072
Z_SMALL = 3584
Z_WIDTH = Z_SMALL + LANES
X_EXT = D_MODEL + LANES
GLA_SAFE_EXP = 80.0


def _log_sigmoid(x):
    return jnp.minimum(x, 0.0) - jnp.log(1.0 + jnp.exp(-jnp.abs(x)))


def _sigmoid(x):
    return 1.0 / (1.0 + jnp.exp(-x))


def _rms(x, g):
    return x * lax.rsqrt(jnp.mean(x * x, axis=-1, keepdims=True) + EPS) * g


def _dot(a, b):
    return jnp.dot(a.astype(BF16), b.astype(BF16), preferred_element_type=F32)


def _dot_nt(a, b):
    return lax.dot_general(a.astype(BF16), b.astype(BF16), (((1,), (1,)), ((), ())),
                           preferred_element_type=F32)


def _dot_f32(a, b):
    return jnp.dot(a, b, precision=HIGHEST, preferred_element_type=F32)


def _params(*semantics):
    return pltpu.CompilerParams(dimension_semantics=semantics, vmem_limit_bytes=VMEM_LIMIT)


def _embed_inproj_kernel(x_ref, meta_ref, g_ref, w_ref, h_ref, z_ref):
    c = pl.program_id(0)
    bn = x_ref.shape[0]
    rows = bn * CHUNK

    @pl.when(c == 0)
    def _():
        lead = jnp.concatenate([jnp.zeros((LEAD_PAD, D_MODEL), F32), meta_ref[...]], axis=0)
        for b in range(bn):
            h_ref[b * CHUNK:(b + 1) * CHUNK, :] = lead

    @pl.when(c > 0)
    def _():
        h_ref[...] = x_ref[...].reshape(rows, D_MODEL)

    z_ref[...] = _dot(_rms(h_ref[...], g_ref[...]), w_ref[...])


def _embed_inproj(x4, meta, g, w, n_chunks):
    bn = x4.shape[0]
    rows = bn * CHUNK
    n = n_chunks * rows
    return pl.pallas_call(
        _embed_inproj_kernel,
        grid=(n_chunks,),
        in_specs=[
            pl.BlockSpec((bn, 1, CHUNK, D_MODEL), lambda c: (0, jnp.maximum(c - 1, 0), 0, 0)),
            pl.BlockSpec((N_META, D_MODEL), lambda c: (0, 0)),
            pl.BlockSpec((1, D_MODEL), lambda c: (0, 0)),
            pl.BlockSpec((D_MODEL, Z_WIDTH), lambda c: (0, 0)),
        ],
        out_specs=[
            pl.BlockSpec((rows, D_MODEL), lambda c: (c, 0)),
            pl.BlockSpec((rows, Z_WIDTH), lambda c: (c, 0)),
        ],
        out_shape=[jax.ShapeDtypeStruct((n, D_MODEL), F32), jax.ShapeDtypeStruct((n, Z_WIDTH), F32)],
        compiler_params=_params("parallel"),
        name="embed_inproj",
    )(x4, meta, g, w)


def _add_inproj_kernel(h_ref, y_ref, g_ref, w_ref, ho_ref, z_ref):
    h = h_ref[...] + y_ref[...]
    ho_ref[...] = h
    z_ref[...] = _dot(_rms(h, g_ref[...]), w_ref[...])


def _add_inproj(h, y, g, w, rows):
    n = h.shape[0]
    return pl.pallas_call(
        _add_inproj_kernel,
        grid=(n // rows,),
        in_specs=[
            pl.BlockSpec((rows, D_MODEL), lambda i: (i, 0)),
            pl.BlockSpec((rows, D_MODEL), lambda i: (i, 0)),
            pl.BlockSpec((1, D_MODEL), lambda i: (0, 0)),
            pl.BlockSpec((D_MODEL, Z_WIDTH), lambda i: (0, 0)),
        ],
        out_specs=[
            pl.BlockSpec((rows, D_MODEL), lambda i: (i, 0)),
            pl.BlockSpec((rows, Z_WIDTH), lambda i: (i, 0)),
        ],
        out_shape=[jax.ShapeDtypeStruct((n, D_MODEL), F32), jax.ShapeDtypeStruct((n, Z_WIDTH), F32)],
        compiler_params=_params("parallel"),
        name="add_inproj",
    )(h, y, g, w)


def _gla_intra_pairwise(q, k, bg, kbuf, bbuf):
    kbuf[...] = k
    bbuf[...] = bg
    lane = lax.broadcasted_iota(jnp.int32, (CHUNK, CHUNK), 1)

    def column(s, a):
        ks = kbuf[pl.ds(s, 1), :]
        bs = bbuf[pl.ds(s, 1), :]
        col = jnp.sum(q * ks * jnp.exp(jnp.minimum(bg - bs, 0.0)), axis=-1, keepdims=True)
        return jnp.where(lane == s, col, a)

    return lax.fori_loop(0, CHUNK, column, jnp.zeros((CHUNK, CHUNK), F32))


def _mixer_kernel(z_ref, cw_ref, cb_ref, gbr_ref, gbc_ref, mg_ref, aw_ref, ab_ref, gg_ref, y_ref,
                  tail_ref, c_ref, m_ref, s_ref, kbuf, bbuf):
    c = pl.program_id(0)
    bn = z_ref.shape[0] // CHUNK

    @pl.when(c == 0)
    def _():
        tail_ref[...] = jnp.zeros_like(tail_ref)
        c_ref[...] = jnp.zeros_like(c_ref)
        m_ref[...] = jnp.zeros_like(m_ref)
        s_ref[...] = jnp.zeros_like(s_ref)

    row = lax.broadcasted_iota(jnp.int32, (CHUNK, CHUNK), 0)
    col = lax.broadcasted_iota(jnp.int32, (CHUNK, CHUNK), 1)
    causal = col <= row
    tri = jnp.where(causal, 1.0, 0.0).astype(F32)
    tri_t = jnp.where(row <= col, 1.0, 0.0).astype(F32)
    first = c == 0
    valid_c = jnp.logical_or(jnp.logical_not(first),
                             lax.broadcasted_iota(jnp.int32, (CHUNK, 1), 0) >= LEAD_PAD)
    valid_r = jnp.logical_or(jnp.logical_not(first),
                             lax.broadcasted_iota(jnp.int32, (1, CHUNK), 1) >= LEAD_PAD)
    row8 = lax.broadcasted_iota(jnp.int32, (SUBLANES, 1), 0)
    ones_col = jnp.where(lax.broadcasted_iota(jnp.int32, (CHUNK, LANES), 1) == 0, 1.0, 0.0).astype(F32)
    neg_inf = -jnp.inf

    def per_batch(b, carry):
        r0 = pl.multiple_of(b * CHUNK, CHUNK)
        rows = pl.ds(r0, CHUNK)

        x = jnp.where(valid_c, z_ref[rows, Z_QM:Z_QM + 2 * M_WIDTH], 0.0)
        prev = tail_ref[b]
        acc = x * cw_ref[CONV_K - 1:CONV_K, :] + cb_ref[...]
        for k in range(1, CONV_K):
            cur = pltpu.roll(x, k, 0)
            fix = pltpu.roll(prev, k, 0)
            top = jnp.where(row8 < k, fix, cur[0:SUBLANES])
            shifted = jnp.concatenate([top, cur[SUBLANES:]], axis=0)
            acc = acc + shifted * cw_ref[CONV_K - 1 - k:CONV_K - k, :]
        tail_ref[b] = x[CHUNK - SUBLANES:]
        qk = acc * _sigmoid(acc)
        qk = jnp.where(valid_c, qk, 0.0)
        q_m = qk[:, :M_WIDTH]
        k_m = qk[:, M_WIDTH:] * (M_HEAD_DIM ** -0.5)
        v_m = jnp.where(valid_c, z_ref[rows, Z_VM:Z_VM + M_WIDTH], 0.0)
        o_m = z_ref[rows, Z_OM:Z_OM + M_WIDTH]

        zs = z_ref[rows, Z_SMALL:Z_SMALL + LANES]
        g_col = zs + gbr_ref[...]
        g_row = zs.T[0:2 * M_HEADS, :] + gbc_ref[...]
        b_col = _dot_f32(tri, jnp.where(valid_c, _log_sigmoid(g_col), 0.0))
        b_row = _dot_f32(jnp.where(valid_r, _log_sigmoid(g_row), 0.0), tri_t)
        ig_col = jnp.where(valid_c, g_col, neg_inf)
        ig_row = jnp.where(valid_r, g_row, neg_inf)

        for hd in range(M_HEADS):
            idx = b * M_HEADS + hd
            lanes = slice(hd * M_HEAD_DIM, (hd + 1) * M_HEAD_DIM)
            q, k, v = q_m[:, lanes], k_m[:, lanes], v_m[:, lanes]
            bc, br = b_col[:, M_HEADS + hd:M_HEADS + hd + 1], b_row[M_HEADS + hd:M_HEADS + hd + 1, :]
            igc, igr = ig_col[:, hd:hd + 1], ig_row[hd:hd + 1, :]
            m_prev = m_ref[idx][0:1, 0:1]
            cst = c_ref[idx]
            dmat = jnp.where(causal, bc - br + igr, neg_inf)
            m_inter = bc + m_prev
            m_t = jnp.maximum(m_inter, jnp.max(dmat, axis=-1, keepdims=True))
            s = _dot_nt(q, k) * jnp.exp(dmat - m_t)
            w_inter = jnp.exp(m_inter - m_t)
            v_aug = jnp.concatenate([v, ones_col], axis=1)
            nd = _dot(s, v_aug) + w_inter * _dot(q, cst)
            num, den = nd[:, :M_HEAD_DIM], nd[:, M_HEAD_DIM:M_HEAD_DIM + 1]
            hh = num / jnp.maximum(jnp.abs(den), jnp.exp(-m_t))
            b_end = bc[CHUNK - 1:CHUNK, :]
            m_new = jnp.maximum(b_end + m_prev, jnp.max(b_end - br + igr, axis=-1, keepdims=True))
            wk = jnp.exp(b_end - bc + igc - m_new)
            decay = jnp.exp(b_end + m_prev - m_new)
            c_ref[idx] = decay * cst + _dot((k * wk).T, v_aug)
            m_ref[idx] = jnp.broadcast_to(m_new, (SUBLANES, LANES))
            y = _rms(hh, mg_ref[:, lanes]) * _sigmoid(o_m[:, lanes])
            y_ref[rows, lanes] = y.astype(y_ref.dtype)

        u = _dot_f32(zs[:, 2 * M_HEADS:2 * M_HEADS + G_RANK], aw_ref[...]) + ab_ref[...]
        la = jnp.where(valid_c, _log_sigmoid(u) * (1.0 / G_TAU), 0.0)
        bg = _dot_f32(tri, la)
        bg_t = bg.T
        q_g = jnp.where(valid_c, z_ref[rows, Z_QG:Z_QG + G_K_WIDTH], 0.0) * (G_K_DIM ** -0.5)
        k_g = jnp.where(valid_c, z_ref[rows, Z_KG:Z_KG + G_K_WIDTH], 0.0)
        v_g = jnp.where(valid_c, z_ref[rows, Z_VG:Z_VG + G_WIDTH], 0.0)
        gate_g = z_ref[rows, Z_GG:Z_GG + G_WIDTH]
        b_end = bg[CHUNK - 1:CHUNK, :]
        q_dec = q_g * jnp.exp(bg)
        k_end = k_g * jnp.exp(b_end - bg)
        for hd in range(G_HEADS):
            idx = b * G_HEADS + hd
            kl = slice(hd * G_K_DIM, (hd + 1) * G_K_DIM)
            vl = slice(hd * G_V_DIM, (hd + 1) * G_V_DIM)
            st = s_ref[idx]
            bh = bg[:, kl]
            safe = jnp.max(-b_end[:, kl]) <= GLA_SAFE_EXP
            a = lax.cond(
                safe,
                lambda: _dot_nt(q_dec[:, kl], k_g[:, kl] * jnp.exp(-bh)),
                lambda: _gla_intra_pairwise(q_g[:, kl], k_g[:, kl], bh, kbuf, bbuf))
            a = jnp.where(causal, a, 0.0)
            o = _dot(a, v_g[:, vl]) + _dot(q_dec[:, kl], st)
            e_end = jnp.exp(bg_t[hd * G_K_DIM:(hd + 1) * G_K_DIM, CHUNK - 1:CHUNK])
            s_ref[idx] = e_end * st + _dot(k_end[:, kl].T, v_g[:, vl])
            gt = gate_g[:, vl]
            y = _rms(o, gg_ref[:, vl]) * (gt * _sigmoid(gt))
            y_ref[rows, M_WIDTH + hd * G_V_DIM:M_WIDTH + (hd + 1) * G_V_DIM] = y.astype(y_ref.dtype)
        return carry

    lax.fori_loop(0, bn, per_batch, 0)


def _mixer(z, conv_w, conv_b, gb_row, gb_col, m_norm_g, a_w2, a_b, g_norm_g, bn):
    rows = bn * CHUNK
    n = z.shape[0]
    full = lambda shape: pl.BlockSpec(shape, lambda c: (0,) * len(shape))
    return pl.pallas_call(
        _mixer_kernel,
        grid=(n // rows,),
        in_specs=[
            pl.BlockSpec((rows, Z_WIDTH), lambda c: (c, 0)),
            full((CONV_K, 2 * M_WIDTH)), full((1, 2 * M_WIDTH)),
            full((1, LANES)), full((2 * M_HEADS, CHUNK)),
            full((1, M_WIDTH)), full((G_RANK, G_K_WIDTH)), full((1, G_K_WIDTH)), full((1, G_WIDTH)),
        ],
        out_specs=pl.BlockSpec((rows, D_MODEL), lambda c: (c, 0)),
        out_shape=jax.ShapeDtypeStruct((n, D_MODEL), BF16),
        scratch_shapes=[
            pltpu.VMEM((bn, SUBLANES, 2 * M_WIDTH), F32),
            pltpu.VMEM((bn * M_HEADS, M_HEAD_DIM, 2 * M_HEAD_DIM), F32),
            pltpu.VMEM((bn * M_HEADS, SUBLANES, LANES), F32),
            pltpu.VMEM((bn * G_HEADS, G_K_DIM, G_V_DIM), F32),
            pltpu.VMEM((CHUNK, G_K_DIM), F32), pltpu.VMEM((CHUNK, G_K_DIM), F32),
        ],
        compiler_params=_params("arbitrary"),
        name="mixer",
    )(z, conv_w, conv_b, gb_row, gb_col, m_norm_g, a_w2, a_b, g_norm_g)


def _outproj_router_kernel(h_ref, y_ref, wo_ref, g_ref, wr_ref, br_ref, h1_ref, xe_ref, route_ref, cnt_ref,
                           carry_ref):
    i = pl.program_id(0)
    rows = h_ref.shape[0]

    @pl.when(i == 0)
    def _():
        carry_ref[...] = jnp.zeros_like(carry_ref)

    h1 = h_ref[...] + jnp.dot(y_ref[...], wo_ref[...], preferred_element_type=F32)
    h1_ref[...] = h1
    xn = _rms(h1, g_ref[...])
    logits = _dot_f32(xn, wr_ref[...]) + br_ref[...]
    lane = lax.broadcasted_iota(jnp.int32, (rows, LANES), 1)
    neg_inf = -jnp.inf
    big = jnp.int32(LANES)

    def first_argmax(vals):
        top = jnp.max(vals, axis=-1, keepdims=True)
        return top, jnp.min(jnp.where(vals == top, lane, big), axis=-1, keepdims=True)

    lg = jnp.where(lane < N_GROUPS, logits, neg_inf)
    g_max, g_sel = first_argmax(lg)
    p_sel = 1.0 / jnp.sum(jnp.exp(lg - g_max), axis=-1, keepdims=True)
    in_group = jnp.logical_and(lane >= N_GROUPS + EXPERTS_PER_GROUP * g_sel,
                               lane < N_GROUPS + EXPERTS_PER_GROUP * (g_sel + 1))
    le = jnp.where(in_group, logits, neg_inf)
    v1, i1 = first_argmax(le)
    v2, i2 = first_argmax(jnp.where(lane == i1, neg_inf, le))
    e21 = jnp.exp(v2 - v1)
    gate1 = p_sel / (1.0 + e21)
    gate2 = p_sel * e21 / (1.0 + e21)
    j1 = i1 - N_GROUPS - EXPERTS_PER_GROUP * g_sel
    j2 = i2 - N_GROUPS - EXPERTS_PER_GROUP * g_sel
    ja, jb = jnp.minimum(j1, j2), jnp.maximum(j1, j2)
    ga = jnp.where(j1 < j2, gate1, gate2)
    gb = jnp.where(j1 < j2, gate2, gate1)
    pair = ((ja * (2 * EXPERTS_PER_GROUP - 1 - ja)) >> 1) + (jb - ja - 1)
    cls = g_sel * N_PAIRS + pair

    xe_ref[:, :D_MODEL] = xn
    xe_ref[:, D_MODEL:] = jnp.where(lane == 0, ga, jnp.where(lane == 1, gb, 0.0))

    onehot = jnp.where(lane == cls, 1.0, 0.0).astype(F32)
    r_i = lax.broadcasted_iota(jnp.int32, (rows, rows), 0)
    c_i = lax.broadcasted_iota(jnp.int32, (rows, rows), 1)
    incl = jnp.where(c_i <= r_i, 1.0, 0.0).astype(BF16)
    prefix = jnp.dot(incl, onehot.astype(BF16), preferred_element_type=F32)
    rank = jnp.sum(onehot * (prefix - 1.0 + carry_ref[...]), axis=-1, keepdims=True)
    carry_ref[...] = carry_ref[...] + prefix[rows - 1:rows, :]
    cnt_ref[...] = carry_ref[...]
    route_ref[...] = jnp.where(lane == 0, cls, jnp.where(lane == 1, rank.astype(jnp.int32), 0))


def _outproj_router(h, y, w_out, g, wr, br, rows):
    n = h.shape[0]
    return pl.pallas_call(
        _outproj_router_kernel,
        grid=(n // rows,),
        in_specs=[
            pl.BlockSpec((rows, D_MODEL), lambda i: (i, 0)),
            pl.BlockSpec((rows, D_MODEL), lambda i: (i, 0)),
            pl.BlockSpec((D_MODEL, D_MODEL), lambda i: (0, 0)),
            pl.BlockSpec((1, D_MODEL), lambda i: (0, 0)),
            pl.BlockSpec((D_MODEL, LANES), lambda i: (0, 0)),
            pl.BlockSpec((1, LANES), lambda i: (0, 0)),
        ],
        out_specs=[
            pl.BlockSpec((rows, D_MODEL), lambda i: (i, 0)),
            pl.BlockSpec((rows, X_EXT), lambda i: (i, 0)),
            pl.BlockSpec((rows, LANES), lambda i: (i, 0)),
            pl.BlockSpec((1, LANES), lambda i: (0, 0)),
        ],
        out_shape=[
            jax.ShapeDtypeStruct((n, D_MODEL), F32),
            jax.ShapeDtypeStruct((n, X_EXT), F32),
            jax.ShapeDtypeStruct((n, LANES), jnp.int32),
            jax.ShapeDtypeStruct((1, LANES), F32),
        ],
        scratch_shapes=[pltpu.VMEM((1, LANES), F32)],
        compiler_params=_params("arbitrary"),
        name="outproj_router",
    )(h, y, w_out, g, wr, br)


def _expert_kernel(ea_ref, eb_ref, nused_ref, tok_ref, x_hbm, wgu_a, wd_a, wgu_b, wd_b, y_hbm,
                   xbuf, obuf, sem_in, sem_out):
    i = pl.program_id(0)
    n_rows = x_hbm.shape[0]

    def gather_row(k):
        src = jnp.minimum(tok_ref[0, 0, k], n_rows - 1)
        return pltpu.make_async_copy(x_hbm.at[pl.ds(src, 1), :], xbuf.at[pl.ds(k, 1), :], sem_in)

    def scatter_row(k):
        return pltpu.make_async_copy(obuf.at[pl.ds(k, 1), :], y_hbm.at[pl.ds(tok_ref[0, 0, k], 1), :], sem_out)

    def for_rows(fn):
        def body(k, carry):
            fn(k)
            return carry
        lax.fori_loop(0, EXPERT_ROWS, body, 0)

    def for_real_rows(fn):
        for_rows(lambda k: pl.when(tok_ref[0, 0, k] < n_rows)(lambda: fn(k)))

    @pl.when(i < nused_ref[0])
    def _():
        for_rows(lambda k: gather_row(k).start())
        for_rows(lambda k: gather_row(k).wait())
        x = xbuf[...]
        xb = x[:, :D_MODEL].astype(BF16)
        ga = x[:, D_MODEL:D_MODEL + 1]
        gb = x[:, D_MODEL + 1:D_MODEL + 2]

        def mlp(wgu, wd):
            gu = jnp.dot(xb, wgu[0], preferred_element_type=F32)
            gate, up = gu[:, :D_EXPERT], gu[:, D_EXPERT:]
            return jnp.dot((gate * _sigmoid(gate) * up).astype(BF16), wd[0], preferred_element_type=F32)

        obuf[...] = ga * mlp(wgu_a, wd_a) + gb * mlp(wgu_b, wd_b)
        for_real_rows(lambda k: scatter_row(k).start())
        for_real_rows(lambda k: scatter_row(k).wait())


def _experts(blk_ea, blk_eb, n_used, slot_tok, xe, wgu, wd):
    n_blocks = blk_ea.shape[0]
    n = xe.shape[0]
    grid_spec = pltpu.PrefetchScalarGridSpec(
        num_scalar_prefetch=3,
        grid=(n_blocks,),
        in_specs=[
            pl.BlockSpec((1, 1, EXPERT_ROWS), lambda i, ea, eb, nu: (i, 0, 0), memory_space=pltpu.SMEM),
            pl.BlockSpec(memory_space=pl.ANY),
            pl.BlockSpec((1, D_MODEL, 2 * D_EXPERT), lambda i, ea, eb, nu: (ea[i], 0, 0)),
            pl.BlockSpec((1, D_EXPERT, D_MODEL), lambda i, ea, eb, nu: (ea[i], 0, 0)),
            pl.BlockSpec((1, D_MODEL, 2 * D_EXPERT), lambda i, ea, eb, nu: (eb[i], 0, 0)),
            pl.BlockSpec((1, D_EXPERT, D_MODEL), lambda i, ea, eb, nu: (eb[i], 0, 0)),
        ],
        out_specs=pl.BlockSpec(memory_space=pl.ANY),
        scratch_shapes=[
            pltpu.VMEM((EXPERT_ROWS, X_EXT), F32),
            pltpu.VMEM((EXPERT_ROWS, D_MODEL), F32),
            pltpu.SemaphoreType.DMA(()),
            pltpu.SemaphoreType.DMA(()),
        ],
    )
    return pl.pallas_call(
        _expert_kernel,
        grid_spec=grid_spec,
        out_shape=jax.ShapeDtypeStruct((n, D_MODEL), F32),
        compiler_params=_params("arbitrary"),
        name="experts",
    )(blk_ea, blk_eb, n_used, slot_tok, xe, wgu, wd, wgu, wd)


def _pair_tables():
    ea, eb = [], []
    for g in range(N_GROUPS):
        for a in range(EXPERTS_PER_GROUP):
            for b in range(a + 1, EXPERTS_PER_GROUP):
                ea.append(g * EXPERTS_PER_GROUP + a)
                eb.append(g * EXPERTS_PER_GROUP + b)
    return np.asarray(ea, np.int32), np.asarray(eb, np.int32)


def _plan_blocks(route, counts, n):
    cls, rank = route[:, 0], route[:, 1]
    cnt = counts[0, :N_CLASSES].astype(jnp.int32)
    padded = (cnt + EXPERT_ROWS - 1) // EXPERT_ROWS * EXPERT_ROWS
    pad_end = jnp.cumsum(padded)
    slot = (pad_end - padded)[cls] + rank
    n_blocks = -(-(n + N_CLASSES * (EXPERT_ROWS - 1)) // EXPERT_ROWS)
    n_slots = n_blocks * EXPERT_ROWS
    slot_tok = jnp.full((n_slots,), n, jnp.int32).at[slot].set(jnp.arange(n, dtype=jnp.int32))
    blk_cls = jnp.minimum(jnp.searchsorted(pad_end, jnp.arange(n_blocks, dtype=jnp.int32) * EXPERT_ROWS,
                                           side='right'), N_CLASSES - 1)
    ea_tab, eb_tab = _pair_tables()
    n_used = (pad_end[-1] // EXPERT_ROWS).astype(jnp.int32).reshape(1)
    return (jnp.asarray(ea_tab)[blk_cls], jnp.asarray(eb_tab)[blk_cls], n_used,
            slot_tok.reshape(n_blocks, 1, EXPERT_ROWS))


def _final_kernel(h_ref, y_ref, g_ref, o_ref):
    out = _rms(h_ref[...] + y_ref[...], g_ref[...])
    o_ref[...] = out.reshape(o_ref.shape)


def _final(h, y, g, bn, n_chunks):
    rows = bn * CHUNK
    return pl.pallas_call(
        _final_kernel,
        grid=(n_chunks - 1,),
        in_specs=[
            pl.BlockSpec((rows, D_MODEL), lambda c: (c + 1, 0)),
            pl.BlockSpec((rows, D_MODEL), lambda c: (c + 1, 0)),
            pl.BlockSpec((1, D_MODEL), lambda c: (0, 0)),
        ],
        out_specs=pl.BlockSpec((bn, 1, CHUNK, D_MODEL), lambda c: (0, c, 0, 0)),
        out_shape=jax.ShapeDtypeStruct((bn, n_chunks - 1, CHUNK, D_MODEL), F32),
        compiler_params=_params("parallel"),
        name="final_norm",
    )(h, y, g)


def _repack_w_in(w):
    o = np.cumsum([0, 2 * M_WIDTH, M_WIDTH, M_WIDTH, M_HEADS, M_HEADS, G_K_WIDTH, G_K_WIDTH, G_WIDTH, G_WIDTH,
                   G_RANK])
    seg = lambda i: w[:, o[i]:o[i + 1]]
    small = jnp.concatenate([seg(3), seg(4), seg(9)], axis=1)
    small = jnp.pad(small, ((0, 0), (0, LANES - small.shape[1])))
    return jnp.concatenate([seg(0), seg(1), seg(2), seg(5), seg(6), seg(7), seg(8), small], axis=1).astype(BF16)


def kernel(x, meta_tokens, norm1_g, w_in, conv_w, conv_b, gate_b, m_norm_g, a_w2, a_b, g_norm_g, w_out, norm2_g,
           wr_g, br_g, wr_e, br_e, w_gate, w_up, w_down, final_norm_g):
    bn, seq, dm = x.shape
    depth = w_in.shape[0]
    assert dm == D_MODEL and seq % CHUNK == 0
    n_chunks = seq // CHUNK + 1
    rows = bn * CHUNK
    n = n_chunks * rows
    x4 = x.reshape(bn, seq // CHUNK, CHUNK, dm)
    row = lambda v: v.reshape(1, -1).astype(F32)

    h = y_moe = None
    for l in range(depth):
        w_in_l = _repack_w_in(w_in[l])
        if l == 0:
            h, z = _embed_inproj(x4, meta_tokens.astype(F32), row(norm1_g[l]), w_in_l, n_chunks)
        else:
            h, z = _add_inproj(h, y_moe, row(norm1_g[l]), w_in_l, rows)
        gb = gate_b[l].astype(F32).reshape(1, 2 * M_HEADS)
        y = _mixer(z, conv_w[l].astype(F32), row(conv_b[l]), jnp.pad(gb, ((0, 0), (0, LANES - 2 * M_HEADS))),
                   jnp.broadcast_to(gb.reshape(2 * M_HEADS, 1), (2 * M_HEADS, CHUNK)),
                   row(m_norm_g[l]), a_w2[l].astype(F32), row(a_b[l]), row(g_norm_g[l]), bn)
        wr = jnp.pad(jnp.concatenate([wr_g[l], wr_e[l]], axis=1).astype(F32),
                     ((0, 0), (0, LANES - N_GROUPS - N_EXPERTS)))
        br = jnp.pad(jnp.concatenate([br_g[l], br_e[l]]).astype(F32), (0, LANES - N_GROUPS - N_EXPERTS))
        h, xe, route, counts = _outproj_router(h, y, w_out[l].astype(BF16), row(norm2_g[l]), wr,
                                               br.reshape(1, LANES), rows)
        blk_ea, blk_eb, n_used, slot_tok = _plan_blocks(route, counts, n)
        wgu = jnp.concatenate([w_gate[l], w_up[l]], axis=-1).astype(BF16)
        y_moe = _experts(blk_ea, blk_eb, n_used, slot_tok, xe, wgu, w_down[l].astype(BF16))
    out = _final(h, y_moe, row(final_norm_g), bn, n_chunks)
    return out.reshape(bn, seq, dm)
```

```python
import functools

import numpy as np
import jax
import jax.numpy as jnp
from jax import lax
from jax.experimental import pallas as pl
from jax.experimental.pallas import tpu as pltpu

F32 = jnp.float32
BF16 = jnp.bfloat16
HIGHEST = lax.Precision.HIGHEST

D_MODEL = 1024
CHUNK = 64
N_META = 16
LEAD_PAD = CHUNK - N_META
EPS = 1e-6
M_HEADS = 4
M_WIDTH = 512
M_HEAD_DIM = 128
CONV_K = 4
G_HEADS = 4
G_WIDTH = 512
G_V_DIM = 128
G_K_DIM = 64
G_K_WIDTH = 256
G_RANK = 16
G_TAU = 16.0
N_GROUPS = 4
EXPERTS_PER_GROUP = 8
N_EXPERTS = 32
D_EXPERT = 256
N_PAIRS = EXPERTS_PER_GROUP * (EXPERTS_PER_GROUP - 1) // 2
N_CLASSES = N_GROUPS * N_PAIRS

LANES = 128
SUBLANES = 8
EXPERT_ROWS = 128
VMEM_LIMIT = 56 * 1024 * 1024

Z_QM, Z_KM, Z_VM, Z_OM = 0, 512, 1024, 1536
Z_QG, Z_KG, Z_VG, Z_GG = 2048, 2304, 2560, 3072
Z_SMALL = 3584
Z_WIDTH = Z_SMALL + LANES
X_EXT = D_MODEL + LANES
GLA_SAFE_EXP = 80.0


def _log_sigmoid(x):
    return jnp.minimum(x, 0.0) - jnp.log(1.0 + jnp.exp(-jnp.abs(x)))


def _sigmoid(x):
    return 1.0 / (1.0 + jnp.exp(-x))


def _rms(x, g):
    return x * lax.rsqrt(jnp.mean(x * x, axis=-1, keepdims=True) + EPS) * g


def _dot(a, b):
    return jnp.dot(a.astype(BF16), b.astype(BF16), preferred_element_type=F32)


def _dot_nt(a, b):
    return lax.dot_general(a.astype(BF16), b.astype(BF16), (((1,), (1,)), ((), ())),
                           preferred_element_type=F32)


def _dot_f32(a, b):
    return jnp.dot(a, b, precision=HIGHEST, preferred_element_type=F32)


def _params(*semantics):
    return pltpu.CompilerParams(dimension_semantics=semantics, vmem_limit_bytes=VMEM_LIMIT)


def _embed_inproj_kernel(x_ref, meta_ref, g_ref, w_ref, h_ref, z_ref):
    c = pl.program_id(0)
    bn = x_ref.shape[0]
    rows = bn * CHUNK

    @pl.when(c == 0)
    def _():
        lead = jnp.concatenate([jnp.zeros((LEAD_PAD, D_MODEL), F32), meta_ref[...]], axis=0)
        for b in range(bn):
            h_ref[b * CHUNK:(b + 1) * CHUNK, :] = lead

    @pl.when(c > 0)
    def _():
        h_ref[...] = x_ref[...].reshape(rows, D_MODEL)

    z_ref[...] = _dot(_rms(h_ref[...], g_ref[...]), w_ref[...])


def _embed_inproj(x4, meta, g, w, n_chunks):
    bn = x4.shape[0]
    rows = bn * CHUNK
    n = n_chunks * rows
    return pl.pallas_call(
        _embed_inproj_kernel,
        grid=(n_chunks,),
        in_specs=[
            pl.BlockSpec((bn, 1, CHUNK, D_MODEL), lambda c: (0, jnp.maximum(c - 1, 0), 0, 0)),
            pl.BlockSpec((N_META, D_MODEL), lambda c: (0, 0)),
            pl.BlockSpec((1, D_MODEL), lambda c: (0, 0)),
            pl.BlockSpec((D_MODEL, Z_WIDTH), lambda c: (0, 0)),
        ],
        out_specs=[
            pl.BlockSpec((rows, D_MODEL), lambda c: (c, 0)),
            pl.BlockSpec((rows, Z_WIDTH), lambda c: (c, 0)),
        ],
        out_shape=[jax.ShapeDtypeStruct((n, D_MODEL), F32), jax.ShapeDtypeStruct((n, Z_WIDTH), F32)],
        compiler_params=_params("parallel"),
        name="embed_inproj",
    )(x4, meta, g, w)


def _add_inproj_kernel(h_ref, y_ref, g_ref, w_ref, ho_ref, z_ref):
    h = h_ref[...] + y_ref[...]
    ho_ref[...] = h
    z_ref[...] = _dot(_rms(h, g_ref[...]), w_ref[...])


def _add_inproj(h, y, g, w, rows):
    n = h.shape[0]
    return pl.pallas_call(
        _add_inproj_kernel,
        grid=(n // rows,),
        in_specs=[
            pl.BlockSpec((rows, D_MODEL), lambda i: (i, 0)),
            pl.BlockSpec((rows, D_MODEL), lambda i: (i, 0)),
            pl.BlockSpec((1, D_MODEL), lambda i: (0, 0)),
            pl.BlockSpec((D_MODEL, Z_WIDTH), lambda i: (0, 0)),
        ],
        out_specs=[
            pl.BlockSpec((rows, D_MODEL), lambda i: (i, 0)),
            pl.BlockSpec((rows, Z_WIDTH), lambda i: (i, 0)),
        ],
        out_shape=[jax.ShapeDtypeStruct((n, D_MODEL), F32), jax.ShapeDtypeStruct((n, Z_WIDTH), F32)],
        compiler_params=_params("parallel"),
        name="add_inproj",
    )(h, y, g, w)


def _gla_intra_pairwise(q, k, bg, kbuf, bbuf):
    kbuf[...] = k
    bbuf[...] = bg
    lane = lax.broadcasted_iota(jnp.int32, (CHUNK, CHUNK), 1)

    def column(s, a):
        ks = kbuf[pl.ds(s, 1), :]
        bs = bbuf[pl.ds(s, 1), :]
        col = jnp.sum(q * ks * jnp.exp(jnp.minimum(bg - bs, 0.0)), axis=-1, keepdims=True)
        return jnp.where(lane == s, col, a)

    return lax.fori_loop(0, CHUNK, column, jnp.zeros((CHUNK, CHUNK), F32))


def _gla_streams(z_ref, rows, valid_c, tri, aw_ref, ab_ref):
    code = z_ref[rows, Z_SMALL:Z_SMALL + LANES][:, 2 * M_HEADS:2 * M_HEADS + G_RANK]
    u = _dot_f32(code, aw_ref[...]) + ab_ref[...]
    la = jnp.where(valid_c, _log_sigmoid(u) * (1.0 / G_TAU), 0.0)
    bg = _dot_f32(tri, la)
    q_g = jnp.where(valid_c, z_ref[rows, Z_QG:Z_QG + G_K_WIDTH], 0.0) * (G_K_DIM ** -0.5)
    k_g = jnp.where(valid_c, z_ref[rows, Z_KG:Z_KG + G_K_WIDTH], 0.0)
    v_g = jnp.where(valid_c, z_ref[rows, Z_VG:Z_VG + G_WIDTH], 0.0)
    return q_g, k_g, v_g, bg


def _gla_gated(o, gate, g):
    return _rms(o, g) * (gate * _sigmoid(gate))


def _mixer_kernel(z_ref, cw_ref, cb_ref, gbr_ref, gbc_ref, mg_ref, aw_ref, ab_ref, gg_ref, y_ref,
                  tail_ref, c_ref, m_ref, s_ref, oi_ref, kbuf, bbuf):
    c = pl.program_id(0)
    bn = z_ref.shape[0] // CHUNK

    @pl.when(c == 0)
    def _():
        tail_ref[...] = jnp.zeros_like(tail_ref)
        c_ref[...] = jnp.zeros_like(c_ref)
        m_ref[...] = jnp.zeros_like(m_ref)
        s_ref[...] = jnp.zeros_like(s_ref)

    row = lax.broadcasted_iota(jnp.int32, (CHUNK, CHUNK), 0)
    col = lax.broadcasted_iota(jnp.int32, (CHUNK, CHUNK), 1)
    causal = col <= row
    tri = jnp.where(causal, 1.0, 0.0).astype(F32)
    tri_t = jnp.where(row <= col, 1.0, 0.0).astype(F32)
    first = c == 0
    valid_c = jnp.logical_or(jnp.logical_not(first),
                             lax.broadcasted_iota(jnp.int32, (CHUNK, 1), 0) >= LEAD_PAD)
    valid_r = jnp.logical_or(jnp.logical_not(first),
                             lax.broadcasted_iota(jnp.int32, (1, CHUNK), 1) >= LEAD_PAD)
    row8 = lax.broadcasted_iota(jnp.int32, (SUBLANES, 1), 0)
    ones_col = jnp.where(lax.broadcasted_iota(jnp.int32, (CHUNK, LANES), 1) == 0, 1.0, 0.0).astype(F32)
    neg_inf = -jnp.inf

    def mlstm_batch(b):
        rows = slice(b * CHUNK, (b + 1) * CHUNK)
        x = jnp.where(valid_c, z_ref[rows, Z_QM:Z_QM + 2 * M_WIDTH], 0.0)
        prev = tail_ref[b]
        acc = x * cw_ref[CONV_K - 1:CONV_K, :] + cb_ref[...]
        for k in range(1, CONV_K):
            cur = pltpu.roll(x, k, 0)
            fix = pltpu.roll(prev, k, 0)
            top = jnp.where(row8 < k, fix, cur[0:SUBLANES])
            shifted = jnp.concatenate([top, cur[SUBLANES:]], axis=0)
            acc = acc + shifted * cw_ref[CONV_K - 1 - k:CONV_K - k, :]
        tail_ref[b] = x[CHUNK - SUBLANES:]
        qk = acc * _sigmoid(acc)
        qk = jnp.where(valid_c, qk, 0.0)
        q_m = qk[:, :M_WIDTH]
        k_m = qk[:, M_WIDTH:] * (M_HEAD_DIM ** -0.5)
        v_m = jnp.where(valid_c, z_ref[rows, Z_VM:Z_VM + M_WIDTH], 0.0)
        o_m = z_ref[rows, Z_OM:Z_OM + M_WIDTH]

        zs = z_ref[rows, Z_SMALL:Z_SMALL + LANES]
        g_col = zs + gbr_ref[...]
        g_row = zs.T[0:2 * M_HEADS, :] + gbc_ref[...]
        b_col = _dot_f32(tri, jnp.where(valid_c, _log_sigmoid(g_col), 0.0))
        b_row = _dot_f32(jnp.where(valid_r, _log_sigmoid(g_row), 0.0), tri_t)
        ig_col = jnp.where(valid_c, g_col, neg_inf)
        ig_row = jnp.where(valid_r, g_row, neg_inf)

        for hd in range(M_HEADS):
            idx = b * M_HEADS + hd
            lanes = slice(hd * M_HEAD_DIM, (hd + 1) * M_HEAD_DIM)
            q, k, v = q_m[:, lanes], k_m[:, lanes], v_m[:, lanes]
            bc, br = b_col[:, M_HEADS + hd:M_HEADS + hd + 1], b_row[M_HEADS + hd:M_HEADS + hd + 1, :]
            igc, igr = ig_col[:, hd:hd + 1], ig_row[hd:hd + 1, :]
            m_prev = m_ref[idx][0:1, 0:1]
            cst = c_ref[idx]
            dmat = jnp.where(causal, bc - br + igr, neg_inf)
            m_inter = bc + m_prev
            m_t = jnp.maximum(m_inter, jnp.max(dmat, axis=-1, keepdims=True))
            s = _dot_nt(q, k) * jnp.exp(dmat - m_t)
            w_inter = jnp.exp(m_inter - m_t)
            v_aug = jnp.concatenate([v, ones_col], axis=1)
            nd = _dot(s, v_aug) + w_inter * _dot(q, cst)
            num, den = nd[:, :M_HEAD_DIM], nd[:, M_HEAD_DIM:M_HEAD_DIM + 1]
            hh = num / jnp.maximum(jnp.abs(den), jnp.exp(-m_t))
            b_end = bc[CHUNK - 1:CHUNK, :]
            m_new = jnp.maximum(b_end + m_prev, jnp.max(b_end - br + igr, axis=-1, keepdims=True))
            wk = jnp.exp(b_end - bc + igc - m_new)
            decay = jnp.exp(b_end + m_prev - m_new)
            c_ref[idx] = decay * cst + _dot((k * wk).T, v_aug)
            m_ref[idx] = jnp.broadcast_to(m_new, (SUBLANES, LANES))
            y = _rms(hh, mg_ref[:, lanes]) * _sigmoid(o_m[:, lanes])
            y_ref[rows, lanes] = y.astype(y_ref.dtype)

    def gla_batch(b):
        rows = slice(b * CHUNK, (b + 1) * CHUNK)
        q_g, k_g, v_g, bg = _gla_streams(z_ref, rows, valid_c, tri, aw_ref, ab_ref)
        gate_g = z_ref[rows, Z_GG:Z_GG + G_WIDTH]
        bg_t = bg.T
        b_end = bg[CHUNK - 1:CHUNK, :]
        q_dec = q_g * jnp.exp(bg)
        k_inv = k_g * jnp.exp(-bg)
        k_end = k_g * jnp.exp(b_end - bg)
        for hd in range(G_HEADS):
            idx = b * G_HEADS + hd
            kl = slice(hd * G_K_DIM, (hd + 1) * G_K_DIM)
            vl = slice(hd * G_V_DIM, (hd + 1) * G_V_DIM)
            yl = slice(M_WIDTH + hd * G_V_DIM, M_WIDTH + (hd + 1) * G_V_DIM)
            st = s_ref[idx]
            a = jnp.where(causal, _dot_nt(q_dec[:, kl], k_inv[:, kl]), 0.0)
            o_inter = _dot(q_dec[:, kl], st)
            oi_ref[rows, vl] = o_inter
            e_end = jnp.exp(bg_t[hd * G_K_DIM:(hd + 1) * G_K_DIM, CHUNK - 1:CHUNK])
            s_ref[idx] = e_end * st + _dot(k_end[:, kl].T, v_g[:, vl])
            y = _gla_gated(_dot(a, v_g[:, vl]) + o_inter, gate_g[:, vl], gg_ref[:, vl])
            y_ref[rows, yl] = y.astype(y_ref.dtype)
        return jnp.max(-b_end) > GLA_SAFE_EXP

    def gla_redo_batch(b, carry):
        rows = pl.ds(pl.multiple_of(b * CHUNK, CHUNK), CHUNK)
        q_g, k_g, v_g, bg = _gla_streams(z_ref, rows, valid_c, tri, aw_ref, ab_ref)
        gate_g = z_ref[rows, Z_GG:Z_GG + G_WIDTH]
        for hd in range(G_HEADS):
            kl = slice(hd * G_K_DIM, (hd + 1) * G_K_DIM)
            vl = slice(hd * G_V_DIM, (hd + 1) * G_V_DIM)
            yl = slice(M_WIDTH + hd * G_V_DIM, M_WIDTH + (hd + 1) * G_V_DIM)
            a = jnp.where(causal, _gla_intra_pairwise(q_g[:, kl], k_g[:, kl], bg[:, kl], kbuf, bbuf), 0.0)
            y = _gla_gated(_dot(a, v_g[:, vl]) + oi_ref[rows, vl], gate_g[:, vl], gg_ref[:, vl])
            y_ref[rows, yl] = y.astype(y_ref.dtype)
        return carry

    unsafe = False
    for b in range(bn):
        mlstm_batch(b)
        unsafe = jnp.logical_or(unsafe, gla_batch(b))

    @pl.when(unsafe)
    def _():
        lax.fori_loop(0, bn, gla_redo_batch, 0)


def _mixer(z, conv_w, conv_b, gb_row, gb_col, m_norm_g, a_w2, a_b, g_norm_g, bn):
    rows = bn * CHUNK
    n = z.shape[0]
    full = lambda shape: pl.BlockSpec(shape, lambda c: (0,) * len(shape))
    return pl.pallas_call(
        _mixer_kernel,
        grid=(n // rows,),
        in_specs=[
            pl.BlockSpec((rows, Z_WIDTH), lambda c: (c, 0)),
            full((CONV_K, 2 * M_WIDTH)), full((1, 2 * M_WIDTH)),
            full((1, LANES)), full((2 * M_HEADS, CHUNK)),
            full((1, M_WIDTH)), full((G_RANK, G_K_WIDTH)), full((1, G_K_WIDTH)), full((1, G_WIDTH)),
        ],
        out_specs=pl.BlockSpec((rows, D_MODEL), lambda c: (c, 0)),
        out_shape=jax.ShapeDtypeStruct((n, D_MODEL), BF16),
        scratch_shapes=[
            pltpu.VMEM((bn, SUBLANES, 2 * M_WIDTH), F32),
            pltpu.VMEM((bn * M_HEADS, M_HEAD_DIM, 2 * M_HEAD_DIM), F32),
            pltpu.VMEM((bn * M_HEADS, SUBLANES, LANES), F32),
            pltpu.VMEM((bn * G_HEADS, G_K_DIM, G_V_DIM), F32),
            pltpu.VMEM((rows, G_WIDTH), F32),
            pltpu.VMEM((CHUNK, G_K_DIM), F32), pltpu.VMEM((CHUNK, G_K_DIM), F32),
        ],
        compiler_params=_params("arbitrary"),
        name="mixer",
    )(z, conv_w, conv_b, gb_row, gb_col, m_norm_g, a_w2, a_b, g_norm_g)


def _outproj_router_kernel(h_ref, y_ref, wo_ref, g_ref, wr_ref, br_ref, h1_ref, xe_ref, route_ref, cnt_ref,
                           carry_ref):
    i = pl.program_id(0)
    rows = h_ref.shape[0]

    @pl.when(i == 0)
    def _():
        carry_ref[...] = jnp.zeros_like(carry_ref)

    h1 = h_ref[...] + jnp.dot(y_ref[...], wo_ref[...], preferred_element_type=F32)
    h1_ref[...] = h1
    xn = _rms(h1, g_ref[...])
    logits = _dot_f32(xn, wr_ref[...]) + br_ref[...]
    lane = lax.broadcasted_iota(jnp.int32, (rows, LANES), 1)
    neg_inf = -jnp.inf
    big = jnp.int32(LANES)

    def first_argmax(vals):
        top = jnp.max(vals, axis=-1, keepdims=True)
        return top, jnp.min(jnp.where(vals == top, lane, big), axis=-1, keepdims=True)

    lg = jnp.where(lane < N_GROUPS, logits, neg_inf)
    g_max, g_sel = first_argmax(lg)
    p_sel = 1.0 / jnp.sum(jnp.exp(lg - g_max), axis=-1, keepdims=True)
    in_group = jnp.logical_and(lane >= N_GROUPS + EXPERTS_PER_GROUP * g_sel,
                               lane < N_GROUPS + EXPERTS_PER_GROUP * (g_sel + 1))
    le = jnp.where(in_group, logits, neg_inf)
    v1, i1 = first_argmax(le)
    v2, i2 = first_argmax(jnp.where(lane == i1, neg_inf, le))
    e21 = jnp.exp(v2 - v1)
    gate1 = p_sel / (1.0 + e21)
    gate2 = p_sel * e21 / (1.0 + e21)
    j1 = i1 - N_GROUPS - EXPERTS_PER_GROUP * g_sel
    j2 = i2 - N_GROUPS - EXPERTS_PER_GROUP * g_sel
    ja, jb = jnp.minimum(j1, j2), jnp.maximum(j1, j2)
    ga = jnp.where(j1 < j2, gate1, gate2)
    gb = jnp.where(j1 < j2, gate2, gate1)
    pair = ((ja * (2 * EXPERTS_PER_GROUP - 1 - ja)) >> 1) + (jb - ja - 1)
    cls = g_sel * N_PAIRS + pair

    xe_ref[:, :D_MODEL] = xn
    xe_ref[:, D_MODEL:] = jnp.where(lane == 0, ga, jnp.where(lane == 1, gb, 0.0))

    onehot = jnp.where(lane == cls, 1.0, 0.0).astype(F32)
    r_i = lax.broadcasted_iota(jnp.int32, (rows, rows), 0)
    c_i = lax.broadcasted_iota(jnp.int32, (rows, rows), 1)
    incl = jnp.where(c_i <= r_i, 1.0, 0.0).astype(BF16)
    prefix = jnp.dot(incl, onehot.astype(BF16), preferred_element_type=F32)
    rank = jnp.sum(onehot * (prefix - 1.0 + carry_ref[...]), axis=-1, keepdims=True)
    carry_ref[...] = carry_ref[...] + prefix[rows - 1:rows, :]
    cnt_ref[...] = carry_ref[...]
    route_ref[...] = jnp.where(lane == 0, cls, jnp.where(lane == 1, rank.astype(jnp.int32), 0))


def _outproj_router(h, y, w_out, g, wr, br, rows):
    n = h.shape[0]
    return pl.pallas_call(
        _outproj_router_kernel,
        grid=(n // rows,),
        in_specs=[
            pl.BlockSpec((rows, D_MODEL), lambda i: (i, 0)),
            pl.BlockSpec((rows, D_MODEL), lambda i: (i, 0)),
            pl.BlockSpec((D_MODEL, D_MODEL), lambda i: (0, 0)),
            pl.BlockSpec((1, D_MODEL), lambda i: (0, 0)),
            pl.BlockSpec((D_MODEL, LANES), lambda i: (0, 0)),
            pl.BlockSpec((1, LANES), lambda i: (0, 0)),
        ],
        out_specs=[
            pl.BlockSpec((rows, D_MODEL), lambda i: (i, 0)),
            pl.BlockSpec((rows, X_EXT), lambda i: (i, 0)),
            pl.BlockSpec((rows, LANES), lambda i: (i, 0)),
            pl.BlockSpec((1, LANES), lambda i: (0, 0)),
        ],
        out_shape=[
            jax.ShapeDtypeStruct((n, D_MODEL), F32),
            jax.ShapeDtypeStruct((n, X_EXT), F32),
            jax.ShapeDtypeStruct((n, LANES), jnp.int32),
            jax.ShapeDtypeStruct((1, LANES), F32),
        ],
        scratch_shapes=[pltpu.VMEM((1, LANES), F32)],
        compiler_params=_params("arbitrary"),
        name="outproj_router",
    )(h, y, w_out, g, wr, br)


def _expert_kernel(ea_ref, eb_ref, nused_ref, tok_ref, x_hbm, wgu_a, wd_a, wgu_b, wd_b, y_hbm,
                   xbuf, obuf, sem_in, sem_out):
    i = pl.program_id(0)
    n_rows = x_hbm.shape[0]

    def gather_row(k):
        src = jnp.minimum(tok_ref[0, 0, k], n_rows - 1)
        return pltpu.make_async_copy(x_hbm.at[pl.ds(src, 1), :], xbuf.at[pl.ds(k, 1), :], sem_in)

    def scatter_row(k):
        return pltpu.make_async_copy(obuf.at[pl.ds(k, 1), :], y_hbm.at[pl.ds(tok_ref[0, 0, k], 1), :], sem_out)

    def for_rows(fn):
        def body(k, carry):
            fn(k)
            return carry
        lax.fori_loop(0, EXPERT_ROWS, body, 0)

    def for_real_rows(fn):
        for_rows(lambda k: pl.when(tok_ref[0, 0, k] < n_rows)(lambda: fn(k)))

    @pl.when(i < nused_ref[0])
    def _():
        for_rows(lambda k: gather_row(k).start())
        for_rows(lambda k: gather_row(k).wait())
        x = xbuf[...]
        xb = x[:, :D_MODEL].astype(BF16)
        ga = x[:, D_MODEL:D_MODEL + 1]
        gb = x[:, D_MODEL + 1:D_MODEL + 2]

        def mlp(wgu, wd):
            gu = jnp.dot(xb, wgu[0], preferred_element_type=F32)
            gate, up = gu[:, :D_EXPERT], gu[:, D_EXPERT:]
            return jnp.dot((gate * _sigmoid(gate) * up).astype(BF16), wd[0], preferred_element_type=F32)

        obuf[...] = ga * mlp(wgu_a, wd_a) + gb * mlp(wgu_b, wd_b)
        for_real_rows(lambda k: scatter_row(k).start())
        for_real_rows(lambda k: scatter_row(k).wait())


def _experts(blk_ea, blk_eb, n_used, slot_tok, xe, wgu, wd):
    n_blocks = blk_ea.shape[0]
    n = xe.shape[0]
    grid_spec = pltpu.PrefetchScalarGridSpec(
        num_scalar_prefetch=3,
        grid=(n_blocks,),
        in_specs=[
            pl.BlockSpec((1, 1, EXPERT_ROWS), lambda i, ea, eb, nu: (i, 0, 0), memory_space=pltpu.SMEM),
            pl.BlockSpec(memory_space=pl.ANY),
            pl.BlockSpec((1, D_MODEL, 2 * D_EXPERT), lambda i, ea, eb, nu: (ea[i], 0, 0)),
            pl.BlockSpec((1, D_EXPERT, D_MODEL), lambda i, ea, eb, nu: (ea[i], 0, 0)),
            pl.BlockSpec((1, D_MODEL, 2 * D_EXPERT), lambda i, ea, eb, nu: (eb[i], 0, 0)),
            pl.BlockSpec((1, D_EXPERT, D_MODEL), lambda i, ea, eb, nu: (eb[i], 0, 0)),
        ],
        out_specs=pl.BlockSpec(memory_space=pl.ANY),
        scratch_shapes=[
            pltpu.VMEM((EXPERT_ROWS, X_EXT), F32),
            pltpu.VMEM((EXPERT_ROWS, D_MODEL), F32),
            pltpu.SemaphoreType.DMA(()),
            pltpu.SemaphoreType.DMA(()),
        ],
    )
    return pl.pallas_call(
        _expert_kernel,
        grid_spec=grid_spec,
        out_shape=jax.ShapeDtypeStruct((n, D_MODEL), F32),
        compiler_params=_params("arbitrary"),
        name="experts",
    )(blk_ea, blk_eb, n_used, slot_tok, xe, wgu, wd, wgu, wd)


def _pair_tables():
    ea, eb = [], []
    for g in range(N_GROUPS):
        for a in range(EXPERTS_PER_GROUP):
            for b in range(a + 1, EXPERTS_PER_GROUP):
                ea.append(g * EXPERTS_PER_GROUP + a)
                eb.append(g * EXPERTS_PER_GROUP + b)
    return np.asarray(ea, np.int32), np.asarray(eb, np.int32)


def _plan_blocks(route, counts, n):
    cls, rank = route[:, 0], route[:, 1]
    cnt = counts[0, :N_CLASSES].astype(jnp.int32)
    padded = (cnt + EXPERT_ROWS - 1) // EXPERT_ROWS * EXPERT_ROWS
    pad_end = jnp.cumsum(padded)
    slot = (pad_end - padded)[cls] + rank
    n_blocks = -(-(n + N_CLASSES * (EXPERT_ROWS - 1)) // EXPERT_ROWS)
    n_slots = n_blocks * EXPERT_ROWS
    slot_tok = jnp.full((n_slots,), n, jnp.int32).at[slot].set(jnp.arange(n, dtype=jnp.int32))
    blk_cls = jnp.minimum(jnp.searchsorted(pad_end, jnp.arange(n_blocks, dtype=jnp.int32) * EXPERT_ROWS,
                                           side='right'), N_CLASSES - 1)
    ea_tab, eb_tab = _pair_tables()
    n_used = (pad_end[-1] // EXPERT_ROWS).astype(jnp.int32).reshape(1)
    return (jnp.asarray(ea_tab)[blk_cls], jnp.asarray(eb_tab)[blk_cls], n_used,
            slot_tok.reshape(n_blocks, 1, EXPERT_ROWS))


def _final_kernel(h_ref, y_ref, g_ref, o_ref):
    out = _rms(h_ref[...] + y_ref[...], g_ref[...])
    o_ref[...] = out.reshape(o_ref.shape)


def _final(h, y, g, bn, n_chunks):
    rows = bn * CHUNK
    return pl.pallas_call(
        _final_kernel,
        grid=(n_chunks - 1,),
        in_specs=[
            pl.BlockSpec((rows, D_MODEL), lambda c: (c + 1, 0)),
            pl.BlockSpec((rows, D_MODEL), lambda c: (c + 1, 0)),
            pl.BlockSpec((1, D_MODEL), lambda c: (0, 0)),
        ],
        out_specs=pl.BlockSpec((bn, 1, CHUNK, D_MODEL), lambda c: (0, c, 0, 0)),
        out_shape=jax.ShapeDtypeStruct((bn, n_chunks - 1, CHUNK, D_MODEL), F32),
        compiler_params=_params("parallel"),
        name="final_norm",
    )(h, y, g)


def _repack_w_in(w):
    o = np.cumsum([0, 2 * M_WIDTH, M_WIDTH, M_WIDTH, M_HEADS, M_HEADS, G_K_WIDTH, G_K_WIDTH, G_WIDTH, G_WIDTH,
                   G_RANK])
    seg = lambda i: w[:, o[i]:o[i + 1]]
    small = jnp.concatenate([seg(3), seg(4), seg(9)], axis=1)
    small = jnp.pad(small, ((0, 0), (0, LANES - small.shape[1])))
    return jnp.concatenate([seg(0), seg(1), seg(2), seg(5), seg(6), seg(7), seg(8), small], axis=1).astype(BF16)


def kernel(x, meta_tokens, norm1_g, w_in, conv_w, conv_b, gate_b, m_norm_g, a_w2, a_b, g_norm_g, w_out, norm2_g,
           wr_g, br_g, wr_e, br_e, w_gate, w_up, w_down, final_norm_g):
    bn, seq, dm = x.shape
    depth = w_in.shape[0]
    assert dm == D_MODEL and seq % CHUNK == 0
    n_chunks = seq // CHUNK + 1
    rows = bn * CHUNK
    n = n_chunks * rows
    x4 = x.reshape(bn, seq // CHUNK, CHUNK, dm)
    row = lambda v: v.reshape(1, -1).astype(F32)

    h = y_moe = None
    for l in range(depth):
        w_in_l = _repack_w_in(w_in[l])
        if l == 0:
            h, z = _embed_inproj(x4, meta_tokens.astype(F32), row(norm1_g[l]), w_in_l, n_chunks)
        else:
            h, z = _add_inproj(h, y_moe, row(norm1_g[l]), w_in_l, rows)
        gb = gate_b[l].astype(F32).reshape(1, 2 * M_HEADS)
        y = _mixer(z, conv_w[l].astype(F32), row(conv_b[l]), jnp.pad(gb, ((0, 0), (0, LANES - 2 * M_HEADS))),
                   jnp.broadcast_to(gb.reshape(2 * M_HEADS, 1), (2 * M_HEADS, CHUNK)),
                   row(m_norm_g[l]), a_w2[l].astype(F32), row(a_b[l]), row(g_norm_g[l]), bn)
        wr = jnp.pad(jnp.concatenate([wr_g[l], wr_e[l]], axis=1).astype(F32),
                     ((0, 0), (0, LANES - N_GROUPS - N_EXPERTS)))
        br = jnp.pad(jnp.concatenate([br_g[l], br_e[l]]).astype(F32), (0, LANES - N_GROUPS - N_EXPERTS))
        h, xe, route, counts = _outproj_router(h, y, w_out[l].astype(BF16), row(norm2_g[l]), wr,
                                               br.reshape(1, LANES), rows)
        blk_ea, blk_eb, n_used, slot_tok = _plan_blocks(route, counts, n)
        wgu = jnp.concatenate([w_gate[l], w_up[l]], axis=-1).astype(BF16)
        y_moe = _experts(blk_ea, blk_eb, n_used, slot_tok, xe, wgu, w_down[l].astype(BF16))
    out = _final(h, y_moe, row(final_norm_g), bn, n_chunks)
    return out.reshape(bn, seq, dm)
```

```python
import functools

import numpy as np
import jax
import jax.numpy as jnp
from jax import lax
from jax.experimental import pallas as pl
from jax.experimental.pallas import tpu as pltpu

F32 = jnp.float32
BF16 = jnp.bfloat16
HIGHEST = lax.Precision.HIGHEST

D_MODEL = 1024
CHUNK = 64
N_META = 16
LEAD_PAD = CHUNK - N_META
EPS = 1e-6
M_HEADS = 4
M_WIDTH = 512
M_HEAD_DIM = 128
CONV_K = 4
G_HEADS = 4
G_WIDTH = 512
G_V_DIM = 128
G_K_DIM = 64
G_K_WIDTH = 256
G_RANK = 16
G_TAU = 16.0
N_GROUPS = 4
EXPERTS_PER_GROUP = 8
N_EXPERTS = 32
D_EXPERT = 256
N_PAIRS = EXPERTS_PER_GROUP * (EXPERTS_PER_GROUP - 1) // 2
N_CLASSES = N_GROUPS * N_PAIRS

LANES = 128
SUBLANES = 8
EXPERT_ROWS = 128
VMEM_LIMIT = 56 * 1024 * 1024

Z_QM, Z_KM, Z_VM, Z_OM = 0, 512, 1024, 1536
Z_QG, Z_KG, Z_VG, Z_GG = 2048, 2304, 2560, 3072
Z_SMALL = 3584
Z_WIDTH = Z_SMALL + LANES
X_EXT = D_MODEL + LANES
GLA_SAFE_EXP = 80.0


def _log_sigmoid(x):
    return jnp.minimum(x, 0.0) - jnp.log(1.0 + jnp.exp(-jnp.abs(x)))


def _sigmoid(x):
    return 1.0 / (1.0 + jnp.exp(-x))


def _rms(x, g):
    return x * lax.rsqrt(jnp.mean(x * x, axis=-1, keepdims=True) + EPS) * g


def _dot(a, b):
    return jnp.dot(a.astype(BF16), b.astype(BF16), preferred_element_type=F32)


def _dot_nt(a, b):
    return lax.dot_general(a.astype(BF16), b.astype(BF16), (((1,), (1,)), ((), ())),
                           preferred_element_type=F32)


def _dot_f32(a, b):
    return jnp.dot(a, b, precision=HIGHEST, preferred_element_type=F32)


def _params(*semantics):
    return pltpu.CompilerParams(dimension_semantics=semantics, vmem_limit_bytes=VMEM_LIMIT)


def _embed_inproj_kernel(x_ref, meta_ref, g_ref, w_ref, h_ref, z_ref):
    c = pl.program_id(0)
    bn = x_ref.shape[0]
    rows = bn * CHUNK

    @pl.when(c == 0)
    def _():
        lead = jnp.concatenate([jnp.zeros((LEAD_PAD, D_MODEL), F32), meta_ref[...]], axis=0)
        for b in range(bn):
            h_ref[b * CHUNK:(b + 1) * CHUNK, :] = lead

    @pl.when(c > 0)
    def _():
        h_ref[...] = x_ref[...].reshape(rows, D_MODEL)

    z_ref[...] = _dot(_rms(h_ref[...], g_ref[...]), w_ref[...])


def _embed_inproj(x4, meta, g, w, n_chunks):
    bn = x4.shape[0]
    rows = bn * CHUNK
    n = n_chunks * rows
    return pl.pallas_call(
        _embed_inproj_kernel,
        grid=(n_chunks,),
        in_specs=[
            pl.BlockSpec((bn, 1, CHUNK, D_MODEL), lambda c: (0, jnp.maximum(c - 1, 0), 0, 0)),
            pl.BlockSpec((N_META, D_MODEL), lambda c: (0, 0)),
            pl.BlockSpec((1, D_MODEL), lambda c: (0, 0)),
            pl.BlockSpec((D_MODEL, Z_WIDTH), lambda c: (0, 0)),
        ],
        out_specs=[
            pl.BlockSpec((rows, D_MODEL), lambda c: (c, 0)),
            pl.BlockSpec((rows, Z_WIDTH), lambda c: (c, 0)),
        ],
        out_shape=[jax.ShapeDtypeStruct((n, D_MODEL), F32), jax.ShapeDtypeStruct((n, Z_WIDTH), F32)],
        compiler_params=_params("parallel"),
        name="embed_inproj",
    )(x4, meta, g, w)


def _add_inproj_kernel(h_ref, y_ref, g_ref, w_ref, ho_ref, z_ref):
    h = h_ref[...] + y_ref[...]
    ho_ref[...] = h
    z_ref[...] = _dot(_rms(h, g_ref[...]), w_ref[...])


def _add_inproj(h, y, g, w, rows):
    n = h.shape[0]
    return pl.pallas_call(
        _add_inproj_kernel,
        grid=(n // rows,),
        in_specs=[
            pl.BlockSpec((rows, D_MODEL), lambda i: (i, 0)),
            pl.BlockSpec((rows, D_MODEL), lambda i: (i, 0)),
            pl.BlockSpec((1, D_MODEL), lambda i: (0, 0)),
            pl.BlockSpec((D_MODEL, Z_WIDTH), lambda i: (0, 0)),
        ],
        out_specs=[
            pl.BlockSpec((rows, D_MODEL), lambda i: (i, 0)),
            pl.BlockSpec((rows, Z_WIDTH), lambda i: (i, 0)),
        ],
        out_shape=[jax.ShapeDtypeStruct((n, D_MODEL), F32), jax.ShapeDtypeStruct((n, Z_WIDTH), F32)],
        compiler_params=_params("parallel"),
        name="add_inproj",
    )(h, y, g, w)


def _gla_intra_pairwise(q, k, bg, kbuf, bbuf):
    kbuf[...] = k
    bbuf[...] = bg
    lane = lax.broadcasted_iota(jnp.int32, (CHUNK, CHUNK), 1)

    def column(s, a):
        ks = kbuf[pl.ds(s, 1), :]
        bs = bbuf[pl.ds(s, 1), :]
        col = jnp.sum(q * ks * jnp.exp(jnp.minimum(bg - bs, 0.0)), axis=-1, keepdims=True)
        return jnp.where(lane == s, col, a)

    return lax.fori_loop(0, CHUNK, column, jnp.zeros((CHUNK, CHUNK), F32))


def _gla_streams(z_ref, rows, valid_c, tri, aw_ref, ab_ref):
    code = z_ref[rows, Z_SMALL:Z_SMALL + LANES][:, 2 * M_HEADS:2 * M_HEADS + G_RANK]
    u = _dot_f32(code, aw_ref[...]) + ab_ref[...]
    la = jnp.where(valid_c, _log_sigmoid(u) * (1.0 / G_TAU), 0.0)
    bg = _dot_f32(tri, la)
    q_g = jnp.where(valid_c, z_ref[rows, Z_QG:Z_QG + G_K_WIDTH], 0.0) * (G_K_DIM ** -0.5)
    k_g = jnp.where(valid_c, z_ref[rows, Z_KG:Z_KG + G_K_WIDTH], 0.0)
    v_g = jnp.where(valid_c, z_ref[rows, Z_VG:Z_VG + G_WIDTH], 0.0)
    return q_g, k_g, v_g, bg


def _gla_gated(o, gate, g):
    return _rms(o, g) * (gate * _sigmoid(gate))


def _mixer_kernel(z_ref, cw_ref, cb_ref, gbr_ref, gbc_ref, mg_ref, aw_ref, ab_ref, gg_ref, y_ref,
                  tail_ref, c_ref, m_ref, s_ref, oi_ref, kbuf, bbuf):
    c = pl.program_id(0)
    bn = z_ref.shape[0] // CHUNK

    @pl.when(c == 0)
    def _():
        tail_ref[...] = jnp.zeros_like(tail_ref)
        c_ref[...] = jnp.zeros_like(c_ref)
        m_ref[...] = jnp.zeros_like(m_ref)
        s_ref[...] = jnp.zeros_like(s_ref)

    row = lax.broadcasted_iota(jnp.int32, (CHUNK, CHUNK), 0)
    col = lax.broadcasted_iota(jnp.int32, (CHUNK, CHUNK), 1)
    causal = col <= row
    tri = jnp.where(causal, 1.0, 0.0).astype(F32)
    tri_t = jnp.where(row <= col, 1.0, 0.0).astype(F32)
    first = c == 0
    valid_c = jnp.logical_or(jnp.logical_not(first),
                             lax.broadcasted_iota(jnp.int32, (CHUNK, 1), 0) >= LEAD_PAD)
    valid_r = jnp.logical_or(jnp.logical_not(first),
                             lax.broadcasted_iota(jnp.int32, (1, CHUNK), 1) >= LEAD_PAD)
    row8 = lax.broadcasted_iota(jnp.int32, (SUBLANES, 1), 0)
    ones_col = jnp.where(lax.broadcasted_iota(jnp.int32, (CHUNK, LANES), 1) == 0, 1.0, 0.0).astype(F32)
    neg_inf = -jnp.inf

    def mlstm_batch(b):
        rows = slice(b * CHUNK, (b + 1) * CHUNK)
        x = jnp.where(valid_c, z_ref[rows, Z_QM:Z_QM + 2 * M_WIDTH], 0.0)
        prev = tail_ref[b]
        acc = x * cw_ref[CONV_K - 1:CONV_K, :] + cb_ref[...]
        for k in range(1, CONV_K):
            cur = pltpu.roll(x, k, 0)
            fix = pltpu.roll(prev, k, 0)
            top = jnp.where(row8 < k, fix, cur[0:SUBLANES])
            shifted = jnp.concatenate([top, cur[SUBLANES:]], axis=0)
            acc = acc + shifted * cw_ref[CONV_K - 1 - k:CONV_K - k, :]
        tail_ref[b] = x[CHUNK - SUBLANES:]
        qk = acc * _sigmoid(acc)
        qk = jnp.where(valid_c, qk, 0.0)
        q_m = qk[:, :M_WIDTH]
        k_m = qk[:, M_WIDTH:] * (M_HEAD_DIM ** -0.5)
        v_m = jnp.where(valid_c, z_ref[rows, Z_VM:Z_VM + M_WIDTH], 0.0)
        o_m = z_ref[rows, Z_OM:Z_OM + M_WIDTH]

        zs = z_ref[rows, Z_SMALL:Z_SMALL + LANES]
        g_col = zs + gbr_ref[...]
        g_row = zs.T[0:2 * M_HEADS, :] + gbc_ref[...]
        b_col = _dot_f32(tri, jnp.where(valid_c, _log_sigmoid(g_col), 0.0))
        b_row = _dot_f32(jnp.where(valid_r, _log_sigmoid(g_row), 0.0), tri_t)
        ig_col = jnp.where(valid_c, g_col, neg_inf)
        ig_row = jnp.where(valid_r, g_row, neg_inf)

        for hd in range(M_HEADS):
            idx = b * M_HEADS + hd
            lanes = slice(hd * M_HEAD_DIM, (hd + 1) * M_HEAD_DIM)
            q, k, v = q_m[:, lanes], k_m[:, lanes], v_m[:, lanes]
            bc, br = b_col[:, M_HEADS + hd:M_HEADS + hd + 1], b_row[M_HEADS + hd:M_HEADS + hd + 1, :]
            igc, igr = ig_col[:, hd:hd + 1], ig_row[hd:hd + 1, :]
            m_prev = m_ref[idx][0:1, 0:1]
            cst = c_ref[idx]
            dmat = jnp.where(causal, bc - br + igr, neg_inf)
            m_inter = bc + m_prev
            m_t = jnp.maximum(m_inter, jnp.max(dmat, axis=-1, keepdims=True))
            s = _dot_nt(q, k) * jnp.exp(dmat - m_t)
            w_inter = jnp.exp(m_inter - m_t)
            v_aug = jnp.concatenate([v, ones_col], axis=1)
            nd = _dot(s, v_aug) + w_inter * _dot(q, cst)
            num, den = nd[:, :M_HEAD_DIM], nd[:, M_HEAD_DIM:M_HEAD_DIM + 1]
            hh = num / jnp.maximum(jnp.abs(den), jnp.exp(-m_t))
            b_end = bc[CHUNK - 1:CHUNK, :]
            m_new = jnp.maximum(b_end + m_prev, jnp.max(b_end - br + igr, axis=-1, keepdims=True))
            wk = jnp.exp(b_end - bc + igc - m_new)
            decay = jnp.exp(b_end + m_prev - m_new)
            c_ref[idx] = decay * cst + _dot((k * wk).T, v_aug)
            m_ref[idx] = jnp.broadcast_to(m_new, (SUBLANES, LANES))
            y = _rms(hh, mg_ref[:, lanes]) * _sigmoid(o_m[:, lanes])
            y_ref[rows, lanes] = y.astype(y_ref.dtype)

    def gla_batch(b):
        rows = slice(b * CHUNK, (b + 1) * CHUNK)
        q_g, k_g, v_g, bg = _gla_streams(z_ref, rows, valid_c, tri, aw_ref, ab_ref)
        gate_g = z_ref[rows, Z_GG:Z_GG + G_WIDTH]
        bg_t = bg.T
        b_end = bg[CHUNK - 1:CHUNK, :]
        q_dec = q_g * jnp.exp(bg)
        k_inv = k_g * jnp.exp(-bg)
        k_end = k_g * jnp.exp(b_end - bg)
        for hd in range(G_HEADS):
            idx = b * G_HEADS + hd
            kl = slice(hd * G_K_DIM, (hd + 1) * G_K_DIM)
            vl = slice(hd * G_V_DIM, (hd + 1) * G_V_DIM)
            yl = slice(M_WIDTH + hd * G_V_DIM, M_WIDTH + (hd + 1) * G_V_DIM)
            st = s_ref[idx]
            a = jnp.where(causal, _dot_nt(q_dec[:, kl], k_inv[:, kl]), 0.0)
            o_inter = _dot(q_dec[:, kl], st)
            oi_ref[rows, vl] = o_inter
            e_end = jnp.exp(bg_t[hd * G_K_DIM:(hd + 1) * G_K_DIM, CHUNK - 1:CHUNK])
            s_ref[idx] = e_end * st + _dot(k_end[:, kl].T, v_g[:, vl])
            y = _gla_gated(_dot(a, v_g[:, vl]) + o_inter, gate_g[:, vl], gg_ref[:, vl])
            y_ref[rows, yl] = y.astype(y_ref.dtype)
        return jnp.max(-b_end) > GLA_SAFE_EXP

    def gla_redo_batch(b, carry):
        rows = pl.ds(pl.multiple_of(b * CHUNK, CHUNK), CHUNK)
        q_g, k_g, v_g, bg = _gla_streams(z_ref, rows, valid_c, tri, aw_ref, ab_ref)
        gate_g = z_ref[rows, Z_GG:Z_GG + G_WIDTH]
        for hd in range(G_HEADS):
            kl = slice(hd * G_K_DIM, (hd + 1) * G_K_DIM)
            vl = slice(hd * G_V_DIM, (hd + 1) * G_V_DIM)
            yl = slice(M_WIDTH + hd * G_V_DIM, M_WIDTH + (hd + 1) * G_V_DIM)
            a = jnp.where(causal, _gla_intra_pairwise(q_g[:, kl], k_g[:, kl], bg[:, kl], kbuf, bbuf), 0.0)
            y = _gla_gated(_dot(a, v_g[:, vl]) + oi_ref[rows, vl], gate_g[:, vl], gg_ref[:, vl])
            y_ref[rows, yl] = y.astype(y_ref.dtype)
        return carry

    unsafe = False
    for b in range(bn):
        mlstm_batch(b)
        unsafe = jnp.logical_or(unsafe, gla_batch(b))

    @pl.when(unsafe)
    def _():
        lax.fori_loop(0, bn, gla_redo_batch, 0)


def _mixer(z, conv_w, conv_b, gb_row, gb_col, m_norm_g, a_w2, a_b, g_norm_g, bn):
    rows = bn * CHUNK
    n = z.shape[0]
    full = lambda shape: pl.BlockSpec(shape, lambda c: (0,) * len(shape))
    return pl.pallas_call(
        _mixer_kernel,
        grid=(n // rows,),
        in_specs=[
            pl.BlockSpec((rows, Z_WIDTH), lambda c: (c, 0)),
            full((CONV_K, 2 * M_WIDTH)), full((1, 2 * M_WIDTH)),
            full((1, LANES)), full((2 * M_HEADS, CHUNK)),
            full((1, M_WIDTH)), full((G_RANK, G_K_WIDTH)), full((1, G_K_WIDTH)), full((1, G_WIDTH)),
        ],
        out_specs=pl.BlockSpec((rows, D_MODEL), lambda c: (c, 0)),
        out_shape=jax.ShapeDtypeStruct((n, D_MODEL), BF16),
        scratch_shapes=[
            pltpu.VMEM((bn, SUBLANES, 2 * M_WIDTH), F32),
            pltpu.VMEM((bn * M_HEADS, M_HEAD_DIM, 2 * M_HEAD_DIM), F32),
            pltpu.VMEM((bn * M_HEADS, SUBLANES, LANES), F32),
            pltpu.VMEM((bn * G_HEADS, G_K_DIM, G_V_DIM), F32),
            pltpu.VMEM((rows, G_WIDTH), F32),
            pltpu.VMEM((CHUNK, G_K_DIM), F32), pltpu.VMEM((CHUNK, G_K_DIM), F32),
        ],
        compiler_params=_params("arbitrary"),
        name="mixer",
    )(z, conv_w, conv_b, gb_row, gb_col, m_norm_g, a_w2, a_b, g_norm_g)


def _outproj_router_kernel(h_ref, y_ref, wo_ref, g_ref, wr_ref, br_ref, h1_ref, xe_ref, route_ref, cnt_ref,
                           carry_ref):
    i = pl.program_id(0)
    rows = h_ref.shape[0]

    @pl.when(i == 0)
    def _():
        carry_ref[...] = jnp.zeros_like(carry_ref)

    h1 = h_ref[...] + jnp.dot(y_ref[...], wo_ref[...], preferred_element_type=F32)
    h1_ref[...] = h1
    xn = _rms(h1, g_ref[...])
    logits = _dot_f32(xn, wr_ref[...]) + br_ref[...]
    lane = lax.broadcasted_iota(jnp.int32, (rows, LANES), 1)
    neg_inf = -jnp.inf
    big = jnp.int32(LANES)

    def first_argmax(vals):
        top = jnp.max(vals, axis=-1, keepdims=True)
        return top, jnp.min(jnp.where(vals == top, lane, big), axis=-1, keepdims=True)

    lg = jnp.where(lane < N_GROUPS, logits, neg_inf)
    g_max, g_sel = first_argmax(lg)
    p_sel = 1.0 / jnp.sum(jnp.exp(lg - g_max), axis=-1, keepdims=True)
    in_group = jnp.logical_and(lane >= N_GROUPS + EXPERTS_PER_GROUP * g_sel,
                               lane < N_GROUPS + EXPERTS_PER_GROUP * (g_sel + 1))
    le = jnp.where(in_group, logits, neg_inf)
    v1, i1 = first_argmax(le)
    v2, i2 = first_argmax(jnp.where(lane == i1, neg_inf, le))
    e21 = jnp.exp(v2 - v1)
    gate1 = p_sel / (1.0 + e21)
    gate2 = p_sel * e21 / (1.0 + e21)
    j1 = i1 - N_GROUPS - EXPERTS_PER_GROUP * g_sel
    j2 = i2 - N_GROUPS - EXPERTS_PER_GROUP * g_sel
    ja, jb = jnp.minimum(j1, j2), jnp.maximum(j1, j2)
    ga = jnp.where(j1 < j2, gate1, gate2)
    gb = jnp.where(j1 < j2, gate2, gate1)
    pair = ((ja * (2 * EXPERTS_PER_GROUP - 1 - ja)) >> 1) + (jb - ja - 1)
    cls = g_sel * N_PAIRS + pair

    xe_ref[:, :D_MODEL] = xn
    xe_ref[:, D_MODEL:] = jnp.where(lane == 0, ga, jnp.where(lane == 1, gb, 0.0))

    onehot = jnp.where(lane == cls, 1.0, 0.0).astype(F32)
    r_i = lax.broadcasted_iota(jnp.int32, (rows, rows), 0)
    c_i = lax.broadcasted_iota(jnp.int32, (rows, rows), 1)
    incl = jnp.where(c_i <= r_i, 1.0, 0.0).astype(BF16)
    prefix = jnp.dot(incl, onehot.astype(BF16), preferred_element_type=F32)
    rank = jnp.sum(onehot * (prefix - 1.0 + carry_ref[...]), axis=-1, keepdims=True)
    carry_ref[...] = carry_ref[...] + prefix[rows - 1:rows, :]
    cnt_ref[...] = carry_ref[...]
    route_ref[...] = jnp.where(lane == 0, cls, jnp.where(lane == 1, rank.astype(jnp.int32), 0))


def _outproj_router(h, y, w_out, g, wr, br, rows):
    n = h.shape[0]
    return pl.pallas_call(
        _outproj_router_kernel,
        grid=(n // rows,),
        in_specs=[
            pl.BlockSpec((rows, D_MODEL), lambda i: (i, 0)),
            pl.BlockSpec((rows, D_MODEL), lambda i: (i, 0)),
            pl.BlockSpec((D_MODEL, D_MODEL), lambda i: (0, 0)),
            pl.BlockSpec((1, D_MODEL), lambda i: (0, 0)),
            pl.BlockSpec((D_MODEL, LANES), lambda i: (0, 0)),
            pl.BlockSpec((1, LANES), lambda i: (0, 0)),
        ],
        out_specs=[
            pl.BlockSpec((rows, D_MODEL), lambda i: (i, 0)),
            pl.BlockSpec((rows, X_EXT), lambda i: (i, 0)),
            pl.BlockSpec((rows, LANES), lambda i: (i, 0)),
            pl.BlockSpec((1, LANES), lambda i: (0, 0)),
        ],
        out_shape=[
            jax.ShapeDtypeStruct((n, D_MODEL), F32),
            jax.ShapeDtypeStruct((n, X_EXT), F32),
            jax.ShapeDtypeStruct((n, LANES), jnp.int32),
            jax.ShapeDtypeStruct((1, LANES), F32),
        ],
        scratch_shapes=[pltpu.VMEM((1, LANES), F32)],
        compiler_params=_params("arbitrary"),
        name="outproj_router",
    )(h, y, w_out, g, wr, br)


def _expert_kernel(ea_ref, eb_ref, nreal_ref, nused_ref, tok_ref, nxt_ref, x_hbm, wgu_a, wd_a, wgu_b, wd_b, y_hbm,
                   xbuf, obuf, sem_in, sem_out):
    i = pl.program_id(0)
    n_rows = x_hbm.shape[0]
    n_used = nused_ref[0]
    slot = lax.rem(i, 2)

    def gather_row(toks, k, s):
        src = jnp.minimum(toks[0, 0, k], n_rows - 1)
        return pltpu.make_async_copy(x_hbm.at[pl.ds(src, 1), :], xbuf.at[s].at[pl.ds(k, 1), :], sem_in.at[s])

    def start_gather(toks, s):
        for k in range(EXPERT_ROWS):
            gather_row(toks, k, s).start()

    def wait_gather(s):
        pltpu.make_async_copy(x_hbm.at[pl.ds(0, EXPERT_ROWS), :], xbuf.at[s], sem_in.at[s]).wait()

    def wait_scatter(s, rows):
        def wait_sized(size):
            def body(_, carry):
                pltpu.make_async_copy(obuf.at[s].at[pl.ds(0, size), :], y_hbm.at[pl.ds(0, size), :],
                                      sem_out.at[s]).wait()
                return carry
            return body
        lax.fori_loop(0, rows >> 3, wait_sized(SUBLANES), 0)
        lax.fori_loop(0, rows & (SUBLANES - 1), wait_sized(1), 0)

    @pl.when(i < n_used)
    def _():
        @pl.when(i == 0)
        def _():
            start_gather(tok_ref, 0)

        wait_gather(slot)
        start_gather(nxt_ref, 1 - slot)

        @pl.when(i >= 2)
        def _():
            wait_scatter(slot, nreal_ref[jnp.maximum(i - 2, 0)])

        x = xbuf[slot]
        xb = x[:, :D_MODEL].astype(BF16)
        ga = x[:, D_MODEL:D_MODEL + 1]
        gb = x[:, D_MODEL + 1:D_MODEL + 2]

        def mlp(wgu, wd):
            gu = jnp.dot(xb, wgu[0], preferred_element_type=F32)
            gate, up = gu[:, :D_EXPERT], gu[:, D_EXPERT:]
            return jnp.dot((gate * _sigmoid(gate) * up).astype(BF16), wd[0], preferred_element_type=F32)

        obuf[slot] = ga * mlp(wgu_a, wd_a) + gb * mlp(wgu_b, wd_b)

        def scatter_row(k, carry):
            pltpu.make_async_copy(obuf.at[slot].at[pl.ds(k, 1), :], y_hbm.at[pl.ds(tok_ref[0, 0, k], 1), :],
                                  sem_out.at[slot]).start()
            return carry
        lax.fori_loop(0, nreal_ref[i], scatter_row, 0)

        @pl.when(i == n_used - 1)
        def _():
            wait_gather(1 - slot)
            wait_scatter(slot, nreal_ref[i])

            @pl.when(i >= 1)
            def _():
                wait_scatter(1 - slot, nreal_ref[jnp.maximum(i - 1, 0)])


def _experts(blk_ea, blk_eb, n_real, n_used, slot_tok, xe, wgu, wd):
    n_blocks = blk_ea.shape[0]
    n = xe.shape[0]
    grid_spec = pltpu.PrefetchScalarGridSpec(
        num_scalar_prefetch=4,
        grid=(n_blocks,),
        in_specs=[
            pl.BlockSpec((1, 1, EXPERT_ROWS), lambda i, *_: (i, 0, 0), memory_space=pltpu.SMEM),
            pl.BlockSpec((1, 1, EXPERT_ROWS), lambda i, *_: (jnp.minimum(i + 1, n_blocks - 1), 0, 0),
                         memory_space=pltpu.SMEM),
            pl.BlockSpec(memory_space=pl.ANY),
            pl.BlockSpec((1, D_MODEL, 2 * D_EXPERT), lambda i, ea, eb, nr, nu: (ea[i], 0, 0)),
            pl.BlockSpec((1, D_EXPERT, D_MODEL), lambda i, ea, eb, nr, nu: (ea[i], 0, 0)),
            pl.BlockSpec((1, D_MODEL, 2 * D_EXPERT), lambda i, ea, eb, nr, nu: (eb[i], 0, 0)),
            pl.BlockSpec((1, D_EXPERT, D_MODEL), lambda i, ea, eb, nr, nu: (eb[i], 0, 0)),
        ],
        out_specs=pl.BlockSpec(memory_space=pl.ANY),
        scratch_shapes=[
            pltpu.VMEM((2, EXPERT_ROWS, X_EXT), F32),
            pltpu.VMEM((2, EXPERT_ROWS, D_MODEL), F32),
            pltpu.SemaphoreType.DMA((2,)),
            pltpu.SemaphoreType.DMA((2,)),
        ],
    )
    return pl.pallas_call(
        _expert_kernel,
        grid_spec=grid_spec,
        out_shape=jax.ShapeDtypeStruct((n, D_MODEL), F32),
        compiler_params=_params("arbitrary"),
        name="experts",
    )(blk_ea, blk_eb, n_real, n_used, slot_tok, slot_tok, xe, wgu, wd, wgu, wd)


def _pair_tables():
    ea, eb = [], []
    for g in range(N_GROUPS):
        for a in range(EXPERTS_PER_GROUP):
            for b in range(a + 1, EXPERTS_PER_GROUP):
                ea.append(g * EXPERTS_PER_GROUP + a)
                eb.append(g * EXPERTS_PER_GROUP + b)
    return np.asarray(ea, np.int32), np.asarray(eb, np.int32)


def _plan_blocks(route, counts, n):
    cls, rank = route[:, 0], route[:, 1]
    cnt = counts[0, :N_CLASSES].astype(jnp.int32)
    padded = (cnt + EXPERT_ROWS - 1) // EXPERT_ROWS * EXPERT_ROWS
    pad_end = jnp.cumsum(padded)
    slot = (pad_end - padded)[cls] + rank
    n_blocks = -(-(n + N_CLASSES * (EXPERT_ROWS - 1)) // EXPERT_ROWS)
    n_slots = n_blocks * EXPERT_ROWS
    slot_tok = jnp.full((n_slots,), n, jnp.int32).at[slot].set(jnp.arange(n, dtype=jnp.int32))
    blk_cls = jnp.minimum(jnp.searchsorted(pad_end, jnp.arange(n_blocks, dtype=jnp.int32) * EXPERT_ROWS,
                                           side='right'), N_CLASSES - 1)
    ea_tab, eb_tab = _pair_tables()
    n_used = (pad_end[-1] // EXPERT_ROWS).astype(jnp.int32).reshape(1)
    slot_tok = slot_tok.reshape(n_blocks, 1, EXPERT_ROWS)
    n_real = jnp.sum(slot_tok[:, 0, :] < n, axis=1).astype(jnp.int32)
    return jnp.asarray(ea_tab)[blk_cls], jnp.asarray(eb_tab)[blk_cls], n_real, n_used, slot_tok


def _final_kernel(h_ref, y_ref, g_ref, o_ref):
    out = _rms(h_ref[...] + y_ref[...], g_ref[...])
    o_ref[...] = out.reshape(o_ref.shape)


def _final(h, y, g, bn, n_chunks):
    rows = bn * CHUNK
    return pl.pallas_call(
        _final_kernel,
        grid=(n_chunks - 1,),
        in_specs=[
            pl.BlockSpec((rows, D_MODEL), lambda c: (c + 1, 0)),
            pl.BlockSpec((rows, D_MODEL), lambda c: (c + 1, 0)),
            pl.BlockSpec((1, D_MODEL), lambda c: (0, 0)),
        ],
        out_specs=pl.BlockSpec((bn, 1, CHUNK, D_MODEL), lambda c: (0, c, 0, 0)),
        out_shape=jax.ShapeDtypeStruct((bn, n_chunks - 1, CHUNK, D_MODEL), F32),
        compiler_params=_params("parallel"),
        name="final_norm",
    )(h, y, g)


def _repack_w_in(w):
    o = np.cumsum([0, 2 * M_WIDTH, M_WIDTH, M_WIDTH, M_HEADS, M_HEADS, G_K_WIDTH, G_K_WIDTH, G_WIDTH, G_WIDTH,
                   G_RANK])
    seg = lambda i: w[:, o[i]:o[i + 1]]
    small = jnp.concatenate([seg(3), seg(4), seg(9)], axis=1)
    small = jnp.pad(small, ((0, 0), (0, LANES - small.shape[1])))
    return jnp.concatenate([seg(0), seg(1), seg(2), seg(5), seg(6), seg(7), seg(8), small], axis=1).astype(BF16)


def kernel(x, meta_tokens, norm1_g, w_in, conv_w, conv_b, gate_b, m_norm_g, a_w2, a_b, g_norm_g, w_out, norm2_g,
           wr_g, br_g, wr_e, br_e, w_gate, w_up, w_down, final_norm_g):
    bn, seq, dm = x.shape
    depth = w_in.shape[0]
    assert dm == D_MODEL and seq % CHUNK == 0
    n_chunks = seq // CHUNK + 1
    rows = bn * CHUNK
    n = n_chunks * rows
    x4 = x.reshape(bn, seq // CHUNK, CHUNK, dm)
    row = lambda v: v.reshape(1, -1).astype(F32)

    h = y_moe = None
    for l in range(depth):
        w_in_l = _repack_w_in(w_in[l])
        if l == 0:
            h, z = _embed_inproj(x4, meta_tokens.astype(F32), row(norm1_g[l]), w_in_l, n_chunks)
        else:
            h, z = _add_inproj(h, y_moe, row(norm1_g[l]), w_in_l, rows)
        gb = gate_b[l].astype(F32).reshape(1, 2 * M_HEADS)
        y = _mixer(z, conv_w[l].astype(F32), row(conv_b[l]), jnp.pad(gb, ((0, 0), (0, LANES - 2 * M_HEADS))),
                   jnp.broadcast_to(gb.reshape(2 * M_HEADS, 1), (2 * M_HEADS, CHUNK)),
                   row(m_norm_g[l]), a_w2[l].astype(F32), row(a_b[l]), row(g_norm_g[l]), bn)
        wr = jnp.pad(jnp.concatenate([wr_g[l], wr_e[l]], axis=1).astype(F32),
                     ((0, 0), (0, LANES - N_GROUPS - N_EXPERTS)))
        br = jnp.pad(jnp.concatenate([br_g[l], br_e[l]]).astype(F32), (0, LANES - N_GROUPS - N_EXPERTS))
        h, xe, route, counts = _outproj_router(h, y, w_out[l].astype(BF16), row(norm2_g[l]), wr,
                                               br.reshape(1, LANES), rows)
        blk_ea, blk_eb, n_real, n_used, slot_tok = _plan_blocks(route, counts, n)
        wgu = jnp.concatenate([w_gate[l], w_up[l]], axis=-1).astype(BF16)
        y_moe = _experts(blk_ea, blk_eb, n_real, n_used, slot_tok, xe, wgu, w_down[l].astype(BF16))
    out = _final(h, y_moe, row(final_norm_g), bn, n_chunks)
    return out.reshape(bn, seq, dm)
```

```python
import functools

import numpy as np
import jax
import jax.numpy as jnp
from jax import lax
from jax.experimental import pallas as pl
from jax.experimental.pallas import tpu as pltpu

F32 = jnp.float32
BF16 = jnp.bfloat16
HIGHEST = lax.Precision.HIGHEST

D_MODEL = 1024
CHUNK = 64
N_META = 16
LEAD_PAD = CHUNK - N_META
EPS = 1e-6
M_HEADS = 4
M_WIDTH = 512
M_HEAD_DIM = 128
CONV_K = 4
G_HEADS = 4
G_WIDTH = 512
G_V_DIM = 128
G_K_DIM = 64
G_K_WIDTH = 256
G_RANK = 16
G_TAU = 16.0
N_GROUPS = 4
EXPERTS_PER_GROUP = 8
N_EXPERTS = 32
D_EXPERT = 256
N_PAIRS = EXPERTS_PER_GROUP * (EXPERTS_PER_GROUP - 1) // 2
N_CLASSES = N_GROUPS * N_PAIRS

LANES = 128
SUBLANES = 8
EXPERT_ROWS = 128
VMEM_LIMIT = 56 * 1024 * 1024

Z_QM, Z_KM, Z_VM, Z_OM = 0, 512, 1024, 1536
Z_QG, Z_KG, Z_VG, Z_GG = 2048, 2304, 2560, 3072
Z_SMALL = 3584
Z_WIDTH = Z_SMALL + LANES
X_EXT = D_MODEL + LANES
GLA_SAFE_EXP = 80.0


def _log_sigmoid(x):
    return jnp.minimum(x, 0.0) - jnp.log(1.0 + jnp.exp(-jnp.abs(x)))


def _sigmoid(x):
    return 1.0 / (1.0 + jnp.exp(-x))


def _rms(x, g):
    return x * lax.rsqrt(jnp.mean(x * x, axis=-1, keepdims=True) + EPS) * g


def _dot(a, b):
    return jnp.dot(a.astype(BF16), b.astype(BF16), preferred_element_type=F32)


def _dot_nt(a, b):
    return lax.dot_general(a.astype(BF16), b.astype(BF16), (((1,), (1,)), ((), ())),
                           preferred_element_type=F32)


def _dot_f32(a, b):
    return jnp.dot(a, b, precision=HIGHEST, preferred_element_type=F32)


def _params(*semantics):
    return pltpu.CompilerParams(dimension_semantics=semantics, vmem_limit_bytes=VMEM_LIMIT)


def _embed_inproj_kernel(x_ref, meta_ref, g_ref, w_ref, h_ref, z_ref):
    c = pl.program_id(0)
    bn = x_ref.shape[0]
    rows = bn * CHUNK

    @pl.when(c == 0)
    def _():
        lead = jnp.concatenate([jnp.zeros((LEAD_PAD, D_MODEL), F32), meta_ref[...]], axis=0)
        for b in range(bn):
            h_ref[b * CHUNK:(b + 1) * CHUNK, :] = lead

    @pl.when(c > 0)
    def _():
        h_ref[...] = x_ref[...].reshape(rows, D_MODEL)

    z_ref[...] = _dot(_rms(h_ref[...], g_ref[...]), w_ref[...])


def _embed_inproj(x4, meta, g, w, n_chunks):
    bn = x4.shape[0]
    rows = bn * CHUNK
    n = n_chunks * rows
    return pl.pallas_call(
        _embed_inproj_kernel,
        grid=(n_chunks,),
        in_specs=[
            pl.BlockSpec((bn, 1, CHUNK, D_MODEL), lambda c: (0, jnp.maximum(c - 1, 0), 0, 0)),
            pl.BlockSpec((N_META, D_MODEL), lambda c: (0, 0)),
            pl.BlockSpec((1, D_MODEL), lambda c: (0, 0)),
            pl.BlockSpec((D_MODEL, Z_WIDTH), lambda c: (0, 0)),
        ],
        out_specs=[
            pl.BlockSpec((rows, D_MODEL), lambda c: (c, 0)),
            pl.BlockSpec((rows, Z_WIDTH), lambda c: (c, 0)),
        ],
        out_shape=[jax.ShapeDtypeStruct((n, D_MODEL), F32), jax.ShapeDtypeStruct((n, Z_WIDTH), F32)],
        compiler_params=_params("parallel"),
        name="embed_inproj",
    )(x4, meta, g, w)


def _moe_rows(i, n_steps, off_ref, cur, nxt, ys_hbm, ybuf, sem):
    rows = cur[0].shape[-1]
    tile = rows * SUBLANES
    s = lax.rem(i, 2)

    def start(idx, slot):
        cls_ref, rank_ref = idx
        for k in range(rows):
            src = pl.multiple_of((off_ref[cls_ref[0, 0, k]] + rank_ref[0, 0, k]) * SUBLANES, SUBLANES)
            dst = pl.multiple_of(slot * tile + k * SUBLANES, SUBLANES)
            pltpu.make_async_copy(ys_hbm.at[pl.ds(src, SUBLANES), :], ybuf.at[pl.ds(dst, SUBLANES), :],
                                  sem.at[slot]).start()

    def wait(slot):
        pltpu.make_async_copy(ys_hbm.at[pl.ds(0, tile), :], ybuf.at[pl.ds(pl.multiple_of(slot * tile, tile), tile), :],
                              sem.at[slot]).wait()

    @pl.when(i == 0)
    def _():
        start(cur, 0)

    wait(s)
    start(nxt, 1 - s)

    @pl.when(i == n_steps - 1)
    def _():
        wait(1 - s)

    base = s * tile
    return jnp.concatenate([ybuf[pl.ds(base + sg, rows, stride=SUBLANES), :] for sg in range(SUBLANES)], axis=1)


def _add_inproj_kernel(off_ref, cls_ref, rank_ref, ncls_ref, nrank_ref, h_ref, ys_hbm, g_ref, w_ref, ho_ref, z_ref,
                       ybuf, sem):
    y = _moe_rows(pl.program_id(0), pl.num_programs(0), off_ref, (cls_ref, rank_ref), (ncls_ref, nrank_ref),
                  ys_hbm, ybuf, sem)
    h = h_ref[...] + y
    ho_ref[...] = h
    z_ref[...] = _dot(_rms(h, g_ref[...]), w_ref[...])


def _add_inproj(off, cls, rank, h, ys, g, w):
    n = h.shape[0]
    n_steps, _, rows = cls.shape
    smem = lambda m: pl.BlockSpec((1, 1, rows), m, memory_space=pltpu.SMEM)
    cur = lambda i, off: (i, 0, 0)
    nxt = lambda i, off: (jnp.minimum(i + 1, n_steps - 1), 0, 0)
    grid_spec = pltpu.PrefetchScalarGridSpec(
        num_scalar_prefetch=1,
        grid=(n_steps,),
        in_specs=[
            smem(cur), smem(cur), smem(nxt), smem(nxt),
            pl.BlockSpec((rows, D_MODEL), lambda i, off: (i, 0)),
            pl.BlockSpec(memory_space=pl.ANY),
            pl.BlockSpec((1, D_MODEL), lambda i, off: (0, 0)),
            pl.BlockSpec((D_MODEL, Z_WIDTH), lambda i, off: (0, 0)),
        ],
        out_specs=[
            pl.BlockSpec((rows, D_MODEL), lambda i, off: (i, 0)),
            pl.BlockSpec((rows, Z_WIDTH), lambda i, off: (i, 0)),
        ],
        scratch_shapes=[pltpu.VMEM((2 * rows * SUBLANES, LANES), F32), pltpu.SemaphoreType.DMA((2,))],
    )
    return pl.pallas_call(
        _add_inproj_kernel,
        grid_spec=grid_spec,
        out_shape=[jax.ShapeDtypeStruct((n, D_MODEL), F32), jax.ShapeDtypeStruct((n, Z_WIDTH), F32)],
        compiler_params=_params("arbitrary"),
        name="add_inproj",
    )(off, cls, rank, cls, rank, h, ys, g, w)


def _gla_intra_pairwise(q, k, bg, kbuf, bbuf):
    kbuf[...] = k
    bbuf[...] = bg
    lane = lax.broadcasted_iota(jnp.int32, (CHUNK, CHUNK), 1)

    def column(s, a):
        ks = kbuf[pl.ds(s, 1), :]
        bs = bbuf[pl.ds(s, 1), :]
        col = jnp.sum(q * ks * jnp.exp(jnp.minimum(bg - bs, 0.0)), axis=-1, keepdims=True)
        return jnp.where(lane == s, col, a)

    return lax.fori_loop(0, CHUNK, column, jnp.zeros((CHUNK, CHUNK), F32))


def _gla_streams(z_ref, rows, valid_c, tri, aw_ref, ab_ref):
    code = z_ref[rows, Z_SMALL:Z_SMALL + LANES][:, 2 * M_HEADS:2 * M_HEADS + G_RANK]
    u = _dot_f32(code, aw_ref[...]) + ab_ref[...]
    la = jnp.where(valid_c, _log_sigmoid(u) * (1.0 / G_TAU), 0.0)
    bg = _dot_f32(tri, la)
    q_g = jnp.where(valid_c, z_ref[rows, Z_QG:Z_QG + G_K_WIDTH], 0.0) * (G_K_DIM ** -0.5)
    k_g = jnp.where(valid_c, z_ref[rows, Z_KG:Z_KG + G_K_WIDTH], 0.0)
    v_g = jnp.where(valid_c, z_ref[rows, Z_VG:Z_VG + G_WIDTH], 0.0)
    return q_g, k_g, v_g, bg


def _gla_gated(o, gate, g):
    return _rms(o, g) * (gate * _sigmoid(gate))


def _mixer_kernel(z_ref, cw_ref, cb_ref, gbr_ref, gbc_ref, mg_ref, aw_ref, ab_ref, gg_ref, y_ref,
                  tail_ref, c_ref, m_ref, s_ref, oi_ref, kbuf, bbuf):
    c = pl.program_id(0)
    bn = z_ref.shape[0] // CHUNK

    @pl.when(c == 0)
    def _():
        tail_ref[...] = jnp.zeros_like(tail_ref)
        c_ref[...] = jnp.zeros_like(c_ref)
        m_ref[...] = jnp.zeros_like(m_ref)
        s_ref[...] = jnp.zeros_like(s_ref)

    row = lax.broadcasted_iota(jnp.int32, (CHUNK, CHUNK), 0)
    col = lax.broadcasted_iota(jnp.int32, (CHUNK, CHUNK), 1)
    causal = col <= row
    tri = jnp.where(causal, 1.0, 0.0).astype(F32)
    tri_t = jnp.where(row <= col, 1.0, 0.0).astype(F32)
    first = c == 0
    valid_c = jnp.logical_or(jnp.logical_not(first),
                             lax.broadcasted_iota(jnp.int32, (CHUNK, 1), 0) >= LEAD_PAD)
    valid_r = jnp.logical_or(jnp.logical_not(first),
                             lax.broadcasted_iota(jnp.int32, (1, CHUNK), 1) >= LEAD_PAD)
    row8 = lax.broadcasted_iota(jnp.int32, (SUBLANES, 1), 0)
    ones_col = jnp.where(lax.broadcasted_iota(jnp.int32, (CHUNK, LANES), 1) == 0, 1.0, 0.0).astype(F32)
    neg_inf = -jnp.inf

    def mlstm_batch(b):
        rows = slice(b * CHUNK, (b + 1) * CHUNK)
        x = jnp.where(valid_c, z_ref[rows, Z_QM:Z_QM + 2 * M_WIDTH], 0.0)
        prev = tail_ref[b]
        acc = x * cw_ref[CONV_K - 1:CONV_K, :] + cb_ref[...]
        for k in range(1, CONV_K):
            cur = pltpu.roll(x, k, 0)
            fix = pltpu.roll(prev, k, 0)
            top = jnp.where(row8 < k, fix, cur[0:SUBLANES])
            shifted = jnp.concatenate([top, cur[SUBLANES:]], axis=0)
            acc = acc + shifted * cw_ref[CONV_K - 1 - k:CONV_K - k, :]
        tail_ref[b] = x[CHUNK - SUBLANES:]
        qk = acc * _sigmoid(acc)
        qk = jnp.where(valid_c, qk, 0.0)
        q_m = qk[:, :M_WIDTH]
        k_m = qk[:, M_WIDTH:] * (M_HEAD_DIM ** -0.5)
        v_m = jnp.where(valid_c, z_ref[rows, Z_VM:Z_VM + M_WIDTH], 0.0)
        o_m = z_ref[rows, Z_OM:Z_OM + M_WIDTH]

        zs = z_ref[rows, Z_SMALL:Z_SMALL + LANES]
        g_col = zs + gbr_ref[...]
        g_row = zs.T[0:2 * M_HEADS, :] + gbc_ref[...]
        b_col = _dot_f32(tri, jnp.where(valid_c, _log_sigmoid(g_col), 0.0))
        b_row = _dot_f32(jnp.where(valid_r, _log_sigmoid(g_row), 0.0), tri_t)
        ig_col = jnp.where(valid_c, g_col, neg_inf)
        ig_row = jnp.where(valid_r, g_row, neg_inf)

        for hd in range(M_HEADS):
            idx = b * M_HEADS + hd
            lanes = slice(hd * M_HEAD_DIM, (hd + 1) * M_HEAD_DIM)
            q, k, v = q_m[:, lanes], k_m[:, lanes], v_m[:, lanes]
            bc, br = b_col[:, M_HEADS + hd:M_HEADS + hd + 1], b_row[M_HEADS + hd:M_HEADS + hd + 1, :]
            igc, igr = ig_col[:, hd:hd + 1], ig_row[hd:hd + 1, :]
            m_prev = m_ref[idx][0:1, 0:1]
            cst = c_ref[idx]
            dmat = jnp.where(causal, bc - br + igr, neg_inf)
            m_inter = bc + m_prev
            m_t = jnp.maximum(m_inter, jnp.max(dmat, axis=-1, keepdims=True))
            s = _dot_nt(q, k) * jnp.exp(dmat - m_t)
            w_inter = jnp.exp(m_inter - m_t)
            v_aug = jnp.concatenate([v, ones_col], axis=1)
            nd = _dot(s, v_aug) + w_inter * _dot(q, cst)
            num, den = nd[:, :M_HEAD_DIM], nd[:, M_HEAD_DIM:M_HEAD_DIM + 1]
            hh = num / jnp.maximum(jnp.abs(den), jnp.exp(-m_t))
            b_end = bc[CHUNK - 1:CHUNK, :]
            m_new = jnp.maximum(b_end + m_prev, jnp.max(b_end - br + igr, axis=-1, keepdims=True))
            wk = jnp.exp(b_end - bc + igc - m_new)
            decay = jnp.exp(b_end + m_prev - m_new)
            c_ref[idx] = decay * cst + _dot((k * wk).T, v_aug)
            m_ref[idx] = jnp.broadcast_to(m_new, (SUBLANES, LANES))
            y = _rms(hh, mg_ref[:, lanes]) * _sigmoid(o_m[:, lanes])
            y_ref[rows, lanes] = y.astype(y_ref.dtype)

    def gla_batch(b):
        rows = slice(b * CHUNK, (b + 1) * CHUNK)
        q_g, k_g, v_g, bg = _gla_streams(z_ref, rows, valid_c, tri, aw_ref, ab_ref)
        gate_g = z_ref[rows, Z_GG:Z_GG + G_WIDTH]
        bg_t = bg.T
        b_end = bg[CHUNK - 1:CHUNK, :]
        q_dec = q_g * jnp.exp(bg)
        k_inv = k_g * jnp.exp(-bg)
        k_end = k_g * jnp.exp(b_end - bg)
        for hd in range(G_HEADS):
            idx = b * G_HEADS + hd
            kl = slice(hd * G_K_DIM, (hd + 1) * G_K_DIM)
            vl = slice(hd * G_V_DIM, (hd + 1) * G_V_DIM)
            yl = slice(M_WIDTH + hd * G_V_DIM, M_WIDTH + (hd + 1) * G_V_DIM)
            st = s_ref[idx]
            a = jnp.where(causal, _dot_nt(q_dec[:, kl], k_inv[:, kl]), 0.0)
            o_inter = _dot(q_dec[:, kl], st)
            oi_ref[rows, vl] = o_inter
            e_end = jnp.exp(bg_t[hd * G_K_DIM:(hd + 1) * G_K_DIM, CHUNK - 1:CHUNK])
            s_ref[idx] = e_end * st + _dot(k_end[:, kl].T, v_g[:, vl])
            y = _gla_gated(_dot(a, v_g[:, vl]) + o_inter, gate_g[:, vl], gg_ref[:, vl])
            y_ref[rows, yl] = y.astype(y_ref.dtype)
        return jnp.max(-b_end) > GLA_SAFE_EXP

    def gla_redo_batch(b, carry):
        rows = pl.ds(pl.multiple_of(b * CHUNK, CHUNK), CHUNK)
        q_g, k_g, v_g, bg = _gla_streams(z_ref, rows, valid_c, tri, aw_ref, ab_ref)
        gate_g = z_ref[rows, Z_GG:Z_GG + G_WIDTH]
        for hd in range(G_HEADS):
            kl = slice(hd * G_K_DIM, (hd + 1) * G_K_DIM)
            vl = slice(hd * G_V_DIM, (hd + 1) * G_V_DIM)
            yl = slice(M_WIDTH + hd * G_V_DIM, M_WIDTH + (hd + 1) * G_V_DIM)
            a = jnp.where(causal, _gla_intra_pairwise(q_g[:, kl], k_g[:, kl], bg[:, kl], kbuf, bbuf), 0.0)
            y = _gla_gated(_dot(a, v_g[:, vl]) + oi_ref[rows, vl], gate_g[:, vl], gg_ref[:, vl])
            y_ref[rows, yl] = y.astype(y_ref.dtype)
        return carry

    unsafe = False
    for b in range(bn):
        mlstm_batch(b)
        unsafe = jnp.logical_or(unsafe, gla_batch(b))

    @pl.when(unsafe)
    def _():
        lax.fori_loop(0, bn, gla_redo_batch, 0)


def _mixer(z, conv_w, conv_b, gb_row, gb_col, m_norm_g, a_w2, a_b, g_norm_g, bn):
    rows = bn * CHUNK
    n = z.shape[0]
    full = lambda shape: pl.BlockSpec(shape, lambda c: (0,) * len(shape))
    return pl.pallas_call(
        _mixer_kernel,
        grid=(n // rows,),
        in_specs=[
            pl.BlockSpec((rows, Z_WIDTH), lambda c: (c, 0)),
            full((CONV_K, 2 * M_WIDTH)), full((1, 2 * M_WIDTH)),
            full((1, LANES)), full((2 * M_HEADS, CHUNK)),
            full((1, M_WIDTH)), full((G_RANK, G_K_WIDTH)), full((1, G_K_WIDTH)), full((1, G_WIDTH)),
        ],
        out_specs=pl.BlockSpec((rows, D_MODEL), lambda c: (c, 0)),
        out_shape=jax.ShapeDtypeStruct((n, D_MODEL), BF16),
        scratch_shapes=[
            pltpu.VMEM((bn, SUBLANES, 2 * M_WIDTH), F32),
            pltpu.VMEM((bn * M_HEADS, M_HEAD_DIM, 2 * M_HEAD_DIM), F32),
            pltpu.VMEM((bn * M_HEADS, SUBLANES, LANES), F32),
            pltpu.VMEM((bn * G_HEADS, G_K_DIM, G_V_DIM), F32),
            pltpu.VMEM((rows, G_WIDTH), F32),
            pltpu.VMEM((CHUNK, G_K_DIM), F32), pltpu.VMEM((CHUNK, G_K_DIM), F32),
        ],
        compiler_params=_params("arbitrary"),
        name="mixer",
    )(z, conv_w, conv_b, gb_row, gb_col, m_norm_g, a_w2, a_b, g_norm_g)


def _outproj_router_kernel(h_ref, y_ref, wo_ref, g_ref, wr_ref, br_ref, h1_ref, xe_ref, route_ref, cnt_ref,
                           carry_ref):
    i = pl.program_id(0)
    rows = h_ref.shape[0]

    @pl.when(i == 0)
    def _():
        carry_ref[...] = jnp.zeros_like(carry_ref)

    h1 = h_ref[...] + jnp.dot(y_ref[...], wo_ref[...], preferred_element_type=F32)
    h1_ref[...] = h1
    xn = _rms(h1, g_ref[...])
    logits = _dot_f32(xn, wr_ref[...]) + br_ref[...]
    lane = lax.broadcasted_iota(jnp.int32, (rows, LANES), 1)
    neg_inf = -jnp.inf
    big = jnp.int32(LANES)

    def first_argmax(vals):
        top = jnp.max(vals, axis=-1, keepdims=True)
        return top, jnp.min(jnp.where(vals == top, lane, big), axis=-1, keepdims=True)

    lg = jnp.where(lane < N_GROUPS, logits, neg_inf)
    g_max, g_sel = first_argmax(lg)
    p_sel = 1.0 / jnp.sum(jnp.exp(lg - g_max), axis=-1, keepdims=True)
    in_group = jnp.logical_and(lane >= N_GROUPS + EXPERTS_PER_GROUP * g_sel,
                               lane < N_GROUPS + EXPERTS_PER_GROUP * (g_sel + 1))
    le = jnp.where(in_group, logits, neg_inf)
    v1, i1 = first_argmax(le)
    v2, i2 = first_argmax(jnp.where(lane == i1, neg_inf, le))
    e21 = jnp.exp(v2 - v1)
    gate1 = p_sel / (1.0 + e21)
    gate2 = p_sel * e21 / (1.0 + e21)
    j1 = i1 - N_GROUPS - EXPERTS_PER_GROUP * g_sel
    j2 = i2 - N_GROUPS - EXPERTS_PER_GROUP * g_sel
    ja, jb = jnp.minimum(j1, j2), jnp.maximum(j1, j2)
    ga = jnp.where(j1 < j2, gate1, gate2)
    gb = jnp.where(j1 < j2, gate2, gate1)
    pair = ((ja * (2 * EXPERTS_PER_GROUP - 1 - ja)) >> 1) + (jb - ja - 1)
    cls = g_sel * N_PAIRS + pair

    bits = lax.bitcast_convert_type(xn.astype(jnp.bfloat16).astype(F32), jnp.uint32)
    half = D_MODEL // 2
    packed = bits[:, half:] | (bits[:, :half] >> 16)
    gates = lax.bitcast_convert_type(jnp.where(lane == 0, ga, jnp.where(lane == 1, gb, 0.0)), jnp.uint32)
    for sg in range(SUBLANES):
        if sg < half // LANES:
            val = packed[:, sg * LANES:(sg + 1) * LANES]
        else:
            val = gates if sg == half // LANES else jnp.zeros((rows, LANES), jnp.uint32)
        xe_ref[pl.ds(sg, rows, stride=SUBLANES), :] = val

    onehot = jnp.where(lane == cls, 1.0, 0.0).astype(F32)
    r_i = lax.broadcasted_iota(jnp.int32, (rows, rows), 0)
    c_i = lax.broadcasted_iota(jnp.int32, (rows, rows), 1)
    incl = jnp.where(c_i <= r_i, 1.0, 0.0).astype(BF16)
    prefix = jnp.dot(incl, onehot.astype(BF16), preferred_element_type=F32)
    rank = jnp.sum(onehot * (prefix - 1.0 + carry_ref[...]), axis=-1, keepdims=True)
    carry_ref[...] = carry_ref[...] + prefix[rows - 1:rows, :]
    cnt_ref[...] = carry_ref[...]
    route_ref[...] = jnp.where(lane == 0, cls, jnp.where(lane == 1, rank.astype(jnp.int32), 0))


def _outproj_router(h, y, w_out, g, wr, br, rows):
    n = h.shape[0]
    return pl.pallas_call(
        _outproj_router_kernel,
        grid=(n // rows,),
        in_specs=[
            pl.BlockSpec((rows, D_MODEL), lambda i: (i, 0)),
            pl.BlockSpec((rows, D_MODEL), lambda i: (i, 0)),
            pl.BlockSpec((D_MODEL, D_MODEL), lambda i: (0, 0)),
            pl.BlockSpec((1, D_MODEL), lambda i: (0, 0)),
            pl.BlockSpec((D_MODEL, LANES), lambda i: (0, 0)),
            pl.BlockSpec((1, LANES), lambda i: (0, 0)),
        ],
        out_specs=[
            pl.BlockSpec((rows, D_MODEL), lambda i: (i, 0)),
            pl.BlockSpec((rows * SUBLANES, LANES), lambda i: (i, 0)),
            pl.BlockSpec((rows, LANES), lambda i: (i, 0)),
            pl.BlockSpec((1, LANES), lambda i: (0, 0)),
        ],
        out_shape=[
            jax.ShapeDtypeStruct((n, D_MODEL), F32),
            jax.ShapeDtypeStruct((n * SUBLANES, LANES), jnp.uint32),
            jax.ShapeDtypeStruct((n, LANES), jnp.int32),
            jax.ShapeDtypeStruct((1, LANES), F32),
        ],
        scratch_shapes=[pltpu.VMEM((1, LANES), F32)],
        compiler_params=_params("arbitrary"),
        name="outproj_router",
    )(h, y, w_out, g, wr, br)


def _dispatch_kernel(off_ref, cls_ref, rank_ref, x_hbm, xs_hbm, sem):
    i = pl.program_id(0)
    rows = cls_ref.shape[-1]
    s = lax.rem(i, 2)

    def wait(slot):
        pltpu.make_async_copy(x_hbm.at[pl.ds(0, rows)], xs_hbm.at[pl.ds(0, rows)], sem.at[slot]).wait()

    for k in range(rows):
        dst = off_ref[cls_ref[0, 0, k]] + rank_ref[0, 0, k]
        pltpu.make_async_copy(x_hbm.at[i * rows + k], xs_hbm.at[dst], sem.at[s]).start()

    @pl.when(i > 0)
    def _():
        wait(1 - s)

    @pl.when(i == pl.num_programs(0) - 1)
    def _():
        wait(s)


def _dispatch(off, cls, rank, xe, n_slots):
    n_steps, _, rows = cls.shape
    smem = pl.BlockSpec((1, 1, rows), lambda i, off: (i, 0, 0), memory_space=pltpu.SMEM)
    grid_spec = pltpu.PrefetchScalarGridSpec(
        num_scalar_prefetch=1,
        grid=(n_steps,),
        in_specs=[smem, smem, pl.BlockSpec(memory_space=pl.ANY)],
        out_specs=pl.BlockSpec(memory_space=pl.ANY),
        scratch_shapes=[pltpu.SemaphoreType.DMA((2,))],
    )
    return pl.pallas_call(
        _dispatch_kernel,
        grid_spec=grid_spec,
        out_shape=jax.ShapeDtypeStruct((n_slots, SUBLANES, LANES), xe.dtype),
        compiler_params=_params("arbitrary"),
        name="dispatch",
    )(off, cls, rank, xe)


def _expert_kernel(ea_ref, eb_ref, nreal_ref, nused_ref, xs_hbm, wgu_a, wd_a, wgu_b, wd_b, ys_hbm,
                   xbuf, obuf, sem_in, sem_out):
    i = pl.program_id(0)
    n_used = nused_ref[0]
    slot = lax.rem(i, 2)
    tile = EXPERT_ROWS * SUBLANES

    def in_copy(blk, s):
        size = pl.multiple_of(nreal_ref[blk] * SUBLANES, SUBLANES)
        return pltpu.make_async_copy(xs_hbm.at[pl.ds(pl.multiple_of(blk * tile, tile), size), :],
                                     xbuf.at[pl.ds(pl.multiple_of(s * tile, tile), size), :], sem_in.at[s])

    def out_copy(blk, s):
        size = pl.multiple_of(nreal_ref[blk] * SUBLANES, SUBLANES)
        return pltpu.make_async_copy(obuf.at[pl.ds(pl.multiple_of(s * tile, tile), size), :],
                                     ys_hbm.at[pl.ds(pl.multiple_of(blk * tile, tile), size), :], sem_out.at[s])

    @pl.when(i < n_used)
    def _():
        @pl.when(i == 0)
        def _():
            xbuf[...] = jnp.zeros_like(xbuf)
            in_copy(0, 0).start()

        in_copy(i, slot).wait()

        @pl.when(i + 1 < n_used)
        def _():
            in_copy(i + 1, 1 - slot).start()

        @pl.when(i >= 2)
        def _():
            out_copy(jnp.maximum(i - 2, 0), slot).wait()

        base = slot * tile
        seg = lambda sg: xbuf[pl.ds(base + sg, EXPERT_ROWS, stride=SUBLANES), :]
        words = [seg(sg) for sg in range(D_MODEL // 2 // LANES)]
        as_f32 = lambda w: lax.bitcast_convert_type(w, F32)
        xb = jnp.concatenate([as_f32(w << 16) for w in words] + [as_f32(w & jnp.uint32(0xFFFF0000)) for w in words],
                             axis=1).astype(BF16)
        gates = as_f32(seg(D_MODEL // 2 // LANES))
        ga, gb = gates[:, 0:1], gates[:, 1:2]

        def mlp(wgu, wd):
            gu = jnp.dot(xb, wgu[0], preferred_element_type=F32)
            gate, up = gu[:, :D_EXPERT], gu[:, D_EXPERT:]
            return jnp.dot((gate * _sigmoid(gate) * up).astype(BF16), wd[0], preferred_element_type=F32)

        y = ga * mlp(wgu_a, wd_a) + gb * mlp(wgu_b, wd_b)
        for sg in range(SUBLANES):
            obuf[pl.ds(base + sg, EXPERT_ROWS, stride=SUBLANES), :] = y[:, sg * LANES:(sg + 1) * LANES]
        out_copy(i, slot).start()

        @pl.when(i == n_used - 1)
        def _():
            out_copy(i, slot).wait()

            @pl.when(i >= 1)
            def _():
                out_copy(jnp.maximum(i - 1, 0), 1 - slot).wait()


def _experts(blk_ea, blk_eb, n_real, n_used, xs, wgu, wd):
    n_blocks = blk_ea.shape[0]
    tile = EXPERT_ROWS * SUBLANES
    grid_spec = pltpu.PrefetchScalarGridSpec(
        num_scalar_prefetch=4,
        grid=(n_blocks,),
        in_specs=[
            pl.BlockSpec(memory_space=pl.ANY),
            pl.BlockSpec((1, D_MODEL, 2 * D_EXPERT), lambda i, ea, eb, nr, nu: (ea[i], 0, 0)),
            pl.BlockSpec((1, D_EXPERT, D_MODEL), lambda i, ea, eb, nr, nu: (ea[i], 0, 0)),
            pl.BlockSpec((1, D_MODEL, 2 * D_EXPERT), lambda i, ea, eb, nr, nu: (eb[i], 0, 0)),
            pl.BlockSpec((1, D_EXPERT, D_MODEL), lambda i, ea, eb, nr, nu: (eb[i], 0, 0)),
        ],
        out_specs=pl.BlockSpec(memory_space=pl.ANY),
        scratch_shapes=[
            pltpu.VMEM((2 * tile, LANES), jnp.uint32),
            pltpu.VMEM((2 * tile, LANES), F32),
            pltpu.SemaphoreType.DMA((2,)),
            pltpu.SemaphoreType.DMA((2,)),
        ],
    )
    return pl.pallas_call(
        _expert_kernel,
        grid_spec=grid_spec,
        out_shape=jax.ShapeDtypeStruct((n_blocks * tile, LANES), F32),
        compiler_params=_params("arbitrary"),
        name="experts",
    )(blk_ea, blk_eb, n_real, n_used, xs, wgu, wd, wgu, wd)


def _pair_tables():
    ea, eb = [], []
    for g in range(N_GROUPS):
        for a in range(EXPERTS_PER_GROUP):
            for b in range(a + 1, EXPERTS_PER_GROUP):
                ea.append(g * EXPERTS_PER_GROUP + a)
                eb.append(g * EXPERTS_PER_GROUP + b)
    return np.asarray(ea, np.int32), np.asarray(eb, np.int32)


def _plan_blocks(counts, n):
    cnt = counts[0, :N_CLASSES].astype(jnp.int32)
    padded = (cnt + EXPERT_ROWS - 1) // EXPERT_ROWS * EXPERT_ROWS
    pad_end = jnp.cumsum(padded)
    off = pad_end - padded
    n_blocks = -(-(n + N_CLASSES * (EXPERT_ROWS - 1)) // EXPERT_ROWS)
    blk_start = jnp.arange(n_blocks, dtype=jnp.int32) * EXPERT_ROWS
    blk_cls = jnp.minimum(jnp.searchsorted(pad_end, blk_start, side='right'), N_CLASSES - 1)
    n_real = jnp.clip(cnt[blk_cls] - (blk_start - off[blk_cls]), 0, EXPERT_ROWS).astype(jnp.int32)
    ea_tab, eb_tab = _pair_tables()
    n_used = (pad_end[-1] // EXPERT_ROWS).astype(jnp.int32).reshape(1)
    off = jnp.pad(off, (0, LANES - N_CLASSES)).astype(jnp.int32)
    return jnp.asarray(ea_tab)[blk_cls], jnp.asarray(eb_tab)[blk_cls], n_real, n_used, off, n_blocks


def _final_kernel(off_ref, cls_ref, rank_ref, ncls_ref, nrank_ref, h_ref, ys_hbm, g_ref, o_ref, ybuf, sem):
    y = _moe_rows(pl.program_id(0), pl.num_programs(0), off_ref, (cls_ref, rank_ref), (ncls_ref, nrank_ref),
                  ys_hbm, ybuf, sem)
    out = _rms(h_ref[...] + y, g_ref[...])
    o_ref[...] = out.reshape(o_ref.shape)


def _final(off, cls, rank, h, ys, g, bn):
    n_chunks, _, rows = cls.shape
    smem = lambda m: pl.BlockSpec((1, 1, rows), m, memory_space=pltpu.SMEM)
    cur = lambda c, off: (c + 1, 0, 0)
    nxt = lambda c, off: (jnp.minimum(c + 2, n_chunks - 1), 0, 0)
    grid_spec = pltpu.PrefetchScalarGridSpec(
        num_scalar_prefetch=1,
        grid=(n_chunks - 1,),
        in_specs=[
            smem(cur), smem(cur), smem(nxt), smem(nxt),
            pl.BlockSpec((rows, D_MODEL), lambda c, off: (c + 1, 0)),
            pl.BlockSpec(memory_space=pl.ANY),
            pl.BlockSpec((1, D_MODEL), lambda c, off: (0, 0)),
        ],
        out_specs=pl.BlockSpec((bn, 1, CHUNK, D_MODEL), lambda c, off: (0, c, 0, 0)),
        scratch_shapes=[pltpu.VMEM((2 * rows * SUBLANES, LANES), F32), pltpu.SemaphoreType.DMA((2,))],
    )
    return pl.pallas_call(
        _final_kernel,
        grid_spec=grid_spec,
        out_shape=jax.ShapeDtypeStruct((bn, n_chunks - 1, CHUNK, D_MODEL), F32),
        compiler_params=_params("arbitrary"),
        name="final_norm",
    )(off, cls, rank, cls, rank, h, ys, g)


def _repack_w_in(w):
    o = np.cumsum([0, 2 * M_WIDTH, M_WIDTH, M_WIDTH, M_HEADS, M_HEADS, G_K_WIDTH, G_K_WIDTH, G_WIDTH, G_WIDTH,
                   G_RANK])
    seg = lambda i: w[:, o[i]:o[i + 1]]
    small = jnp.concatenate([seg(3), seg(4), seg(9)], axis=1)
    small = jnp.pad(small, ((0, 0), (0, LANES - small.shape[1])))
    return jnp.concatenate([seg(0), seg(1), seg(2), seg(5), seg(6), seg(7), seg(8), small], axis=1).astype(BF16)


def kernel(x, meta_tokens, norm1_g, w_in, conv_w, conv_b, gate_b, m_norm_g, a_w2, a_b, g_norm_g, w_out, norm2_g,
           wr_g, br_g, wr_e, br_e, w_gate, w_up, w_down, final_norm_g):
    bn, seq, dm = x.shape
    depth = w_in.shape[0]
    assert dm == D_MODEL and seq % CHUNK == 0
    n_chunks = seq // CHUNK + 1
    rows = bn * CHUNK
    n = n_chunks * rows
    x4 = x.reshape(bn, seq // CHUNK, CHUNK, dm)
    row = lambda v: v.reshape(1, -1).astype(F32)

    h = ys = off = cls = rank = None
    for l in range(depth):
        w_in_l = _repack_w_in(w_in[l])
        if l == 0:
            h, z = _embed_inproj(x4, meta_tokens.astype(F32), row(norm1_g[l]), w_in_l, n_chunks)
        else:
            h, z = _add_inproj(off, cls, rank, h, ys, row(norm1_g[l]), w_in_l)
        gb = gate_b[l].astype(F32).reshape(1, 2 * M_HEADS)
        y = _mixer(z, conv_w[l].astype(F32), row(conv_b[l]), jnp.pad(gb, ((0, 0), (0, LANES - 2 * M_HEADS))),
                   jnp.broadcast_to(gb.reshape(2 * M_HEADS, 1), (2 * M_HEADS, CHUNK)),
                   row(m_norm_g[l]), a_w2[l].astype(F32), row(a_b[l]), row(g_norm_g[l]), bn)
        wr = jnp.pad(jnp.concatenate([wr_g[l], wr_e[l]], axis=1).astype(F32),
                     ((0, 0), (0, LANES - N_GROUPS - N_EXPERTS)))
        br = jnp.pad(jnp.concatenate([br_g[l], br_e[l]]).astype(F32), (0, LANES - N_GROUPS - N_EXPERTS))
        h, xe, route, counts = _outproj_router(h, y, w_out[l].astype(BF16), row(norm2_g[l]), wr,
                                               br.reshape(1, LANES), rows)
        blk_ea, blk_eb, n_real, n_used, off, n_blocks = _plan_blocks(counts, n)
        cls = route[:, 0].reshape(n_chunks, 1, rows)
        rank = route[:, 1].reshape(n_chunks, 1, rows)
        xs = _dispatch(off, cls, rank, xe.reshape(n, SUBLANES, LANES), n_blocks * EXPERT_ROWS)
        wgu = jnp.concatenate([w_gate[l], w_up[l]], axis=-1).astype(BF16)
        ys = _experts(blk_ea, blk_eb, n_real, n_used, xs.reshape(n_blocks * EXPERT_ROWS * SUBLANES, LANES), wgu,
                      w_down[l].astype(BF16))
    out = _final(off, cls, rank, h, ys, row(final_norm_g), bn)
    return out.reshape(bn, seq, dm)
```

```python
import functools

import numpy as np
import jax
import jax.numpy as jnp
from jax import lax
from jax.experimental import pallas as pl
from jax.experimental.pallas import tpu as pltpu

F32 = jnp.float32
BF16 = jnp.bfloat16
HIGHEST = lax.Precision.HIGHEST

D_MODEL = 1024
CHUNK = 64
N_META = 16
LEAD_PAD = CHUNK - N_META
EPS = 1e-6
M_HEADS = 4
M_WIDTH = 512
M_HEAD_DIM = 128
CONV_K = 4
G_HEADS = 4
G_WIDTH = 512
G_V_DIM = 128
G_K_DIM = 64
G_K_WIDTH = 256
G_RANK = 16
G_TAU = 16.0
N_GROUPS = 4
EXPERTS_PER_GROUP = 8
N_EXPERTS = 32
D_EXPERT = 256
N_PAIRS = EXPERTS_PER_GROUP * (EXPERTS_PER_GROUP - 1) // 2
N_CLASSES = N_GROUPS * N_PAIRS

LANES = 128
SUBLANES = 8
EXPERT_ROWS = 128
VMEM_LIMIT = 56 * 1024 * 1024

Z_QM, Z_KM, Z_VM, Z_OM = 0, 512, 1024, 1536
Z_QG, Z_KG, Z_VG, Z_GG = 2048, 2304, 2560, 3072
Z_SMALL = 3584
Z_WIDTH = Z_SMALL + LANES
X_EXT = D_MODEL + LANES
GLA_SAFE_EXP = 80.0


def _log_sigmoid(x):
    return jnp.minimum(x, 0.0) - jnp.log(1.0 + jnp.exp(-jnp.abs(x)))


def _sigmoid(x):
    return 1.0 / (1.0 + jnp.exp(-x))


def _rms(x, g):
    return x * lax.rsqrt(jnp.mean(x * x, axis=-1, keepdims=True) + EPS) * g


def _dot(a, b):
    return jnp.dot(a.astype(BF16), b.astype(BF16), preferred_element_type=F32)


def _dot_nt(a, b):
    return lax.dot_general(a.astype(BF16), b.astype(BF16), (((1,), (1,)), ((), ())),
                           preferred_element_type=F32)


def _dot_f32(a, b):
    return jnp.dot(a, b, precision=HIGHEST, preferred_element_type=F32)


def _params(*semantics):
    return pltpu.CompilerParams(dimension_semantics=semantics, vmem_limit_bytes=VMEM_LIMIT)


def _embed_inproj_kernel(x_ref, meta_ref, g_ref, w_ref, h_ref, z_ref):
    c = pl.program_id(0)
    bn = x_ref.shape[0]
    rows = bn * CHUNK

    @pl.when(c == 0)
    def _():
        lead = jnp.concatenate([jnp.zeros((LEAD_PAD, D_MODEL), F32), meta_ref[...]], axis=0)
        for b in range(bn):
            h_ref[b * CHUNK:(b + 1) * CHUNK, :] = lead

    @pl.when(c > 0)
    def _():
        h_ref[...] = x_ref[...].reshape(rows, D_MODEL)

    z_ref[...] = _dot(_rms(h_ref[...], g_ref[...]), w_ref[...])


def _embed_inproj(x4, meta, g, w, n_chunks):
    bn = x4.shape[0]
    rows = bn * CHUNK
    n = n_chunks * rows
    return pl.pallas_call(
        _embed_inproj_kernel,
        grid=(n_chunks,),
        in_specs=[
            pl.BlockSpec((bn, 1, CHUNK, D_MODEL), lambda c: (0, jnp.maximum(c - 1, 0), 0, 0)),
            pl.BlockSpec((N_META, D_MODEL), lambda c: (0, 0)),
            pl.BlockSpec((1, D_MODEL), lambda c: (0, 0)),
            pl.BlockSpec((D_MODEL, Z_WIDTH), lambda c: (0, 0)),
        ],
        out_specs=[
            pl.BlockSpec((rows, D_MODEL), lambda c: (c, 0)),
            pl.BlockSpec((rows, Z_WIDTH), lambda c: (c, 0)),
        ],
        out_shape=[jax.ShapeDtypeStruct((n, D_MODEL), F32), jax.ShapeDtypeStruct((n, Z_WIDTH), F32)],
        compiler_params=_params("parallel"),
        name="embed_inproj",
    )(x4, meta, g, w)


def _moe_rows(i, n_steps, off_ref, cur, nxt, ys_hbm, ybuf, sem):
    rows = cur[0].shape[-1]
    tile = rows * SUBLANES
    s = lax.rem(i, 2)

    def start(idx, slot):
        cls_ref, rank_ref = idx
        for k in range(rows):
            src = pl.multiple_of((off_ref[cls_ref[0, 0, k]] + rank_ref[0, 0, k]) * SUBLANES, SUBLANES)
            dst = pl.multiple_of(slot * tile + k * SUBLANES, SUBLANES)
            pltpu.make_async_copy(ys_hbm.at[pl.ds(src, SUBLANES), :], ybuf.at[pl.ds(dst, SUBLANES), :],
                                  sem.at[slot]).start()

    def wait(slot):
        pltpu.make_async_copy(ys_hbm.at[pl.ds(0, tile), :], ybuf.at[pl.ds(pl.multiple_of(slot * tile, tile), tile), :],
                              sem.at[slot]).wait()

    @pl.when(i == 0)
    def _():
        start(cur, 0)

    wait(s)
    start(nxt, 1 - s)

    @pl.when(i == n_steps - 1)
    def _():
        wait(1 - s)

    base = s * tile
    return jnp.concatenate([ybuf[pl.ds(base + sg, rows, stride=SUBLANES), :] for sg in range(SUBLANES)], axis=1)


def _add_inproj_kernel(off_ref, cls_ref, rank_ref, ncls_ref, nrank_ref, h_ref, ys_hbm, g_ref, w_ref, ho_ref, z_ref,
                       ybuf, sem):
    y = _moe_rows(pl.program_id(0), pl.num_programs(0), off_ref, (cls_ref, rank_ref), (ncls_ref, nrank_ref),
                  ys_hbm, ybuf, sem)
    h = h_ref[...] + y
    ho_ref[...] = h
    z_ref[...] = _dot(_rms(h, g_ref[...]), w_ref[...])


def _add_inproj(off, cls, rank, h, ys, g, w):
    n = h.shape[0]
    n_steps, _, rows = cls.shape
    smem = lambda m: pl.BlockSpec((1, 1, rows), m, memory_space=pltpu.SMEM)
    cur = lambda i, off: (i, 0, 0)
    nxt = lambda i, off: (jnp.minimum(i + 1, n_steps - 1), 0, 0)
    grid_spec = pltpu.PrefetchScalarGridSpec(
        num_scalar_prefetch=1,
        grid=(n_steps,),
        in_specs=[
            smem(cur), smem(cur), smem(nxt), smem(nxt),
            pl.BlockSpec((rows, D_MODEL), lambda i, off: (i, 0)),
            pl.BlockSpec(memory_space=pl.ANY),
            pl.BlockSpec((1, D_MODEL), lambda i, off: (0, 0)),
            pl.BlockSpec((D_MODEL, Z_WIDTH), lambda i, off: (0, 0)),
        ],
        out_specs=[
            pl.BlockSpec((rows, D_MODEL), lambda i, off: (i, 0)),
            pl.BlockSpec((rows, Z_WIDTH), lambda i, off: (i, 0)),
        ],
        scratch_shapes=[pltpu.VMEM((2 * rows * SUBLANES, LANES), F32), pltpu.SemaphoreType.DMA((2,))],
    )
    return pl.pallas_call(
        _add_inproj_kernel,
        grid_spec=grid_spec,
        out_shape=[jax.ShapeDtypeStruct((n, D_MODEL), F32), jax.ShapeDtypeStruct((n, Z_WIDTH), F32)],
        compiler_params=_params("arbitrary"),
        name="add_inproj",
    )(off, cls, rank, cls, rank, h, ys, g, w)


def _gla_intra_pairwise(q, k, bg, kbuf, bbuf):
    kbuf[...] = k
    bbuf[...] = bg
    lane = lax.broadcasted_iota(jnp.int32, (CHUNK, CHUNK), 1)

    def column(s, a):
        ks = kbuf[pl.ds(s, 1), :]
        bs = bbuf[pl.ds(s, 1), :]
        col = jnp.sum(q * ks * jnp.exp(jnp.minimum(bg - bs, 0.0)), axis=-1, keepdims=True)
        return jnp.where(lane == s, col, a)

    return lax.fori_loop(0, CHUNK, column, jnp.zeros((CHUNK, CHUNK), F32))


def _gla_streams(z_ref, rows, valid_c, tri, aw_ref, ab_ref):
    code = z_ref[rows, Z_SMALL:Z_SMALL + LANES][:, 2 * M_HEADS:2 * M_HEADS + G_RANK]
    u = _dot_f32(code, aw_ref[...]) + ab_ref[...]
    la = jnp.where(valid_c, _log_sigmoid(u) * (1.0 / G_TAU), 0.0)
    bg = _dot_f32(tri, la)
    q_g = jnp.where(valid_c, z_ref[rows, Z_QG:Z_QG + G_K_WIDTH], 0.0) * (G_K_DIM ** -0.5)
    k_g = jnp.where(valid_c, z_ref[rows, Z_KG:Z_KG + G_K_WIDTH], 0.0)
    v_g = jnp.where(valid_c, z_ref[rows, Z_VG:Z_VG + G_WIDTH], 0.0)
    return q_g, k_g, v_g, bg


def _gla_gated(o, gate, g):
    return _rms(o, g) * (gate * _sigmoid(gate))


def _mixer_kernel(z_ref, cw_ref, cb_ref, gbr_ref, gbc_ref, mg_ref, aw_ref, ab_ref, gg_ref, y_ref,
                  tail_ref, c_ref, m_ref, s_ref, oi_ref, kbuf, bbuf):
    c = pl.program_id(0)
    bn = z_ref.shape[0] // CHUNK

    @pl.when(c == 0)
    def _():
        tail_ref[...] = jnp.zeros_like(tail_ref)
        c_ref[...] = jnp.zeros_like(c_ref)
        m_ref[...] = jnp.zeros_like(m_ref)
        s_ref[...] = jnp.zeros_like(s_ref)

    row = lax.broadcasted_iota(jnp.int32, (CHUNK, CHUNK), 0)
    col = lax.broadcasted_iota(jnp.int32, (CHUNK, CHUNK), 1)
    causal = col <= row
    tri = jnp.where(causal, 1.0, 0.0).astype(F32)
    tri_t = jnp.where(row <= col, 1.0, 0.0).astype(F32)
    first = c == 0
    valid_c = jnp.logical_or(jnp.logical_not(first),
                             lax.broadcasted_iota(jnp.int32, (CHUNK, 1), 0) >= LEAD_PAD)
    valid_r = jnp.logical_or(jnp.logical_not(first),
                             lax.broadcasted_iota(jnp.int32, (1, CHUNK), 1) >= LEAD_PAD)
    row8 = lax.broadcasted_iota(jnp.int32, (SUBLANES, 1), 0)
    ones_col = jnp.where(lax.broadcasted_iota(jnp.int32, (CHUNK, LANES), 1) == 0, 1.0, 0.0).astype(F32)
    neg_inf = -jnp.inf

    def mlstm_batch(b):
        rows = slice(b * CHUNK, (b + 1) * CHUNK)
        x = jnp.where(valid_c, z_ref[rows, Z_QM:Z_QM + 2 * M_WIDTH], 0.0)
        prev = tail_ref[b]
        acc = x * cw_ref[CONV_K - 1:CONV_K, :] + cb_ref[...]
        for k in range(1, CONV_K):
            cur = pltpu.roll(x, k, 0)
            fix = pltpu.roll(prev, k, 0)
            top = jnp.where(row8 < k, fix, cur[0:SUBLANES])
            shifted = jnp.concatenate([top, cur[SUBLANES:]], axis=0)
            acc = acc + shifted * cw_ref[CONV_K - 1 - k:CONV_K - k, :]
        tail_ref[b] = x[CHUNK - SUBLANES:]
        qk = acc * _sigmoid(acc)
        qk = jnp.where(valid_c, qk, 0.0)
        q_m = qk[:, :M_WIDTH]
        k_m = qk[:, M_WIDTH:] * (M_HEAD_DIM ** -0.5)
        v_m = jnp.where(valid_c, z_ref[rows, Z_VM:Z_VM + M_WIDTH], 0.0)
        o_m = z_ref[rows, Z_OM:Z_OM + M_WIDTH]

        zs = z_ref[rows, Z_SMALL:Z_SMALL + LANES]
        g_col = zs + gbr_ref[...]
        g_row = zs.T[0:2 * M_HEADS, :] + gbc_ref[...]
        b_col = _dot_f32(tri, jnp.where(valid_c, _log_sigmoid(g_col), 0.0))
        b_row = _dot_f32(jnp.where(valid_r, _log_sigmoid(g_row), 0.0), tri_t)
        ig_col = jnp.where(valid_c, g_col, neg_inf)
        ig_row = jnp.where(valid_r, g_row, neg_inf)

        for hd in range(M_HEADS):
            idx = b * M_HEADS + hd
            lanes = slice(hd * M_HEAD_DIM, (hd + 1) * M_HEAD_DIM)
            q, k, v = q_m[:, lanes], k_m[:, lanes], v_m[:, lanes]
            bc, br = b_col[:, M_HEADS + hd:M_HEADS + hd + 1], b_row[M_HEADS + hd:M_HEADS + hd + 1, :]
            igc, igr = ig_col[:, hd:hd + 1], ig_row[hd:hd + 1, :]
            m_prev = m_ref[idx][0:1, 0:1]
            cst = c_ref[idx]
            dmat = jnp.where(causal, bc - br + igr, neg_inf)
            m_inter = bc + m_prev
            m_t = jnp.maximum(m_inter, jnp.max(dmat, axis=-1, keepdims=True))
            s = _dot_nt(q, k) * jnp.exp(dmat - m_t)
            w_inter = jnp.exp(m_inter - m_t)
            v_aug = jnp.concatenate([v, ones_col], axis=1)
            nd = _dot(s, v_aug) + w_inter * _dot(q, cst)
            num, den = nd[:, :M_HEAD_DIM], nd[:, M_HEAD_DIM:M_HEAD_DIM + 1]
            hh = num / jnp.maximum(jnp.abs(den), jnp.exp(-m_t))
            b_end = bc[CHUNK - 1:CHUNK, :]
            m_new = jnp.maximum(b_end + m_prev, jnp.max(b_end - br + igr, axis=-1, keepdims=True))
            wk = jnp.exp(b_end - bc + igc - m_new)
            decay = jnp.exp(b_end + m_prev - m_new)
            c_ref[idx] = decay * cst + _dot((k * wk).T, v_aug)
            m_ref[idx] = jnp.broadcast_to(m_new, (SUBLANES, LANES))
            y = _rms(hh, mg_ref[:, lanes]) * _sigmoid(o_m[:, lanes])
            y_ref[rows, lanes] = y.astype(y_ref.dtype)

    def gla_batch(b):
        rows = slice(b * CHUNK, (b + 1) * CHUNK)
        q_g, k_g, v_g, bg = _gla_streams(z_ref, rows, valid_c, tri, aw_ref, ab_ref)
        gate_g = z_ref[rows, Z_GG:Z_GG + G_WIDTH]
        bg_t = bg.T
        b_end = bg[CHUNK - 1:CHUNK, :]
        q_dec = q_g * jnp.exp(bg)
        k_inv = k_g * jnp.exp(-bg)
        k_end = k_g * jnp.exp(b_end - bg)
        for hd in range(G_HEADS):
            idx = b * G_HEADS + hd
            kl = slice(hd * G_K_DIM, (hd + 1) * G_K_DIM)
            vl = slice(hd * G_V_DIM, (hd + 1) * G_V_DIM)
            yl = slice(M_WIDTH + hd * G_V_DIM, M_WIDTH + (hd + 1) * G_V_DIM)
            st = s_ref[idx]
            a = jnp.where(causal, _dot_nt(q_dec[:, kl], k_inv[:, kl]), 0.0)
            o_inter = _dot(q_dec[:, kl], st)
            oi_ref[rows, vl] = o_inter
            e_end = jnp.exp(bg_t[hd * G_K_DIM:(hd + 1) * G_K_DIM, CHUNK - 1:CHUNK])
            s_ref[idx] = e_end * st + _dot(k_end[:, kl].T, v_g[:, vl])
            y = _gla_gated(_dot(a, v_g[:, vl]) + o_inter, gate_g[:, vl], gg_ref[:, vl])
            y_ref[rows, yl] = y.astype(y_ref.dtype)
        return jnp.max(-b_end) > GLA_SAFE_EXP

    def gla_redo_batch(b, carry):
        rows = pl.ds(pl.multiple_of(b * CHUNK, CHUNK), CHUNK)
        q_g, k_g, v_g, bg = _gla_streams(z_ref, rows, valid_c, tri, aw_ref, ab_ref)
        gate_g = z_ref[rows, Z_GG:Z_GG + G_WIDTH]
        for hd in range(G_HEADS):
            kl = slice(hd * G_K_DIM, (hd + 1) * G_K_DIM)
            vl = slice(hd * G_V_DIM, (hd + 1) * G_V_DIM)
            yl = slice(M_WIDTH + hd * G_V_DIM, M_WIDTH + (hd + 1) * G_V_DIM)
            a = jnp.where(causal, _gla_intra_pairwise(q_g[:, kl], k_g[:, kl], bg[:, kl], kbuf, bbuf), 0.0)
            y = _gla_gated(_dot(a, v_g[:, vl]) + oi_ref[rows, vl], gate_g[:, vl], gg_ref[:, vl])
            y_ref[rows, yl] = y.astype(y_ref.dtype)
        return carry

    unsafe = False
    for b in range(bn):
        mlstm_batch(b)
        unsafe = jnp.logical_or(unsafe, gla_batch(b))

    @pl.when(unsafe)
    def _():
        lax.fori_loop(0, bn, gla_redo_batch, 0)


def _mixer(z, conv_w, conv_b, gb_row, gb_col, m_norm_g, a_w2, a_b, g_norm_g, bn):
    rows = bn * CHUNK
    n = z.shape[0]
    full = lambda shape: pl.BlockSpec(shape, lambda c: (0,) * len(shape))
    return pl.pallas_call(
        _mixer_kernel,
        grid=(n // rows,),
        in_specs=[
            pl.BlockSpec((rows, Z_WIDTH), lambda c: (c, 0)),
            full((CONV_K, 2 * M_WIDTH)), full((1, 2 * M_WIDTH)),
            full((1, LANES)), full((2 * M_HEADS, CHUNK)),
            full((1, M_WIDTH)), full((G_RANK, G_K_WIDTH)), full((1, G_K_WIDTH)), full((1, G_WIDTH)),
        ],
        out_specs=pl.BlockSpec((rows, D_MODEL), lambda c: (c, 0)),
        out_shape=jax.ShapeDtypeStruct((n, D_MODEL), BF16),
        scratch_shapes=[
            pltpu.VMEM((bn, SUBLANES, 2 * M_WIDTH), F32),
            pltpu.VMEM((bn * M_HEADS, M_HEAD_DIM, 2 * M_HEAD_DIM), F32),
            pltpu.VMEM((bn * M_HEADS, SUBLANES, LANES), F32),
            pltpu.VMEM((bn * G_HEADS, G_K_DIM, G_V_DIM), F32),
            pltpu.VMEM((rows, G_WIDTH), F32),
            pltpu.VMEM((CHUNK, G_K_DIM), F32), pltpu.VMEM((CHUNK, G_K_DIM), F32),
        ],
        compiler_params=_params("arbitrary"),
        name="mixer",
    )(z, conv_w, conv_b, gb_row, gb_col, m_norm_g, a_w2, a_b, g_norm_g)


def _outproj_router_kernel(h_ref, y_ref, wo_ref, g_ref, wr_ref, br_ref, h1_ref, xe_ref, route_ref, cnt_ref,
                           carry_ref):
    i = pl.program_id(0)
    rows = h_ref.shape[0]

    @pl.when(i == 0)
    def _():
        carry_ref[...] = jnp.zeros_like(carry_ref)

    h1 = h_ref[...] + jnp.dot(y_ref[...], wo_ref[...], preferred_element_type=F32)
    h1_ref[...] = h1
    xn = _rms(h1, g_ref[...])
    logits = _dot_f32(xn, wr_ref[...]) + br_ref[...]
    lane = lax.broadcasted_iota(jnp.int32, (rows, LANES), 1)
    neg_inf = -jnp.inf
    big = jnp.int32(LANES)

    def first_argmax(vals):
        top = jnp.max(vals, axis=-1, keepdims=True)
        return top, jnp.min(jnp.where(vals == top, lane, big), axis=-1, keepdims=True)

    lg = jnp.where(lane < N_GROUPS, logits, neg_inf)
    g_max, g_sel = first_argmax(lg)
    p_sel = 1.0 / jnp.sum(jnp.exp(lg - g_max), axis=-1, keepdims=True)
    in_group = jnp.logical_and(lane >= N_GROUPS + EXPERTS_PER_GROUP * g_sel,
                               lane < N_GROUPS + EXPERTS_PER_GROUP * (g_sel + 1))
    le = jnp.where(in_group, logits, neg_inf)
    v1, i1 = first_argmax(le)
    v2, i2 = first_argmax(jnp.where(lane == i1, neg_inf, le))
    e21 = jnp.exp(v2 - v1)
    gate1 = p_sel / (1.0 + e21)
    gate2 = p_sel * e21 / (1.0 + e21)
    j1 = i1 - N_GROUPS - EXPERTS_PER_GROUP * g_sel
    j2 = i2 - N_GROUPS - EXPERTS_PER_GROUP * g_sel
    ja, jb = jnp.minimum(j1, j2), jnp.maximum(j1, j2)
    ga = jnp.where(j1 < j2, gate1, gate2)
    gb = jnp.where(j1 < j2, gate2, gate1)
    pair = ((ja * (2 * EXPERTS_PER_GROUP - 1 - ja)) >> 1) + (jb - ja - 1)
    cls = g_sel * N_PAIRS + pair

    bits = lax.bitcast_convert_type(xn.astype(jnp.bfloat16).astype(F32), jnp.uint32)
    half = D_MODEL // 2
    packed = bits[:, half:] | (bits[:, :half] >> 16)
    gates = lax.bitcast_convert_type(jnp.where(lane == 0, ga, jnp.where(lane == 1, gb, 0.0)), jnp.uint32)
    for sg in range(SUBLANES):
        if sg < half // LANES:
            val = packed[:, sg * LANES:(sg + 1) * LANES]
        else:
            val = gates if sg == half // LANES else jnp.zeros((rows, LANES), jnp.uint32)
        xe_ref[pl.ds(sg, rows, stride=SUBLANES), :] = val

    onehot = jnp.where(lane == cls, 1.0, 0.0).astype(F32)
    r_i = lax.broadcasted_iota(jnp.int32, (rows, rows), 0)
    c_i = lax.broadcasted_iota(jnp.int32, (rows, rows), 1)
    incl = jnp.where(c_i <= r_i, 1.0, 0.0).astype(BF16)
    prefix = jnp.dot(incl, onehot.astype(BF16), preferred_element_type=F32)
    rank = jnp.sum(onehot * (prefix - 1.0 + carry_ref[...]), axis=-1, keepdims=True)
    carry_ref[...] = carry_ref[...] + prefix[rows - 1:rows, :]
    cnt_ref[...] = carry_ref[...]
    route_ref[...] = jnp.where(lane == 0, cls, jnp.where(lane == 1, rank.astype(jnp.int32), 0))


def _outproj_router(h, y, w_out, g, wr, br, rows):
    n = h.shape[0]
    return pl.pallas_call(
        _outproj_router_kernel,
        grid=(n // rows,),
        in_specs=[
            pl.BlockSpec((rows, D_MODEL), lambda i: (i, 0)),
            pl.BlockSpec((rows, D_MODEL), lambda i: (i, 0)),
            pl.BlockSpec((D_MODEL, D_MODEL), lambda i: (0, 0)),
            pl.BlockSpec((1, D_MODEL), lambda i: (0, 0)),
            pl.BlockSpec((D_MODEL, LANES), lambda i: (0, 0)),
            pl.BlockSpec((1, LANES), lambda i: (0, 0)),
        ],
        out_specs=[
            pl.BlockSpec((rows, D_MODEL), lambda i: (i, 0)),
            pl.BlockSpec((rows * SUBLANES, LANES), lambda i: (i, 0)),
            pl.BlockSpec((rows, LANES), lambda i: (i, 0)),
            pl.BlockSpec((1, LANES), lambda i: (0, 0)),
        ],
        out_shape=[
            jax.ShapeDtypeStruct((n, D_MODEL), F32),
            jax.ShapeDtypeStruct((n * SUBLANES, LANES), jnp.uint32),
            jax.ShapeDtypeStruct((n, LANES), jnp.int32),
            jax.ShapeDtypeStruct((1, LANES), F32),
        ],
        scratch_shapes=[pltpu.VMEM((1, LANES), F32)],
        compiler_params=_params("arbitrary"),
        name="outproj_router",
    )(h, y, w_out, g, wr, br)


def _dispatch_kernel(off_ref, cls_ref, rank_ref, x_ref, xs_hbm, stage, sem):
    i = pl.program_id(0)
    rows = cls_ref.shape[-1]
    s = lax.rem(i, 2)

    def wait(slot):
        pltpu.make_async_copy(stage.at[slot], xs_hbm.at[pl.ds(0, rows)], sem.at[slot]).wait()

    @pl.when(i >= 2)
    def _():
        wait(s)

    stage[s] = x_ref[...]
    for k in range(rows):
        dst = off_ref[cls_ref[0, 0, k]] + rank_ref[0, 0, k]
        pltpu.make_async_copy(stage.at[s].at[k], xs_hbm.at[dst], sem.at[s]).start()

    @pl.when(i == pl.num_programs(0) - 1)
    def _():
        wait(s)

        @pl.when(i >= 1)
        def _():
            wait(1 - s)


def _dispatch(off, cls, rank, xe, n_slots):
    n_steps, _, rows = cls.shape
    smem = pl.BlockSpec((1, 1, rows), lambda i, off: (i, 0, 0), memory_space=pltpu.SMEM)
    grid_spec = pltpu.PrefetchScalarGridSpec(
        num_scalar_prefetch=1,
        grid=(n_steps,),
        in_specs=[smem, smem, pl.BlockSpec((rows, SUBLANES, LANES), lambda i, off: (i, 0, 0))],
        out_specs=pl.BlockSpec(memory_space=pl.ANY),
        scratch_shapes=[pltpu.VMEM((2, rows, SUBLANES, LANES), xe.dtype), pltpu.SemaphoreType.DMA((2,))],
    )
    return pl.pallas_call(
        _dispatch_kernel,
        grid_spec=grid_spec,
        out_shape=jax.ShapeDtypeStruct((n_slots, SUBLANES, LANES), xe.dtype),
        compiler_params=_params("arbitrary"),
        name="dispatch",
    )(off, cls, rank, xe)


def _expert_kernel(ea_ref, eb_ref, nreal_ref, nused_ref, xs_hbm, wgu_a, wd_a, wgu_b, wd_b, ys_hbm,
                   xbuf, obuf, sem_in, sem_out):
    i = pl.program_id(0)
    n_used = nused_ref[0]
    slot = lax.rem(i, 2)
    tile = EXPERT_ROWS * SUBLANES

    def in_copy(blk, s):
        size = pl.multiple_of(nreal_ref[blk] * SUBLANES, SUBLANES)
        return pltpu.make_async_copy(xs_hbm.at[pl.ds(pl.multiple_of(blk * tile, tile), size), :],
                                     xbuf.at[pl.ds(pl.multiple_of(s * tile, tile), size), :], sem_in.at[s])

    def out_copy(blk, s):
        size = pl.multiple_of(nreal_ref[blk] * SUBLANES, SUBLANES)
        return pltpu.make_async_copy(obuf.at[pl.ds(pl.multiple_of(s * tile, tile), size), :],
                                     ys_hbm.at[pl.ds(pl.multiple_of(blk * tile, tile), size), :], sem_out.at[s])

    @pl.when(i < n_used)
    def _():
        @pl.when(i == 0)
        def _():
            xbuf[...] = jnp.zeros_like(xbuf)
            in_copy(0, 0).start()

        in_copy(i, slot).wait()

        @pl.when(i + 1 < n_used)
        def _():
            in_copy(i + 1, 1 - slot).start()

        @pl.when(i >= 2)
        def _():
            out_copy(jnp.maximum(i - 2, 0), slot).wait()

        base = slot * tile
        seg = lambda sg: xbuf[pl.ds(base + sg, EXPERT_ROWS, stride=SUBLANES), :]
        words = [seg(sg) for sg in range(D_MODEL // 2 // LANES)]
        as_f32 = lambda w: lax.bitcast_convert_type(w, F32)
        xb = jnp.concatenate([as_f32(w << 16) for w in words] + [as_f32(w & jnp.uint32(0xFFFF0000)) for w in words],
                             axis=1).astype(BF16)
        gates = as_f32(seg(D_MODEL // 2 // LANES))
        ga, gb = gates[:, 0:1], gates[:, 1:2]

        def mlp(wgu, wd):
            gu = jnp.dot(xb, wgu[0], preferred_element_type=F32)
            gate, up = gu[:, :D_EXPERT], gu[:, D_EXPERT:]
            return jnp.dot((gate * _sigmoid(gate) * up).astype(BF16), wd[0], preferred_element_type=F32)

        y = ga * mlp(wgu_a, wd_a) + gb * mlp(wgu_b, wd_b)
        for sg in range(SUBLANES):
            obuf[pl.ds(base + sg, EXPERT_ROWS, stride=SUBLANES), :] = y[:, sg * LANES:(sg + 1) * LANES]
        out_copy(i, slot).start()

        @pl.when(i == n_used - 1)
        def _():
            out_copy(i, slot).wait()

            @pl.when(i >= 1)
            def _():
                out_copy(jnp.maximum(i - 1, 0), 1 - slot).wait()


def _experts(blk_ea, blk_eb, n_real, n_used, xs, wgu, wd):
    n_blocks = blk_ea.shape[0]
    tile = EXPERT_ROWS * SUBLANES
    grid_spec = pltpu.PrefetchScalarGridSpec(
        num_scalar_prefetch=4,
        grid=(n_blocks,),
        in_specs=[
            pl.BlockSpec(memory_space=pl.ANY),
            pl.BlockSpec((1, D_MODEL, 2 * D_EXPERT), lambda i, ea, eb, nr, nu: (ea[i], 0, 0)),
            pl.BlockSpec((1, D_EXPERT, D_MODEL), lambda i, ea, eb, nr, nu: (ea[i], 0, 0)),
            pl.BlockSpec((1, D_MODEL, 2 * D_EXPERT), lambda i, ea, eb, nr, nu: (eb[i], 0, 0)),
            pl.BlockSpec((1, D_EXPERT, D_MODEL), lambda i, ea, eb, nr, nu: (eb[i], 0, 0)),
        ],
        out_specs=pl.BlockSpec(memory_space=pl.ANY),
        scratch_shapes=[
            pltpu.VMEM((2 * tile, LANES), jnp.uint32),
            pltpu.VMEM((2 * tile, LANES), F32),
            pltpu.SemaphoreType.DMA((2,)),
            pltpu.SemaphoreType.DMA((2,)),
        ],
    )
    return pl.pallas_call(
        _expert_kernel,
        grid_spec=grid_spec,
        out_shape=jax.ShapeDtypeStruct((n_blocks * tile, LANES), F32),
        compiler_params=_params("arbitrary"),
        name="experts",
    )(blk_ea, blk_eb, n_real, n_used, xs, wgu, wd, wgu, wd)


def _pair_tables():
    ea, eb = [], []
    for g in range(N_GROUPS):
        for a in range(EXPERTS_PER_GROUP):
            for b in range(a + 1, EXPERTS_PER_GROUP):
                ea.append(g * EXPERTS_PER_GROUP + a)
                eb.append(g * EXPERTS_PER_GROUP + b)
    return np.asarray(ea, np.int32), np.asarray(eb, np.int32)


def _plan_blocks(counts, n):
    cnt = counts[0, :N_CLASSES].astype(jnp.int32)
    padded = (cnt + EXPERT_ROWS - 1) // EXPERT_ROWS * EXPERT_ROWS
    pad_end = jnp.cumsum(padded)
    off = pad_end - padded
    n_blocks = -(-(n + N_CLASSES * (EXPERT_ROWS - 1)) // EXPERT_ROWS)
    blk_start = jnp.arange(n_blocks, dtype=jnp.int32) * EXPERT_ROWS
    blk_cls = jnp.minimum(jnp.searchsorted(pad_end, blk_start, side='right'), N_CLASSES - 1)
    n_real = jnp.clip(cnt[blk_cls] - (blk_start - off[blk_cls]), 0, EXPERT_ROWS).astype(jnp.int32)
    ea_tab, eb_tab = _pair_tables()
    n_used = (pad_end[-1] // EXPERT_ROWS).astype(jnp.int32).reshape(1)
    off = jnp.pad(off, (0, LANES - N_CLASSES)).astype(jnp.int32)
    return jnp.asarray(ea_tab)[blk_cls], jnp.asarray(eb_tab)[blk_cls], n_real, n_used, off, n_blocks


def _final_kernel(off_ref, cls_ref, rank_ref, ncls_ref, nrank_ref, h_ref, ys_hbm, g_ref, o_ref, ybuf, sem):
    y = _moe_rows(pl.program_id(0), pl.num_programs(0), off_ref, (cls_ref, rank_ref), (ncls_ref, nrank_ref),
                  ys_hbm, ybuf, sem)
    out = _rms(h_ref[...] + y, g_ref[...])
    o_ref[...] = out.reshape(o_ref.shape)


def _final(off, cls, rank, h, ys, g, bn):
    n_chunks, _, rows = cls.shape
    smem = lambda m: pl.BlockSpec((1, 1, rows), m, memory_space=pltpu.SMEM)
    cur = lambda c, off: (c + 1, 0, 0)
    nxt = lambda c, off: (jnp.minimum(c + 2, n_chunks - 1), 0, 0)
    grid_spec = pltpu.PrefetchScalarGridSpec(
        num_scalar_prefetch=1,
        grid=(n_chunks - 1,),
        in_specs=[
            smem(cur), smem(cur), smem(nxt), smem(nxt),
            pl.BlockSpec((rows, D_MODEL), lambda c, off: (c + 1, 0)),
            pl.BlockSpec(memory_space=pl.ANY),
            pl.BlockSpec((1, D_MODEL), lambda c, off: (0, 0)),
        ],
        out_specs=pl.BlockSpec((bn, 1, CHUNK, D_MODEL), lambda c, off: (0, c, 0, 0)),
        scratch_shapes=[pltpu.VMEM((2 * rows * SUBLANES, LANES), F32), pltpu.SemaphoreType.DMA((2,))],
    )
    return pl.pallas_call(
        _final_kernel,
        grid_spec=grid_spec,
        out_shape=jax.ShapeDtypeStruct((bn, n_chunks - 1, CHUNK, D_MODEL), F32),
        compiler_params=_params("arbitrary"),
        name="final_norm",
    )(off, cls, rank, cls, rank, h, ys, g)


def _repack_w_in(w):
    o = np.cumsum([0, 2 * M_WIDTH, M_WIDTH, M_WIDTH, M_HEADS, M_HEADS, G_K_WIDTH, G_K_WIDTH, G_WIDTH, G_WIDTH,
                   G_RANK])
    seg = lambda i: w[:, o[i]:o[i + 1]]
    small = jnp.concatenate([seg(3), seg(4), seg(9)], axis=1)
    small = jnp.pad(small, ((0, 0), (0, LANES - small.shape[1])))
    return jnp.concatenate([seg(0), seg(1), seg(2), seg(5), seg(6), seg(7), seg(8), small], axis=1).astype(BF16)


def kernel(x, meta_tokens, norm1_g, w_in, conv_w, conv_b, gate_b, m_norm_g, a_w2, a_b, g_norm_g, w_out, norm2_g,
           wr_g, br_g, wr_e, br_e, w_gate, w_up, w_down, final_norm_g):
    bn, seq, dm = x.shape
    depth = w_in.shape[0]
    assert dm == D_MODEL and seq % CHUNK == 0
    n_chunks = seq // CHUNK + 1
    rows = bn * CHUNK
    n = n_chunks * rows
    x4 = x.reshape(bn, seq // CHUNK, CHUNK, dm)
    row = lambda v: v.reshape(1, -1).astype(F32)

    h = ys = off = cls = rank = None
    for l in range(depth):
        w_in_l = _repack_w_in(w_in[l])
        if l == 0:
            h, z = _embed_inproj(x4, meta_tokens.astype(F32), row(norm1_g[l]), w_in_l, n_chunks)
        else:
            h, z = _add_inproj(off, cls, rank, h, ys, row(norm1_g[l]), w_in_l)
        gb = gate_b[l].astype(F32).reshape(1, 2 * M_HEADS)
        y = _mixer(z, conv_w[l].astype(F32), row(conv_b[l]), jnp.pad(gb, ((0, 0), (0, LANES - 2 * M_HEADS))),
                   jnp.broadcast_to(gb.reshape(2 * M_HEADS, 1), (2 * M_HEADS, CHUNK)),
                   row(m_norm_g[l]), a_w2[l].astype(F32), row(a_b[l]), row(g_norm_g[l]), bn)
        wr = jnp.pad(jnp.concatenate([wr_g[l], wr_e[l]], axis=1).astype(F32),
                     ((0, 0), (0, LANES - N_GROUPS - N_EXPERTS)))
        br = jnp.pad(jnp.concatenate([br_g[l], br_e[l]]).astype(F32), (0, LANES - N_GROUPS - N_EXPERTS))
        h, xe, route, counts = _outproj_router(h, y, w_out[l].astype(BF16), row(norm2_g[l]), wr,
                                               br.reshape(1, LANES), rows)
        blk_ea, blk_eb, n_real, n_used, off, n_blocks = _plan_blocks(counts, n)
        cls = route[:, 0].reshape(n_chunks, 1, rows)
        rank = route[:, 1].reshape(n_chunks, 1, rows)
        xs = _dispatch(off, cls, rank, xe.reshape(n, SUBLANES, LANES), n_blocks * EXPERT_ROWS)
        wgu = jnp.concatenate([w_gate[l], w_up[l]], axis=-1).astype(BF16)
        ys = _experts(blk_ea, blk_eb, n_real, n_used, xs.reshape(n_blocks * EXPERT_ROWS * SUBLANES, LANES), wgu,
                      w_down[l].astype(BF16))
    out = _final(off, cls, rank, h, ys, row(final_norm_g), bn)
    return out.reshape(bn, seq, dm)
```

```python
import functools

import numpy as np
import jax
import jax.numpy as jnp
from jax import lax
from jax.experimental import pallas as pl
from jax.experimental.pallas import tpu as pltpu

F32 = jnp.float32
BF16 = jnp.bfloat16
HIGHEST = lax.Precision.HIGHEST

D_MODEL = 1024
CHUNK = 64
CHUNK_LOG2 = CHUNK.bit_length() - 1
N_META = 16
LEAD_PAD = CHUNK - N_META
EPS = 1e-6
M_HEADS = 4
M_WIDTH = 512
M_HEAD_DIM = 128
CONV_K = 4
G_HEADS = 4
G_WIDTH = 512
G_V_DIM = 128
G_K_DIM = 64
G_K_WIDTH = 256
G_RANK = 16
G_TAU = 16.0
N_GROUPS = 4
EXPERTS_PER_GROUP = 8
N_EXPERTS = 32
D_EXPERT = 256
N_PAIRS = EXPERTS_PER_GROUP * (EXPERTS_PER_GROUP - 1) // 2
N_CLASSES = N_GROUPS * N_PAIRS

LANES = 128
SUBLANES = 8
EXPERT_ROWS = 128
VMEM_LIMIT = 56 * 1024 * 1024

Z_QM, Z_KM, Z_VM, Z_OM = 0, 512, 1024, 1536
Z_QG, Z_KG, Z_VG, Z_GG = 2048, 2304, 2560, 3072
Z_SMALL = 3584
Z_WIDTH = Z_SMALL + LANES
X_EXT = D_MODEL + LANES
GLA_SAFE_EXP = 80.0


def _log_sigmoid(x):
    return jnp.minimum(x, 0.0) - jnp.log(1.0 + jnp.exp(-jnp.abs(x)))


def _sigmoid(x):
    return 1.0 / (1.0 + jnp.exp(-x))


def _rms(x, g):
    return x * lax.rsqrt(jnp.mean(x * x, axis=-1, keepdims=True) + EPS) * g


def _dot(a, b):
    return jnp.dot(a.astype(BF16), b.astype(BF16), preferred_element_type=F32)


def _dot_nt(a, b):
    return lax.dot_general(a.astype(BF16), b.astype(BF16), (((1,), (1,)), ((), ())),
                           preferred_element_type=F32)


def _dot_f32(a, b):
    return jnp.dot(a, b, precision=HIGHEST, preferred_element_type=F32)


def _params(*semantics):
    return pltpu.CompilerParams(dimension_semantics=semantics, vmem_limit_bytes=VMEM_LIMIT)


def _embed_inproj_kernel(x_ref, meta_ref, g_ref, w_ref, h_ref, z_ref):
    c = pl.program_id(0)
    bn = x_ref.shape[0]
    rows = bn * CHUNK

    @pl.when(c == 0)
    def _():
        lead = jnp.concatenate([jnp.zeros((LEAD_PAD, D_MODEL), F32), meta_ref[...]], axis=0)
        for b in range(bn):
            h_ref[b * CHUNK:(b + 1) * CHUNK, :] = lead

    @pl.when(c > 0)
    def _():
        h_ref[...] = x_ref[...].reshape(rows, D_MODEL)

    z_ref[...] = _dot(_rms(h_ref[...], g_ref[...]), w_ref[...])


def _embed_inproj(x4, meta, g, w, n_chunks):
    bn = x4.shape[0]
    rows = bn * CHUNK
    n = n_chunks * rows
    return pl.pallas_call(
        _embed_inproj_kernel,
        grid=(n_chunks,),
        in_specs=[
            pl.BlockSpec((bn, 1, CHUNK, D_MODEL), lambda c: (0, jnp.maximum(c - 1, 0), 0, 0)),
            pl.BlockSpec((N_META, D_MODEL), lambda c: (0, 0)),
            pl.BlockSpec((1, D_MODEL), lambda c: (0, 0)),
            pl.BlockSpec((D_MODEL, Z_WIDTH), lambda c: (0, 0)),
        ],
        out_specs=[
            pl.BlockSpec((rows, D_MODEL), lambda c: (c, 0)),
            pl.BlockSpec((rows, Z_WIDTH), lambda c: (c, 0)),
        ],
        out_shape=[jax.ShapeDtypeStruct((n, D_MODEL), F32), jax.ShapeDtypeStruct((n, Z_WIDTH), F32)],
        compiler_params=_params("parallel"),
        name="embed_inproj",
    )(x4, meta, g, w)


def _moe_rows(i, n_steps, off_ref, cur, nxt, ys_hbm, ybuf, sem):
    rows = cur[0].shape[-1]
    tile = rows * SUBLANES
    s = lax.rem(i, 2)

    def start(idx, slot):
        cls_ref, rank_ref = idx
        for k in range(rows):
            src = pl.multiple_of((off_ref[cls_ref[0, 0, k]] + rank_ref[0, 0, k]) * SUBLANES, SUBLANES)
            dst = pl.multiple_of(slot * tile + k * SUBLANES, SUBLANES)
            pltpu.make_async_copy(ys_hbm.at[pl.ds(src, SUBLANES), :], ybuf.at[pl.ds(dst, SUBLANES), :],
                                  sem.at[slot]).start()

    def wait(slot):
        pltpu.make_async_copy(ys_hbm.at[pl.ds(0, tile), :], ybuf.at[pl.ds(pl.multiple_of(slot * tile, tile), tile), :],
                              sem.at[slot]).wait()

    @pl.when(i == 0)
    def _():
        start(cur, 0)

    wait(s)
    start(nxt, 1 - s)

    @pl.when(i == n_steps - 1)
    def _():
        wait(1 - s)

    base = s * tile
    return jnp.concatenate([ybuf[pl.ds(base + sg, rows, stride=SUBLANES), :] for sg in range(SUBLANES)], axis=1)


def _add_inproj_kernel(off_ref, cls_ref, rank_ref, ncls_ref, nrank_ref, h_ref, ys_hbm, g_ref, w_ref, ho_ref, z_ref,
                       ybuf, sem):
    y = _moe_rows(pl.program_id(0), pl.num_programs(0), off_ref, (cls_ref, rank_ref), (ncls_ref, nrank_ref),
                  ys_hbm, ybuf, sem)
    h = h_ref[...] + y
    ho_ref[...] = h
    z_ref[...] = _dot(_rms(h, g_ref[...]), w_ref[...])


def _add_inproj(off, cls, rank, h, ys, g, w):
    n = h.shape[0]
    n_steps, _, rows = cls.shape
    smem = lambda m: pl.BlockSpec((1, 1, rows), m, memory_space=pltpu.SMEM)
    cur = lambda i, off: (i, 0, 0)
    nxt = lambda i, off: (jnp.minimum(i + 1, n_steps - 1), 0, 0)
    grid_spec = pltpu.PrefetchScalarGridSpec(
        num_scalar_prefetch=1,
        grid=(n_steps,),
        in_specs=[
            smem(cur), smem(cur), smem(nxt), smem(nxt),
            pl.BlockSpec((rows, D_MODEL), lambda i, off: (i, 0)),
            pl.BlockSpec(memory_space=pl.ANY),
            pl.BlockSpec((1, D_MODEL), lambda i, off: (0, 0)),
            pl.BlockSpec((D_MODEL, Z_WIDTH), lambda i, off: (0, 0)),
        ],
        out_specs=[
            pl.BlockSpec((rows, D_MODEL), lambda i, off: (i, 0)),
            pl.BlockSpec((rows, Z_WIDTH), lambda i, off: (i, 0)),
        ],
        scratch_shapes=[pltpu.VMEM((2 * rows * SUBLANES, LANES), F32), pltpu.SemaphoreType.DMA((2,))],
    )
    return pl.pallas_call(
        _add_inproj_kernel,
        grid_spec=grid_spec,
        out_shape=[jax.ShapeDtypeStruct((n, D_MODEL), F32), jax.ShapeDtypeStruct((n, Z_WIDTH), F32)],
        compiler_params=_params("arbitrary"),
        name="add_inproj",
    )(off, cls, rank, cls, rank, h, ys, g, w)


def _gla_intra_pairwise(q, k, bg, kbuf, bbuf):
    kbuf[...] = k
    bbuf[...] = bg
    lane = lax.broadcasted_iota(jnp.int32, (CHUNK, CHUNK), 1)

    def column(s, a):
        ks = kbuf[pl.ds(s, 1), :]
        bs = bbuf[pl.ds(s, 1), :]
        col = jnp.sum(q * ks * jnp.exp(jnp.minimum(bg - bs, 0.0)), axis=-1, keepdims=True)
        return jnp.where(lane == s, col, a)

    return lax.fori_loop(0, CHUNK, column, jnp.zeros((CHUNK, CHUNK), F32))


def _gla_qkv(z_ref, rows, valid_c):
    q_g = jnp.where(valid_c, z_ref[rows, Z_QG:Z_QG + G_K_WIDTH], 0.0) * (G_K_DIM ** -0.5)
    k_g = jnp.where(valid_c, z_ref[rows, Z_KG:Z_KG + G_K_WIDTH], 0.0)
    v_g = jnp.where(valid_c, z_ref[rows, Z_VG:Z_VG + G_WIDTH], 0.0)
    return q_g, k_g, v_g


def _split3(x):
    hi = x.astype(BF16)
    r = x - hi.astype(F32)
    mid = r.astype(BF16)
    return hi, mid, (r - mid.astype(F32)).astype(BF16)


def _gla_gated(o, gate, g):
    return _rms(o, g) * (gate * _sigmoid(gate))


def _mixer_kernel(z_ref, cw_ref, cb_ref, gbr_ref, gbc_ref, mg_ref, aw_ref, ab_ref, gg_ref, y_ref,
                  tail_ref, c_ref, m_ref, s_ref, oi_ref, bg_ref, kbuf, bbuf):
    c = pl.program_id(0)
    bn = z_ref.shape[0] // CHUNK

    @pl.when(c == 0)
    def _():
        tail_ref[...] = jnp.zeros_like(tail_ref)
        c_ref[...] = jnp.zeros_like(c_ref)
        m_ref[...] = jnp.zeros_like(m_ref)
        s_ref[...] = jnp.zeros_like(s_ref)

    rows_all = bn * CHUNK
    row = lax.broadcasted_iota(jnp.int32, (CHUNK, CHUNK), 0)
    col = lax.broadcasted_iota(jnp.int32, (CHUNK, CHUNK), 1)
    causal = col <= row
    later = jnp.logical_not(c == 0)
    valid_c = jnp.logical_or(later, lax.broadcasted_iota(jnp.int32, (CHUNK, 1), 0) >= LEAD_PAD)
    pos_c = lax.broadcasted_iota(jnp.int32, (rows_all, 1), 0) & (CHUNK - 1)
    pos_r = lax.broadcasted_iota(jnp.int32, (1, rows_all), 1) & (CHUNK - 1)
    valid_ca = jnp.logical_or(later, pos_c >= LEAD_PAD)
    valid_ra = jnp.logical_or(later, pos_r >= LEAD_PAD)
    row8 = lax.broadcasted_iota(jnp.int32, (SUBLANES, 1), 0)
    ones_col = jnp.where(lax.broadcasted_iota(jnp.int32, (CHUNK, LANES), 1) == 0, 1.0, 0.0).astype(F32)
    neg_inf = -jnp.inf

    r_a = lax.broadcasted_iota(jnp.int32, (rows_all, rows_all), 0)
    c_a = lax.broadcasted_iota(jnp.int32, (rows_all, rows_all), 1)
    same = (r_a >> CHUNK_LOG2) == (c_a >> CHUNK_LOG2)
    tri = jnp.where(jnp.logical_and(same, c_a <= r_a), 1.0, 0.0).astype(BF16)
    tri_t = jnp.where(jnp.logical_and(same, r_a <= c_a), 1.0, 0.0).astype(BF16)
    zs = z_ref[:, Z_SMALL:Z_SMALL + LANES]
    g_col = zs + gbr_ref[...]
    g_row = zs.T[0:2 * M_HEADS, :] + gbc_ref[...]
    u = _dot_f32(zs[:, 2 * M_HEADS:2 * M_HEADS + G_RANK], aw_ref[...]) + ab_ref[...]
    la = jnp.where(valid_ca, _log_sigmoid(u) * (1.0 / G_TAU), 0.0)
    lf_col = jnp.where(valid_ca, _log_sigmoid(g_col), 0.0)
    cum = sum(jnp.dot(tri, p, preferred_element_type=F32) for p in _split3(jnp.concatenate([lf_col, la], axis=1)))
    b_col_all, bg_all = cum[:, :LANES], cum[:, LANES:]
    bg_ref[...] = bg_all
    lf_row = jnp.where(valid_ra, _log_sigmoid(g_row), 0.0)
    b_row_all = sum(jnp.dot(p, tri_t, preferred_element_type=F32) for p in _split3(lf_row))
    ig_col_all = jnp.where(valid_ca, g_col, neg_inf)
    ig_row_all = jnp.where(valid_ra, g_row, neg_inf)

    def mlstm_batch(b):
        rows = slice(b * CHUNK, (b + 1) * CHUNK)
        x = jnp.where(valid_c, z_ref[rows, Z_QM:Z_QM + 2 * M_WIDTH], 0.0)
        prev = tail_ref[b]
        acc = x * cw_ref[CONV_K - 1:CONV_K, :] + cb_ref[...]
        for k in range(1, CONV_K):
            cur = pltpu.roll(x, k, 0)
            fix = pltpu.roll(prev, k, 0)
            top = jnp.where(row8 < k, fix, cur[0:SUBLANES])
            shifted = jnp.concatenate([top, cur[SUBLANES:]], axis=0)
            acc = acc + shifted * cw_ref[CONV_K - 1 - k:CONV_K - k, :]
        tail_ref[b] = x[CHUNK - SUBLANES:]
        qk = acc * _sigmoid(acc)
        qk = jnp.where(valid_c, qk, 0.0)
        q_m = qk[:, :M_WIDTH]
        k_m = qk[:, M_WIDTH:] * (M_HEAD_DIM ** -0.5)
        v_m = jnp.where(valid_c, z_ref[rows, Z_VM:Z_VM + M_WIDTH], 0.0)
        o_m = z_ref[rows, Z_OM:Z_OM + M_WIDTH]

        b_col, ig_col = b_col_all[rows], ig_col_all[rows]
        b_row, ig_row = b_row_all[:, rows], ig_row_all[:, rows]

        q_bf, k_bf = q_m.astype(BF16), k_m.astype(BF16)
        heads = []
        for hd in range(M_HEADS):
            idx = b * M_HEADS + hd
            lanes = slice(hd * M_HEAD_DIM, (hd + 1) * M_HEAD_DIM)
            bc, br = b_col[:, M_HEADS + hd:M_HEADS + hd + 1], b_row[M_HEADS + hd:M_HEADS + hd + 1, :]
            igc, igr = ig_col[:, hd:hd + 1], ig_row[hd:hd + 1, :]
            m_prev = m_ref[idx][0:1, 0:1]
            dmat = jnp.where(causal, bc - br + igr, neg_inf)
            m_inter = bc + m_prev
            m_t = jnp.maximum(m_inter, jnp.max(dmat, axis=-1, keepdims=True))
            b_end = bc[CHUNK - 1:CHUNK, :]
            m_new = jnp.maximum(b_end + m_prev, jnp.max(b_end - br + igr, axis=-1, keepdims=True))
            wk = jnp.exp(b_end - bc + igc - m_new)
            heads.append(dict(
                idx=idx, lanes=lanes, q=q_bf[:, lanes], k=k_bf[:, lanes],
                kw_t=(k_m[:, lanes] * wk).T.astype(BF16),
                v_aug=jnp.concatenate([v_m[:, lanes], ones_col], axis=1).astype(BF16),
                e=jnp.exp(dmat - m_t), w_inter=jnp.exp(m_inter - m_t), floor=jnp.exp(-m_t),
                decay=jnp.exp(b_end + m_prev - m_new), m_new=m_new))
        return dict(rows=rows, heads=heads, o_m=o_m)

    def mlstm_first_dots(st):
        for hd in st['heads']:
            cst = c_ref[hd['idx']]
            hd['qk'] = _dot_nt(hd['q'], hd['k'])
            hd['inter'] = _dot(hd['q'], cst)
            c_ref[hd['idx']] = hd['decay'] * cst + _dot(hd['kw_t'], hd['v_aug'])
            m_ref[hd['idx']] = jnp.broadcast_to(hd['m_new'], (SUBLANES, LANES))

    def mlstm_finish(st):
        rows = st['rows']
        for hd in st['heads']:
            lanes = hd['lanes']
            nd = _dot(hd['qk'] * hd['e'], hd['v_aug']) + hd['w_inter'] * hd['inter']
            num, den = nd[:, :M_HEAD_DIM], nd[:, M_HEAD_DIM:M_HEAD_DIM + 1]
            hh = num / jnp.maximum(jnp.abs(den), hd['floor'])
            y = _rms(hh, mg_ref[:, lanes]) * _sigmoid(st['o_m'][:, lanes])
            y_ref[rows, lanes] = y.astype(y_ref.dtype)

    def gla_batch(b):
        rows = slice(b * CHUNK, (b + 1) * CHUNK)
        q_g, k_g, v_g = _gla_qkv(z_ref, rows, valid_c)
        bg = bg_all[rows]
        bg_t = bg.T
        b_end = bg[CHUNK - 1:CHUNK, :]
        q_dec = (q_g * jnp.exp(bg)).astype(BF16)
        k_inv = (k_g * jnp.exp(-bg)).astype(BF16)
        k_end = k_g * jnp.exp(b_end - bg)
        v_bf = v_g.astype(BF16)
        heads = []
        for hd in range(G_HEADS):
            kl = slice(hd * G_K_DIM, (hd + 1) * G_K_DIM)
            vl = slice(hd * G_V_DIM, (hd + 1) * G_V_DIM)
            heads.append(dict(
                idx=b * G_HEADS + hd, vl=vl, yl=slice(M_WIDTH + hd * G_V_DIM, M_WIDTH + (hd + 1) * G_V_DIM),
                q_dec=q_dec[:, kl], k_inv=k_inv[:, kl], k_end_t=k_end[:, kl].T.astype(BF16), v=v_bf[:, vl],
                e_end=jnp.exp(bg_t[hd * G_K_DIM:(hd + 1) * G_K_DIM, CHUNK - 1:CHUNK])))
        return dict(rows=rows, heads=heads, unsafe=jnp.max(-b_end) > GLA_SAFE_EXP)

    def gla_first_dots(st):
        for hd in st['heads']:
            state = s_ref[hd['idx']]
            hd['a'] = _dot_nt(hd['q_dec'], hd['k_inv'])
            hd['o_inter'] = _dot(hd['q_dec'], state)
            s_ref[hd['idx']] = hd['e_end'] * state + _dot(hd['k_end_t'], hd['v'])

    def gla_finish(st):
        rows = st['rows']
        gate_g = z_ref[rows, Z_GG:Z_GG + G_WIDTH]
        for hd in st['heads']:
            vl = hd['vl']
            oi_ref[rows, vl] = hd['o_inter']
            o = _dot(jnp.where(causal, hd['a'], 0.0), hd['v']) + hd['o_inter']
            y_ref[rows, hd['yl']] = _gla_gated(o, gate_g[:, vl], gg_ref[:, vl]).astype(y_ref.dtype)

    def gla_redo_batch(b, carry):
        rows = pl.ds(pl.multiple_of(b * CHUNK, CHUNK), CHUNK)
        q_g, k_g, v_g = _gla_qkv(z_ref, rows, valid_c)
        bg = bg_ref[rows, :]
        gate_g = z_ref[rows, Z_GG:Z_GG + G_WIDTH]
        for hd in range(G_HEADS):
            kl = slice(hd * G_K_DIM, (hd + 1) * G_K_DIM)
            vl = slice(hd * G_V_DIM, (hd + 1) * G_V_DIM)
            yl = slice(M_WIDTH + hd * G_V_DIM, M_WIDTH + (hd + 1) * G_V_DIM)
            a = jnp.where(causal, _gla_intra_pairwise(q_g[:, kl], k_g[:, kl], bg[:, kl], kbuf, bbuf), 0.0)
            y = _gla_gated(_dot(a, v_g[:, vl]) + oi_ref[rows, vl], gate_g[:, vl], gg_ref[:, vl])
            y_ref[rows, yl] = y.astype(y_ref.dtype)
        return carry

    mlstm = [mlstm_batch(b) for b in range(bn)]
    gla = [gla_batch(b) for b in range(bn)]
    for b in range(bn):
        mlstm_first_dots(mlstm[b])
        gla_first_dots(gla[b])
    for b in range(bn):
        mlstm_finish(mlstm[b])
        gla_finish(gla[b])
    unsafe = functools.reduce(jnp.logical_or, [st['unsafe'] for st in gla])

    @pl.when(unsafe)
    def _():
        lax.fori_loop(0, bn, gla_redo_batch, 0)


def _mixer(z, conv_w, conv_b, gb_row, gb_col, m_norm_g, a_w2, a_b, g_norm_g, bn):
    rows = bn * CHUNK
    n = z.shape[0]
    full = lambda shape: pl.BlockSpec(shape, lambda c: (0,) * len(shape))
    return pl.pallas_call(
        _mixer_kernel,
        grid=(n // rows,),
        in_specs=[
            pl.BlockSpec((rows, Z_WIDTH), lambda c: (c, 0)),
            full((CONV_K, 2 * M_WIDTH)), full((1, 2 * M_WIDTH)),
            full((1, LANES)), full((2 * M_HEADS, rows)),
            full((1, M_WIDTH)), full((G_RANK, G_K_WIDTH)), full((1, G_K_WIDTH)), full((1, G_WIDTH)),
        ],
        out_specs=pl.BlockSpec((rows, D_MODEL), lambda c: (c, 0)),
        out_shape=jax.ShapeDtypeStruct((n, D_MODEL), BF16),
        scratch_shapes=[
            pltpu.VMEM((bn, SUBLANES, 2 * M_WIDTH), F32),
            pltpu.VMEM((bn * M_HEADS, M_HEAD_DIM, 2 * M_HEAD_DIM), F32),
            pltpu.VMEM((bn * M_HEADS, SUBLANES, LANES), F32),
            pltpu.VMEM((bn * G_HEADS, G_K_DIM, G_V_DIM), F32),
            pltpu.VMEM((rows, G_WIDTH), F32),
            pltpu.VMEM((rows, G_K_WIDTH), F32),
            pltpu.VMEM((CHUNK, G_K_DIM), F32), pltpu.VMEM((CHUNK, G_K_DIM), F32),
        ],
        compiler_params=_params("arbitrary"),
        name="mixer",
    )(z, conv_w, conv_b, gb_row, gb_col, m_norm_g, a_w2, a_b, g_norm_g)


def _outproj_router_kernel(h_ref, y_ref, wo_ref, g_ref, wr_ref, br_ref, h1_ref, xe_ref, route_ref, cnt_ref,
                           carry_ref):
    i = pl.program_id(0)
    rows = h_ref.shape[0]

    @pl.when(i == 0)
    def _():
        carry_ref[...] = jnp.zeros_like(carry_ref)

    h1 = h_ref[...] + jnp.dot(y_ref[...], wo_ref[...], preferred_element_type=F32)
    h1_ref[...] = h1
    xn = _rms(h1, g_ref[...])
    logits = _dot_f32(xn, wr_ref[...]) + br_ref[...]
    lane = lax.broadcasted_iota(jnp.int32, (rows, LANES), 1)
    neg_inf = -jnp.inf
    big = jnp.int32(LANES)

    def first_argmax(vals):
        top = jnp.max(vals, axis=-1, keepdims=True)
        return top, jnp.min(jnp.where(vals == top, lane, big), axis=-1, keepdims=True)

    lg = jnp.where(lane < N_GROUPS, logits, neg_inf)
    g_max, g_sel = first_argmax(lg)
    p_sel = 1.0 / jnp.sum(jnp.exp(lg - g_max), axis=-1, keepdims=True)
    in_group = jnp.logical_and(lane >= N_GROUPS + EXPERTS_PER_GROUP * g_sel,
                               lane < N_GROUPS + EXPERTS_PER_GROUP * (g_sel + 1))
    le = jnp.where(in_group, logits, neg_inf)
    v1, i1 = first_argmax(le)
    v2, i2 = first_argmax(jnp.where(lane == i1, neg_inf, le))
    e21 = jnp.exp(v2 - v1)
    gate1 = p_sel / (1.0 + e21)
    gate2 = p_sel * e21 / (1.0 + e21)
    j1 = i1 - N_GROUPS - EXPERTS_PER_GROUP * g_sel
    j2 = i2 - N_GROUPS - EXPERTS_PER_GROUP * g_sel
    ja, jb = jnp.minimum(j1, j2), jnp.maximum(j1, j2)
    ga = jnp.where(j1 < j2, gate1, gate2)
    gb = jnp.where(j1 < j2, gate2, gate1)
    pair = ((ja * (2 * EXPERTS_PER_GROUP - 1 - ja)) >> 1) + (jb - ja - 1)
    cls = g_sel * N_PAIRS + pair

    bits = lax.bitcast_convert_type(xn.astype(jnp.bfloat16).astype(F32), jnp.uint32)
    half = D_MODEL // 2
    packed = bits[:, half:] | (bits[:, :half] >> 16)
    gates = lax.bitcast_convert_type(jnp.where(lane == 0, ga, jnp.where(lane == 1, gb, 0.0)), jnp.uint32)
    for sg in range(SUBLANES):
        if sg < half // LANES:
            val = packed[:, sg * LANES:(sg + 1) * LANES]
        else:
            val = gates if sg == half // LANES else jnp.zeros((rows, LANES), jnp.uint32)
        xe_ref[pl.ds(sg, rows, stride=SUBLANES), :] = val

    onehot = jnp.where(lane == cls, 1.0, 0.0).astype(F32)
    r_i = lax.broadcasted_iota(jnp.int32, (rows, rows), 0)
    c_i = lax.broadcasted_iota(jnp.int32, (rows, rows), 1)
    incl = jnp.where(c_i <= r_i, 1.0, 0.0).astype(BF16)
    prefix = jnp.dot(incl, onehot.astype(BF16), preferred_element_type=F32)
    rank = jnp.sum(onehot * (prefix - 1.0 + carry_ref[...]), axis=-1, keepdims=True)
    carry_ref[...] = carry_ref[...] + prefix[rows - 1:rows, :]
    cnt_ref[...] = carry_ref[...]
    route_ref[...] = jnp.where(lane == 0, cls, jnp.where(lane == 1, rank.astype(jnp.int32), 0))


def _outproj_router(h, y, w_out, g, wr, br, rows):
    n = h.shape[0]
    return pl.pallas_call(
        _outproj_router_kernel,
        grid=(n // rows,),
        in_specs=[
            pl.BlockSpec((rows, D_MODEL), lambda i: (i, 0)),
            pl.BlockSpec((rows, D_MODEL), lambda i: (i, 0)),
            pl.BlockSpec((D_MODEL, D_MODEL), lambda i: (0, 0)),
            pl.BlockSpec((1, D_MODEL), lambda i: (0, 0)),
            pl.BlockSpec((D_MODEL, LANES), lambda i: (0, 0)),
            pl.BlockSpec((1, LANES), lambda i: (0, 0)),
        ],
        out_specs=[
            pl.BlockSpec((rows, D_MODEL), lambda i: (i, 0)),
            pl.BlockSpec((rows * SUBLANES, LANES), lambda i: (i, 0)),
            pl.BlockSpec((rows, LANES), lambda i: (i, 0)),
            pl.BlockSpec((1, LANES), lambda i: (0, 0)),
        ],
        out_shape=[
            jax.ShapeDtypeStruct((n, D_MODEL), F32),
            jax.ShapeDtypeStruct((n * SUBLANES, LANES), jnp.uint32),
            jax.ShapeDtypeStruct((n, LANES), jnp.int32),
            jax.ShapeDtypeStruct((1, LANES), F32),
        ],
        scratch_shapes=[pltpu.VMEM((1, LANES), F32)],
        compiler_params=_params("arbitrary"),
        name="outproj_router",
    )(h, y, w_out, g, wr, br)


def _dispatch_kernel(off_ref, cls_ref, rank_ref, x_ref, xs_hbm, stage, sem):
    i = pl.program_id(0)
    rows = cls_ref.shape[-1]
    s = lax.rem(i, 2)

    def wait(slot):
        pltpu.make_async_copy(stage.at[slot], xs_hbm.at[pl.ds(0, rows)], sem.at[slot]).wait()

    @pl.when(i >= 2)
    def _():
        wait(s)

    stage[s] = x_ref[...]
    for k in range(rows):
        dst = off_ref[cls_ref[0, 0, k]] + rank_ref[0, 0, k]
        pltpu.make_async_copy(stage.at[s].at[k], xs_hbm.at[dst], sem.at[s]).start()

    @pl.when(i == pl.num_programs(0) - 1)
    def _():
        wait(s)

        @pl.when(i >= 1)
        def _():
            wait(1 - s)


def _dispatch(off, cls, rank, xe, n_slots):
    n_steps, _, rows = cls.shape
    smem = pl.BlockSpec((1, 1, rows), lambda i, off: (i, 0, 0), memory_space=pltpu.SMEM)
    grid_spec = pltpu.PrefetchScalarGridSpec(
        num_scalar_prefetch=1,
        grid=(n_steps,),
        in_specs=[smem, smem, pl.BlockSpec((rows, SUBLANES, LANES), lambda i, off: (i, 0, 0))],
        out_specs=pl.BlockSpec(memory_space=pl.ANY),
        scratch_shapes=[pltpu.VMEM((2, rows, SUBLANES, LANES), xe.dtype), pltpu.SemaphoreType.DMA((2,))],
    )
    return pl.pallas_call(
        _dispatch_kernel,
        grid_spec=grid_spec,
        out_shape=jax.ShapeDtypeStruct((n_slots, SUBLANES, LANES), xe.dtype),
        compiler_params=_params("arbitrary"),
        name="dispatch",
    )(off, cls, rank, xe)


def _expert_kernel(ea_ref, eb_ref, nreal_ref, nused_ref, xs_hbm, wgu_a, wd_a, wgu_b, wd_b, ys_hbm,
                   xbuf, obuf, sem_in, sem_out):
    i = pl.program_id(0)
    n_used = nused_ref[0]
    slot = lax.rem(i, 2)
    tile = EXPERT_ROWS * SUBLANES

    def in_copy(blk, s):
        size = pl.multiple_of(nreal_ref[blk] * SUBLANES, SUBLANES)
        return pltpu.make_async_copy(xs_hbm.at[pl.ds(pl.multiple_of(blk * tile, tile), size), :],
                                     xbuf.at[pl.ds(pl.multiple_of(s * tile, tile), size), :], sem_in.at[s])

    def out_copy(blk, s):
        size = pl.multiple_of(nreal_ref[blk] * SUBLANES, SUBLANES)
        return pltpu.make_async_copy(obuf.at[pl.ds(pl.multiple_of(s * tile, tile), size), :],
                                     ys_hbm.at[pl.ds(pl.multiple_of(blk * tile, tile), size), :], sem_out.at[s])

    @pl.when(i < n_used)
    def _():
        @pl.when(i == 0)
        def _():
            xbuf[...] = jnp.zeros_like(xbuf)
            in_copy(0, 0).start()

        in_copy(i, slot).wait()

        @pl.when(i + 1 < n_used)
        def _():
            in_copy(i + 1, 1 - slot).start()

        @pl.when(i >= 2)
        def _():
            out_copy(jnp.maximum(i - 2, 0), slot).wait()

        base = slot * tile
        seg = lambda sg: xbuf[pl.ds(base + sg, EXPERT_ROWS, stride=SUBLANES), :]
        words = [seg(sg) for sg in range(D_MODEL // 2 // LANES)]
        as_f32 = lambda w: lax.bitcast_convert_type(w, F32)
        xb = jnp.concatenate([as_f32(w << 16) for w in words] + [as_f32(w & jnp.uint32(0xFFFF0000)) for w in words],
                             axis=1).astype(BF16)
        gates = as_f32(seg(D_MODEL // 2 // LANES))
        ga, gb = gates[:, 0:1], gates[:, 1:2]

        def mlp(wgu, wd):
            gu = jnp.dot(xb, wgu[0], preferred_element_type=F32)
            gate, up = gu[:, :D_EXPERT], gu[:, D_EXPERT:]
            return jnp.dot((gate * _sigmoid(gate) * up).astype(BF16), wd[0], preferred_element_type=F32)

        y = ga * mlp(wgu_a, wd_a) + gb * mlp(wgu_b, wd_b)
        for sg in range(SUBLANES):
            obuf[pl.ds(base + sg, EXPERT_ROWS, stride=SUBLANES), :] = y[:, sg * LANES:(sg + 1) * LANES]
        out_copy(i, slot).start()

        @pl.when(i == n_used - 1)
        def _():
            out_copy(i, slot).wait()

            @pl.when(i >= 1)
            def _():
                out_copy(jnp.maximum(i - 1, 0), 1 - slot).wait()


def _experts(blk_ea, blk_eb, n_real, n_used, xs, wgu, wd):
    n_blocks = blk_ea.shape[0]
    tile = EXPERT_ROWS * SUBLANES
    grid_spec = pltpu.PrefetchScalarGridSpec(
        num_scalar_prefetch=4,
        grid=(n_blocks,),
        in_specs=[
            pl.BlockSpec(memory_space=pl.ANY),
            pl.BlockSpec((1, D_MODEL, 2 * D_EXPERT), lambda i, ea, eb, nr, nu: (ea[i], 0, 0)),
            pl.BlockSpec((1, D_EXPERT, D_MODEL), lambda i, ea, eb, nr, nu: (ea[i], 0, 0)),
            pl.BlockSpec((1, D_MODEL, 2 * D_EXPERT), lambda i, ea, eb, nr, nu: (eb[i], 0, 0)),
            pl.BlockSpec((1, D_EXPERT, D_MODEL), lambda i, ea, eb, nr, nu: (eb[i], 0, 0)),
        ],
        out_specs=pl.BlockSpec(memory_space=pl.ANY),
        scratch_shapes=[
            pltpu.VMEM((2 * tile, LANES), jnp.uint32),
            pltpu.VMEM((2 * tile, LANES), F32),
            pltpu.SemaphoreType.DMA((2,)),
            pltpu.SemaphoreType.DMA((2,)),
        ],
    )
    return pl.pallas_call(
        _expert_kernel,
        grid_spec=grid_spec,
        out_shape=jax.ShapeDtypeStruct((n_blocks * tile, LANES), F32),
        compiler_params=_params("arbitrary"),
        name="experts",
    )(blk_ea, blk_eb, n_real, n_used, xs, wgu, wd, wgu, wd)


def _pair_tables():
    ea, eb = [], []
    for g in range(N_GROUPS):
        for a in range(EXPERTS_PER_GROUP):
            for b in range(a + 1, EXPERTS_PER_GROUP):
                ea.append(g * EXPERTS_PER_GROUP + a)
                eb.append(g * EXPERTS_PER_GROUP + b)
    return np.asarray(ea, np.int32), np.asarray(eb, np.int32)


def _plan_blocks(counts, n):
    cnt = counts[0, :N_CLASSES].astype(jnp.int32)
    padded = (cnt + EXPERT_ROWS - 1) // EXPERT_ROWS * EXPERT_ROWS
    pad_end = jnp.cumsum(padded)
    off = pad_end - padded
    n_blocks = -(-(n + N_CLASSES * (EXPERT_ROWS - 1)) // EXPERT_ROWS)
    blk_start = jnp.arange(n_blocks, dtype=jnp.int32) * EXPERT_ROWS
    blk_cls = jnp.minimum(jnp.searchsorted(pad_end, blk_start, side='right'), N_CLASSES - 1)
    n_real = jnp.clip(cnt[blk_cls] - (blk_start - off[blk_cls]), 0, EXPERT_ROWS).astype(jnp.int32)
    ea_tab, eb_tab = _pair_tables()
    n_used = (pad_end[-1] // EXPERT_ROWS).astype(jnp.int32).reshape(1)
    off = jnp.pad(off, (0, LANES - N_CLASSES)).astype(jnp.int32)
    return jnp.asarray(ea_tab)[blk_cls], jnp.asarray(eb_tab)[blk_cls], n_real, n_used, off, n_blocks


def _final_kernel(off_ref, cls_ref, rank_ref, ncls_ref, nrank_ref, h_ref, ys_hbm, g_ref, o_ref, ybuf, sem):
    y = _moe_rows(pl.program_id(0), pl.num_programs(0), off_ref, (cls_ref, rank_ref), (ncls_ref, nrank_ref),
                  ys_hbm, ybuf, sem)
    out = _rms(h_ref[...] + y, g_ref[...])
    o_ref[...] = out.reshape(o_ref.shape)


def _final(off, cls, rank, h, ys, g, bn):
    n_chunks, _, rows = cls.shape
    smem = lambda m: pl.BlockSpec((1, 1, rows), m, memory_space=pltpu.SMEM)
    cur = lambda c, off: (c + 1, 0, 0)
    nxt = lambda c, off: (jnp.minimum(c + 2, n_chunks - 1), 0, 0)
    grid_spec = pltpu.PrefetchScalarGridSpec(
        num_scalar_prefetch=1,
        grid=(n_chunks - 1,),
        in_specs=[
            smem(cur), smem(cur), smem(nxt), smem(nxt),
            pl.BlockSpec((rows, D_MODEL), lambda c, off: (c + 1, 0)),
            pl.BlockSpec(memory_space=pl.ANY),
            pl.BlockSpec((1, D_MODEL), lambda c, off: (0, 0)),
        ],
        out_specs=pl.BlockSpec((bn, 1, CHUNK, D_MODEL), lambda c, off: (0, c, 0, 0)),
        scratch_shapes=[pltpu.VMEM((2 * rows * SUBLANES, LANES), F32), pltpu.SemaphoreType.DMA((2,))],
    )
    return pl.pallas_call(
        _final_kernel,
        grid_spec=grid_spec,
        out_shape=jax.ShapeDtypeStruct((bn, n_chunks - 1, CHUNK, D_MODEL), F32),
        compiler_params=_params("arbitrary"),
        name="final_norm",
    )(off, cls, rank, cls, rank, h, ys, g)


def _repack_w_in(w):
    o = np.cumsum([0, 2 * M_WIDTH, M_WIDTH, M_WIDTH, M_HEADS, M_HEADS, G_K_WIDTH, G_K_WIDTH, G_WIDTH, G_WIDTH,
                   G_RANK])
    seg = lambda i: w[:, o[i]:o[i + 1]]
    small = jnp.concatenate([seg(3), seg(4), seg(9)], axis=1)
    small = jnp.pad(small, ((0, 0), (0, LANES - small.shape[1])))
    return jnp.concatenate([seg(0), seg(1), seg(2), seg(5), seg(6), seg(7), seg(8), small], axis=1).astype(BF16)


def kernel(x, meta_tokens, norm1_g, w_in, conv_w, conv_b, gate_b, m_norm_g, a_w2, a_b, g_norm_g, w_out, norm2_g,
           wr_g, br_g, wr_e, br_e, w_gate, w_up, w_down, final_norm_g):
    bn, seq, dm = x.shape
    depth = w_in.shape[0]
    assert dm == D_MODEL and seq % CHUNK == 0
    n_chunks = seq // CHUNK + 1
    rows = bn * CHUNK
    n = n_chunks * rows
    x4 = x.reshape(bn, seq // CHUNK, CHUNK, dm)
    row = lambda v: v.reshape(1, -1).astype(F32)

    h = ys = off = cls = rank = None
    for l in range(depth):
        w_in_l = _repack_w_in(w_in[l])
        if l == 0:
            h, z = _embed_inproj(x4, meta_tokens.astype(F32), row(norm1_g[l]), w_in_l, n_chunks)
        else:
            h, z = _add_inproj(off, cls, rank, h, ys, row(norm1_g[l]), w_in_l)
        gb = gate_b[l].astype(F32).reshape(1, 2 * M_HEADS)
        y = _mixer(z, conv_w[l].astype(F32), row(conv_b[l]), jnp.pad(gb, ((0, 0), (0, LANES - 2 * M_HEADS))),
                   jnp.broadcast_to(gb.reshape(2 * M_HEADS, 1), (2 * M_HEADS, rows)),
                   row(m_norm_g[l]), a_w2[l].astype(F32), row(a_b[l]), row(g_norm_g[l]), bn)
        wr = jnp.pad(jnp.concatenate([wr_g[l], wr_e[l]], axis=1).astype(F32),
                     ((0, 0), (0, LANES - N_GROUPS - N_EXPERTS)))
        br = jnp.pad(jnp.concatenate([br_g[l], br_e[l]]).astype(F32), (0, LANES - N_GROUPS - N_EXPERTS))
        h, xe, route, counts = _outproj_router(h, y, w_out[l].astype(BF16), row(norm2_g[l]), wr,
                                               br.reshape(1, LANES), rows)
        blk_ea, blk_eb, n_real, n_used, off, n_blocks = _plan_blocks(counts, n)
        cls = route[:, 0].reshape(n_chunks, 1, rows)
        rank = route[:, 1].reshape(n_chunks, 1, rows)
        xs = _dispatch(off, cls, rank, xe.reshape(n, SUBLANES, LANES), n_blocks * EXPERT_ROWS)
        wgu = jnp.concatenate([w_gate[l], w_up[l]], axis=-1).astype(BF16)
        ys = _experts(blk_ea, blk_eb, n_real, n_used, xs.reshape(n_blocks * EXPERT_ROWS * SUBLANES, LANES), wgu,
                      w_down[l].astype(BF16))
    out = _final(off, cls, rank, h, ys, row(final_norm_g), bn)
    return out.reshape(bn, seq, dm)
```

```python
import functools

import numpy as np
import jax
import jax.numpy as jnp
from jax import lax
from jax.experimental import pallas as pl
from jax.experimental.pallas import tpu as pltpu

F32 = jnp.float32
BF16 = jnp.bfloat16
HIGHEST = lax.Precision.HIGHEST

D_MODEL = 1024
CHUNK = 64
CHUNK_LOG2 = CHUNK.bit_length() - 1
N_META = 16
LEAD_PAD = CHUNK - N_META
EPS = 1e-6
M_HEADS = 4
M_WIDTH = 512
M_HEAD_DIM = 128
CONV_K = 4
G_HEADS = 4
G_WIDTH = 512
G_V_DIM = 128
G_K_DIM = 64
G_K_WIDTH = 256
G_RANK = 16
G_TAU = 16.0
N_GROUPS = 4
EXPERTS_PER_GROUP = 8
N_EXPERTS = 32
D_EXPERT = 256
N_PAIRS = EXPERTS_PER_GROUP * (EXPERTS_PER_GROUP - 1) // 2
N_CLASSES = N_GROUPS * N_PAIRS

LANES = 128
SUBLANES = 8
EXPERT_ROWS = 128
VMEM_LIMIT = 56 * 1024 * 1024

Z_QM, Z_KM, Z_VM, Z_OM = 0, 512, 1024, 1536
Z_QG, Z_KG, Z_VG, Z_GG = 2048, 2304, 2560, 3072
Z_SMALL = 3584
Z_WIDTH = Z_SMALL + LANES
X_EXT = D_MODEL + LANES
GLA_SAFE_EXP = 80.0


def _log_sigmoid(x):
    return jnp.minimum(x, 0.0) - jnp.log(1.0 + jnp.exp(-jnp.abs(x)))


def _sigmoid(x):
    return 1.0 / (1.0 + jnp.exp(-x))


def _rms(x, g):
    return x * lax.rsqrt(jnp.mean(x * x, axis=-1, keepdims=True) + EPS) * g


def _dot(a, b):
    return jnp.dot(a.astype(BF16), b.astype(BF16), preferred_element_type=F32)


def _dot_nt(a, b):
    return lax.dot_general(a.astype(BF16), b.astype(BF16), (((1,), (1,)), ((), ())),
                           preferred_element_type=F32)


def _dot_f32(a, b):
    return jnp.dot(a, b, precision=HIGHEST, preferred_element_type=F32)


def _params(*semantics):
    return pltpu.CompilerParams(dimension_semantics=semantics, vmem_limit_bytes=VMEM_LIMIT)


def _embed_kernel(x_ref, meta_ref, h_ref):
    c = pl.program_id(0)
    bn = x_ref.shape[0]
    rows = bn * CHUNK

    @pl.when(c == 0)
    def _():
        lead = jnp.concatenate([jnp.zeros((LEAD_PAD, D_MODEL), F32), meta_ref[...]], axis=0)
        for b in range(bn):
            h_ref[b * CHUNK:(b + 1) * CHUNK, :] = lead

    @pl.when(c > 0)
    def _():
        h_ref[...] = x_ref[...].reshape(rows, D_MODEL)


def _embed(x4, meta, n_chunks):
    bn = x4.shape[0]
    rows = bn * CHUNK
    return pl.pallas_call(
        _embed_kernel,
        grid=(n_chunks,),
        in_specs=[
            pl.BlockSpec((bn, 1, CHUNK, D_MODEL), lambda c: (0, jnp.maximum(c - 1, 0), 0, 0)),
            pl.BlockSpec((N_META, D_MODEL), lambda c: (0, 0)),
        ],
        out_specs=pl.BlockSpec((rows, D_MODEL), lambda c: (c, 0)),
        out_shape=jax.ShapeDtypeStruct((n_chunks * rows, D_MODEL), F32),
        compiler_params=_params("parallel"),
        name="embed",
    )(x4, meta)


def _moe_rows(i, n_steps, off_ref, cur, nxt, ys_hbm, ybuf, sem):
    rows = cur[0].shape[-1]
    tile = rows * SUBLANES
    s = lax.rem(i, 2)

    def start(idx, slot):
        cls_ref, rank_ref = idx
        for k in range(rows):
            src = pl.multiple_of((off_ref[cls_ref[0, 0, k]] + rank_ref[0, 0, k]) * SUBLANES, SUBLANES)
            dst = pl.multiple_of(slot * tile + k * SUBLANES, SUBLANES)
            pltpu.make_async_copy(ys_hbm.at[pl.ds(src, SUBLANES), :], ybuf.at[pl.ds(dst, SUBLANES), :],
                                  sem.at[slot]).start()

    def wait(slot):
        pltpu.make_async_copy(ys_hbm.at[pl.ds(0, tile), :], ybuf.at[pl.ds(pl.multiple_of(slot * tile, tile), tile), :],
                              sem.at[slot]).wait()

    @pl.when(i == 0)
    def _():
        start(cur, 0)

    wait(s)
    start(nxt, 1 - s)

    @pl.when(i == n_steps - 1)
    def _():
        wait(1 - s)

    base = s * tile
    return jnp.concatenate([ybuf[pl.ds(base + sg, rows, stride=SUBLANES), :] for sg in range(SUBLANES)], axis=1)


def _add_moe_kernel(off_ref, cls_ref, rank_ref, ncls_ref, nrank_ref, h_ref, ys_hbm, ho_ref, ybuf, sem):
    y = _moe_rows(pl.program_id(0), pl.num_programs(0), off_ref, (cls_ref, rank_ref), (ncls_ref, nrank_ref),
                  ys_hbm, ybuf, sem)
    ho_ref[...] = h_ref[...] + y


def _add_moe(off, cls, rank, h, ys):
    n = h.shape[0]
    n_steps, _, rows = cls.shape
    smem = lambda m: pl.BlockSpec((1, 1, rows), m, memory_space=pltpu.SMEM)
    cur = lambda i, off: (i, 0, 0)
    nxt = lambda i, off: (jnp.minimum(i + 1, n_steps - 1), 0, 0)
    grid_spec = pltpu.PrefetchScalarGridSpec(
        num_scalar_prefetch=1,
        grid=(n_steps,),
        in_specs=[
            smem(cur), smem(cur), smem(nxt), smem(nxt),
            pl.BlockSpec((rows, D_MODEL), lambda i, off: (i, 0)),
            pl.BlockSpec(memory_space=pl.ANY),
        ],
        out_specs=pl.BlockSpec((rows, D_MODEL), lambda i, off: (i, 0)),
        scratch_shapes=[pltpu.VMEM((2 * rows * SUBLANES, LANES), F32), pltpu.SemaphoreType.DMA((2,))],
    )
    return pl.pallas_call(
        _add_moe_kernel,
        grid_spec=grid_spec,
        out_shape=jax.ShapeDtypeStruct((n, D_MODEL), F32),
        compiler_params=_params("arbitrary"),
        name="add_moe",
    )(off, cls, rank, cls, rank, h, ys)


def _gla_intra_pairwise(q, k, bg, kbuf, bbuf):
    kbuf[...] = k
    bbuf[...] = bg
    lane = lax.broadcasted_iota(jnp.int32, (CHUNK, CHUNK), 1)

    def column(s, a):
        ks = kbuf[pl.ds(s, 1), :]
        bs = bbuf[pl.ds(s, 1), :]
        col = jnp.sum(q * ks * jnp.exp(jnp.minimum(bg - bs, 0.0)), axis=-1, keepdims=True)
        return jnp.where(lane == s, col, a)

    return lax.fori_loop(0, CHUNK, column, jnp.zeros((CHUNK, CHUNK), F32))


def _gla_qkv(z_ref, rows, valid_c):
    q_g = jnp.where(valid_c, z_ref[rows, Z_QG:Z_QG + G_K_WIDTH], 0.0) * (G_K_DIM ** -0.5)
    k_g = jnp.where(valid_c, z_ref[rows, Z_KG:Z_KG + G_K_WIDTH], 0.0)
    v_g = jnp.where(valid_c, z_ref[rows, Z_VG:Z_VG + G_WIDTH], 0.0)
    return q_g, k_g, v_g


def _split3(x):
    hi = x.astype(BF16)
    r = x - hi.astype(F32)
    mid = r.astype(BF16)
    return hi, mid, (r - mid.astype(F32)).astype(BF16)


def _gla_gated(o, gate, g):
    return _rms(o, g) * (gate * _sigmoid(gate))


IN_PROJ_SPLITS = (0, 1280, 2560, Z_WIDTH)


def _mixer_kernel(hc_ref, hn_ref, g1_ref, w_ref, cw_ref, cb_ref, gbr_ref, gbc_ref, mg_ref, aw_ref, ab_ref, gg_ref,
                  y_ref, za_ref, zb_ref, tail_ref, c_ref, m_ref, s_ref, oi_ref, bg_ref, kbuf, bbuf):
    c = pl.program_id(0)

    @pl.when(c == 0)
    def _():
        tail_ref[...] = jnp.zeros_like(tail_ref)
        c_ref[...] = jnp.zeros_like(c_ref)
        m_ref[...] = jnp.zeros_like(m_ref)
        s_ref[...] = jnp.zeros_like(s_ref)
        za_ref[...] = _dot(_rms(hc_ref[...], g1_ref[...]), w_ref[...])

    args = (hn_ref, g1_ref, w_ref, cw_ref, cb_ref, gbr_ref, gbc_ref, mg_ref, aw_ref, ab_ref, gg_ref, y_ref,
            tail_ref, c_ref, m_ref, s_ref, oi_ref, bg_ref, kbuf, bbuf)
    even = lax.rem(c, 2) == 0

    @pl.when(even)
    def _():
        _mixer_step(c, za_ref, zb_ref, *args)

    @pl.when(jnp.logical_not(even))
    def _():
        _mixer_step(c, zb_ref, za_ref, *args)


def _mixer_step(c, z_ref, zn_ref, hn_ref, g1_ref, w_ref, cw_ref, cb_ref, gbr_ref, gbc_ref, mg_ref, aw_ref, ab_ref,
                gg_ref, y_ref, tail_ref, c_ref, m_ref, s_ref, oi_ref, bg_ref, kbuf, bbuf):
    bn = z_ref.shape[0] // CHUNK
    xn_next = _rms(hn_ref[...], g1_ref[...]).astype(BF16)

    def project_next(part):
        lo, hi = IN_PROJ_SPLITS[part], IN_PROJ_SPLITS[part + 1]
        zn_ref[:, lo:hi] = jnp.dot(xn_next, w_ref[:, lo:hi], preferred_element_type=F32)

    rows_all = bn * CHUNK
    row = lax.broadcasted_iota(jnp.int32, (CHUNK, CHUNK), 0)
    col = lax.broadcasted_iota(jnp.int32, (CHUNK, CHUNK), 1)
    causal = col <= row
    later = jnp.logical_not(c == 0)
    valid_c = jnp.logical_or(later, lax.broadcasted_iota(jnp.int32, (CHUNK, 1), 0) >= LEAD_PAD)
    pos_c = lax.broadcasted_iota(jnp.int32, (rows_all, 1), 0) & (CHUNK - 1)
    pos_r = lax.broadcasted_iota(jnp.int32, (1, rows_all), 1) & (CHUNK - 1)
    valid_ca = jnp.logical_or(later, pos_c >= LEAD_PAD)
    valid_ra = jnp.logical_or(later, pos_r >= LEAD_PAD)
    row8 = lax.broadcasted_iota(jnp.int32, (SUBLANES, 1), 0)
    ones_col = jnp.where(lax.broadcasted_iota(jnp.int32, (CHUNK, LANES), 1) == 0, 1.0, 0.0).astype(F32)
    neg_inf = -jnp.inf

    r_a = lax.broadcasted_iota(jnp.int32, (rows_all, rows_all), 0)
    c_a = lax.broadcasted_iota(jnp.int32, (rows_all, rows_all), 1)
    same = (r_a >> CHUNK_LOG2) == (c_a >> CHUNK_LOG2)
    tri = jnp.where(jnp.logical_and(same, c_a <= r_a), 1.0, 0.0).astype(BF16)
    tri_t = jnp.where(jnp.logical_and(same, r_a <= c_a), 1.0, 0.0).astype(BF16)
    zs = z_ref[:, Z_SMALL:Z_SMALL + LANES]
    g_col = zs + gbr_ref[...]
    g_row = zs.T[0:2 * M_HEADS, :] + gbc_ref[...]
    u = _dot_f32(zs[:, 2 * M_HEADS:2 * M_HEADS + G_RANK], aw_ref[...]) + ab_ref[...]
    project_next(0)
    la = jnp.where(valid_ca, _log_sigmoid(u) * (1.0 / G_TAU), 0.0)
    lf_col = jnp.where(valid_ca, _log_sigmoid(g_col), 0.0)
    cum = sum(jnp.dot(tri, p, preferred_element_type=F32) for p in _split3(jnp.concatenate([lf_col, la], axis=1)))
    b_col_all, bg_all = cum[:, :LANES], cum[:, LANES:]
    bg_ref[...] = bg_all
    lf_row = jnp.where(valid_ra, _log_sigmoid(g_row), 0.0)
    b_row_all = sum(jnp.dot(p, tri_t, preferred_element_type=F32) for p in _split3(lf_row))
    ig_col_all = jnp.where(valid_ca, g_col, neg_inf)
    ig_row_all = jnp.where(valid_ra, g_row, neg_inf)

    def mlstm_batch(b):
        rows = slice(b * CHUNK, (b + 1) * CHUNK)
        x = jnp.where(valid_c, z_ref[rows, Z_QM:Z_QM + 2 * M_WIDTH], 0.0)
        prev = tail_ref[b]
        acc = x * cw_ref[CONV_K - 1:CONV_K, :] + cb_ref[...]
        for k in range(1, CONV_K):
            cur = pltpu.roll(x, k, 0)
            fix = pltpu.roll(prev, k, 0)
            top = jnp.where(row8 < k, fix, cur[0:SUBLANES])
            shifted = jnp.concatenate([top, cur[SUBLANES:]], axis=0)
            acc = acc + shifted * cw_ref[CONV_K - 1 - k:CONV_K - k, :]
        tail_ref[b] = x[CHUNK - SUBLANES:]
        qk = acc * _sigmoid(acc)
        qk = jnp.where(valid_c, qk, 0.0)
        q_m = qk[:, :M_WIDTH]
        k_m = qk[:, M_WIDTH:] * (M_HEAD_DIM ** -0.5)
        v_m = jnp.where(valid_c, z_ref[rows, Z_VM:Z_VM + M_WIDTH], 0.0)
        o_m = z_ref[rows, Z_OM:Z_OM + M_WIDTH]

        b_col, ig_col = b_col_all[rows], ig_col_all[rows]
        b_row, ig_row = b_row_all[:, rows], ig_row_all[:, rows]

        q_bf, k_bf = q_m.astype(BF16), k_m.astype(BF16)
        heads = []
        for hd in range(M_HEADS):
            idx = b * M_HEADS + hd
            lanes = slice(hd * M_HEAD_DIM, (hd + 1) * M_HEAD_DIM)
            bc, br = b_col[:, M_HEADS + hd:M_HEADS + hd + 1], b_row[M_HEADS + hd:M_HEADS + hd + 1, :]
            igc, igr = ig_col[:, hd:hd + 1], ig_row[hd:hd + 1, :]
            m_prev = m_ref[idx][0:1, 0:1]
            dmat = jnp.where(causal, bc - br + igr, neg_inf)
            m_inter = bc + m_prev
            m_t = jnp.maximum(m_inter, jnp.max(dmat, axis=-1, keepdims=True))
            b_end = bc[CHUNK - 1:CHUNK, :]
            m_new = jnp.maximum(b_end + m_prev, jnp.max(b_end - br + igr, axis=-1, keepdims=True))
            wk = jnp.exp(b_end - bc + igc - m_new)
            heads.append(dict(
                idx=idx, lanes=lanes, q=q_bf[:, lanes], k=k_bf[:, lanes],
                kw_t=(k_m[:, lanes] * wk).T.astype(BF16),
                v_aug=jnp.concatenate([v_m[:, lanes], ones_col], axis=1).astype(BF16),
                e=jnp.exp(dmat - m_t), w_inter=jnp.exp(m_inter - m_t), floor=jnp.exp(-m_t),
                decay=jnp.exp(b_end + m_prev - m_new), m_new=m_new))
        return dict(rows=rows, heads=heads, o_m=o_m)

    def mlstm_first_dots(st):
        for hd in st['heads']:
            cst = c_ref[hd['idx']]
            hd['qk'] = _dot_nt(hd['q'], hd['k'])
            hd['inter'] = _dot(hd['q'], cst)
            c_ref[hd['idx']] = hd['decay'] * cst + _dot(hd['kw_t'], hd['v_aug'])
            m_ref[hd['idx']] = jnp.broadcast_to(hd['m_new'], (SUBLANES, LANES))

    def mlstm_finish(st):
        rows = st['rows']
        for hd in st['heads']:
            lanes = hd['lanes']
            nd = _dot(hd['qk'] * hd['e'], hd['v_aug']) + hd['w_inter'] * hd['inter']
            num, den = nd[:, :M_HEAD_DIM], nd[:, M_HEAD_DIM:M_HEAD_DIM + 1]
            hh = num / jnp.maximum(jnp.abs(den), hd['floor'])
            y = _rms(hh, mg_ref[:, lanes]) * _sigmoid(st['o_m'][:, lanes])
            y_ref[rows, lanes] = y.astype(y_ref.dtype)

    def gla_batch(b):
        rows = slice(b * CHUNK, (b + 1) * CHUNK)
        q_g, k_g, v_g = _gla_qkv(z_ref, rows, valid_c)
        bg = bg_all[rows]
        bg_t = bg.T
        b_end = bg[CHUNK - 1:CHUNK, :]
        q_dec = (q_g * jnp.exp(bg)).astype(BF16)
        k_inv = (k_g * jnp.exp(-bg)).astype(BF16)
        k_end = k_g * jnp.exp(b_end - bg)
        v_bf = v_g.astype(BF16)
        heads = []
        for hd in range(G_HEADS):
            kl = slice(hd * G_K_DIM, (hd + 1) * G_K_DIM)
            vl = slice(hd * G_V_DIM, (hd + 1) * G_V_DIM)
            heads.append(dict(
                idx=b * G_HEADS + hd, vl=vl, yl=slice(M_WIDTH + hd * G_V_DIM, M_WIDTH + (hd + 1) * G_V_DIM),
                q_dec=q_dec[:, kl], k_inv=k_inv[:, kl], k_end_t=k_end[:, kl].T.astype(BF16), v=v_bf[:, vl],
                e_end=jnp.exp(bg_t[hd * G_K_DIM:(hd + 1) * G_K_DIM, CHUNK - 1:CHUNK])))
        return dict(rows=rows, heads=heads, unsafe=jnp.max(-b_end) > GLA_SAFE_EXP)

    def gla_first_dots(st):
        for hd in st['heads']:
            state = s_ref[hd['idx']]
            hd['a'] = _dot_nt(hd['q_dec'], hd['k_inv'])
            hd['o_inter'] = _dot(hd['q_dec'], state)
            s_ref[hd['idx']] = hd['e_end'] * state + _dot(hd['k_end_t'], hd['v'])

    def gla_finish(st):
        rows = st['rows']
        gate_g = z_ref[rows, Z_GG:Z_GG + G_WIDTH]
        for hd in st['heads']:
            vl = hd['vl']
            oi_ref[rows, vl] = hd['o_inter']
            o = _dot(jnp.where(causal, hd['a'], 0.0), hd['v']) + hd['o_inter']
            y_ref[rows, hd['yl']] = _gla_gated(o, gate_g[:, vl], gg_ref[:, vl]).astype(y_ref.dtype)

    def gla_redo_batch(b, carry):
        rows = pl.ds(pl.multiple_of(b * CHUNK, CHUNK), CHUNK)
        q_g, k_g, v_g = _gla_qkv(z_ref, rows, valid_c)
        bg = bg_ref[rows, :]
        gate_g = z_ref[rows, Z_GG:Z_GG + G_WIDTH]
        for hd in range(G_HEADS):
            kl = slice(hd * G_K_DIM, (hd + 1) * G_K_DIM)
            vl = slice(hd * G_V_DIM, (hd + 1) * G_V_DIM)
            yl = slice(M_WIDTH + hd * G_V_DIM, M_WIDTH + (hd + 1) * G_V_DIM)
            a = jnp.where(causal, _gla_intra_pairwise(q_g[:, kl], k_g[:, kl], bg[:, kl], kbuf, bbuf), 0.0)
            y = _gla_gated(_dot(a, v_g[:, vl]) + oi_ref[rows, vl], gate_g[:, vl], gg_ref[:, vl])
            y_ref[rows, yl] = y.astype(y_ref.dtype)
        return carry

    project_next(1)
    mlstm = [mlstm_batch(b) for b in range(bn)]
    gla = [gla_batch(b) for b in range(bn)]
    for b in range(bn):
        mlstm_first_dots(mlstm[b])
        gla_first_dots(gla[b])
    project_next(2)
    for b in range(bn):
        mlstm_finish(mlstm[b])
        gla_finish(gla[b])
    unsafe = functools.reduce(jnp.logical_or, [st['unsafe'] for st in gla])

    @pl.when(unsafe)
    def _():
        lax.fori_loop(0, bn, gla_redo_batch, 0)


def _mixer(h, norm_g, w_in, conv_w, conv_b, gb_row, gb_col, m_norm_g, a_w2, a_b, g_norm_g, bn):
    rows = bn * CHUNK
    n = h.shape[0]
    n_chunks = n // rows
    full = lambda shape: pl.BlockSpec(shape, lambda c: (0,) * len(shape))
    return pl.pallas_call(
        _mixer_kernel,
        grid=(n_chunks,),
        in_specs=[
            pl.BlockSpec((rows, D_MODEL), lambda c: (c, 0)),
            pl.BlockSpec((rows, D_MODEL), lambda c: (jnp.minimum(c + 1, n_chunks - 1), 0)),
            full((1, D_MODEL)), full((D_MODEL, Z_WIDTH)),
            full((CONV_K, 2 * M_WIDTH)), full((1, 2 * M_WIDTH)),
            full((1, LANES)), full((2 * M_HEADS, rows)),
            full((1, M_WIDTH)), full((G_RANK, G_K_WIDTH)), full((1, G_K_WIDTH)), full((1, G_WIDTH)),
        ],
        out_specs=pl.BlockSpec((rows, D_MODEL), lambda c: (c, 0)),
        out_shape=jax.ShapeDtypeStruct((n, D_MODEL), BF16),
        scratch_shapes=[
            pltpu.VMEM((rows, Z_WIDTH), F32), pltpu.VMEM((rows, Z_WIDTH), F32),
            pltpu.VMEM((bn, SUBLANES, 2 * M_WIDTH), F32),
            pltpu.VMEM((bn * M_HEADS, M_HEAD_DIM, 2 * M_HEAD_DIM), F32),
            pltpu.VMEM((bn * M_HEADS, SUBLANES, LANES), F32),
            pltpu.VMEM((bn * G_HEADS, G_K_DIM, G_V_DIM), F32),
            pltpu.VMEM((rows, G_WIDTH), F32),
            pltpu.VMEM((rows, G_K_WIDTH), F32),
            pltpu.VMEM((CHUNK, G_K_DIM), F32), pltpu.VMEM((CHUNK, G_K_DIM), F32),
        ],
        compiler_params=_params("arbitrary"),
        name="mixer",
    )(h, h, norm_g, w_in, conv_w, conv_b, gb_row, gb_col, m_norm_g, a_w2, a_b, g_norm_g)


def _outproj_router_kernel(h_ref, y_ref, wo_ref, g_ref, wr_ref, br_ref, h1_ref, xe_ref, route_ref, cnt_ref,
                           carry_ref):
    i = pl.program_id(0)
    rows = h_ref.shape[0]

    @pl.when(i == 0)
    def _():
        carry_ref[...] = jnp.zeros_like(carry_ref)

    h1 = h_ref[...] + jnp.dot(y_ref[...], wo_ref[...], preferred_element_type=F32)
    h1_ref[...] = h1
    xn = _rms(h1, g_ref[...])
    logits = _dot_f32(xn, wr_ref[...]) + br_ref[...]
    lane = lax.broadcasted_iota(jnp.int32, (rows, LANES), 1)
    neg_inf = -jnp.inf
    big = jnp.int32(LANES)

    def first_argmax(vals):
        top = jnp.max(vals, axis=-1, keepdims=True)
        return top, jnp.min(jnp.where(vals == top, lane, big), axis=-1, keepdims=True)

    lg = jnp.where(lane < N_GROUPS, logits, neg_inf)
    g_max, g_sel = first_argmax(lg)
    p_sel = 1.0 / jnp.sum(jnp.exp(lg - g_max), axis=-1, keepdims=True)
    in_group = jnp.logical_and(lane >= N_GROUPS + EXPERTS_PER_GROUP * g_sel,
                               lane < N_GROUPS + EXPERTS_PER_GROUP * (g_sel + 1))
    le = jnp.where(in_group, logits, neg_inf)
    v1, i1 = first_argmax(le)
    v2, i2 = first_argmax(jnp.where(lane == i1, neg_inf, le))
    e21 = jnp.exp(v2 - v1)
    gate1 = p_sel / (1.0 + e21)
    gate2 = p_sel * e21 / (1.0 + e21)
    j1 = i1 - N_GROUPS - EXPERTS_PER_GROUP * g_sel
    j2 = i2 - N_GROUPS - EXPERTS_PER_GROUP * g_sel
    ja, jb = jnp.minimum(j1, j2), jnp.maximum(j1, j2)
    ga = jnp.where(j1 < j2, gate1, gate2)
    gb = jnp.where(j1 < j2, gate2, gate1)
    pair = ((ja * (2 * EXPERTS_PER_GROUP - 1 - ja)) >> 1) + (jb - ja - 1)
    cls = g_sel * N_PAIRS + pair

    bits = lax.bitcast_convert_type(xn.astype(jnp.bfloat16).astype(F32), jnp.uint32)
    half = D_MODEL // 2
    packed = bits[:, half:] | (bits[:, :half] >> 16)
    gates = lax.bitcast_convert_type(jnp.where(lane == 0, ga, jnp.where(lane == 1, gb, 0.0)), jnp.uint32)
    for sg in range(SUBLANES):
        if sg < half // LANES:
            val = packed[:, sg * LANES:(sg + 1) * LANES]
        else:
            val = gates if sg == half // LANES else jnp.zeros((rows, LANES), jnp.uint32)
        xe_ref[pl.ds(sg, rows, stride=SUBLANES), :] = val

    onehot = jnp.where(lane == cls, 1.0, 0.0).astype(F32)
    r_i = lax.broadcasted_iota(jnp.int32, (rows, rows), 0)
    c_i = lax.broadcasted_iota(jnp.int32, (rows, rows), 1)
    incl = jnp.where(c_i <= r_i, 1.0, 0.0).astype(BF16)
    prefix = jnp.dot(incl, onehot.astype(BF16), preferred_element_type=F32)
    rank = jnp.sum(onehot * (prefix - 1.0 + carry_ref[...]), axis=-1, keepdims=True)
    carry_ref[...] = carry_ref[...] + prefix[rows - 1:rows, :]
    cnt_ref[...] = carry_ref[...]
    route_ref[...] = jnp.where(lane == 0, cls, jnp.where(lane == 1, rank.astype(jnp.int32), 0))


def _outproj_router(h, y, w_out, g, wr, br, rows):
    n = h.shape[0]
    return pl.pallas_call(
        _outproj_router_kernel,
        grid=(n // rows,),
        in_specs=[
            pl.BlockSpec((rows, D_MODEL), lambda i: (i, 0)),
            pl.BlockSpec((rows, D_MODEL), lambda i: (i, 0)),
            pl.BlockSpec((D_MODEL, D_MODEL), lambda i: (0, 0)),
            pl.BlockSpec((1, D_MODEL), lambda i: (0, 0)),
            pl.BlockSpec((D_MODEL, LANES), lambda i: (0, 0)),
            pl.BlockSpec((1, LANES), lambda i: (0, 0)),
        ],
        out_specs=[
            pl.BlockSpec((rows, D_MODEL), lambda i: (i, 0)),
            pl.BlockSpec((rows * SUBLANES, LANES), lambda i: (i, 0)),
            pl.BlockSpec((rows, LANES), lambda i: (i, 0)),
            pl.BlockSpec((1, LANES), lambda i: (0, 0)),
        ],
        out_shape=[
            jax.ShapeDtypeStruct((n, D_MODEL), F32),
            jax.ShapeDtypeStruct((n * SUBLANES, LANES), jnp.uint32),
            jax.ShapeDtypeStruct((n, LANES), jnp.int32),
            jax.ShapeDtypeStruct((1, LANES), F32),
        ],
        scratch_shapes=[pltpu.VMEM((1, LANES), F32)],
        compiler_params=_params("arbitrary"),
        name="outproj_router",
    )(h, y, w_out, g, wr, br)


def _dispatch_kernel(off_ref, cls_ref, rank_ref, x_ref, xs_hbm, stage, sem):
    i = pl.program_id(0)
    rows = cls_ref.shape[-1]
    s = lax.rem(i, 2)

    def wait(slot):
        pltpu.make_async_copy(stage.at[slot], xs_hbm.at[pl.ds(0, rows)], sem.at[slot]).wait()

    @pl.when(i >= 2)
    def _():
        wait(s)

    stage[s] = x_ref[...]
    for k in range(rows):
        dst = off_ref[cls_ref[0, 0, k]] + rank_ref[0, 0, k]
        pltpu.make_async_copy(stage.at[s].at[k], xs_hbm.at[dst], sem.at[s]).start()

    @pl.when(i == pl.num_programs(0) - 1)
    def _():
        wait(s)

        @pl.when(i >= 1)
        def _():
            wait(1 - s)


def _dispatch(off, cls, rank, xe, n_slots):
    n_steps, _, rows = cls.shape
    smem = pl.BlockSpec((1, 1, rows), lambda i, off: (i, 0, 0), memory_space=pltpu.SMEM)
    grid_spec = pltpu.PrefetchScalarGridSpec(
        num_scalar_prefetch=1,
        grid=(n_steps,),
        in_specs=[smem, smem, pl.BlockSpec((rows, SUBLANES, LANES), lambda i, off: (i, 0, 0))],
        out_specs=pl.BlockSpec(memory_space=pl.ANY),
        scratch_shapes=[pltpu.VMEM((2, rows, SUBLANES, LANES), xe.dtype), pltpu.SemaphoreType.DMA((2,))],
    )
    return pl.pallas_call(
        _dispatch_kernel,
        grid_spec=grid_spec,
        out_shape=jax.ShapeDtypeStruct((n_slots, SUBLANES, LANES), xe.dtype),
        compiler_params=_params("arbitrary"),
        name="dispatch",
    )(off, cls, rank, xe)


def _expert_kernel(ea_ref, eb_ref, nreal_ref, nused_ref, xs_hbm, wgu_a, wd_a, wgu_b, wd_b, ys_hbm,
                   xbuf, obuf, sem_in, sem_out):
    i = pl.program_id(0)
    n_used = nused_ref[0]
    slot = lax.rem(i, 2)
    tile = EXPERT_ROWS * SUBLANES

    def in_copy(blk, s):
        size = pl.multiple_of(nreal_ref[blk] * SUBLANES, SUBLANES)
        return pltpu.make_async_copy(xs_hbm.at[pl.ds(pl.multiple_of(blk * tile, tile), size), :],
                                     xbuf.at[pl.ds(pl.multiple_of(s * tile, tile), size), :], sem_in.at[s])

    def out_copy(blk, s):
        size = pl.multiple_of(nreal_ref[blk] * SUBLANES, SUBLANES)
        return pltpu.make_async_copy(obuf.at[pl.ds(pl.multiple_of(s * tile, tile), size), :],
                                     ys_hbm.at[pl.ds(pl.multiple_of(blk * tile, tile), size), :], sem_out.at[s])

    @pl.when(i < n_used)
    def _():
        @pl.when(i == 0)
        def _():
            xbuf[...] = jnp.zeros_like(xbuf)
            in_copy(0, 0).start()

        in_copy(i, slot).wait()

        @pl.when(i + 1 < n_used)
        def _():
            in_copy(i + 1, 1 - slot).start()

        @pl.when(i >= 2)
        def _():
            out_copy(jnp.maximum(i - 2, 0), slot).wait()

        base = slot * tile
        seg = lambda sg: xbuf[pl.ds(base + sg, EXPERT_ROWS, stride=SUBLANES), :]
        words = [seg(sg) for sg in range(D_MODEL // 2 // LANES)]
        as_f32 = lambda w: lax.bitcast_convert_type(w, F32)
        xb = jnp.concatenate([as_f32(w << 16) for w in words] + [as_f32(w & jnp.uint32(0xFFFF0000)) for w in words],
                             axis=1).astype(BF16)
        gates = as_f32(seg(D_MODEL // 2 // LANES))
        ga, gb = gates[:, 0:1], gates[:, 1:2]

        def mlp(wgu, wd):
            gu = jnp.dot(xb, wgu[0], preferred_element_type=F32)
            gate, up = gu[:, :D_EXPERT], gu[:, D_EXPERT:]
            return jnp.dot((gate * _sigmoid(gate) * up).astype(BF16), wd[0], preferred_element_type=F32)

        y = ga * mlp(wgu_a, wd_a) + gb * mlp(wgu_b, wd_b)
        for sg in range(SUBLANES):
            obuf[pl.ds(base + sg, EXPERT_ROWS, stride=SUBLANES), :] = y[:, sg * LANES:(sg + 1) * LANES]
        out_copy(i, slot).start()

        @pl.when(i == n_used - 1)
        def _():
            out_copy(i, slot).wait()

            @pl.when(i >= 1)
            def _():
                out_copy(jnp.maximum(i - 1, 0), 1 - slot).wait()


def _experts(blk_ea, blk_eb, n_real, n_used, xs, wgu, wd):
    n_blocks = blk_ea.shape[0]
    tile = EXPERT_ROWS * SUBLANES
    grid_spec = pltpu.PrefetchScalarGridSpec(
        num_scalar_prefetch=4,
        grid=(n_blocks,),
        in_specs=[
            pl.BlockSpec(memory_space=pl.ANY),
            pl.BlockSpec((1, D_MODEL, 2 * D_EXPERT), lambda i, ea, eb, nr, nu: (ea[i], 0, 0)),
            pl.BlockSpec((1, D_EXPERT, D_MODEL), lambda i, ea, eb, nr, nu: (ea[i], 0, 0)),
            pl.BlockSpec((1, D_MODEL, 2 * D_EXPERT), lambda i, ea, eb, nr, nu: (eb[i], 0, 0)),
            pl.BlockSpec((1, D_EXPERT, D_MODEL), lambda i, ea, eb, nr, nu: (eb[i], 0, 0)),
        ],
        out_specs=pl.BlockSpec(memory_space=pl.ANY),
        scratch_shapes=[
            pltpu.VMEM((2 * tile, LANES), jnp.uint32),
            pltpu.VMEM((2 * tile, LANES), F32),
            pltpu.SemaphoreType.DMA((2,)),
            pltpu.SemaphoreType.DMA((2,)),
        ],
    )
    return pl.pallas_call(
        _expert_kernel,
        grid_spec=grid_spec,
        out_shape=jax.ShapeDtypeStruct((n_blocks * tile, LANES), F32),
        compiler_params=_params("arbitrary"),
        name="experts",
    )(blk_ea, blk_eb, n_real, n_used, xs, wgu, wd, wgu, wd)


def _pair_tables():
    ea, eb = [], []
    for g in range(N_GROUPS):
        for a in range(EXPERTS_PER_GROUP):
            for b in range(a + 1, EXPERTS_PER_GROUP):
                ea.append(g * EXPERTS_PER_GROUP + a)
                eb.append(g * EXPERTS_PER_GROUP + b)
    return np.asarray(ea, np.int32), np.asarray(eb, np.int32)


def _plan_blocks(counts, n):
    cnt = counts[0, :N_CLASSES].astype(jnp.int32)
    padded = (cnt + EXPERT_ROWS - 1) // EXPERT_ROWS * EXPERT_ROWS
    pad_end = jnp.cumsum(padded)
    off = pad_end - padded
    n_blocks = -(-(n + N_CLASSES * (EXPERT_ROWS - 1)) // EXPERT_ROWS)
    blk_start = jnp.arange(n_blocks, dtype=jnp.int32) * EXPERT_ROWS
    blk_cls = jnp.minimum(jnp.searchsorted(pad_end, blk_start, side='right'), N_CLASSES - 1)
    n_real = jnp.clip(cnt[blk_cls] - (blk_start - off[blk_cls]), 0, EXPERT_ROWS).astype(jnp.int32)
    ea_tab, eb_tab = _pair_tables()
    n_used = (pad_end[-1] // EXPERT_ROWS).astype(jnp.int32).reshape(1)
    off = jnp.pad(off, (0, LANES - N_CLASSES)).astype(jnp.int32)
    return jnp.asarray(ea_tab)[blk_cls], jnp.asarray(eb_tab)[blk_cls], n_real, n_used, off, n_blocks


def _final_kernel(off_ref, cls_ref, rank_ref, ncls_ref, nrank_ref, h_ref, ys_hbm, g_ref, o_ref, ybuf, sem):
    y = _moe_rows(pl.program_id(0), pl.num_programs(0), off_ref, (cls_ref, rank_ref), (ncls_ref, nrank_ref),
                  ys_hbm, ybuf, sem)
    out = _rms(h_ref[...] + y, g_ref[...])
    o_ref[...] = out.reshape(o_ref.shape)


def _final(off, cls, rank, h, ys, g, bn):
    n_chunks, _, rows = cls.shape
    smem = lambda m: pl.BlockSpec((1, 1, rows), m, memory_space=pltpu.SMEM)
    cur = lambda c, off: (c + 1, 0, 0)
    nxt = lambda c, off: (jnp.minimum(c + 2, n_chunks - 1), 0, 0)
    grid_spec = pltpu.PrefetchScalarGridSpec(
        num_scalar_prefetch=1,
        grid=(n_chunks - 1,),
        in_specs=[
            smem(cur), smem(cur), smem(nxt), smem(nxt),
            pl.BlockSpec((rows, D_MODEL), lambda c, off: (c + 1, 0)),
            pl.BlockSpec(memory_space=pl.ANY),
            pl.BlockSpec((1, D_MODEL), lambda c, off: (0, 0)),
        ],
        out_specs=pl.BlockSpec((bn, 1, CHUNK, D_MODEL), lambda c, off: (0, c, 0, 0)),
        scratch_shapes=[pltpu.VMEM((2 * rows * SUBLANES, LANES), F32), pltpu.SemaphoreType.DMA((2,))],
    )
    return pl.pallas_call(
        _final_kernel,
        grid_spec=grid_spec,
        out_shape=jax.ShapeDtypeStruct((bn, n_chunks - 1, CHUNK, D_MODEL), F32),
        compiler_params=_params("arbitrary"),
        name="final_norm",
    )(off, cls, rank, cls, rank, h, ys, g)


def _repack_w_in(w):
    o = np.cumsum([0, 2 * M_WIDTH, M_WIDTH, M_WIDTH, M_HEADS, M_HEADS, G_K_WIDTH, G_K_WIDTH, G_WIDTH, G_WIDTH,
                   G_RANK])
    seg = lambda i: w[:, o[i]:o[i + 1]]
    small = jnp.concatenate([seg(3), seg(4), seg(9)], axis=1)
    small = jnp.pad(small, ((0, 0), (0, LANES - small.shape[1])))
    return jnp.concatenate([seg(0), seg(1), seg(2), seg(5), seg(6), seg(7), seg(8), small], axis=1).astype(BF16)


def kernel(x, meta_tokens, norm1_g, w_in, conv_w, conv_b, gate_b, m_norm_g, a_w2, a_b, g_norm_g, w_out, norm2_g,
           wr_g, br_g, wr_e, br_e, w_gate, w_up, w_down, final_norm_g):
    bn, seq, dm = x.shape
    depth = w_in.shape[0]
    assert dm == D_MODEL and seq % CHUNK == 0
    n_chunks = seq // CHUNK + 1
    rows = bn * CHUNK
    n = n_chunks * rows
    x4 = x.reshape(bn, seq // CHUNK, CHUNK, dm)
    row = lambda v: v.reshape(1, -1).astype(F32)

    h = ys = off = cls = rank = None
    for l in range(depth):
        if l == 0:
            h = _embed(x4, meta_tokens.astype(F32), n_chunks)
        else:
            h = _add_moe(off, cls, rank, h, ys)
        gb = gate_b[l].astype(F32).reshape(1, 2 * M_HEADS)
        y = _mixer(h, row(norm1_g[l]), _repack_w_in(w_in[l]), conv_w[l].astype(F32), row(conv_b[l]),
                   jnp.pad(gb, ((0, 0), (0, LANES - 2 * M_HEADS))),
                   jnp.broadcast_to(gb.reshape(2 * M_HEADS, 1), (2 * M_HEADS, rows)),
                   row(m_norm_g[l]), a_w2[l].astype(F32), row(a_b[l]), row(g_norm_g[l]), bn)
        wr = jnp.pad(jnp.concatenate([wr_g[l], wr_e[l]], axis=1).astype(F32),
                     ((0, 0), (0, LANES - N_GROUPS - N_EXPERTS)))
        br = jnp.pad(jnp.concatenate([br_g[l], br_e[l]]).astype(F32), (0, LANES - N_GROUPS - N_EXPERTS))
        h, xe, route, counts = _outproj_router(h, y, w_out[l].astype(BF16), row(norm2_g[l]), wr,
                                               br.reshape(1, LANES), rows)
        blk_ea, blk_eb, n_real, n_used, off, n_blocks = _plan_blocks(counts, n)
        cls = route[:, 0].reshape(n_chunks, 1, rows)
        rank = route[:, 1].reshape(n_chunks, 1, rows)
        xs = _dispatch(off, cls, rank, xe.reshape(n, SUBLANES, LANES), n_blocks * EXPERT_ROWS)
        wgu = jnp.concatenate([w_gate[l], w_up[l]], axis=-1).astype(BF16)
        ys = _experts(blk_ea, blk_eb, n_real, n_used, xs.reshape(n_blocks * EXPERT_ROWS * SUBLANES, LANES), wgu,
                      w_down[l].astype(BF16))
    out = _final(off, cls, rank, h, ys, row(final_norm_g), bn)
    return out.reshape(bn, seq, dm)
```

```python
import functools

import numpy as np
import jax
import jax.numpy as jnp
from jax import lax
from jax.experimental import pallas as pl
from jax.experimental.pallas import tpu as pltpu

F32 = jnp.float32
BF16 = jnp.bfloat16
HIGHEST = lax.Precision.HIGHEST

D_MODEL = 1024
CHUNK = 64
CHUNK_LOG2 = CHUNK.bit_length() - 1
N_META = 16
LEAD_PAD = CHUNK - N_META
EPS = 1e-6
M_HEADS = 4
M_WIDTH = 512
M_HEAD_DIM = 128
CONV_K = 4
G_HEADS = 4
G_WIDTH = 512
G_V_DIM = 128
G_K_DIM = 64
G_K_WIDTH = 256
G_RANK = 16
G_TAU = 16.0
N_GROUPS = 4
EXPERTS_PER_GROUP = 8
N_EXPERTS = 32
D_EXPERT = 256
N_PAIRS = EXPERTS_PER_GROUP * (EXPERTS_PER_GROUP - 1) // 2
N_CLASSES = N_GROUPS * N_PAIRS

LANES = 128
SUBLANES = 8
EXPERT_ROWS = 128
VMEM_LIMIT = 56 * 1024 * 1024

Z_QM, Z_KM, Z_VM, Z_OM = 0, 512, 1024, 1536
Z_QG, Z_KG, Z_VG, Z_GG = 2048, 2304, 2560, 3072
Z_SMALL = 3584
Z_WIDTH = Z_SMALL + LANES
X_EXT = D_MODEL + LANES
GLA_SAFE_EXP = 80.0


def _log_sigmoid(x):
    return jnp.minimum(x, 0.0) - jnp.log(1.0 + jnp.exp(-jnp.abs(x)))


def _sigmoid(x):
    return 1.0 / (1.0 + jnp.exp(-x))


def _rms(x, g):
    return x * lax.rsqrt(jnp.mean(x * x, axis=-1, keepdims=True) + EPS) * g


def _dot(a, b):
    return jnp.dot(a.astype(BF16), b.astype(BF16), preferred_element_type=F32)


def _dot_nt(a, b):
    return lax.dot_general(a.astype(BF16), b.astype(BF16), (((1,), (1,)), ((), ())),
                           preferred_element_type=F32)


def _dot_f32(a, b):
    return jnp.dot(a, b, precision=HIGHEST, preferred_element_type=F32)


def _params(*semantics):
    return pltpu.CompilerParams(dimension_semantics=semantics, vmem_limit_bytes=VMEM_LIMIT)


def _embed_kernel(x_ref, meta_ref, h_ref):
    c = pl.program_id(0)
    bn = x_ref.shape[0]
    rows = bn * CHUNK

    @pl.when(c == 0)
    def _():
        lead = jnp.concatenate([jnp.zeros((LEAD_PAD, D_MODEL), F32), meta_ref[...]], axis=0)
        for b in range(bn):
            h_ref[b * CHUNK:(b + 1) * CHUNK, :] = lead

    @pl.when(c > 0)
    def _():
        h_ref[...] = x_ref[...].reshape(rows, D_MODEL)


def _embed(x4, meta, n_chunks):
    bn = x4.shape[0]
    rows = bn * CHUNK
    return pl.pallas_call(
        _embed_kernel,
        grid=(n_chunks,),
        in_specs=[
            pl.BlockSpec((bn, 1, CHUNK, D_MODEL), lambda c: (0, jnp.maximum(c - 1, 0), 0, 0)),
            pl.BlockSpec((N_META, D_MODEL), lambda c: (0, 0)),
        ],
        out_specs=pl.BlockSpec((rows, D_MODEL), lambda c: (c, 0)),
        out_shape=jax.ShapeDtypeStruct((n_chunks * rows, D_MODEL), F32),
        compiler_params=_params("parallel"),
        name="embed",
    )(x4, meta)


def _moe_rows(i, n_steps, off_ref, cur, nxt, ys_hbm, ybuf, sem):
    rows = cur[0].shape[-1]
    tile = rows * SUBLANES
    s = lax.rem(i, 2)

    def start(idx, slot):
        cls_ref, rank_ref = idx
        for k in range(rows):
            src = pl.multiple_of((off_ref[cls_ref[0, 0, k]] + rank_ref[0, 0, k]) * SUBLANES, SUBLANES)
            dst = pl.multiple_of(slot * tile + k * SUBLANES, SUBLANES)
            pltpu.make_async_copy(ys_hbm.at[pl.ds(src, SUBLANES), :], ybuf.at[pl.ds(dst, SUBLANES), :],
                                  sem.at[slot]).start()

    def wait(slot):
        pltpu.make_async_copy(ys_hbm.at[pl.ds(0, tile), :], ybuf.at[pl.ds(pl.multiple_of(slot * tile, tile), tile), :],
                              sem.at[slot]).wait()

    @pl.when(i == 0)
    def _():
        start(cur, 0)

    wait(s)
    start(nxt, 1 - s)

    @pl.when(i == n_steps - 1)
    def _():
        wait(1 - s)

    base = s * tile
    return jnp.concatenate([ybuf[pl.ds(base + sg, rows, stride=SUBLANES), :] for sg in range(SUBLANES)], axis=1)


def _add_moe_kernel(off_ref, cls_ref, rank_ref, ncls_ref, nrank_ref, h_ref, ys_hbm, ho_ref, ybuf, sem):
    y = _moe_rows(pl.program_id(0), pl.num_programs(0), off_ref, (cls_ref, rank_ref), (ncls_ref, nrank_ref),
                  ys_hbm, ybuf, sem)
    ho_ref[...] = h_ref[...] + y


def _add_moe(off, cls, rank, h, ys):
    n = h.shape[0]
    n_steps, _, rows = cls.shape
    smem = lambda m: pl.BlockSpec((1, 1, rows), m, memory_space=pltpu.SMEM)
    cur = lambda i, off: (i, 0, 0)
    nxt = lambda i, off: (jnp.minimum(i + 1, n_steps - 1), 0, 0)
    grid_spec = pltpu.PrefetchScalarGridSpec(
        num_scalar_prefetch=1,
        grid=(n_steps,),
        in_specs=[
            smem(cur), smem(cur), smem(nxt), smem(nxt),
            pl.BlockSpec((rows, D_MODEL), lambda i, off: (i, 0)),
            pl.BlockSpec(memory_space=pl.ANY),
        ],
        out_specs=pl.BlockSpec((rows, D_MODEL), lambda i, off: (i, 0)),
        scratch_shapes=[pltpu.VMEM((2 * rows * SUBLANES, LANES), F32), pltpu.SemaphoreType.DMA((2,))],
    )
    return pl.pallas_call(
        _add_moe_kernel,
        grid_spec=grid_spec,
        out_shape=jax.ShapeDtypeStruct((n, D_MODEL), F32),
        compiler_params=_params("arbitrary"),
        name="add_moe",
    )(off, cls, rank, cls, rank, h, ys)


def _gla_intra_pairwise(q, k, bg, kbuf, bbuf):
    kbuf[...] = k
    bbuf[...] = bg
    lane = lax.broadcasted_iota(jnp.int32, (CHUNK, CHUNK), 1)

    def column(s, a):
        ks = kbuf[pl.ds(s, 1), :]
        bs = bbuf[pl.ds(s, 1), :]
        col = jnp.sum(q * ks * jnp.exp(jnp.minimum(bg - bs, 0.0)), axis=-1, keepdims=True)
        return jnp.where(lane == s, col, a)

    return lax.fori_loop(0, CHUNK, column, jnp.zeros((CHUNK, CHUNK), F32))


def _gla_qkv(z_ref, rows, valid_c):
    q_g = jnp.where(valid_c, z_ref[rows, Z_QG:Z_QG + G_K_WIDTH], 0.0) * (G_K_DIM ** -0.5)
    k_g = jnp.where(valid_c, z_ref[rows, Z_KG:Z_KG + G_K_WIDTH], 0.0)
    v_g = jnp.where(valid_c, z_ref[rows, Z_VG:Z_VG + G_WIDTH], 0.0)
    return q_g, k_g, v_g


def _split3(x):
    hi = x.astype(BF16)
    r = x - hi.astype(F32)
    mid = r.astype(BF16)
    return hi, mid, (r - mid.astype(F32)).astype(BF16)


def _gla_gated(o, gate, g):
    return _rms(o, g) * (gate * _sigmoid(gate))


IN_PROJ_SPLITS = (0, 1280, 2560, Z_WIDTH)


def _layer_kernel(hc_ref, hn_ref, hp_ref, g1_ref, w_ref, cw_ref, cb_ref, gbr_ref, gbc_ref, mg_ref, aw_ref, ab_ref, gg_ref,
                  wo_ref, g2_ref, wrh_ref, wrl_ref, br_ref, h1_ref, xe_ref, route_ref, cnt_ref,
                  za_ref, zb_ref, ya_ref, yb_ref, tail_ref, c_ref, m_ref, s_ref, oi_ref, bg_ref, kbuf, bbuf, carry_ref):
    c = pl.program_id(0)

    @pl.when(c == 0)
    def _():
        tail_ref[...] = jnp.zeros_like(tail_ref)
        c_ref[...] = jnp.zeros_like(c_ref)
        m_ref[...] = jnp.zeros_like(m_ref)
        s_ref[...] = jnp.zeros_like(s_ref)
        carry_ref[...] = jnp.zeros_like(carry_ref)
        yb_ref[...] = jnp.zeros_like(yb_ref)
        za_ref[...] = _dot(_rms(hc_ref[...], g1_ref[...]), w_ref[...])

    args = (hn_ref, g1_ref, w_ref, cw_ref, cb_ref, gbr_ref, gbc_ref, mg_ref, aw_ref, ab_ref, gg_ref)
    state = (tail_ref, c_ref, m_ref, s_ref, oi_ref, bg_ref, kbuf, bbuf)
    even = lax.rem(c, 2) == 0

    def router(y_prev_ref):
        first = functools.partial(_router_logits, hp_ref, y_prev_ref, wo_ref, g2_ref, wrh_ref, wrl_ref, br_ref, h1_ref)
        second = functools.partial(_router_assign, c > 0, xe_ref, route_ref, cnt_ref, carry_ref)
        return first, second

    @pl.when(even)
    def _():
        _mixer_step(c, za_ref, zb_ref, *args, ya_ref, *state, router(yb_ref))

    @pl.when(jnp.logical_not(even))
    def _():
        _mixer_step(c, zb_ref, za_ref, *args, yb_ref, *state, router(ya_ref))


def _mixer_step(c, z_ref, zn_ref, hn_ref, g1_ref, w_ref, cw_ref, cb_ref, gbr_ref, gbc_ref, mg_ref, aw_ref, ab_ref,
                gg_ref, y_ref, tail_ref, c_ref, m_ref, s_ref, oi_ref, bg_ref, kbuf, bbuf, router):
    route_logits, route_assign = router
    bn = z_ref.shape[0] // CHUNK
    xn_next = _rms(hn_ref[...], g1_ref[...]).astype(BF16)

    def project_next(part):
        lo, hi = IN_PROJ_SPLITS[part], IN_PROJ_SPLITS[part + 1]
        zn_ref[:, lo:hi] = jnp.dot(xn_next, w_ref[:, lo:hi], preferred_element_type=F32)

    rows_all = bn * CHUNK
    row = lax.broadcasted_iota(jnp.int32, (CHUNK, CHUNK), 0)
    col = lax.broadcasted_iota(jnp.int32, (CHUNK, CHUNK), 1)
    causal = col <= row
    later = jnp.logical_not(c == 0)
    valid_c = jnp.logical_or(later, lax.broadcasted_iota(jnp.int32, (CHUNK, 1), 0) >= LEAD_PAD)
    pos_c = lax.broadcasted_iota(jnp.int32, (rows_all, 1), 0) & (CHUNK - 1)
    pos_r = lax.broadcasted_iota(jnp.int32, (1, rows_all), 1) & (CHUNK - 1)
    valid_ca = jnp.logical_or(later, pos_c >= LEAD_PAD)
    valid_ra = jnp.logical_or(later, pos_r >= LEAD_PAD)
    row8 = lax.broadcasted_iota(jnp.int32, (SUBLANES, 1), 0)
    ones_col = jnp.where(lax.broadcasted_iota(jnp.int32, (CHUNK, LANES), 1) == 0, 1.0, 0.0).astype(F32)
    neg_inf = -jnp.inf

    r_a = lax.broadcasted_iota(jnp.int32, (rows_all, rows_all), 0)
    c_a = lax.broadcasted_iota(jnp.int32, (rows_all, rows_all), 1)
    same = (r_a >> CHUNK_LOG2) == (c_a >> CHUNK_LOG2)
    tri = jnp.where(jnp.logical_and(same, c_a <= r_a), 1.0, 0.0).astype(BF16)
    tri_t = jnp.where(jnp.logical_and(same, r_a <= c_a), 1.0, 0.0).astype(BF16)
    zs = z_ref[:, Z_SMALL:Z_SMALL + LANES]
    g_col = zs + gbr_ref[...]
    g_row = zs.T[0:2 * M_HEADS, :] + gbc_ref[...]
    u = _dot_f32(zs[:, 2 * M_HEADS:2 * M_HEADS + G_RANK], aw_ref[...]) + ab_ref[...]
    project_next(0)
    la = jnp.where(valid_ca, _log_sigmoid(u) * (1.0 / G_TAU), 0.0)
    lf_col = jnp.where(valid_ca, _log_sigmoid(g_col), 0.0)
    cum = sum(jnp.dot(tri, p, preferred_element_type=F32) for p in _split3(jnp.concatenate([lf_col, la], axis=1)))
    b_col_all, bg_all = cum[:, :LANES], cum[:, LANES:]
    bg_ref[...] = bg_all
    lf_row = jnp.where(valid_ra, _log_sigmoid(g_row), 0.0)
    b_row_all = sum(jnp.dot(p, tri_t, preferred_element_type=F32) for p in _split3(lf_row))
    ig_col_all = jnp.where(valid_ca, g_col, neg_inf)
    ig_row_all = jnp.where(valid_ra, g_row, neg_inf)

    def mlstm_batch(b):
        rows = slice(b * CHUNK, (b + 1) * CHUNK)
        x = jnp.where(valid_c, z_ref[rows, Z_QM:Z_QM + 2 * M_WIDTH], 0.0)
        prev = tail_ref[b]
        acc = x * cw_ref[CONV_K - 1:CONV_K, :] + cb_ref[...]
        for k in range(1, CONV_K):
            cur = pltpu.roll(x, k, 0)
            fix = pltpu.roll(prev, k, 0)
            top = jnp.where(row8 < k, fix, cur[0:SUBLANES])
            shifted = jnp.concatenate([top, cur[SUBLANES:]], axis=0)
            acc = acc + shifted * cw_ref[CONV_K - 1 - k:CONV_K - k, :]
        tail_ref[b] = x[CHUNK - SUBLANES:]
        qk = acc * _sigmoid(acc)
        qk = jnp.where(valid_c, qk, 0.0)
        q_m = qk[:, :M_WIDTH]
        k_m = qk[:, M_WIDTH:] * (M_HEAD_DIM ** -0.5)
        v_m = jnp.where(valid_c, z_ref[rows, Z_VM:Z_VM + M_WIDTH], 0.0)
        o_m = z_ref[rows, Z_OM:Z_OM + M_WIDTH]

        b_col, ig_col = b_col_all[rows], ig_col_all[rows]
        b_row, ig_row = b_row_all[:, rows], ig_row_all[:, rows]

        q_bf, k_bf = q_m.astype(BF16), k_m.astype(BF16)
        heads = []
        for hd in range(M_HEADS):
            idx = b * M_HEADS + hd
            lanes = slice(hd * M_HEAD_DIM, (hd + 1) * M_HEAD_DIM)
            bc, br = b_col[:, M_HEADS + hd:M_HEADS + hd + 1], b_row[M_HEADS + hd:M_HEADS + hd + 1, :]
            igc, igr = ig_col[:, hd:hd + 1], ig_row[hd:hd + 1, :]
            m_prev = m_ref[idx][0:1, 0:1]
            dmat = jnp.where(causal, bc - br + igr, neg_inf)
            m_inter = bc + m_prev
            m_t = jnp.maximum(m_inter, jnp.max(dmat, axis=-1, keepdims=True))
            b_end = bc[CHUNK - 1:CHUNK, :]
            m_new = jnp.maximum(b_end + m_prev, jnp.max(b_end - br + igr, axis=-1, keepdims=True))
            wk = jnp.exp(b_end - bc + igc - m_new)
            heads.append(dict(
                idx=idx, lanes=lanes, q=q_bf[:, lanes], k=k_bf[:, lanes],
                kw_t=(k_m[:, lanes] * wk).T.astype(BF16),
                v_aug=jnp.concatenate([v_m[:, lanes], ones_col], axis=1).astype(BF16),
                e=jnp.exp(dmat - m_t), w_inter=jnp.exp(m_inter - m_t), floor=jnp.exp(-m_t),
                decay=jnp.exp(b_end + m_prev - m_new), m_new=m_new))
        return dict(rows=rows, heads=heads, o_m=o_m)

    def mlstm_first_dots(st):
        for hd in st['heads']:
            cst = c_ref[hd['idx']]
            hd['qk'] = _dot_nt(hd['q'], hd['k'])
            hd['inter'] = _dot(hd['q'], cst)
            c_ref[hd['idx']] = hd['decay'] * cst + _dot(hd['kw_t'], hd['v_aug'])
            m_ref[hd['idx']] = jnp.broadcast_to(hd['m_new'], (SUBLANES, LANES))

    def mlstm_finish(st):
        rows = st['rows']
        for hd in st['heads']:
            lanes = hd['lanes']
            nd = _dot(hd['qk'] * hd['e'], hd['v_aug']) + hd['w_inter'] * hd['inter']
            num, den = nd[:, :M_HEAD_DIM], nd[:, M_HEAD_DIM:M_HEAD_DIM + 1]
            hh = num / jnp.maximum(jnp.abs(den), hd['floor'])
            y = _rms(hh, mg_ref[:, lanes]) * _sigmoid(st['o_m'][:, lanes])
            y_ref[rows, lanes] = y.astype(y_ref.dtype)

    def gla_batch(b):
        rows = slice(b * CHUNK, (b + 1) * CHUNK)
        q_g, k_g, v_g = _gla_qkv(z_ref, rows, valid_c)
        bg = bg_all[rows]
        bg_t = bg.T
        b_end = bg[CHUNK - 1:CHUNK, :]
        q_dec = (q_g * jnp.exp(bg)).astype(BF16)
        k_inv = (k_g * jnp.exp(-bg)).astype(BF16)
        k_end = k_g * jnp.exp(b_end - bg)
        v_bf = v_g.astype(BF16)
        heads = []
        for hd in range(G_HEADS):
            kl = slice(hd * G_K_DIM, (hd + 1) * G_K_DIM)
            vl = slice(hd * G_V_DIM, (hd + 1) * G_V_DIM)
            heads.append(dict(
                idx=b * G_HEADS + hd, vl=vl, yl=slice(M_WIDTH + hd * G_V_DIM, M_WIDTH + (hd + 1) * G_V_DIM),
                q_dec=q_dec[:, kl], k_inv=k_inv[:, kl], k_end_t=k_end[:, kl].T.astype(BF16), v=v_bf[:, vl],
                e_end=jnp.exp(bg_t[hd * G_K_DIM:(hd + 1) * G_K_DIM, CHUNK - 1:CHUNK])))
        return dict(rows=rows, heads=heads, unsafe=jnp.max(-b_end) > GLA_SAFE_EXP)

    def gla_first_dots(st):
        for hd in st['heads']:
            state = s_ref[hd['idx']]
            hd['a'] = _dot_nt(hd['q_dec'], hd['k_inv'])
            hd['o_inter'] = _dot(hd['q_dec'], state)
            s_ref[hd['idx']] = hd['e_end'] * state + _dot(hd['k_end_t'], hd['v'])

    def gla_finish(st):
        rows = st['rows']
        gate_g = z_ref[rows, Z_GG:Z_GG + G_WIDTH]
        for hd in st['heads']:
            vl = hd['vl']
            oi_ref[rows, vl] = hd['o_inter']
            o = _dot(jnp.where(causal, hd['a'], 0.0), hd['v']) + hd['o_inter']
            y_ref[rows, hd['yl']] = _gla_gated(o, gate_g[:, vl], gg_ref[:, vl]).astype(y_ref.dtype)

    def gla_redo_batch(b, carry):
        rows = pl.ds(pl.multiple_of(b * CHUNK, CHUNK), CHUNK)
        q_g, k_g, v_g = _gla_qkv(z_ref, rows, valid_c)
        bg = bg_ref[rows, :]
        gate_g = z_ref[rows, Z_GG:Z_GG + G_WIDTH]
        for hd in range(G_HEADS):
            kl = slice(hd * G_K_DIM, (hd + 1) * G_K_DIM)
            vl = slice(hd * G_V_DIM, (hd + 1) * G_V_DIM)
            yl = slice(M_WIDTH + hd * G_V_DIM, M_WIDTH + (hd + 1) * G_V_DIM)
            a = jnp.where(causal, _gla_intra_pairwise(q_g[:, kl], k_g[:, kl], bg[:, kl], kbuf, bbuf), 0.0)
            y = _gla_gated(_dot(a, v_g[:, vl]) + oi_ref[rows, vl], gate_g[:, vl], gg_ref[:, vl])
            y_ref[rows, yl] = y.astype(y_ref.dtype)
        return carry

    project_next(1)
    routed = route_logits()
    mlstm = [mlstm_batch(b) for b in range(bn)]
    gla = [gla_batch(b) for b in range(bn)]
    for b in range(bn):
        mlstm_first_dots(mlstm[b])
        gla_first_dots(gla[b])
    project_next(2)
    route_assign(*routed)
    for b in range(bn):
        mlstm_finish(mlstm[b])
        gla_finish(gla[b])
    unsafe = functools.reduce(jnp.logical_or, [st['unsafe'] for st in gla])

    @pl.when(unsafe)
    def _():
        lax.fori_loop(0, bn, gla_redo_batch, 0)


def _layer(h, norm_g, w_in, conv_w, conv_b, gb_row, gb_col, m_norm_g, a_w2, a_b, g_norm_g,
           w_out, norm2_g, wr_hi, wr_lo, br, bn):
    rows = bn * CHUNK
    n = h.shape[0]
    n_chunks = n // rows
    last = n_chunks - 1
    full = lambda shape: pl.BlockSpec(shape, lambda c: (0,) * len(shape))
    h_at = lambda m: pl.BlockSpec((rows, D_MODEL), m)
    prev = lambda c: (jnp.maximum(c - 1, 0), 0)
    return pl.pallas_call(
        _layer_kernel,
        grid=(n_chunks + 1,),
        in_specs=[
            h_at(lambda c: (jnp.minimum(c, last), 0)),
            h_at(lambda c: (jnp.minimum(c + 1, last), 0)),
            h_at(prev),
            full((1, D_MODEL)), full((D_MODEL, Z_WIDTH)),
            full((CONV_K, 2 * M_WIDTH)), full((1, 2 * M_WIDTH)),
            full((1, LANES)), full((2 * M_HEADS, rows)),
            full((1, M_WIDTH)), full((G_RANK, G_K_WIDTH)), full((1, G_K_WIDTH)), full((1, G_WIDTH)),
            full((D_MODEL, D_MODEL)), full((1, D_MODEL)), full((D_MODEL, LANES)), full((D_MODEL, LANES)),
            full((1, LANES)),
        ],
        out_specs=[
            h_at(prev),
            pl.BlockSpec((rows * SUBLANES, LANES), prev),
            pl.BlockSpec((rows, LANES), prev),
            pl.BlockSpec((1, LANES), lambda c: (0, 0)),
        ],
        out_shape=[
            jax.ShapeDtypeStruct((n, D_MODEL), F32),
            jax.ShapeDtypeStruct((n * SUBLANES, LANES), jnp.uint32),
            jax.ShapeDtypeStruct((n, LANES), jnp.int32),
            jax.ShapeDtypeStruct((1, LANES), F32),
        ],
        scratch_shapes=[
            pltpu.VMEM((rows, Z_WIDTH), F32), pltpu.VMEM((rows, Z_WIDTH), F32),
            pltpu.VMEM((rows, D_MODEL), BF16), pltpu.VMEM((rows, D_MODEL), BF16),
            pltpu.VMEM((bn, SUBLANES, 2 * M_WIDTH), F32),
            pltpu.VMEM((bn * M_HEADS, M_HEAD_DIM, 2 * M_HEAD_DIM), F32),
            pltpu.VMEM((bn * M_HEADS, SUBLANES, LANES), F32),
            pltpu.VMEM((bn * G_HEADS, G_K_DIM, G_V_DIM), F32),
            pltpu.VMEM((rows, G_WIDTH), F32),
            pltpu.VMEM((rows, G_K_WIDTH), F32),
            pltpu.VMEM((CHUNK, G_K_DIM), F32), pltpu.VMEM((CHUNK, G_K_DIM), F32),
            pltpu.VMEM((1, LANES), F32),
        ],
        compiler_params=_params("arbitrary"),
        name="layer",
    )(h, h, h, norm_g, w_in, conv_w, conv_b, gb_row, gb_col, m_norm_g, a_w2, a_b, g_norm_g,
      w_out, norm2_g, wr_hi, wr_lo, br)


def _router_logits(h_ref, y_ref, wo_ref, g_ref, wrh_ref, wrl_ref, br_ref, h1_ref):
    h1 = h_ref[...] + jnp.dot(y_ref[...], wo_ref[...], preferred_element_type=F32)
    h1_ref[...] = h1
    xn = _rms(h1, g_ref[...])
    x_hi = xn.astype(BF16)
    x_lo = (xn - x_hi.astype(F32)).astype(BF16)
    dot = functools.partial(jnp.dot, preferred_element_type=F32)
    logits = dot(x_hi, wrh_ref[...]) + dot(x_hi, wrl_ref[...]) + dot(x_lo, wrh_ref[...]) + br_ref[...]
    return xn, logits


def _router_assign(live, xe_ref, route_ref, cnt_ref, carry_ref, xn, logits):
    rows = xn.shape[0]
    lane = lax.broadcasted_iota(jnp.int32, (rows, LANES), 1)
    neg_inf = -jnp.inf
    big = jnp.int32(LANES)

    def first_argmax(vals):
        top = jnp.max(vals, axis=-1, keepdims=True)
        return top, jnp.min(jnp.where(vals == top, lane, big), axis=-1, keepdims=True)

    lg = jnp.where(lane < N_GROUPS, logits, neg_inf)
    g_max, g_sel = first_argmax(lg)
    p_sel = 1.0 / jnp.sum(jnp.exp(lg - g_max), axis=-1, keepdims=True)
    in_group = jnp.logical_and(lane >= N_GROUPS + EXPERTS_PER_GROUP * g_sel,
                               lane < N_GROUPS + EXPERTS_PER_GROUP * (g_sel + 1))
    le = jnp.where(in_group, logits, neg_inf)
    v1, i1 = first_argmax(le)
    v2, i2 = first_argmax(jnp.where(lane == i1, neg_inf, le))
    e21 = jnp.exp(v2 - v1)
    gate1 = p_sel / (1.0 + e21)
    gate2 = p_sel * e21 / (1.0 + e21)
    j1 = i1 - N_GROUPS - EXPERTS_PER_GROUP * g_sel
    j2 = i2 - N_GROUPS - EXPERTS_PER_GROUP * g_sel
    ja, jb = jnp.minimum(j1, j2), jnp.maximum(j1, j2)
    ga = jnp.where(j1 < j2, gate1, gate2)
    gb = jnp.where(j1 < j2, gate2, gate1)
    pair = ((ja * (2 * EXPERTS_PER_GROUP - 1 - ja)) >> 1) + (jb - ja - 1)
    cls = g_sel * N_PAIRS + pair

    bits = lax.bitcast_convert_type(xn.astype(jnp.bfloat16).astype(F32), jnp.uint32)
    half = D_MODEL // 2
    packed = bits[:, half:] | (bits[:, :half] >> 16)
    gates = lax.bitcast_convert_type(jnp.where(lane == 0, ga, jnp.where(lane == 1, gb, 0.0)), jnp.uint32)
    for sg in range(SUBLANES):
        if sg < half // LANES:
            val = packed[:, sg * LANES:(sg + 1) * LANES]
        else:
            val = gates if sg == half // LANES else jnp.zeros((rows, LANES), jnp.uint32)
        xe_ref[pl.ds(sg, rows, stride=SUBLANES), :] = val

    onehot = jnp.where(jnp.logical_and(live, lane == cls), 1.0, 0.0).astype(F32)
    r_i = lax.broadcasted_iota(jnp.int32, (rows, rows), 0)
    c_i = lax.broadcasted_iota(jnp.int32, (rows, rows), 1)
    incl = jnp.where(c_i <= r_i, 1.0, 0.0).astype(BF16)
    prefix = jnp.dot(incl, onehot.astype(BF16), preferred_element_type=F32)
    rank = jnp.sum(onehot * (prefix - 1.0 + carry_ref[...]), axis=-1, keepdims=True)
    carry_ref[...] = carry_ref[...] + prefix[rows - 1:rows, :]
    cnt_ref[...] = carry_ref[...]
    route_ref[...] = jnp.where(lane == 0, cls, jnp.where(lane == 1, rank.astype(jnp.int32), 0))


def _dispatch_kernel(off_ref, cls_ref, rank_ref, x_ref, xs_hbm, stage, sem):
    i = pl.program_id(0)
    rows = cls_ref.shape[-1]
    s = lax.rem(i, 2)

    def wait(slot):
        pltpu.make_async_copy(stage.at[slot], xs_hbm.at[pl.ds(0, rows)], sem.at[slot]).wait()

    @pl.when(i >= 2)
    def _():
        wait(s)

    stage[s] = x_ref[...]
    for k in range(rows):
        dst = off_ref[cls_ref[0, 0, k]] + rank_ref[0, 0, k]
        pltpu.make_async_copy(stage.at[s].at[k], xs_hbm.at[dst], sem.at[s]).start()

    @pl.when(i == pl.num_programs(0) - 1)
    def _():
        wait(s)

        @pl.when(i >= 1)
        def _():
            wait(1 - s)


def _dispatch(off, cls, rank, xe, n_slots):
    n_steps, _, rows = cls.shape
    smem = pl.BlockSpec((1, 1, rows), lambda i, off: (i, 0, 0), memory_space=pltpu.SMEM)
    grid_spec = pltpu.PrefetchScalarGridSpec(
        num_scalar_prefetch=1,
        grid=(n_steps,),
        in_specs=[smem, smem, pl.BlockSpec((rows, SUBLANES, LANES), lambda i, off: (i, 0, 0))],
        out_specs=pl.BlockSpec(memory_space=pl.ANY),
        scratch_shapes=[pltpu.VMEM((2, rows, SUBLANES, LANES), xe.dtype), pltpu.SemaphoreType.DMA((2,))],
    )
    return pl.pallas_call(
        _dispatch_kernel,
        grid_spec=grid_spec,
        out_shape=jax.ShapeDtypeStruct((n_slots, SUBLANES, LANES), xe.dtype),
        compiler_params=_params("arbitrary"),
        name="dispatch",
    )(off, cls, rank, xe)


def _expert_kernel(ea_ref, eb_ref, nreal_ref, nused_ref, xs_hbm, wgu_a, wd_a, wgu_b, wd_b, ys_hbm,
                   xbuf, obuf, sem_in, sem_out):
    i = pl.program_id(0)
    n_used = nused_ref[0]
    slot = lax.rem(i, 2)
    tile = EXPERT_ROWS * SUBLANES

    def in_copy(blk, s):
        size = pl.multiple_of(nreal_ref[blk] * SUBLANES, SUBLANES)
        return pltpu.make_async_copy(xs_hbm.at[pl.ds(pl.multiple_of(blk * tile, tile), size), :],
                                     xbuf.at[pl.ds(pl.multiple_of(s * tile, tile), size), :], sem_in.at[s])

    def out_copy(blk, s):
        size = pl.multiple_of(nreal_ref[blk] * SUBLANES, SUBLANES)
        return pltpu.make_async_copy(obuf.at[pl.ds(pl.multiple_of(s * tile, tile), size), :],
                                     ys_hbm.at[pl.ds(pl.multiple_of(blk * tile, tile), size), :], sem_out.at[s])

    @pl.when(i < n_used)
    def _():
        @pl.when(i == 0)
        def _():
            xbuf[...] = jnp.zeros_like(xbuf)
            in_copy(0, 0).start()

        in_copy(i, slot).wait()

        @pl.when(i + 1 < n_used)
        def _():
            in_copy(i + 1, 1 - slot).start()

        @pl.when(i >= 2)
        def _():
            out_copy(jnp.maximum(i - 2, 0), slot).wait()

        base = slot * tile
        seg = lambda sg: xbuf[pl.ds(base + sg, EXPERT_ROWS, stride=SUBLANES), :]
        words = [seg(sg) for sg in range(D_MODEL // 2 // LANES)]
        as_f32 = lambda w: lax.bitcast_convert_type(w, F32)
        xb = jnp.concatenate([as_f32(w << 16) for w in words] + [as_f32(w & jnp.uint32(0xFFFF0000)) for w in words],
                             axis=1).astype(BF16)
        gates = as_f32(seg(D_MODEL // 2 // LANES))
        ga, gb = gates[:, 0:1], gates[:, 1:2]

        def mlp(wgu, wd):
            gu = jnp.dot(xb, wgu[0], preferred_element_type=F32)
            gate, up = gu[:, :D_EXPERT], gu[:, D_EXPERT:]
            return jnp.dot((gate * _sigmoid(gate) * up).astype(BF16), wd[0], preferred_element_type=F32)

        y = ga * mlp(wgu_a, wd_a) + gb * mlp(wgu_b, wd_b)
        for sg in range(SUBLANES):
            obuf[pl.ds(base + sg, EXPERT_ROWS, stride=SUBLANES), :] = y[:, sg * LANES:(sg + 1) * LANES]
        out_copy(i, slot).start()

        @pl.when(i == n_used - 1)
        def _():
            out_copy(i, slot).wait()

            @pl.when(i >= 1)
            def _():
                out_copy(jnp.maximum(i - 1, 0), 1 - slot).wait()


def _experts(blk_ea, blk_eb, n_real, n_used, xs, wgu, wd):
    n_blocks = blk_ea.shape[0]
    tile = EXPERT_ROWS * SUBLANES
    grid_spec = pltpu.PrefetchScalarGridSpec(
        num_scalar_prefetch=4,
        grid=(n_blocks,),
        in_specs=[
            pl.BlockSpec(memory_space=pl.ANY),
            pl.BlockSpec((1, D_MODEL, 2 * D_EXPERT), lambda i, ea, eb, nr, nu: (ea[i], 0, 0)),
            pl.BlockSpec((1, D_EXPERT, D_MODEL), lambda i, ea, eb, nr, nu: (ea[i], 0, 0)),
            pl.BlockSpec((1, D_MODEL, 2 * D_EXPERT), lambda i, ea, eb, nr, nu: (eb[i], 0, 0)),
            pl.BlockSpec((1, D_EXPERT, D_MODEL), lambda i, ea, eb, nr, nu: (eb[i], 0, 0)),
        ],
        out_specs=pl.BlockSpec(memory_space=pl.ANY),
        scratch_shapes=[
            pltpu.VMEM((2 * tile, LANES), jnp.uint32),
            pltpu.VMEM((2 * tile, LANES), F32),
            pltpu.SemaphoreType.DMA((2,)),
            pltpu.SemaphoreType.DMA((2,)),
        ],
    )
    return pl.pallas_call(
        _expert_kernel,
        grid_spec=grid_spec,
        out_shape=jax.ShapeDtypeStruct((n_blocks * tile, LANES), F32),
        compiler_params=_params("arbitrary"),
        name="experts",
    )(blk_ea, blk_eb, n_real, n_used, xs, wgu, wd, wgu, wd)


def _pair_tables():
    ea, eb = [], []
    for g in range(N_GROUPS):
        for a in range(EXPERTS_PER_GROUP):
            for b in range(a + 1, EXPERTS_PER_GROUP):
                ea.append(g * EXPERTS_PER_GROUP + a)
                eb.append(g * EXPERTS_PER_GROUP + b)
    return np.asarray(ea, np.int32), np.asarray(eb, np.int32)


def _plan_blocks(counts, n):
    cnt = counts[0, :N_CLASSES].astype(jnp.int32)
    padded = (cnt + EXPERT_ROWS - 1) // EXPERT_ROWS * EXPERT_ROWS
    pad_end = jnp.cumsum(padded)
    off = pad_end - padded
    n_blocks = -(-(n + N_CLASSES * (EXPERT_ROWS - 1)) // EXPERT_ROWS)
    blk_start = jnp.arange(n_blocks, dtype=jnp.int32) * EXPERT_ROWS
    blk_cls = jnp.minimum(jnp.searchsorted(pad_end, blk_start, side='right'), N_CLASSES - 1)
    n_real = jnp.clip(cnt[blk_cls] - (blk_start - off[blk_cls]), 0, EXPERT_ROWS).astype(jnp.int32)
    ea_tab, eb_tab = _pair_tables()
    n_used = (pad_end[-1] // EXPERT_ROWS).astype(jnp.int32).reshape(1)
    off = jnp.pad(off, (0, LANES - N_CLASSES)).astype(jnp.int32)
    return jnp.asarray(ea_tab)[blk_cls], jnp.asarray(eb_tab)[blk_cls], n_real, n_used, off, n_blocks


def _final_kernel(off_ref, cls_ref, rank_ref, ncls_ref, nrank_ref, h_ref, ys_hbm, g_ref, o_ref, ybuf, sem):
    y = _moe_rows(pl.program_id(0), pl.num_programs(0), off_ref, (cls_ref, rank_ref), (ncls_ref, nrank_ref),
                  ys_hbm, ybuf, sem)
    out = _rms(h_ref[...] + y, g_ref[...])
    o_ref[...] = out.reshape(o_ref.shape)


def _final(off, cls, rank, h, ys, g, bn):
    n_chunks, _, rows = cls.shape
    smem = lambda m: pl.BlockSpec((1, 1, rows), m, memory_space=pltpu.SMEM)
    cur = lambda c, off: (c + 1, 0, 0)
    nxt = lambda c, off: (jnp.minimum(c + 2, n_chunks - 1), 0, 0)
    grid_spec = pltpu.PrefetchScalarGridSpec(
        num_scalar_prefetch=1,
        grid=(n_chunks - 1,),
        in_specs=[
            smem(cur), smem(cur), smem(nxt), smem(nxt),
            pl.BlockSpec((rows, D_MODEL), lambda c, off: (c + 1, 0)),
            pl.BlockSpec(memory_space=pl.ANY),
            pl.BlockSpec((1, D_MODEL), lambda c, off: (0, 0)),
        ],
        out_specs=pl.BlockSpec((bn, 1, CHUNK, D_MODEL), lambda c, off: (0, c, 0, 0)),
        scratch_shapes=[pltpu.VMEM((2 * rows * SUBLANES, LANES), F32), pltpu.SemaphoreType.DMA((2,))],
    )
    return pl.pallas_call(
        _final_kernel,
        grid_spec=grid_spec,
        out_shape=jax.ShapeDtypeStruct((bn, n_chunks - 1, CHUNK, D_MODEL), F32),
        compiler_params=_params("arbitrary"),
        name="final_norm",
    )(off, cls, rank, cls, rank, h, ys, g)


def _repack_w_in(w):
    o = np.cumsum([0, 2 * M_WIDTH, M_WIDTH, M_WIDTH, M_HEADS, M_HEADS, G_K_WIDTH, G_K_WIDTH, G_WIDTH, G_WIDTH,
                   G_RANK])
    seg = lambda i: w[:, o[i]:o[i + 1]]
    small = jnp.concatenate([seg(3), seg(4), seg(9)], axis=1)
    small = jnp.pad(small, ((0, 0), (0, LANES - small.shape[1])))
    return jnp.concatenate([seg(0), seg(1), seg(2), seg(5), seg(6), seg(7), seg(8), small], axis=1).astype(BF16)


def kernel(x, meta_tokens, norm1_g, w_in, conv_w, conv_b, gate_b, m_norm_g, a_w2, a_b, g_norm_g, w_out, norm2_g,
           wr_g, br_g, wr_e, br_e, w_gate, w_up, w_down, final_norm_g):
    bn, seq, dm = x.shape
    depth = w_in.shape[0]
    assert dm == D_MODEL and seq % CHUNK == 0
    n_chunks = seq // CHUNK + 1
    rows = bn * CHUNK
    n = n_chunks * rows
    x4 = x.reshape(bn, seq // CHUNK, CHUNK, dm)
    row = lambda v: v.reshape(1, -1).astype(F32)

    h = ys = off = cls = rank = None
    for l in range(depth):
        if l == 0:
            h = _embed(x4, meta_tokens.astype(F32), n_chunks)
        else:
            h = _add_moe(off, cls, rank, h, ys)
        gb = gate_b[l].astype(F32).reshape(1, 2 * M_HEADS)
        wr = jnp.pad(jnp.concatenate([wr_g[l], wr_e[l]], axis=1).astype(F32),
                     ((0, 0), (0, LANES - N_GROUPS - N_EXPERTS)))
        wr_hi = wr.astype(BF16)
        wr_lo = (wr - wr_hi.astype(F32)).astype(BF16)
        br = jnp.pad(jnp.concatenate([br_g[l], br_e[l]]).astype(F32), (0, LANES - N_GROUPS - N_EXPERTS))
        h, xe, route, counts = _layer(
            h, row(norm1_g[l]), _repack_w_in(w_in[l]), conv_w[l].astype(F32), row(conv_b[l]),
            jnp.pad(gb, ((0, 0), (0, LANES - 2 * M_HEADS))),
            jnp.broadcast_to(gb.reshape(2 * M_HEADS, 1), (2 * M_HEADS, rows)),
            row(m_norm_g[l]), a_w2[l].astype(F32), row(a_b[l]), row(g_norm_g[l]),
            w_out[l].astype(BF16), row(norm2_g[l]), wr_hi, wr_lo, br.reshape(1, LANES), bn)
        blk_ea, blk_eb, n_real, n_used, off, n_blocks = _plan_blocks(counts, n)
        cls = route[:, 0].reshape(n_chunks, 1, rows)
        rank = route[:, 1].reshape(n_chunks, 1, rows)
        xs = _dispatch(off, cls, rank, xe.reshape(n, SUBLANES, LANES), n_blocks * EXPERT_ROWS)
        wgu = jnp.concatenate([w_gate[l], w_up[l]], axis=-1).astype(BF16)
        ys = _experts(blk_ea, blk_eb, n_real, n_used, xs.reshape(n_blocks * EXPERT_ROWS * SUBLANES, LANES), wgu,
                      w_down[l].astype(BF16))
    out = _final(off, cls, rank, h, ys, row(final_norm_g), bn)
    return out.reshape(bn, seq, dm)
```

```python
import functools

import numpy as np
import jax
import jax.numpy as jnp
from jax import lax
from jax.experimental import pallas as pl
from jax.experimental.pallas import tpu as pltpu

F32 = jnp.float32
BF16 = jnp.bfloat16
HIGHEST = lax.Precision.HIGHEST

D_MODEL = 1024
CHUNK = 64
CHUNK_LOG2 = CHUNK.bit_length() - 1
N_META = 16
LEAD_PAD = CHUNK - N_META
EPS = 1e-6
M_HEADS = 4
M_WIDTH = 512
M_HEAD_DIM = 128
CONV_K = 4
G_HEADS = 4
G_WIDTH = 512
G_V_DIM = 128
G_K_DIM = 64
G_K_WIDTH = 256
G_RANK = 16
G_TAU = 16.0
N_GROUPS = 4
EXPERTS_PER_GROUP = 8
N_EXPERTS = 32
D_EXPERT = 256
N_PAIRS = EXPERTS_PER_GROUP * (EXPERTS_PER_GROUP - 1) // 2
N_CLASSES = N_GROUPS * N_PAIRS

LANES = 128
SUBLANES = 8
EXPERT_ROWS = 128
VMEM_LIMIT = 56 * 1024 * 1024

Z_QM, Z_KM, Z_VM, Z_OM = 0, 512, 1024, 1536
Z_QG, Z_KG, Z_VG, Z_GG = 2048, 2304, 2560, 3072
Z_SMALL = 3584
Z_WIDTH = Z_SMALL + LANES
X_EXT = D_MODEL + LANES
GLA_SAFE_EXP = 80.0


def _log_sigmoid(x):
    return jnp.minimum(x, 0.0) - jnp.log(1.0 + jnp.exp(-jnp.abs(x)))


def _sigmoid(x):
    return 0.5 * jnp.tanh(0.5 * x) + 0.5


def _rms(x, g):
    return x * lax.rsqrt(jnp.mean(x * x, axis=-1, keepdims=True) + EPS) * g


def _dot(a, b):
    return jnp.dot(a.astype(BF16), b.astype(BF16), preferred_element_type=F32)


def _dot_nt(a, b):
    return lax.dot_general(a.astype(BF16), b.astype(BF16), (((1,), (1,)), ((), ())),
                           preferred_element_type=F32)


def _dot_f32(a, b):
    return jnp.dot(a, b, precision=HIGHEST, preferred_element_type=F32)


def _params(*semantics):
    return pltpu.CompilerParams(dimension_semantics=semantics, vmem_limit_bytes=VMEM_LIMIT)


def _embed_kernel(x_ref, meta_ref, h_ref):
    c = pl.program_id(0)
    bn = x_ref.shape[0]
    rows = bn * CHUNK

    @pl.when(c == 0)
    def _():
        lead = jnp.concatenate([jnp.zeros((LEAD_PAD, D_MODEL), F32), meta_ref[...]], axis=0)
        for b in range(bn):
            h_ref[b * CHUNK:(b + 1) * CHUNK, :] = lead

    @pl.when(c > 0)
    def _():
        h_ref[...] = x_ref[...].reshape(rows, D_MODEL)


def _embed(x4, meta, n_chunks):
    bn = x4.shape[0]
    rows = bn * CHUNK
    return pl.pallas_call(
        _embed_kernel,
        grid=(n_chunks,),
        in_specs=[
            pl.BlockSpec((bn, 1, CHUNK, D_MODEL), lambda c: (0, jnp.maximum(c - 1, 0), 0, 0)),
            pl.BlockSpec((N_META, D_MODEL), lambda c: (0, 0)),
        ],
        out_specs=pl.BlockSpec((rows, D_MODEL), lambda c: (c, 0)),
        out_shape=jax.ShapeDtypeStruct((n_chunks * rows, D_MODEL), F32),
        compiler_params=_params("parallel"),
        name="embed",
    )(x4, meta)


def _moe_rows(i, n_steps, off_ref, cur, nxt, ys_hbm, ybuf, sem):
    rows = cur[0].shape[-1]
    tile = rows * SUBLANES
    s = lax.rem(i, 2)

    def start(idx, slot):
        cls_ref, rank_ref = idx
        for k in range(rows):
            src = pl.multiple_of((off_ref[cls_ref[0, 0, k]] + rank_ref[0, 0, k]) * SUBLANES, SUBLANES)
            dst = pl.multiple_of(slot * tile + k * SUBLANES, SUBLANES)
            pltpu.make_async_copy(ys_hbm.at[pl.ds(src, SUBLANES), :], ybuf.at[pl.ds(dst, SUBLANES), :],
                                  sem.at[slot]).start()

    def wait(slot):
        pltpu.make_async_copy(ys_hbm.at[pl.ds(0, tile), :], ybuf.at[pl.ds(pl.multiple_of(slot * tile, tile), tile), :],
                              sem.at[slot]).wait()

    @pl.when(i == 0)
    def _():
        start(cur, 0)

    wait(s)
    start(nxt, 1 - s)

    @pl.when(i == n_steps - 1)
    def _():
        wait(1 - s)

    base = s * tile
    return jnp.concatenate([ybuf[pl.ds(base + sg, rows, stride=SUBLANES), :] for sg in range(SUBLANES)], axis=1)


def _add_moe_kernel(off_ref, cls_ref, rank_ref, ncls_ref, nrank_ref, h_ref, ys_hbm, ho_ref, ybuf, sem):
    y = _moe_rows(pl.program_id(0), pl.num_programs(0), off_ref, (cls_ref, rank_ref), (ncls_ref, nrank_ref),
                  ys_hbm, ybuf, sem)
    ho_ref[...] = h_ref[...] + y


def _add_moe(off, cls, rank, h, ys):
    n = h.shape[0]
    n_steps, _, rows = cls.shape
    smem = lambda m: pl.BlockSpec((1, 1, rows), m, memory_space=pltpu.SMEM)
    cur = lambda i, off: (i, 0, 0)
    nxt = lambda i, off: (jnp.minimum(i + 1, n_steps - 1), 0, 0)
    grid_spec = pltpu.PrefetchScalarGridSpec(
        num_scalar_prefetch=1,
        grid=(n_steps,),
        in_specs=[
            smem(cur), smem(cur), smem(nxt), smem(nxt),
            pl.BlockSpec((rows, D_MODEL), lambda i, off: (i, 0)),
            pl.BlockSpec(memory_space=pl.ANY),
        ],
        out_specs=pl.BlockSpec((rows, D_MODEL), lambda i, off: (i, 0)),
        scratch_shapes=[pltpu.VMEM((2 * rows * SUBLANES, LANES), F32), pltpu.SemaphoreType.DMA((2,))],
    )
    return pl.pallas_call(
        _add_moe_kernel,
        grid_spec=grid_spec,
        out_shape=jax.ShapeDtypeStruct((n, D_MODEL), F32),
        compiler_params=_params("arbitrary"),
        name="add_moe",
    )(off, cls, rank, cls, rank, h, ys)


def _gla_intra_pairwise(q, k, bg, kbuf, bbuf):
    kbuf[...] = k
    bbuf[...] = bg
    lane = lax.broadcasted_iota(jnp.int32, (CHUNK, CHUNK), 1)

    def column(s, a):
        ks = kbuf[pl.ds(s, 1), :]
        bs = bbuf[pl.ds(s, 1), :]
        col = jnp.sum(q * ks * jnp.exp(jnp.minimum(bg - bs, 0.0)), axis=-1, keepdims=True)
        return jnp.where(lane == s, col, a)

    return lax.fori_loop(0, CHUNK, column, jnp.zeros((CHUNK, CHUNK), F32))


def _gla_qkv(z_ref, rows, valid_c):
    q_g = jnp.where(valid_c, z_ref[rows, Z_QG:Z_QG + G_K_WIDTH], 0.0) * (G_K_DIM ** -0.5)
    k_g = jnp.where(valid_c, z_ref[rows, Z_KG:Z_KG + G_K_WIDTH], 0.0)
    v_g = jnp.where(valid_c, z_ref[rows, Z_VG:Z_VG + G_WIDTH], 0.0)
    return q_g, k_g, v_g


def _split3(x):
    hi = x.astype(BF16)
    r = x - hi.astype(F32)
    mid = r.astype(BF16)
    return hi, mid, (r - mid.astype(F32)).astype(BF16)


def _gla_gated(o, gate, g):
    return _rms(o, g) * (gate * _sigmoid(gate))


IN_PROJ_SPLITS = (0, 1280, 2560, Z_WIDTH)


def _layer_kernel(hc_ref, hn_ref, hp_ref, g1_ref, w_ref, cw_ref, cb_ref, gbr_ref, gbc_ref, mg_ref, aw_ref, ab_ref, gg_ref,
                  wo_ref, g2_ref, wrh_ref, wrl_ref, br_ref, h1_ref, xe_ref, route_ref, cnt_ref,
                  za_ref, zb_ref, ya_ref, yb_ref, tail_ref, c_ref, m_ref, s_ref, oi_ref, bg_ref, kbuf, bbuf, carry_ref):
    c = pl.program_id(0)

    @pl.when(c == 0)
    def _():
        tail_ref[...] = jnp.zeros_like(tail_ref)
        c_ref[...] = jnp.zeros_like(c_ref)
        m_ref[...] = jnp.zeros_like(m_ref)
        s_ref[...] = jnp.zeros_like(s_ref)
        carry_ref[...] = jnp.zeros_like(carry_ref)
        yb_ref[...] = jnp.zeros_like(yb_ref)
        za_ref[...] = _dot(_rms(hc_ref[...], g1_ref[...]), w_ref[...])

    args = (hn_ref, g1_ref, w_ref, cw_ref, cb_ref, gbr_ref, gbc_ref, mg_ref, aw_ref, ab_ref, gg_ref)
    state = (tail_ref, c_ref, m_ref, s_ref, oi_ref, bg_ref, kbuf, bbuf)
    even = lax.rem(c, 2) == 0

    def router(y_prev_ref):
        first = functools.partial(_router_logits, hp_ref, y_prev_ref, wo_ref, g2_ref, wrh_ref, wrl_ref, br_ref, h1_ref)
        second = functools.partial(_router_assign, c > 0, xe_ref, route_ref, cnt_ref, carry_ref)
        return first, second

    @pl.when(even)
    def _():
        _mixer_step(c, za_ref, zb_ref, *args, ya_ref, *state, router(yb_ref))

    @pl.when(jnp.logical_not(even))
    def _():
        _mixer_step(c, zb_ref, za_ref, *args, yb_ref, *state, router(ya_ref))


def _mixer_step(c, z_ref, zn_ref, hn_ref, g1_ref, w_ref, cw_ref, cb_ref, gbr_ref, gbc_ref, mg_ref, aw_ref, ab_ref,
                gg_ref, y_ref, tail_ref, c_ref, m_ref, s_ref, oi_ref, bg_ref, kbuf, bbuf, router):
    route_logits, route_assign = router
    bn = z_ref.shape[0] // CHUNK
    xn_next = _rms(hn_ref[...], g1_ref[...]).astype(BF16)

    def project_next(part):
        lo, hi = IN_PROJ_SPLITS[part], IN_PROJ_SPLITS[part + 1]
        zn_ref[:, lo:hi] = jnp.dot(xn_next, w_ref[:, lo:hi], preferred_element_type=F32)

    rows_all = bn * CHUNK
    row = lax.broadcasted_iota(jnp.int32, (CHUNK, CHUNK), 0)
    col = lax.broadcasted_iota(jnp.int32, (CHUNK, CHUNK), 1)
    causal = col <= row
    later = jnp.logical_not(c == 0)
    valid_c = jnp.logical_or(later, lax.broadcasted_iota(jnp.int32, (CHUNK, 1), 0) >= LEAD_PAD)
    pos_c = lax.broadcasted_iota(jnp.int32, (rows_all, 1), 0) & (CHUNK - 1)
    pos_r = lax.broadcasted_iota(jnp.int32, (1, rows_all), 1) & (CHUNK - 1)
    valid_ca = jnp.logical_or(later, pos_c >= LEAD_PAD)
    valid_ra = jnp.logical_or(later, pos_r >= LEAD_PAD)
    row8 = lax.broadcasted_iota(jnp.int32, (SUBLANES, 1), 0)
    ones_col = jnp.where(lax.broadcasted_iota(jnp.int32, (CHUNK, LANES), 1) == 0, 1.0, 0.0).astype(F32)
    neg_inf = -jnp.inf

    r_a = lax.broadcasted_iota(jnp.int32, (rows_all, rows_all), 0)
    c_a = lax.broadcasted_iota(jnp.int32, (rows_all, rows_all), 1)
    same = (r_a >> CHUNK_LOG2) == (c_a >> CHUNK_LOG2)
    tri = jnp.where(jnp.logical_and(same, c_a <= r_a), 1.0, 0.0).astype(BF16)
    tri_t = jnp.where(jnp.logical_and(same, r_a <= c_a), 1.0, 0.0).astype(BF16)
    zs = z_ref[:, Z_SMALL:Z_SMALL + LANES]
    g_col = zs + gbr_ref[...]
    g_row = zs.T[0:2 * M_HEADS, :] + gbc_ref[...]
    u = _dot_f32(zs[:, 2 * M_HEADS:2 * M_HEADS + G_RANK], aw_ref[...]) + ab_ref[...]
    project_next(0)
    la = jnp.where(valid_ca, _log_sigmoid(u) * (1.0 / G_TAU), 0.0)
    lf_col = jnp.where(valid_ca, _log_sigmoid(g_col), 0.0)
    cum = sum(jnp.dot(tri, p, preferred_element_type=F32) for p in _split3(jnp.concatenate([lf_col, la], axis=1)))
    b_col_all, bg_all = cum[:, :LANES], cum[:, LANES:]
    bg_ref[...] = bg_all
    lf_row = jnp.where(valid_ra, _log_sigmoid(g_row), 0.0)
    b_row_all = sum(jnp.dot(p, tri_t, preferred_element_type=F32) for p in _split3(lf_row))
    ig_col_all = jnp.where(valid_ca, g_col, neg_inf)
    ig_row_all = jnp.where(valid_ra, g_row, neg_inf)

    def mlstm_batch(b):
        rows = slice(b * CHUNK, (b + 1) * CHUNK)
        x = jnp.where(valid_c, z_ref[rows, Z_QM:Z_QM + 2 * M_WIDTH], 0.0)
        prev = tail_ref[b]
        acc = x * cw_ref[CONV_K - 1:CONV_K, :] + cb_ref[...]
        for k in range(1, CONV_K):
            cur = pltpu.roll(x, k, 0)
            fix = pltpu.roll(prev, k, 0)
            top = jnp.where(row8 < k, fix, cur[0:SUBLANES])
            shifted = jnp.concatenate([top, cur[SUBLANES:]], axis=0)
            acc = acc + shifted * cw_ref[CONV_K - 1 - k:CONV_K - k, :]
        tail_ref[b] = x[CHUNK - SUBLANES:]
        qk = acc * _sigmoid(acc)
        qk = jnp.where(valid_c, qk, 0.0)
        q_m = qk[:, :M_WIDTH]
        k_m = qk[:, M_WIDTH:] * (M_HEAD_DIM ** -0.5)
        v_m = jnp.where(valid_c, z_ref[rows, Z_VM:Z_VM + M_WIDTH], 0.0)
        o_m = z_ref[rows, Z_OM:Z_OM + M_WIDTH]

        b_col, ig_col = b_col_all[rows], ig_col_all[rows]
        b_row, ig_row = b_row_all[:, rows], ig_row_all[:, rows]

        q_bf, k_bf = q_m.astype(BF16), k_m.astype(BF16)
        heads = []
        for hd in range(M_HEADS):
            idx = b * M_HEADS + hd
            lanes = slice(hd * M_HEAD_DIM, (hd + 1) * M_HEAD_DIM)
            bc, br = b_col[:, M_HEADS + hd:M_HEADS + hd + 1], b_row[M_HEADS + hd:M_HEADS + hd + 1, :]
            igc, igr = ig_col[:, hd:hd + 1], ig_row[hd:hd + 1, :]
            m_prev = m_ref[idx][0:1, 0:1]
            dmat = jnp.where(causal, bc - br + igr, neg_inf)
            m_inter = bc + m_prev
            m_t = jnp.maximum(m_inter, jnp.max(dmat, axis=-1, keepdims=True))
            b_end = bc[CHUNK - 1:CHUNK, :]
            m_new = jnp.maximum(b_end + m_prev, jnp.max(b_end - br + igr, axis=-1, keepdims=True))
            wk = jnp.exp(b_end - bc + igc - m_new)
            heads.append(dict(
                idx=idx, lanes=lanes, q=q_bf[:, lanes], k=k_bf[:, lanes],
                kw_t=(k_m[:, lanes] * wk).T.astype(BF16),
                v_aug=jnp.concatenate([v_m[:, lanes], ones_col], axis=1).astype(BF16),
                e=jnp.exp(dmat - m_t), w_inter=jnp.exp(m_inter - m_t), floor=jnp.exp(-m_t),
                decay=jnp.exp(b_end + m_prev - m_new), m_new=m_new))
        return dict(rows=rows, heads=heads, o_m=o_m)

    def mlstm_first_dots(st):
        for hd in st['heads']:
            cst = c_ref[hd['idx']]
            hd['qk'] = _dot_nt(hd['q'], hd['k'])
            hd['inter'] = _dot(hd['q'], cst)
            c_ref[hd['idx']] = hd['decay'] * cst + _dot(hd['kw_t'], hd['v_aug'])
            m_ref[hd['idx']] = jnp.broadcast_to(hd['m_new'], (SUBLANES, LANES))

    def mlstm_finish(st):
        rows = st['rows']
        for hd in st['heads']:
            lanes = hd['lanes']
            nd = _dot(hd['qk'] * hd['e'], hd['v_aug']) + hd['w_inter'] * hd['inter']
            num, den = nd[:, :M_HEAD_DIM], nd[:, M_HEAD_DIM:M_HEAD_DIM + 1]
            hh = num / jnp.maximum(jnp.abs(den), hd['floor'])
            y = _rms(hh, mg_ref[:, lanes]) * _sigmoid(st['o_m'][:, lanes])
            y_ref[rows, lanes] = y.astype(y_ref.dtype)

    def gla_batch(b):
        rows = slice(b * CHUNK, (b + 1) * CHUNK)
        q_g, k_g, v_g = _gla_qkv(z_ref, rows, valid_c)
        bg = bg_all[rows]
        bg_t = bg.T
        b_end = bg[CHUNK - 1:CHUNK, :]
        q_dec = (q_g * jnp.exp(bg)).astype(BF16)
        k_inv = (k_g * jnp.exp(-bg)).astype(BF16)
        k_end = k_g * jnp.exp(b_end - bg)
        v_bf = v_g.astype(BF16)
        heads = []
        for hd in range(G_HEADS):
            kl = slice(hd * G_K_DIM, (hd + 1) * G_K_DIM)
            vl = slice(hd * G_V_DIM, (hd + 1) * G_V_DIM)
            heads.append(dict(
                idx=b * G_HEADS + hd, vl=vl, yl=slice(M_WIDTH + hd * G_V_DIM, M_WIDTH + (hd + 1) * G_V_DIM),
                q_dec=q_dec[:, kl], k_inv=k_inv[:, kl], k_end_t=k_end[:, kl].T.astype(BF16), v=v_bf[:, vl],
                e_end=jnp.exp(bg_t[hd * G_K_DIM:(hd + 1) * G_K_DIM, CHUNK - 1:CHUNK])))
        return dict(rows=rows, heads=heads, unsafe=jnp.max(-b_end) > GLA_SAFE_EXP)

    def gla_first_dots(st):
        for hd in st['heads']:
            state = s_ref[hd['idx']]
            hd['a'] = _dot_nt(hd['q_dec'], hd['k_inv'])
            hd['o_inter'] = _dot(hd['q_dec'], state)
            s_ref[hd['idx']] = hd['e_end'] * state + _dot(hd['k_end_t'], hd['v'])

    def gla_finish(st):
        rows = st['rows']
        gate_g = z_ref[rows, Z_GG:Z_GG + G_WIDTH]
        for hd in st['heads']:
            vl = hd['vl']
            oi_ref[rows, vl] = hd['o_inter']
            o = _dot(jnp.where(causal, hd['a'], 0.0), hd['v']) + hd['o_inter']
            y_ref[rows, hd['yl']] = _gla_gated(o, gate_g[:, vl], gg_ref[:, vl]).astype(y_ref.dtype)

    def gla_redo_batch(b, carry):
        rows = pl.ds(pl.multiple_of(b * CHUNK, CHUNK), CHUNK)
        q_g, k_g, v_g = _gla_qkv(z_ref, rows, valid_c)
        bg = bg_ref[rows, :]
        gate_g = z_ref[rows, Z_GG:Z_GG + G_WIDTH]
        for hd in range(G_HEADS):
            kl = slice(hd * G_K_DIM, (hd + 1) * G_K_DIM)
            vl = slice(hd * G_V_DIM, (hd + 1) * G_V_DIM)
            yl = slice(M_WIDTH + hd * G_V_DIM, M_WIDTH + (hd + 1) * G_V_DIM)
            a = jnp.where(causal, _gla_intra_pairwise(q_g[:, kl], k_g[:, kl], bg[:, kl], kbuf, bbuf), 0.0)
            y = _gla_gated(_dot(a, v_g[:, vl]) + oi_ref[rows, vl], gate_g[:, vl], gg_ref[:, vl])
            y_ref[rows, yl] = y.astype(y_ref.dtype)
        return carry

    project_next(1)
    routed = route_logits()
    mlstm = [mlstm_batch(b) for b in range(bn)]
    gla = [gla_batch(b) for b in range(bn)]
    for b in range(bn):
        mlstm_first_dots(mlstm[b])
        gla_first_dots(gla[b])
    project_next(2)
    route_assign(*routed)
    for b in range(bn):
        mlstm_finish(mlstm[b])
        gla_finish(gla[b])
    unsafe = functools.reduce(jnp.logical_or, [st['unsafe'] for st in gla])

    @pl.when(unsafe)
    def _():
        lax.fori_loop(0, bn, gla_redo_batch, 0)


def _layer(h, norm_g, w_in, conv_w, conv_b, gb_row, gb_col, m_norm_g, a_w2, a_b, g_norm_g,
           w_out, norm2_g, wr_hi, wr_lo, br, bn):
    rows = bn * CHUNK
    n = h.shape[0]
    n_chunks = n // rows
    last = n_chunks - 1
    full = lambda shape: pl.BlockSpec(shape, lambda c: (0,) * len(shape))
    h_at = lambda m: pl.BlockSpec((rows, D_MODEL), m)
    prev = lambda c: (jnp.maximum(c - 1, 0), 0)
    return pl.pallas_call(
        _layer_kernel,
        grid=(n_chunks + 1,),
        in_specs=[
            h_at(lambda c: (jnp.minimum(c, last), 0)),
            h_at(lambda c: (jnp.minimum(c + 1, last), 0)),
            h_at(prev),
            full((1, D_MODEL)), full((D_MODEL, Z_WIDTH)),
            full((CONV_K, 2 * M_WIDTH)), full((1, 2 * M_WIDTH)),
            full((1, LANES)), full((2 * M_HEADS, rows)),
            full((1, M_WIDTH)), full((G_RANK, G_K_WIDTH)), full((1, G_K_WIDTH)), full((1, G_WIDTH)),
            full((D_MODEL, D_MODEL)), full((1, D_MODEL)), full((D_MODEL, LANES)), full((D_MODEL, LANES)),
            full((1, LANES)),
        ],
        out_specs=[
            h_at(prev),
            pl.BlockSpec((rows * SUBLANES, LANES), prev),
            pl.BlockSpec((rows, LANES), prev),
            pl.BlockSpec((1, LANES), lambda c: (0, 0)),
        ],
        out_shape=[
            jax.ShapeDtypeStruct((n, D_MODEL), F32),
            jax.ShapeDtypeStruct((n * SUBLANES, LANES), jnp.uint32),
            jax.ShapeDtypeStruct((n, LANES), jnp.int32),
            jax.ShapeDtypeStruct((1, LANES), F32),
        ],
        scratch_shapes=[
            pltpu.VMEM((rows, Z_WIDTH), F32), pltpu.VMEM((rows, Z_WIDTH), F32),
            pltpu.VMEM((rows, D_MODEL), BF16), pltpu.VMEM((rows, D_MODEL), BF16),
            pltpu.VMEM((bn, SUBLANES, 2 * M_WIDTH), F32),
            pltpu.VMEM((bn * M_HEADS, M_HEAD_DIM, 2 * M_HEAD_DIM), F32),
            pltpu.VMEM((bn * M_HEADS, SUBLANES, LANES), F32),
            pltpu.VMEM((bn * G_HEADS, G_K_DIM, G_V_DIM), F32),
            pltpu.VMEM((rows, G_WIDTH), F32),
            pltpu.VMEM((rows, G_K_WIDTH), F32),
            pltpu.VMEM((CHUNK, G_K_DIM), F32), pltpu.VMEM((CHUNK, G_K_DIM), F32),
            pltpu.VMEM((1, LANES), F32),
        ],
        compiler_params=_params("arbitrary"),
        name="layer",
    )(h, h, h, norm_g, w_in, conv_w, conv_b, gb_row, gb_col, m_norm_g, a_w2, a_b, g_norm_g,
      w_out, norm2_g, wr_hi, wr_lo, br)


def _router_logits(h_ref, y_ref, wo_ref, g_ref, wrh_ref, wrl_ref, br_ref, h1_ref):
    h1 = h_ref[...] + jnp.dot(y_ref[...], wo_ref[...], preferred_element_type=F32)
    h1_ref[...] = h1
    xn = _rms(h1, g_ref[...])
    x_hi = xn.astype(BF16)
    x_lo = (xn - x_hi.astype(F32)).astype(BF16)
    dot = functools.partial(jnp.dot, preferred_element_type=F32)
    logits = dot(x_hi, wrh_ref[...]) + dot(x_hi, wrl_ref[...]) + dot(x_lo, wrh_ref[...]) + br_ref[...]
    return xn, logits


def _router_assign(live, xe_ref, route_ref, cnt_ref, carry_ref, xn, logits):
    rows = xn.shape[0]
    lane = lax.broadcasted_iota(jnp.int32, (rows, LANES), 1)
    neg_inf = -jnp.inf
    big = jnp.int32(LANES)

    def first_argmax(vals):
        top = jnp.max(vals, axis=-1, keepdims=True)
        return top, jnp.min(jnp.where(vals == top, lane, big), axis=-1, keepdims=True)

    lg = jnp.where(lane < N_GROUPS, logits, neg_inf)
    g_max, g_sel = first_argmax(lg)
    p_sel = 1.0 / jnp.sum(jnp.exp(lg - g_max), axis=-1, keepdims=True)
    in_group = jnp.logical_and(lane >= N_GROUPS + EXPERTS_PER_GROUP * g_sel,
                               lane < N_GROUPS + EXPERTS_PER_GROUP * (g_sel + 1))
    le = jnp.where(in_group, logits, neg_inf)
    v1, i1 = first_argmax(le)
    v2, i2 = first_argmax(jnp.where(lane == i1, neg_inf, le))
    e21 = jnp.exp(v2 - v1)
    gate1 = p_sel / (1.0 + e21)
    gate2 = p_sel * e21 / (1.0 + e21)
    j1 = i1 - N_GROUPS - EXPERTS_PER_GROUP * g_sel
    j2 = i2 - N_GROUPS - EXPERTS_PER_GROUP * g_sel
    ja, jb = jnp.minimum(j1, j2), jnp.maximum(j1, j2)
    ga = jnp.where(j1 < j2, gate1, gate2)
    gb = jnp.where(j1 < j2, gate2, gate1)
    pair = ((ja * (2 * EXPERTS_PER_GROUP - 1 - ja)) >> 1) + (jb - ja - 1)
    cls = g_sel * N_PAIRS + pair

    bits = lax.bitcast_convert_type(xn.astype(jnp.bfloat16).astype(F32), jnp.uint32)
    half = D_MODEL // 2
    packed = bits[:, half:] | (bits[:, :half] >> 16)
    gates = lax.bitcast_convert_type(jnp.where(lane == 0, ga, jnp.where(lane == 1, gb, 0.0)), jnp.uint32)
    for sg in range(SUBLANES):
        if sg < half // LANES:
            val = packed[:, sg * LANES:(sg + 1) * LANES]
        else:
            val = gates if sg == half // LANES else jnp.zeros((rows, LANES), jnp.uint32)
        xe_ref[pl.ds(sg, rows, stride=SUBLANES), :] = val

    onehot = jnp.where(jnp.logical_and(live, lane == cls), 1.0, 0.0).astype(F32)
    r_i = lax.broadcasted_iota(jnp.int32, (rows, rows), 0)
    c_i = lax.broadcasted_iota(jnp.int32, (rows, rows), 1)
    incl = jnp.where(c_i <= r_i, 1.0, 0.0).astype(BF16)
    prefix = jnp.dot(incl, onehot.astype(BF16), preferred_element_type=F32)
    rank = jnp.sum(onehot * (prefix - 1.0 + carry_ref[...]), axis=-1, keepdims=True)
    carry_ref[...] = carry_ref[...] + prefix[rows - 1:rows, :]
    cnt_ref[...] = carry_ref[...]
    route_ref[...] = jnp.where(lane == 0, cls, jnp.where(lane == 1, rank.astype(jnp.int32), 0))


def _dispatch_kernel(off_ref, gap_ref, used_ref, cls_ref, rank_ref, x_ref, xs_hbm, stage, zeros, sem, fill_sem):
    i = pl.program_id(0)
    rows = cls_ref.shape[-1]
    s = lax.rem(i, 2)
    n_slots = xs_hbm.shape[0]

    def wait(slot):
        pltpu.make_async_copy(stage.at[slot], xs_hbm.at[pl.ds(0, rows)], sem.at[slot]).wait()

    def for_gaps(act):
        def class_gap(cl, carry):
            size = gap_ref[cl]

            @pl.when(size > 0)
            def _():
                act(pltpu.make_async_copy(zeros.at[pl.ds(0, size)],
                                          xs_hbm.at[pl.ds(off_ref[cl] + used_ref[cl], size)], fill_sem))
            return carry
        lax.fori_loop(0, N_CLASSES, class_gap, 0)

        def tail_block(b, carry):
            act(pltpu.make_async_copy(zeros, xs_hbm.at[pl.ds(b * EXPERT_ROWS, EXPERT_ROWS)], fill_sem))
            return carry
        lax.fori_loop(used_ref[N_CLASSES], n_slots // EXPERT_ROWS, tail_block, 0)

    @pl.when(i == 0)
    def _():
        zeros[...] = jnp.zeros_like(zeros)
        for_gaps(lambda copy: copy.start())

    @pl.when(i >= 2)
    def _():
        wait(s)

    stage[s] = x_ref[...]
    for k in range(rows):
        dst = off_ref[cls_ref[0, 0, k]] + rank_ref[0, 0, k]
        pltpu.make_async_copy(stage.at[s].at[k], xs_hbm.at[dst], sem.at[s]).start()

    @pl.when(i == pl.num_programs(0) - 1)
    def _():
        wait(s)

        @pl.when(i >= 1)
        def _():
            wait(1 - s)
        for_gaps(lambda copy: copy.wait())


def _dispatch(off, gap, used, cls, rank, xe, n_slots):
    n_steps, _, rows = cls.shape
    smem = pl.BlockSpec((1, 1, rows), lambda i, *_: (i, 0, 0), memory_space=pltpu.SMEM)
    grid_spec = pltpu.PrefetchScalarGridSpec(
        num_scalar_prefetch=3,
        grid=(n_steps,),
        in_specs=[smem, smem, pl.BlockSpec((rows, SUBLANES, LANES), lambda i, *_: (i, 0, 0))],
        out_specs=pl.BlockSpec(memory_space=pl.ANY),
        scratch_shapes=[pltpu.VMEM((2, rows, SUBLANES, LANES), xe.dtype),
                        pltpu.VMEM((EXPERT_ROWS, SUBLANES, LANES), xe.dtype),
                        pltpu.SemaphoreType.DMA((2,)), pltpu.SemaphoreType.DMA(())],
    )
    return pl.pallas_call(
        _dispatch_kernel,
        grid_spec=grid_spec,
        out_shape=jax.ShapeDtypeStruct((n_slots, SUBLANES, LANES), xe.dtype),
        compiler_params=_params("arbitrary"),
        name="dispatch",
    )(off, gap, used, cls, rank, xe)


BLOCKS_PER_STEP = 2


def _expert_kernel(ea_ref, eb_ref, nused_ref, xs_hbm, *refs):
    weights = refs[:4 * BLOCKS_PER_STEP]
    ys_hbm, xbuf, obuf, sem_in, sem_out = refs[4 * BLOCKS_PER_STEP:]
    i = pl.program_id(0)
    steps_used = (nused_ref[0] + BLOCKS_PER_STEP - 1) // BLOCKS_PER_STEP
    slot = lax.rem(i, 2)
    block = EXPERT_ROWS * SUBLANES
    tile = BLOCKS_PER_STEP * block

    def in_copy(step, s):
        return pltpu.make_async_copy(xs_hbm.at[pl.ds(pl.multiple_of(step * tile, tile), tile), :],
                                     xbuf.at[pl.ds(pl.multiple_of(s * tile, tile), tile), :], sem_in.at[s])

    def out_copy(step, s):
        return pltpu.make_async_copy(obuf.at[pl.ds(pl.multiple_of(s * tile, tile), tile), :],
                                     ys_hbm.at[pl.ds(pl.multiple_of(step * tile, tile), tile), :], sem_out.at[s])

    @pl.when(i < steps_used)
    def _():
        @pl.when(i == 0)
        def _():
            in_copy(0, 0).start()

        in_copy(i, slot).wait()

        @pl.when(i + 1 < steps_used)
        def _():
            in_copy(i + 1, 1 - slot).start()

        @pl.when(i >= 2)
        def _():
            out_copy(jnp.maximum(i - 2, 0), slot).wait()

        as_f32 = lambda w: lax.bitcast_convert_type(w, F32)
        for blk in range(BLOCKS_PER_STEP):
            wgu_a, wd_a, wgu_b, wd_b = weights[4 * blk:4 * blk + 4]
            base = slot * tile + blk * block
            seg = lambda sg: xbuf[pl.ds(base + sg, EXPERT_ROWS, stride=SUBLANES), :]
            words = [seg(sg) for sg in range(D_MODEL // 2 // LANES)]
            xb = jnp.concatenate([as_f32(w << 16) for w in words]
                                 + [as_f32(w & jnp.uint32(0xFFFF0000)) for w in words], axis=1).astype(BF16)
            gates = as_f32(seg(D_MODEL // 2 // LANES))
            ga, gb = gates[:, 0:1], gates[:, 1:2]

            def mlp(wgu, wd):
                gu = jnp.dot(xb, wgu[0], preferred_element_type=F32)
                gate, up = gu[:, :D_EXPERT], gu[:, D_EXPERT:]
                return jnp.dot((gate * _sigmoid(gate) * up).astype(BF16), wd[0], preferred_element_type=F32)

            y = ga * mlp(wgu_a, wd_a) + gb * mlp(wgu_b, wd_b)
            for sg in range(SUBLANES):
                obuf[pl.ds(base + sg, EXPERT_ROWS, stride=SUBLANES), :] = y[:, sg * LANES:(sg + 1) * LANES]
        out_copy(i, slot).start()

        @pl.when(i == steps_used - 1)
        def _():
            out_copy(i, slot).wait()

            @pl.when(i >= 1)
            def _():
                out_copy(jnp.maximum(i - 1, 0), 1 - slot).wait()

    @pl.when(i == steps_used)
    def _():
        obuf[0:tile, :] = jnp.zeros((tile, LANES), F32)
        n_steps = pl.num_programs(0)

        def each(act):
            def body(step, carry):
                act(out_copy(step, 0))
                return carry
            lax.fori_loop(steps_used, n_steps, body, 0)
        each(lambda copy: copy.start())
        each(lambda copy: copy.wait())


def _experts(blk_ea, blk_eb, n_used, xs, wgu, wd):
    n_blocks = blk_ea.shape[0]
    tile = BLOCKS_PER_STEP * EXPERT_ROWS * SUBLANES
    weight_specs = []
    for blk in range(BLOCKS_PER_STEP):
        at = lambda tab, blk=blk: (lambda i, ea, eb, nu: ((ea, eb)[tab][BLOCKS_PER_STEP * i + blk], 0, 0))
        for tab in (0, 1):
            weight_specs += [pl.BlockSpec((1, D_MODEL, 2 * D_EXPERT), at(tab)),
                             pl.BlockSpec((1, D_EXPERT, D_MODEL), at(tab))]
    grid_spec = pltpu.PrefetchScalarGridSpec(
        num_scalar_prefetch=3,
        grid=(n_blocks // BLOCKS_PER_STEP,),
        in_specs=[pl.BlockSpec(memory_space=pl.ANY)] + weight_specs,
        out_specs=pl.BlockSpec(memory_space=pl.ANY),
        scratch_shapes=[
            pltpu.VMEM((2 * tile, LANES), jnp.uint32),
            pltpu.VMEM((2 * tile, LANES), F32),
            pltpu.SemaphoreType.DMA((2,)),
            pltpu.SemaphoreType.DMA((2,)),
        ],
    )
    return pl.pallas_call(
        _expert_kernel,
        grid_spec=grid_spec,
        out_shape=jax.ShapeDtypeStruct((n_blocks * EXPERT_ROWS * SUBLANES, LANES), F32),
        compiler_params=_params("arbitrary"),
        name="experts",
    )(blk_ea, blk_eb, n_used, xs, *([wgu, wd] * (2 * BLOCKS_PER_STEP)))


def _pair_tables():
    ea, eb = [], []
    for g in range(N_GROUPS):
        for a in range(EXPERTS_PER_GROUP):
            for b in range(a + 1, EXPERTS_PER_GROUP):
                ea.append(g * EXPERTS_PER_GROUP + a)
                eb.append(g * EXPERTS_PER_GROUP + b)
    return np.asarray(ea, np.int32), np.asarray(eb, np.int32)


def _plan_blocks(counts, n):
    cnt = counts[0, :N_CLASSES].astype(jnp.int32)
    padded = (cnt + EXPERT_ROWS - 1) // EXPERT_ROWS * EXPERT_ROWS
    pad_end = jnp.cumsum(padded)
    off = pad_end - padded
    n_blocks = -(-(n + N_CLASSES * (EXPERT_ROWS - 1)) // EXPERT_ROWS)
    n_blocks = -(-n_blocks // BLOCKS_PER_STEP) * BLOCKS_PER_STEP
    blk_start = jnp.arange(n_blocks, dtype=jnp.int32) * EXPERT_ROWS
    blk_cls = jnp.minimum(jnp.sum(pad_end[None, :] <= blk_start[:, None], axis=1), N_CLASSES - 1)
    ea_tab, eb_tab = _pair_tables()
    n_used = (pad_end[-1] // EXPERT_ROWS).astype(jnp.int32).reshape(1)
    used = jnp.concatenate([cnt, n_used])
    return (jnp.asarray(ea_tab)[blk_cls], jnp.asarray(eb_tab)[blk_cls], n_used, off.astype(jnp.int32),
            (padded - cnt).astype(jnp.int32), used.astype(jnp.int32), n_blocks)


def _final_kernel(off_ref, cls_ref, rank_ref, ncls_ref, nrank_ref, h_ref, ys_hbm, g_ref, o_ref, ybuf, sem):
    y = _moe_rows(pl.program_id(0), pl.num_programs(0), off_ref, (cls_ref, rank_ref), (ncls_ref, nrank_ref),
                  ys_hbm, ybuf, sem)
    out = _rms(h_ref[...] + y, g_ref[...])
    o_ref[...] = out.reshape(o_ref.shape)


def _final(off, cls, rank, h, ys, g, bn):
    n_chunks, _, rows = cls.shape
    smem = lambda m: pl.BlockSpec((1, 1, rows), m, memory_space=pltpu.SMEM)
    cur = lambda c, off: (c + 1, 0, 0)
    nxt = lambda c, off: (jnp.minimum(c + 2, n_chunks - 1), 0, 0)
    grid_spec = pltpu.PrefetchScalarGridSpec(
        num_scalar_prefetch=1,
        grid=(n_chunks - 1,),
        in_specs=[
            smem(cur), smem(cur), smem(nxt), smem(nxt),
            pl.BlockSpec((rows, D_MODEL), lambda c, off: (c + 1, 0)),
            pl.BlockSpec(memory_space=pl.ANY),
            pl.BlockSpec((1, D_MODEL), lambda c, off: (0, 0)),
        ],
        out_specs=pl.BlockSpec((bn, 1, CHUNK, D_MODEL), lambda c, off: (0, c, 0, 0)),
        scratch_shapes=[pltpu.VMEM((2 * rows * SUBLANES, LANES), F32), pltpu.SemaphoreType.DMA((2,))],
    )
    return pl.pallas_call(
        _final_kernel,
        grid_spec=grid_spec,
        out_shape=jax.ShapeDtypeStruct((bn, n_chunks - 1, CHUNK, D_MODEL), F32),
        compiler_params=_params("arbitrary"),
        name="final_norm",
    )(off, cls, rank, cls, rank, h, ys, g)


def _repack_w_in(w):
    o = np.cumsum([0, 2 * M_WIDTH, M_WIDTH, M_WIDTH, M_HEADS, M_HEADS, G_K_WIDTH, G_K_WIDTH, G_WIDTH, G_WIDTH,
                   G_RANK])
    seg = lambda i: w[:, o[i]:o[i + 1]]
    small = jnp.concatenate([seg(3), seg(4), seg(9)], axis=1)
    small = jnp.pad(small, ((0, 0), (0, LANES - small.shape[1])))
    return jnp.concatenate([seg(0), seg(1), seg(2), seg(5), seg(6), seg(7), seg(8), small], axis=1).astype(BF16)


def kernel(x, meta_tokens, norm1_g, w_in, conv_w, conv_b, gate_b, m_norm_g, a_w2, a_b, g_norm_g, w_out, norm2_g,
           wr_g, br_g, wr_e, br_e, w_gate, w_up, w_down, final_norm_g):
    bn, seq, dm = x.shape
    depth = w_in.shape[0]
    assert dm == D_MODEL and seq % CHUNK == 0
    n_chunks = seq // CHUNK + 1
    rows = bn * CHUNK
    n = n_chunks * rows
    x4 = x.reshape(bn, seq // CHUNK, CHUNK, dm)
    row = lambda v: v.reshape(1, -1).astype(F32)

    h = ys = off = cls = rank = None
    for l in range(depth):
        if l == 0:
            h = _embed(x4, meta_tokens.astype(F32), n_chunks)
        else:
            h = _add_moe(off, cls, rank, h, ys)
        gb = gate_b[l].astype(F32).reshape(1, 2 * M_HEADS)
        wr = jnp.pad(jnp.concatenate([wr_g[l], wr_e[l]], axis=1).astype(F32),
                     ((0, 0), (0, LANES - N_GROUPS - N_EXPERTS)))
        wr_hi = wr.astype(BF16)
        wr_lo = (wr - wr_hi.astype(F32)).astype(BF16)
        br = jnp.pad(jnp.concatenate([br_g[l], br_e[l]]).astype(F32), (0, LANES - N_GROUPS - N_EXPERTS))
        h, xe, route, counts = _layer(
            h, row(norm1_g[l]), _repack_w_in(w_in[l]), conv_w[l].astype(F32), row(conv_b[l]),
            jnp.pad(gb, ((0, 0), (0, LANES - 2 * M_HEADS))),
            jnp.broadcast_to(gb.reshape(2 * M_HEADS, 1), (2 * M_HEADS, rows)),
            row(m_norm_g[l]), a_w2[l].astype(F32), row(a_b[l]), row(g_norm_g[l]),
            w_out[l].astype(BF16), row(norm2_g[l]), wr_hi, wr_lo, br.reshape(1, LANES), bn)
        blk_ea, blk_eb, n_used, off, gap, used, n_blocks = _plan_blocks(counts, n)
        cls = route[:, 0].reshape(n_chunks, 1, rows)
        rank = route[:, 1].reshape(n_chunks, 1, rows)
        xs = _dispatch(off, gap, used, cls, rank, xe.reshape(n, SUBLANES, LANES), n_blocks * EXPERT_ROWS)
        wgu = jnp.concatenate([w_gate[l], w_up[l]], axis=-1).astype(BF16)
        ys = _experts(blk_ea, blk_eb, n_used, xs.reshape(n_blocks * EXPERT_ROWS * SUBLANES, LANES), wgu,
                      w_down[l].astype(BF16))
    out = _final(off, cls, rank, h, ys, row(final_norm_g), bn)
    return out.reshape(bn, seq, dm)
```

```python
import functools

import numpy as np
import jax
import jax.numpy as jnp
from jax import lax
from jax.experimental import pallas as pl
from jax.experimental.pallas import tpu as pltpu

F32 = jnp.float32
BF16 = jnp.bfloat16
HIGHEST = lax.Precision.HIGHEST

D_MODEL = 1024
CHUNK = 64
CHUNK_LOG2 = CHUNK.bit_length() - 1
N_META = 16
LEAD_PAD = CHUNK - N_META
EPS = 1e-6
M_HEADS = 4
M_WIDTH = 512
M_HEAD_DIM = 128
CONV_K = 4
G_HEADS = 4
G_WIDTH = 512
G_V_DIM = 128
G_K_DIM = 64
G_K_WIDTH = 256
G_RANK = 16
G_TAU = 16.0
N_GROUPS = 4
EXPERTS_PER_GROUP = 8
N_EXPERTS = 32
D_EXPERT = 256
N_PAIRS = EXPERTS_PER_GROUP * (EXPERTS_PER_GROUP - 1) // 2
N_CLASSES = N_GROUPS * N_PAIRS

LANES = 128
SUBLANES = 8
EXPERT_ROWS = 128
VMEM_LIMIT = 56 * 1024 * 1024

Z_QM, Z_KM, Z_VM, Z_OM = 0, 512, 1024, 1536
Z_QG, Z_KG, Z_VG, Z_GG = 2048, 2304, 2560, 3072
Z_SMALL = 3584
Z_WIDTH = Z_SMALL + LANES
X_EXT = D_MODEL + LANES
GLA_SAFE_EXP = 80.0


def _log_sigmoid(x):
    return jnp.minimum(x, 0.0) - jnp.log(1.0 + jnp.exp(-jnp.abs(x)))


def _sigmoid(x):
    return 0.5 * jnp.tanh(0.5 * x) + 0.5


def _rms(x, g):
    return x * lax.rsqrt(jnp.mean(x * x, axis=-1, keepdims=True) + EPS) * g


def _dot(a, b):
    return jnp.dot(a.astype(BF16), b.astype(BF16), preferred_element_type=F32)


def _dot_nt(a, b):
    return lax.dot_general(a.astype(BF16), b.astype(BF16), (((1,), (1,)), ((), ())),
                           preferred_element_type=F32)


def _dot_f32(a, b):
    return jnp.dot(a, b, precision=HIGHEST, preferred_element_type=F32)


def _params(*semantics):
    return pltpu.CompilerParams(dimension_semantics=semantics, vmem_limit_bytes=VMEM_LIMIT)


def _moe_rows(i, n_steps, off_ref, cur, nxt, ys_hbm, ybuf, sem):
    rows = cur[0].shape[-1]
    tile = rows * SUBLANES
    s = lax.rem(i, 2)

    def start(idx, slot):
        cls_ref, rank_ref = idx
        for k in range(rows):
            src = pl.multiple_of((off_ref[cls_ref[0, 0, k]] + rank_ref[0, 0, k]) * SUBLANES, SUBLANES)
            dst = pl.multiple_of(slot * tile + k * SUBLANES, SUBLANES)
            pltpu.make_async_copy(ys_hbm.at[pl.ds(src, SUBLANES), :], ybuf.at[pl.ds(dst, SUBLANES), :],
                                  sem.at[slot]).start(priority=k % 2)

    def wait(slot):
        pltpu.make_async_copy(ys_hbm.at[pl.ds(0, tile), :], ybuf.at[pl.ds(pl.multiple_of(slot * tile, tile), tile), :],
                              sem.at[slot]).wait()

    @pl.when(i == 0)
    def _():
        start(cur, 0)

    wait(s)
    start(nxt, 1 - s)

    @pl.when(i == n_steps - 1)
    def _():
        wait(1 - s)

    base = s * tile
    return jnp.concatenate([ybuf[pl.ds(base + sg, rows, stride=SUBLANES), :] for sg in range(SUBLANES)], axis=1)


def _add_moe_kernel(off_ref, cls_ref, rank_ref, ncls_ref, nrank_ref, h_ref, ys_hbm, ho_ref, ybuf, sem):
    y = _moe_rows(pl.program_id(0), pl.num_programs(0), off_ref, (cls_ref, rank_ref), (ncls_ref, nrank_ref),
                  ys_hbm, ybuf, sem)
    ho_ref[...] = h_ref[...] + y


def _add_moe(off, cls, rank, h, ys):
    n = h.shape[0]
    n_steps, _, rows = cls.shape
    smem = lambda m: pl.BlockSpec((1, 1, rows), m, memory_space=pltpu.SMEM)
    cur = lambda i, off: (i, 0, 0)
    nxt = lambda i, off: (jnp.minimum(i + 1, n_steps - 1), 0, 0)
    grid_spec = pltpu.PrefetchScalarGridSpec(
        num_scalar_prefetch=1,
        grid=(n_steps,),
        in_specs=[
            smem(cur), smem(cur), smem(nxt), smem(nxt),
            pl.BlockSpec((rows, D_MODEL), lambda i, off: (i, 0)),
            pl.BlockSpec(memory_space=pl.ANY),
        ],
        out_specs=pl.BlockSpec((rows, D_MODEL), lambda i, off: (i, 0)),
        scratch_shapes=[pltpu.VMEM((2 * rows * SUBLANES, LANES), F32), pltpu.SemaphoreType.DMA((2,))],
    )
    return pl.pallas_call(
        _add_moe_kernel,
        grid_spec=grid_spec,
        out_shape=jax.ShapeDtypeStruct((n, D_MODEL), F32),
        compiler_params=_params("arbitrary"),
        name="add_moe",
    )(off, cls, rank, cls, rank, h, ys)


def _gla_intra_pairwise(q, k, bg, kbuf, bbuf):
    kbuf[...] = k
    bbuf[...] = bg
    lane = lax.broadcasted_iota(jnp.int32, (CHUNK, CHUNK), 1)

    def column(s, a):
        ks = kbuf[pl.ds(s, 1), :]
        bs = bbuf[pl.ds(s, 1), :]
        col = jnp.sum(q * ks * jnp.exp(jnp.minimum(bg - bs, 0.0)), axis=-1, keepdims=True)
        return jnp.where(lane == s, col, a)

    return lax.fori_loop(0, CHUNK, column, jnp.zeros((CHUNK, CHUNK), F32))


def _gla_qkv(z_ref, rows, valid_c):
    q_g = jnp.where(valid_c, z_ref[rows, Z_QG:Z_QG + G_K_WIDTH], 0.0) * (G_K_DIM ** -0.5)
    k_g = jnp.where(valid_c, z_ref[rows, Z_KG:Z_KG + G_K_WIDTH], 0.0)
    v_g = jnp.where(valid_c, z_ref[rows, Z_VG:Z_VG + G_WIDTH], 0.0)
    return q_g, k_g, v_g


def _split3(x):
    hi = x.astype(BF16)
    r = x - hi.astype(F32)
    mid = r.astype(BF16)
    return hi, mid, (r - mid.astype(F32)).astype(BF16)


def _gla_gated(o, gate, g):
    return _rms(o, g) * (gate * _sigmoid(gate))


IN_PROJ_SPLITS = (0, 1280, 2560, Z_WIDTH)


def _layer_kernel(embed, s0_ref, s1_ref, s2_ref, g1_ref, w_ref, cw_ref, cb_ref, gbr_ref, gbc_ref, mg_ref, aw_ref,
                  ab_ref, gg_ref, wo_ref, g2_ref, wrh_ref, wrl_ref, br_ref, h1_ref, xe_ref, route_ref, cnt_ref,
                  za_ref, zb_ref, ya_ref, yb_ref, tail_ref, c_ref, m_ref, s_ref, oi_ref, bg_ref, kbuf, bbuf, carry_ref):
    c = pl.program_id(0)
    rows = za_ref.shape[0]
    if embed:
        lead = jnp.concatenate([jnp.zeros((LEAD_PAD, D_MODEL), F32), s0_ref[...]] * (rows // CHUNK), axis=0)
        read_first = lambda: lead
        read_next = lambda: s1_ref[...].reshape(rows, D_MODEL)
        read_prev = lambda: jnp.where(c <= 1, lead, s2_ref[...].reshape(rows, D_MODEL))
    else:
        read_first, read_next, read_prev = (lambda: s0_ref[...]), (lambda: s1_ref[...]), (lambda: s2_ref[...])

    @pl.when(c == 0)
    def _():
        tail_ref[...] = jnp.zeros_like(tail_ref)
        c_ref[...] = jnp.zeros_like(c_ref)
        m_ref[...] = jnp.zeros_like(m_ref)
        s_ref[...] = jnp.zeros_like(s_ref)
        carry_ref[...] = jnp.zeros_like(carry_ref)
        yb_ref[...] = jnp.zeros_like(yb_ref)
        za_ref[...] = _dot(_rms(read_first(), g1_ref[...]), w_ref[...])

    args = (read_next, g1_ref, w_ref, cw_ref, cb_ref, gbr_ref, gbc_ref, mg_ref, aw_ref, ab_ref, gg_ref)
    state = (tail_ref, c_ref, m_ref, s_ref, oi_ref, bg_ref, kbuf, bbuf)
    even = lax.rem(c, 2) == 0

    def router(y_prev_ref):
        first = functools.partial(_router_logits, read_prev, y_prev_ref, wo_ref, g2_ref, wrh_ref, wrl_ref, br_ref,
                                  h1_ref)
        second = functools.partial(_router_assign, c > 0, xe_ref, route_ref, cnt_ref, carry_ref)
        return first, second

    @pl.when(even)
    def _():
        _mixer_step(c, za_ref, zb_ref, *args, ya_ref, *state, router(yb_ref))

    @pl.when(jnp.logical_not(even))
    def _():
        _mixer_step(c, zb_ref, za_ref, *args, yb_ref, *state, router(ya_ref))


def _mixer_step(c, z_ref, zn_ref, read_next, g1_ref, w_ref, cw_ref, cb_ref, gbr_ref, gbc_ref, mg_ref, aw_ref, ab_ref,
                gg_ref, y_ref, tail_ref, c_ref, m_ref, s_ref, oi_ref, bg_ref, kbuf, bbuf, router):
    route_logits, route_assign = router
    bn = z_ref.shape[0] // CHUNK
    xn_next = _rms(read_next(), g1_ref[...]).astype(BF16)

    def project_next(part):
        lo, hi = IN_PROJ_SPLITS[part], IN_PROJ_SPLITS[part + 1]
        zn_ref[:, lo:hi] = jnp.dot(xn_next, w_ref[:, lo:hi], preferred_element_type=F32)

    rows_all = bn * CHUNK
    row = lax.broadcasted_iota(jnp.int32, (CHUNK, CHUNK), 0)
    col = lax.broadcasted_iota(jnp.int32, (CHUNK, CHUNK), 1)
    causal = col <= row
    later = jnp.logical_not(c == 0)
    valid_c = jnp.logical_or(later, lax.broadcasted_iota(jnp.int32, (CHUNK, 1), 0) >= LEAD_PAD)
    pos_c = lax.broadcasted_iota(jnp.int32, (rows_all, 1), 0) & (CHUNK - 1)
    pos_r = lax.broadcasted_iota(jnp.int32, (1, rows_all), 1) & (CHUNK - 1)
    valid_ca = jnp.logical_or(later, pos_c >= LEAD_PAD)
    valid_ra = jnp.logical_or(later, pos_r >= LEAD_PAD)
    row8 = lax.broadcasted_iota(jnp.int32, (SUBLANES, 1), 0)
    ones_col = jnp.where(lax.broadcasted_iota(jnp.int32, (CHUNK, LANES), 1) == 0, 1.0, 0.0).astype(F32)
    neg_inf = -jnp.inf

    r_a = lax.broadcasted_iota(jnp.int32, (rows_all, rows_all), 0)
    c_a = lax.broadcasted_iota(jnp.int32, (rows_all, rows_all), 1)
    same = (r_a >> CHUNK_LOG2) == (c_a >> CHUNK_LOG2)
    tri = jnp.where(jnp.logical_and(same, c_a <= r_a), 1.0, 0.0).astype(BF16)
    tri_t = jnp.where(jnp.logical_and(same, r_a <= c_a), 1.0, 0.0).astype(BF16)
    zs = z_ref[:, Z_SMALL:Z_SMALL + LANES]
    g_col = zs + gbr_ref[...]
    g_row = zs.T[0:2 * M_HEADS, :] + gbc_ref[...]
    u = _dot_f32(zs[:, 2 * M_HEADS:2 * M_HEADS + G_RANK], aw_ref[...]) + ab_ref[...]
    project_next(0)
    la = jnp.where(valid_ca, _log_sigmoid(u) * (1.0 / G_TAU), 0.0)
    lf_col = jnp.where(valid_ca, _log_sigmoid(g_col), 0.0)
    cum = sum(jnp.dot(tri, p, preferred_element_type=F32) for p in _split3(jnp.concatenate([lf_col, la], axis=1)))
    b_col_all, bg_all = cum[:, :LANES], cum[:, LANES:]
    bg_ref[...] = bg_all
    lf_row = jnp.where(valid_ra, _log_sigmoid(g_row), 0.0)
    b_row_all = sum(jnp.dot(p, tri_t, preferred_element_type=F32) for p in _split3(lf_row))
    ig_col_all = jnp.where(valid_ca, g_col, neg_inf)
    ig_row_all = jnp.where(valid_ra, g_row, neg_inf)

    def mlstm_batch(b):
        rows = slice(b * CHUNK, (b + 1) * CHUNK)
        x = jnp.where(valid_c, z_ref[rows, Z_QM:Z_QM + 2 * M_WIDTH], 0.0)
        prev = tail_ref[b]
        acc = x * cw_ref[CONV_K - 1:CONV_K, :] + cb_ref[...]
        for k in range(1, CONV_K):
            cur = pltpu.roll(x, k, 0)
            fix = pltpu.roll(prev, k, 0)
            top = jnp.where(row8 < k, fix, cur[0:SUBLANES])
            shifted = jnp.concatenate([top, cur[SUBLANES:]], axis=0)
            acc = acc + shifted * cw_ref[CONV_K - 1 - k:CONV_K - k, :]
        tail_ref[b] = x[CHUNK - SUBLANES:]
        qk = acc * _sigmoid(acc)
        qk = jnp.where(valid_c, qk, 0.0)
        q_m = qk[:, :M_WIDTH]
        k_m = qk[:, M_WIDTH:] * (M_HEAD_DIM ** -0.5)
        v_m = jnp.where(valid_c, z_ref[rows, Z_VM:Z_VM + M_WIDTH], 0.0)
        o_m = z_ref[rows, Z_OM:Z_OM + M_WIDTH]

        b_col, ig_col = b_col_all[rows], ig_col_all[rows]
        b_row, ig_row = b_row_all[:, rows], ig_row_all[:, rows]

        q_bf, k_bf = q_m.astype(BF16), k_m.astype(BF16)
        heads = []
        for hd in range(M_HEADS):
            idx = b * M_HEADS + hd
            lanes = slice(hd * M_HEAD_DIM, (hd + 1) * M_HEAD_DIM)
            bc, br = b_col[:, M_HEADS + hd:M_HEADS + hd + 1], b_row[M_HEADS + hd:M_HEADS + hd + 1, :]
            igc, igr = ig_col[:, hd:hd + 1], ig_row[hd:hd + 1, :]
            m_prev = m_ref[idx][0:1, 0:1]
            dmat = jnp.where(causal, bc - br + igr, neg_inf)
            m_inter = bc + m_prev
            m_t = jnp.maximum(m_inter, jnp.max(dmat, axis=-1, keepdims=True))
            b_end = bc[CHUNK - 1:CHUNK, :]
            m_new = jnp.maximum(b_end + m_prev, jnp.max(b_end - br + igr, axis=-1, keepdims=True))
            wk = jnp.exp(b_end - bc + igc - m_new)
            heads.append(dict(
                idx=idx, lanes=lanes, q=q_bf[:, lanes], k=k_bf[:, lanes],
                kw_t=(k_m[:, lanes] * wk).T.astype(BF16),
                v_aug=jnp.concatenate([v_m[:, lanes], ones_col], axis=1).astype(BF16),
                e=jnp.exp(dmat - m_t), w_inter=jnp.exp(m_inter - m_t), floor=jnp.exp(-m_t),
                decay=jnp.exp(b_end + m_prev - m_new), m_new=m_new))
        return dict(rows=rows, heads=heads, o_m=o_m)

    def mlstm_first_dots(st):
        for hd in st['heads']:
            cst = c_ref[hd['idx']]
            hd['qk'] = _dot_nt(hd['q'], hd['k'])
            hd['inter'] = _dot(hd['q'], cst)
            c_ref[hd['idx']] = hd['decay'] * cst + _dot(hd['kw_t'], hd['v_aug'])
            m_ref[hd['idx']] = jnp.broadcast_to(hd['m_new'], (SUBLANES, LANES))

    def mlstm_finish(st):
        rows = st['rows']
        for hd in st['heads']:
            lanes = hd['lanes']
            nd = _dot(hd['qk'] * hd['e'], hd['v_aug']) + hd['w_inter'] * hd['inter']
            num, den = nd[:, :M_HEAD_DIM], nd[:, M_HEAD_DIM:M_HEAD_DIM + 1]
            hh = num / jnp.maximum(jnp.abs(den), hd['floor'])
            y = _rms(hh, mg_ref[:, lanes]) * _sigmoid(st['o_m'][:, lanes])
            y_ref[rows, lanes] = y.astype(y_ref.dtype)

    def gla_batch(b):
        rows = slice(b * CHUNK, (b + 1) * CHUNK)
        q_g, k_g, v_g = _gla_qkv(z_ref, rows, valid_c)
        bg = bg_all[rows]
        bg_t = bg.T
        b_end = bg[CHUNK - 1:CHUNK, :]
        q_dec = (q_g * jnp.exp(bg)).astype(BF16)
        k_inv = (k_g * jnp.exp(-bg)).astype(BF16)
        k_end = k_g * jnp.exp(b_end - bg)
        v_bf = v_g.astype(BF16)
        heads = []
        for hd in range(G_HEADS):
            kl = slice(hd * G_K_DIM, (hd + 1) * G_K_DIM)
            vl = slice(hd * G_V_DIM, (hd + 1) * G_V_DIM)
            heads.append(dict(
                idx=b * G_HEADS + hd, vl=vl, yl=slice(M_WIDTH + hd * G_V_DIM, M_WIDTH + (hd + 1) * G_V_DIM),
                q_dec=q_dec[:, kl], k_inv=k_inv[:, kl], k_end_t=k_end[:, kl].T.astype(BF16), v=v_bf[:, vl],
                e_end=jnp.exp(bg_t[hd * G_K_DIM:(hd + 1) * G_K_DIM, CHUNK - 1:CHUNK])))
        return dict(rows=rows, heads=heads, unsafe=jnp.max(-b_end) > GLA_SAFE_EXP)

    def gla_first_dots(st):
        for hd in st['heads']:
            state = s_ref[hd['idx']]
            hd['a'] = _dot_nt(hd['q_dec'], hd['k_inv'])
            hd['o_inter'] = _dot(hd['q_dec'], state)
            s_ref[hd['idx']] = hd['e_end'] * state + _dot(hd['k_end_t'], hd['v'])

    def gla_finish(st):
        rows = st['rows']
        gate_g = z_ref[rows, Z_GG:Z_GG + G_WIDTH]
        for hd in st['heads']:
            vl = hd['vl']
            oi_ref[rows, vl] = hd['o_inter']
            o = _dot(jnp.where(causal, hd['a'], 0.0), hd['v']) + hd['o_inter']
            y_ref[rows, hd['yl']] = _gla_gated(o, gate_g[:, vl], gg_ref[:, vl]).astype(y_ref.dtype)

    def gla_redo_batch(b, carry):
        rows = pl.ds(pl.multiple_of(b * CHUNK, CHUNK), CHUNK)
        q_g, k_g, v_g = _gla_qkv(z_ref, rows, valid_c)
        bg = bg_ref[rows, :]
        gate_g = z_ref[rows, Z_GG:Z_GG + G_WIDTH]
        for hd in range(G_HEADS):
            kl = slice(hd * G_K_DIM, (hd + 1) * G_K_DIM)
            vl = slice(hd * G_V_DIM, (hd + 1) * G_V_DIM)
            yl = slice(M_WIDTH + hd * G_V_DIM, M_WIDTH + (hd + 1) * G_V_DIM)
            a = jnp.where(causal, _gla_intra_pairwise(q_g[:, kl], k_g[:, kl], bg[:, kl], kbuf, bbuf), 0.0)
            y = _gla_gated(_dot(a, v_g[:, vl]) + oi_ref[rows, vl], gate_g[:, vl], gg_ref[:, vl])
            y_ref[rows, yl] = y.astype(y_ref.dtype)
        return carry

    project_next(1)
    routed = route_logits()
    mlstm = [mlstm_batch(b) for b in range(bn)]
    gla = [gla_batch(b) for b in range(bn)]
    for b in range(bn):
        mlstm_first_dots(mlstm[b])
        gla_first_dots(gla[b])
    project_next(2)
    route_assign(*routed)
    for b in range(bn):
        mlstm_finish(mlstm[b])
        gla_finish(gla[b])
    unsafe = functools.reduce(jnp.logical_or, [st['unsafe'] for st in gla])

    @pl.when(unsafe)
    def _():
        lax.fori_loop(0, bn, gla_redo_batch, 0)


def _layer(stream, norm_g, w_in, conv_w, conv_b, gb_row, gb_col, m_norm_g, a_w2, a_b, g_norm_g,
           w_out, norm2_g, wr_hi, wr_lo, br, bn, n_chunks):
    rows = bn * CHUNK
    n = n_chunks * rows
    last = n_chunks - 1
    full = lambda shape: pl.BlockSpec(shape, lambda c: (0,) * len(shape))
    h_at = lambda m: pl.BlockSpec((rows, D_MODEL), m)
    prev = lambda c: (jnp.maximum(c - 1, 0), 0)
    embed = isinstance(stream, tuple)
    if embed:
        x4, meta = stream
        frames = lambda m: pl.BlockSpec((bn, 1, CHUNK, D_MODEL), m)
        stream_args = (meta, x4, x4)
        stream_specs = [
            full((N_META, D_MODEL)),
            frames(lambda c: (0, jnp.minimum(c, last - 1), 0, 0)),
            frames(lambda c: (0, jnp.clip(c - 2, 0, last - 1), 0, 0)),
        ]
    else:
        stream_args = (stream, stream, stream)
        stream_specs = [
            h_at(lambda c: (jnp.minimum(c, last), 0)),
            h_at(lambda c: (jnp.minimum(c + 1, last), 0)),
            h_at(prev),
        ]
    return pl.pallas_call(
        functools.partial(_layer_kernel, embed),
        grid=(n_chunks + 1,),
        in_specs=stream_specs + [
            full((1, D_MODEL)), full((D_MODEL, Z_WIDTH)),
            full((CONV_K, 2 * M_WIDTH)), full((1, 2 * M_WIDTH)),
            full((1, LANES)), full((2 * M_HEADS, rows)),
            full((1, M_WIDTH)), full((G_RANK, G_K_WIDTH)), full((1, G_K_WIDTH)), full((1, G_WIDTH)),
            full((D_MODEL, D_MODEL)), full((1, D_MODEL)), full((D_MODEL, LANES)), full((D_MODEL, LANES)),
            full((1, LANES)),
        ],
        out_specs=[
            h_at(prev),
            pl.BlockSpec((rows * SUBLANES, LANES), prev),
            pl.BlockSpec((1, SUBLANES, rows), lambda c: (jnp.maximum(c - 1, 0), 0, 0)),
            pl.BlockSpec((1, LANES), lambda c: (0, 0)),
        ],
        out_shape=[
            jax.ShapeDtypeStruct((n, D_MODEL), F32),
            jax.ShapeDtypeStruct((n * SUBLANES, LANES), jnp.uint32),
            jax.ShapeDtypeStruct((n_chunks, SUBLANES, rows), jnp.int32),
            jax.ShapeDtypeStruct((1, LANES), F32),
        ],
        scratch_shapes=[
            pltpu.VMEM((rows, Z_WIDTH), F32), pltpu.VMEM((rows, Z_WIDTH), F32),
            pltpu.VMEM((rows, D_MODEL), BF16), pltpu.VMEM((rows, D_MODEL), BF16),
            pltpu.VMEM((bn, SUBLANES, 2 * M_WIDTH), F32),
            pltpu.VMEM((bn * M_HEADS, M_HEAD_DIM, 2 * M_HEAD_DIM), F32),
            pltpu.VMEM((bn * M_HEADS, SUBLANES, LANES), F32),
            pltpu.VMEM((bn * G_HEADS, G_K_DIM, G_V_DIM), F32),
            pltpu.VMEM((rows, G_WIDTH), F32),
            pltpu.VMEM((rows, G_K_WIDTH), F32),
            pltpu.VMEM((CHUNK, G_K_DIM), F32), pltpu.VMEM((CHUNK, G_K_DIM), F32),
            pltpu.VMEM((1, LANES), F32),
        ],
        compiler_params=_params("arbitrary"),
        name="layer",
    )(*stream_args, norm_g, w_in, conv_w, conv_b, gb_row, gb_col, m_norm_g, a_w2, a_b, g_norm_g,
      w_out, norm2_g, wr_hi, wr_lo, br)


def _router_logits(read_h, y_ref, wo_ref, g_ref, wrh_ref, wrl_ref, br_ref, h1_ref):
    h1 = read_h() + jnp.dot(y_ref[...], wo_ref[...], preferred_element_type=F32)
    h1_ref[...] = h1
    xn = _rms(h1, g_ref[...])
    x_hi = xn.astype(BF16)
    x_lo = (xn - x_hi.astype(F32)).astype(BF16)
    dot = functools.partial(jnp.dot, preferred_element_type=F32)
    logits = dot(x_hi, wrh_ref[...]) + dot(x_hi, wrl_ref[...]) + dot(x_lo, wrh_ref[...]) + br_ref[...]
    return xn, logits


def _router_assign(live, xe_ref, route_ref, cnt_ref, carry_ref, xn, logits):
    rows = xn.shape[0]
    lane = lax.broadcasted_iota(jnp.int32, (rows, LANES), 1)
    neg_inf = -jnp.inf
    big = jnp.int32(LANES)

    def first_argmax(vals):
        top = jnp.max(vals, axis=-1, keepdims=True)
        return top, jnp.min(jnp.where(vals == top, lane, big), axis=-1, keepdims=True)

    lg = jnp.where(lane < N_GROUPS, logits, neg_inf)
    g_max, g_sel = first_argmax(lg)
    p_sel = 1.0 / jnp.sum(jnp.exp(lg - g_max), axis=-1, keepdims=True)
    in_group = jnp.logical_and(lane >= N_GROUPS + EXPERTS_PER_GROUP * g_sel,
                               lane < N_GROUPS + EXPERTS_PER_GROUP * (g_sel + 1))
    le = jnp.where(in_group, logits, neg_inf)
    v1, i1 = first_argmax(le)
    v2, i2 = first_argmax(jnp.where(lane == i1, neg_inf, le))
    e21 = jnp.exp(v2 - v1)
    gate1 = p_sel / (1.0 + e21)
    gate2 = p_sel * e21 / (1.0 + e21)
    j1 = i1 - N_GROUPS - EXPERTS_PER_GROUP * g_sel
    j2 = i2 - N_GROUPS - EXPERTS_PER_GROUP * g_sel
    ja, jb = jnp.minimum(j1, j2), jnp.maximum(j1, j2)
    ga = jnp.where(j1 < j2, gate1, gate2)
    gb = jnp.where(j1 < j2, gate2, gate1)
    pair = ((ja * (2 * EXPERTS_PER_GROUP - 1 - ja)) >> 1) + (jb - ja - 1)
    cls = g_sel * N_PAIRS + pair

    bits = lax.bitcast_convert_type(xn.astype(jnp.bfloat16).astype(F32), jnp.uint32)
    half = D_MODEL // 2
    packed = bits[:, half:] | (bits[:, :half] >> 16)
    gates = lax.bitcast_convert_type(jnp.where(lane == 0, ga, jnp.where(lane == 1, gb, 0.0)), jnp.uint32)
    for sg in range(SUBLANES):
        if sg < half // LANES:
            val = packed[:, sg * LANES:(sg + 1) * LANES]
        else:
            val = gates if sg == half // LANES else jnp.zeros((rows, LANES), jnp.uint32)
        xe_ref[pl.ds(sg, rows, stride=SUBLANES), :] = val

    onehot = jnp.where(jnp.logical_and(live, lane == cls), 1.0, 0.0).astype(F32)
    r_i = lax.broadcasted_iota(jnp.int32, (rows, rows), 0)
    c_i = lax.broadcasted_iota(jnp.int32, (rows, rows), 1)
    incl = jnp.where(c_i <= r_i, 1.0, 0.0).astype(BF16)
    prefix = jnp.dot(incl, onehot.astype(BF16), preferred_element_type=F32)
    rank = jnp.sum(onehot * (prefix - 1.0 + carry_ref[...]), axis=-1, keepdims=True)
    carry_ref[...] = carry_ref[...] + prefix[rows - 1:rows, :]
    cnt_ref[...] = carry_ref[...]
    route = jnp.where(lane == 0, cls, jnp.where(lane == 1, rank.astype(jnp.int32), 0))
    route_ref[0] = route.T[0:SUBLANES, :]


def _dispatch_kernel(off_ref, gap_ref, used_ref, cls_ref, rank_ref, x_ref, xs_hbm, stage, zeros, sem, fill_sem):
    i = pl.program_id(0)
    rows = cls_ref.shape[-1]
    s = lax.rem(i, 2)
    n_slots = xs_hbm.shape[0]

    def wait(slot):
        pltpu.make_async_copy(stage.at[slot], xs_hbm.at[pl.ds(0, rows)], sem.at[slot]).wait()

    def for_gaps(act):
        def class_gap(cl, carry):
            size = gap_ref[cl]

            @pl.when(size > 0)
            def _():
                act(pltpu.make_async_copy(zeros.at[pl.ds(0, size)],
                                          xs_hbm.at[pl.ds(off_ref[cl] + used_ref[cl], size)], fill_sem))
            return carry
        lax.fori_loop(0, N_CLASSES, class_gap, 0)

        def tail_block(b, carry):
            act(pltpu.make_async_copy(zeros, xs_hbm.at[pl.ds(b * EXPERT_ROWS, EXPERT_ROWS)], fill_sem))
            return carry
        lax.fori_loop(used_ref[N_CLASSES], n_slots // EXPERT_ROWS, tail_block, 0)

    @pl.when(i == 0)
    def _():
        zeros[...] = jnp.zeros_like(zeros)
        for_gaps(lambda copy: copy.start())

    @pl.when(i >= 2)
    def _():
        wait(s)

    stage[s] = x_ref[...]
    for k in range(rows):
        dst = off_ref[cls_ref[0, 0, k]] + rank_ref[0, 0, k]
        pltpu.make_async_copy(stage.at[s].at[k], xs_hbm.at[dst], sem.at[s]).start(priority=k % 2)

    @pl.when(i == pl.num_programs(0) - 1)
    def _():
        wait(s)

        @pl.when(i >= 1)
        def _():
            wait(1 - s)
        for_gaps(lambda copy: copy.wait())


def _dispatch(off, gap, used, cls, rank, xe, n_slots):
    n_steps, _, rows = cls.shape
    smem = pl.BlockSpec((1, 1, rows), lambda i, *_: (i, 0, 0), memory_space=pltpu.SMEM)
    grid_spec = pltpu.PrefetchScalarGridSpec(
        num_scalar_prefetch=3,
        grid=(n_steps,),
        in_specs=[smem, smem, pl.BlockSpec((rows, SUBLANES, LANES), lambda i, *_: (i, 0, 0))],
        out_specs=pl.BlockSpec(memory_space=pl.ANY),
        scratch_shapes=[pltpu.VMEM((2, rows, SUBLANES, LANES), xe.dtype),
                        pltpu.VMEM((EXPERT_ROWS, SUBLANES, LANES), xe.dtype),
                        pltpu.SemaphoreType.DMA((2,)), pltpu.SemaphoreType.DMA(())],
    )
    return pl.pallas_call(
        _dispatch_kernel,
        grid_spec=grid_spec,
        out_shape=jax.ShapeDtypeStruct((n_slots, SUBLANES, LANES), xe.dtype),
        compiler_params=_params("arbitrary"),
        name="dispatch",
    )(off, gap, used, cls, rank, xe)


BLOCKS_PER_STEP = 2


def _expert_kernel(ea_ref, eb_ref, nused_ref, xs_hbm, *refs):
    weights = refs[:4 * BLOCKS_PER_STEP]
    ys_hbm, xbuf, obuf, sem_in, sem_out = refs[4 * BLOCKS_PER_STEP:]
    i = pl.program_id(0)
    steps_used = (nused_ref[0] + BLOCKS_PER_STEP - 1) // BLOCKS_PER_STEP
    slot = lax.rem(i, 2)
    block = EXPERT_ROWS * SUBLANES
    tile = BLOCKS_PER_STEP * block

    def in_copy(step, s):
        return pltpu.make_async_copy(xs_hbm.at[pl.ds(pl.multiple_of(step * tile, tile), tile), :],
                                     xbuf.at[pl.ds(pl.multiple_of(s * tile, tile), tile), :], sem_in.at[s])

    def out_copy(step, s):
        return pltpu.make_async_copy(obuf.at[pl.ds(pl.multiple_of(s * tile, tile), tile), :],
                                     ys_hbm.at[pl.ds(pl.multiple_of(step * tile, tile), tile), :], sem_out.at[s])

    @pl.when(i < steps_used)
    def _():
        @pl.when(i == 0)
        def _():
            in_copy(0, 0).start()

        in_copy(i, slot).wait()

        @pl.when(i + 1 < steps_used)
        def _():
            in_copy(i + 1, 1 - slot).start()

        @pl.when(i >= 2)
        def _():
            out_copy(jnp.maximum(i - 2, 0), slot).wait()

        as_f32 = lambda w: lax.bitcast_convert_type(w, F32)
        for blk in range(BLOCKS_PER_STEP):
            wgu_a, wd_a, wgu_b, wd_b = weights[4 * blk:4 * blk + 4]
            base = slot * tile + blk * block
            seg = lambda sg: xbuf[pl.ds(base + sg, EXPERT_ROWS, stride=SUBLANES), :]
            words = [seg(sg) for sg in range(D_MODEL // 2 // LANES)]
            xb = jnp.concatenate([as_f32(w << 16) for w in words]
                                 + [as_f32(w & jnp.uint32(0xFFFF0000)) for w in words], axis=1).astype(BF16)
            gates = as_f32(seg(D_MODEL // 2 // LANES))
            ga, gb = gates[:, 0:1], gates[:, 1:2]

            def mlp(wgu, wd):
                gu = jnp.dot(xb, wgu[0], preferred_element_type=F32)
                gate, up = gu[:, :D_EXPERT], gu[:, D_EXPERT:]
                return jnp.dot((gate * _sigmoid(gate) * up).astype(BF16), wd[0], preferred_element_type=F32)

            y = ga * mlp(wgu_a, wd_a) + gb * mlp(wgu_b, wd_b)
            for sg in range(SUBLANES):
                obuf[pl.ds(base + sg, EXPERT_ROWS, stride=SUBLANES), :] = y[:, sg * LANES:(sg + 1) * LANES]
        out_copy(i, slot).start()

        @pl.when(i == steps_used - 1)
        def _():
            out_copy(i, slot).wait()

            @pl.when(i >= 1)
            def _():
                out_copy(jnp.maximum(i - 1, 0), 1 - slot).wait()

    @pl.when(i == steps_used)
    def _():
        obuf[0:tile, :] = jnp.zeros((tile, LANES), F32)
        n_steps = pl.num_programs(0)

        def each(act):
            def body(step, carry):
                act(out_copy(step, 0))
                return carry
            lax.fori_loop(steps_used, n_steps, body, 0)
        each(lambda copy: copy.start())
        each(lambda copy: copy.wait())


def _experts(blk_ea, blk_eb, n_used, xs, wgu, wd):
    n_blocks = blk_ea.shape[0]
    tile = BLOCKS_PER_STEP * EXPERT_ROWS * SUBLANES
    weight_specs = []
    for blk in range(BLOCKS_PER_STEP):
        at = lambda tab, blk=blk: (lambda i, ea, eb, nu: ((ea, eb)[tab][BLOCKS_PER_STEP * i + blk], 0, 0))
        for tab in (0, 1):
            weight_specs += [pl.BlockSpec((1, D_MODEL, 2 * D_EXPERT), at(tab)),
                             pl.BlockSpec((1, D_EXPERT, D_MODEL), at(tab))]
    grid_spec = pltpu.PrefetchScalarGridSpec(
        num_scalar_prefetch=3,
        grid=(n_blocks // BLOCKS_PER_STEP,),
        in_specs=[pl.BlockSpec(memory_space=pl.ANY)] + weight_specs,
        out_specs=pl.BlockSpec(memory_space=pl.ANY),
        scratch_shapes=[
            pltpu.VMEM((2 * tile, LANES), jnp.uint32),
            pltpu.VMEM((2 * tile, LANES), F32),
            pltpu.SemaphoreType.DMA((2,)),
            pltpu.SemaphoreType.DMA((2,)),
        ],
    )
    return pl.pallas_call(
        _expert_kernel,
        grid_spec=grid_spec,
        out_shape=jax.ShapeDtypeStruct((n_blocks * EXPERT_ROWS * SUBLANES, LANES), F32),
        compiler_params=_params("arbitrary"),
        name="experts",
    )(blk_ea, blk_eb, n_used, xs, *([wgu, wd] * (2 * BLOCKS_PER_STEP)))


def _pair_tables():
    ea, eb = [], []
    for g in range(N_GROUPS):
        for a in range(EXPERTS_PER_GROUP):
            for b in range(a + 1, EXPERTS_PER_GROUP):
                ea.append(g * EXPERTS_PER_GROUP + a)
                eb.append(g * EXPERTS_PER_GROUP + b)
    return np.asarray(ea, np.int32), np.asarray(eb, np.int32)


def _plan_blocks(counts, n):
    cnt = counts[0, :N_CLASSES].astype(jnp.int32)
    padded = (cnt + EXPERT_ROWS - 1) // EXPERT_ROWS * EXPERT_ROWS
    pad_end = jnp.cumsum(padded)
    off = pad_end - padded
    n_blocks = -(-(n + N_CLASSES * (EXPERT_ROWS - 1)) // EXPERT_ROWS)
    n_blocks = -(-n_blocks // BLOCKS_PER_STEP) * BLOCKS_PER_STEP
    blk_start = jnp.arange(n_blocks, dtype=jnp.int32) * EXPERT_ROWS
    blk_cls = jnp.minimum(jnp.sum(pad_end[None, :] <= blk_start[:, None], axis=1), N_CLASSES - 1)
    ea_tab, eb_tab = _pair_tables()
    n_used = (pad_end[-1] // EXPERT_ROWS).astype(jnp.int32).reshape(1)
    used = jnp.concatenate([cnt, n_used])
    return (jnp.asarray(ea_tab)[blk_cls], jnp.asarray(eb_tab)[blk_cls], n_used, off.astype(jnp.int32),
            (padded - cnt).astype(jnp.int32), used.astype(jnp.int32), n_blocks)


def _final_kernel(off_ref, cls_ref, rank_ref, ncls_ref, nrank_ref, h_ref, ys_hbm, g_ref, o_ref, ybuf, sem):
    y = _moe_rows(pl.program_id(0), pl.num_programs(0), off_ref, (cls_ref, rank_ref), (ncls_ref, nrank_ref),
                  ys_hbm, ybuf, sem)
    out = _rms(h_ref[...] + y, g_ref[...])
    o_ref[...] = out.reshape(o_ref.shape)


def _final(off, cls, rank, h, ys, g, bn):
    n_chunks, _, rows = cls.shape
    smem = lambda m: pl.BlockSpec((1, 1, rows), m, memory_space=pltpu.SMEM)
    cur = lambda c, off: (c + 1, 0, 0)
    nxt = lambda c, off: (jnp.minimum(c + 2, n_chunks - 1), 0, 0)
    grid_spec = pltpu.PrefetchScalarGridSpec(
        num_scalar_prefetch=1,
        grid=(n_chunks - 1,),
        in_specs=[
            smem(cur), smem(cur), smem(nxt), smem(nxt),
            pl.BlockSpec((rows, D_MODEL), lambda c, off: (c + 1, 0)),
            pl.BlockSpec(memory_space=pl.ANY),
            pl.BlockSpec((1, D_MODEL), lambda c, off: (0, 0)),
        ],
        out_specs=pl.BlockSpec((bn, 1, CHUNK, D_MODEL), lambda c, off: (0, c, 0, 0)),
        scratch_shapes=[pltpu.VMEM((2 * rows * SUBLANES, LANES), F32), pltpu.SemaphoreType.DMA((2,))],
    )
    return pl.pallas_call(
        _final_kernel,
        grid_spec=grid_spec,
        out_shape=jax.ShapeDtypeStruct((bn, n_chunks - 1, CHUNK, D_MODEL), F32),
        compiler_params=_params("arbitrary"),
        name="final_norm",
    )(off, cls, rank, cls, rank, h, ys, g)


def _repack_w_in(w):
    o = np.cumsum([0, 2 * M_WIDTH, M_WIDTH, M_WIDTH, M_HEADS, M_HEADS, G_K_WIDTH, G_K_WIDTH, G_WIDTH, G_WIDTH,
                   G_RANK])
    seg = lambda i: w[:, o[i]:o[i + 1]]
    small = jnp.concatenate([seg(3), seg(4), seg(9)], axis=1)
    small = jnp.pad(small, ((0, 0), (0, LANES - small.shape[1])))
    return jnp.concatenate([seg(0), seg(1), seg(2), seg(5), seg(6), seg(7), seg(8), small], axis=1).astype(BF16)


def kernel(x, meta_tokens, norm1_g, w_in, conv_w, conv_b, gate_b, m_norm_g, a_w2, a_b, g_norm_g, w_out, norm2_g,
           wr_g, br_g, wr_e, br_e, w_gate, w_up, w_down, final_norm_g):
    bn, seq, dm = x.shape
    depth = w_in.shape[0]
    assert dm == D_MODEL and seq % CHUNK == 0
    n_chunks = seq // CHUNK + 1
    rows = bn * CHUNK
    n = n_chunks * rows
    x4 = x.reshape(bn, seq // CHUNK, CHUNK, dm)
    row = lambda v: v.reshape(1, -1).astype(F32)

    h = ys = off = cls = rank = None
    for l in range(depth):
        stream = (x4, meta_tokens.astype(F32)) if l == 0 else _add_moe(off, cls, rank, h, ys)
        gb = gate_b[l].astype(F32).reshape(1, 2 * M_HEADS)
        wr = jnp.pad(jnp.concatenate([wr_g[l], wr_e[l]], axis=1).astype(F32),
                     ((0, 0), (0, LANES - N_GROUPS - N_EXPERTS)))
        wr_hi = wr.astype(BF16)
        wr_lo = (wr - wr_hi.astype(F32)).astype(BF16)
        br = jnp.pad(jnp.concatenate([br_g[l], br_e[l]]).astype(F32), (0, LANES - N_GROUPS - N_EXPERTS))
        h, xe, route, counts = _layer(
            stream, row(norm1_g[l]), _repack_w_in(w_in[l]), conv_w[l].astype(F32), row(conv_b[l]),
            jnp.pad(gb, ((0, 0), (0, LANES - 2 * M_HEADS))),
            jnp.broadcast_to(gb.reshape(2 * M_HEADS, 1), (2 * M_HEADS, rows)),
            row(m_norm_g[l]), a_w2[l].astype(F32), row(a_b[l]), row(g_norm_g[l]),
            w_out[l].astype(BF16), row(norm2_g[l]), wr_hi, wr_lo, br.reshape(1, LANES), bn, n_chunks)
        blk_ea, blk_eb, n_used, off, gap, used, n_blocks = _plan_blocks(counts, n)
        cls, rank = route[:, 0:1, :], route[:, 1:2, :]
        xs = _dispatch(off, gap, used, cls, rank, xe.reshape(n, SUBLANES, LANES), n_blocks * EXPERT_ROWS)
        wgu = jnp.concatenate([w_gate[l], w_up[l]], axis=-1).astype(BF16)
        ys = _experts(blk_ea, blk_eb, n_used, xs.reshape(n_blocks * EXPERT_ROWS * SUBLANES, LANES), wgu,
                      w_down[l].astype(BF16))
    out = _final(off, cls, rank, h, ys, row(final_norm_g), bn)
    return out.reshape(bn, seq, dm)
```

```python
import functools

import numpy as np
import jax
import jax.numpy as jnp
from jax import lax
from jax.experimental import pallas as pl
from jax.experimental.pallas import tpu as pltpu

F32 = jnp.float32
BF16 = jnp.bfloat16
HIGHEST = lax.Precision.HIGHEST

D_MODEL = 1024
CHUNK = 64
CHUNK_LOG2 = CHUNK.bit_length() - 1
N_META = 16
LEAD_PAD = CHUNK - N_META
EPS = 1e-6
M_HEADS = 4
M_WIDTH = 512
M_HEAD_DIM = 128
CONV_K = 4
G_HEADS = 4
G_WIDTH = 512
G_V_DIM = 128
G_K_DIM = 64
G_K_WIDTH = 256
G_RANK = 16
G_TAU = 16.0
N_GROUPS = 4
EXPERTS_PER_GROUP = 8
N_EXPERTS = 32
D_EXPERT = 256
N_PAIRS = EXPERTS_PER_GROUP * (EXPERTS_PER_GROUP - 1) // 2
N_CLASSES = N_GROUPS * N_PAIRS

LANES = 128
SUBLANES = 8
EXPERT_ROWS = 128
VMEM_LIMIT = 56 * 1024 * 1024

Z_QM, Z_KM, Z_VM, Z_OM = 0, 512, 1024, 1536
Z_QG, Z_KG, Z_VG, Z_GG = 2048, 2304, 2560, 3072
Z_SMALL = 3584
Z_WIDTH = Z_SMALL + LANES
X_EXT = D_MODEL + LANES
GLA_SAFE_EXP = 80.0


def _log_sigmoid(x):
    return jnp.minimum(x, 0.0) - jnp.log(1.0 + jnp.exp(-jnp.abs(x)))


def _sigmoid(x):
    return 0.5 * jnp.tanh(0.5 * x) + 0.5


def _rms(x, g):
    return x * lax.rsqrt(jnp.mean(x * x, axis=-1, keepdims=True) + EPS) * g


def _dot(a, b):
    return jnp.dot(a.astype(BF16), b.astype(BF16), preferred_element_type=F32)


def _dot_nt(a, b):
    return lax.dot_general(a.astype(BF16), b.astype(BF16), (((1,), (1,)), ((), ())),
                           preferred_element_type=F32)


def _dot_f32(a, b):
    return jnp.dot(a, b, precision=HIGHEST, preferred_element_type=F32)


def _params(*semantics):
    return pltpu.CompilerParams(dimension_semantics=semantics, vmem_limit_bytes=VMEM_LIMIT)


class _MoeStream:
    def __init__(self, off_ref, cur, nxt, ys_hbm, ybuf, sem):
        self.off_ref, self.cur, self.nxt, self.ys_hbm, self.ybuf, self.sem = off_ref, cur, nxt, ys_hbm, ybuf, sem
        self.rows = cur[0].shape[-1]
        self.tile = self.rows * SUBLANES

    def start(self, idx, slot):
        cls_ref, rank_ref = idx
        for k in range(self.rows):
            src = pl.multiple_of((self.off_ref[cls_ref[0, 0, k]] + rank_ref[0, 0, k]) * SUBLANES, SUBLANES)
            dst = pl.multiple_of(slot * self.tile + k * SUBLANES, SUBLANES)
            pltpu.make_async_copy(self.ys_hbm.at[pl.ds(src, SUBLANES), :], self.ybuf.at[pl.ds(dst, SUBLANES), :],
                                  self.sem.at[slot]).start(priority=k % 2)

    def wait(self, slot):
        half = self.ybuf.at[pl.ds(pl.multiple_of(slot * self.tile, self.tile), self.tile), :]
        pltpu.make_async_copy(self.ys_hbm.at[pl.ds(0, self.tile), :], half, self.sem.at[slot]).wait()

    def read(self, slot):
        base = slot * self.tile
        return jnp.concatenate([self.ybuf[pl.ds(base + sg, self.rows, stride=SUBLANES), :]
                                for sg in range(SUBLANES)], axis=1)

    def take(self, slot):
        self.wait(slot)
        self.start(self.nxt, 1 - slot)
        return self.read(slot)


def _moe_rows(i, n_steps, stream):
    s = lax.rem(i, 2)

    @pl.when(i == 0)
    def _():
        stream.start(stream.cur, 0)

    rows = stream.take(s)

    @pl.when(i == n_steps - 1)
    def _():
        stream.wait(1 - s)
    return rows


def _add_moe_kernel(off_ref, cls_ref, rank_ref, ncls_ref, nrank_ref, h_ref, ys_hbm, ho_ref, ybuf, sem):
    y = _moe_rows(pl.program_id(0), pl.num_programs(0),
                  _MoeStream(off_ref, (cls_ref, rank_ref), (ncls_ref, nrank_ref), ys_hbm, ybuf, sem))
    ho_ref[...] = h_ref[...] + y


def _add_moe(off, cls, rank, h, ys):
    n = h.shape[0]
    n_steps, _, rows = cls.shape
    smem = lambda m: pl.BlockSpec((1, 1, rows), m, memory_space=pltpu.SMEM)
    cur = lambda i, off: (i, 0, 0)
    nxt = lambda i, off: (jnp.minimum(i + 1, n_steps - 1), 0, 0)
    grid_spec = pltpu.PrefetchScalarGridSpec(
        num_scalar_prefetch=1,
        grid=(n_steps,),
        in_specs=[
            smem(cur), smem(cur), smem(nxt), smem(nxt),
            pl.BlockSpec((rows, D_MODEL), lambda i, off: (i, 0)),
            pl.BlockSpec(memory_space=pl.ANY),
        ],
        out_specs=pl.BlockSpec((rows, D_MODEL), lambda i, off: (i, 0)),
        scratch_shapes=[pltpu.VMEM((2 * rows * SUBLANES, LANES), F32), pltpu.SemaphoreType.DMA((2,))],
    )
    return pl.pallas_call(
        _add_moe_kernel,
        grid_spec=grid_spec,
        out_shape=jax.ShapeDtypeStruct((n, D_MODEL), F32),
        compiler_params=_params("arbitrary"),
        name="add_moe",
    )(off, cls, rank, cls, rank, h, ys)


def _gla_intra_pairwise(q, k, bg, kbuf, bbuf):
    kbuf[...] = k
    bbuf[...] = bg
    lane = lax.broadcasted_iota(jnp.int32, (CHUNK, CHUNK), 1)

    def column(s, a):
        ks = kbuf[pl.ds(s, 1), :]
        bs = bbuf[pl.ds(s, 1), :]
        col = jnp.sum(q * ks * jnp.exp(jnp.minimum(bg - bs, 0.0)), axis=-1, keepdims=True)
        return jnp.where(lane == s, col, a)

    return lax.fori_loop(0, CHUNK, column, jnp.zeros((CHUNK, CHUNK), F32))


def _gla_qkv(z_ref, rows, valid_c):
    q_g = jnp.where(valid_c, z_ref[rows, Z_QG:Z_QG + G_K_WIDTH], 0.0) * (G_K_DIM ** -0.5)
    k_g = jnp.where(valid_c, z_ref[rows, Z_KG:Z_KG + G_K_WIDTH], 0.0)
    v_g = jnp.where(valid_c, z_ref[rows, Z_VG:Z_VG + G_WIDTH], 0.0)
    return q_g, k_g, v_g


def _split3(x):
    hi = x.astype(BF16)
    r = x - hi.astype(F32)
    mid = r.astype(BF16)
    return hi, mid, (r - mid.astype(F32)).astype(BF16)


def _gla_gated(o, gate, g):
    return _rms(o, g) * (gate * _sigmoid(gate))


IN_PROJ_SPLITS = (0, 1280, 2560, Z_WIDTH)


def _layer_kernel(embed, *refs):
    c = pl.program_id(0)
    (s0_ref, s1_ref, s2_ref), refs = refs[:3], refs[3:]
    (g1_ref, w_ref, cw_ref, cb_ref, gbr_ref, gbc_ref, mg_ref, aw_ref, ab_ref, gg_ref, wo_ref, g2_ref, wrh_ref, wrl_ref,
     br_ref, h1_ref, xe_ref, route_ref, cnt_ref, za_ref, zb_ref, ya_ref, yb_ref, tail_ref, c_ref, m_ref, s_ref, oi_ref,
     bg_ref, kbuf, bbuf, carry_ref) = refs
    rows = za_ref.shape[0]
    if embed:
        lead = jnp.concatenate([jnp.zeros((LEAD_PAD, D_MODEL), F32), s0_ref[...]] * (rows // CHUNK), axis=0)
        read_first = lambda: lead
        read_next = lambda: s1_ref[...].reshape(rows, D_MODEL)
        read_prev = lambda: jnp.where(c <= 1, lead, s2_ref[...].reshape(rows, D_MODEL))
    else:
        read_first, read_next, read_prev = (lambda: s0_ref[...]), (lambda: s1_ref[...]), (lambda: s2_ref[...])

    @pl.when(c == 0)
    def _():
        tail_ref[...] = jnp.zeros_like(tail_ref)
        c_ref[...] = jnp.zeros_like(c_ref)
        m_ref[...] = jnp.zeros_like(m_ref)
        s_ref[...] = jnp.zeros_like(s_ref)
        carry_ref[...] = jnp.zeros_like(carry_ref)
        yb_ref[...] = jnp.zeros_like(yb_ref)
        za_ref[...] = _dot(_rms(read_first(), g1_ref[...]), w_ref[...])

    args = (g1_ref, w_ref, cw_ref, cb_ref, gbr_ref, gbc_ref, mg_ref, aw_ref, ab_ref, gg_ref)
    state = (tail_ref, c_ref, m_ref, s_ref, oi_ref, bg_ref, kbuf, bbuf)
    parity = lax.rem(c, 2)

    def step(z_ref, zn_ref, y_ref, y_prev_ref):
        first = functools.partial(_router_logits, read_prev, y_prev_ref, wo_ref, g2_ref, wrh_ref, wrl_ref, br_ref,
                                  h1_ref)
        second = functools.partial(_router_assign, c > 0, xe_ref, route_ref, cnt_ref, carry_ref)
        _mixer_step(c, z_ref, zn_ref, read_next, *args, y_ref, *state, (first, second))

    @pl.when(parity == 0)
    def _():
        step(za_ref, zb_ref, ya_ref, yb_ref)

    @pl.when(parity == 1)
    def _():
        step(zb_ref, za_ref, yb_ref, ya_ref)


def _mixer_step(c, z_ref, zn_ref, read_next, g1_ref, w_ref, cw_ref, cb_ref, gbr_ref, gbc_ref, mg_ref, aw_ref, ab_ref,
                gg_ref, y_ref, tail_ref, c_ref, m_ref, s_ref, oi_ref, bg_ref, kbuf, bbuf, router):
    route_logits, route_assign = router
    bn = z_ref.shape[0] // CHUNK
    xn_next = _rms(read_next(), g1_ref[...]).astype(BF16)

    def project_next(part):
        lo, hi = IN_PROJ_SPLITS[part], IN_PROJ_SPLITS[part + 1]
        zn_ref[:, lo:hi] = jnp.dot(xn_next, w_ref[:, lo:hi], preferred_element_type=F32)

    rows_all = bn * CHUNK
    row = lax.broadcasted_iota(jnp.int32, (CHUNK, CHUNK), 0)
    col = lax.broadcasted_iota(jnp.int32, (CHUNK, CHUNK), 1)
    causal = col <= row
    later = jnp.logical_not(c == 0)
    valid_c = jnp.logical_or(later, lax.broadcasted_iota(jnp.int32, (CHUNK, 1), 0) >= LEAD_PAD)
    pos_c = lax.broadcasted_iota(jnp.int32, (rows_all, 1), 0) & (CHUNK - 1)
    pos_r = lax.broadcasted_iota(jnp.int32, (1, rows_all), 1) & (CHUNK - 1)
    valid_ca = jnp.logical_or(later, pos_c >= LEAD_PAD)
    valid_ra = jnp.logical_or(later, pos_r >= LEAD_PAD)
    row8 = lax.broadcasted_iota(jnp.int32, (SUBLANES, 1), 0)
    ones_col = jnp.where(lax.broadcasted_iota(jnp.int32, (CHUNK, LANES), 1) == 0, 1.0, 0.0).astype(F32)
    neg_inf = -jnp.inf

    r_a = lax.broadcasted_iota(jnp.int32, (rows_all, rows_all), 0)
    c_a = lax.broadcasted_iota(jnp.int32, (rows_all, rows_all), 1)
    same = (r_a >> CHUNK_LOG2) == (c_a >> CHUNK_LOG2)
    tri = jnp.where(jnp.logical_and(same, c_a <= r_a), 1.0, 0.0).astype(BF16)
    tri_t = jnp.where(jnp.logical_and(same, r_a <= c_a), 1.0, 0.0).astype(BF16)
    zs = z_ref[:, Z_SMALL:Z_SMALL + LANES]
    g_col = zs + gbr_ref[...]
    g_row = zs.T[0:2 * M_HEADS, :] + gbc_ref[...]
    u = _dot_f32(zs[:, 2 * M_HEADS:2 * M_HEADS + G_RANK], aw_ref[...]) + ab_ref[...]
    project_next(0)
    la = jnp.where(valid_ca, _log_sigmoid(u) * (1.0 / G_TAU), 0.0)
    lf_col = jnp.where(valid_ca, _log_sigmoid(g_col), 0.0)
    cum = sum(jnp.dot(tri, p, preferred_element_type=F32) for p in _split3(jnp.concatenate([lf_col, la], axis=1)))
    b_col_all, bg_all = cum[:, :LANES], cum[:, LANES:]
    bg_ref[...] = bg_all
    lf_row = jnp.where(valid_ra, _log_sigmoid(g_row), 0.0)
    b_row_all = sum(jnp.dot(p, tri_t, preferred_element_type=F32) for p in _split3(lf_row))
    ig_col_all = jnp.where(valid_ca, g_col, neg_inf)
    ig_row_all = jnp.where(valid_ra, g_row, neg_inf)

    def mlstm_batch(b):
        rows = slice(b * CHUNK, (b + 1) * CHUNK)
        x = jnp.where(valid_c, z_ref[rows, Z_QM:Z_QM + 2 * M_WIDTH], 0.0)
        prev = tail_ref[b]
        acc = x * cw_ref[CONV_K - 1:CONV_K, :] + cb_ref[...]
        for k in range(1, CONV_K):
            cur = pltpu.roll(x, k, 0)
            fix = pltpu.roll(prev, k, 0)
            top = jnp.where(row8 < k, fix, cur[0:SUBLANES])
            shifted = jnp.concatenate([top, cur[SUBLANES:]], axis=0)
            acc = acc + shifted * cw_ref[CONV_K - 1 - k:CONV_K - k, :]
        tail_ref[b] = x[CHUNK - SUBLANES:]
        qk = acc * _sigmoid(acc)
        qk = jnp.where(valid_c, qk, 0.0)
        q_m = qk[:, :M_WIDTH]
        k_m = qk[:, M_WIDTH:] * (M_HEAD_DIM ** -0.5)
        v_m = jnp.where(valid_c, z_ref[rows, Z_VM:Z_VM + M_WIDTH], 0.0)
        o_m = z_ref[rows, Z_OM:Z_OM + M_WIDTH]

        b_col, ig_col = b_col_all[rows], ig_col_all[rows]
        b_row, ig_row = b_row_all[:, rows], ig_row_all[:, rows]

        q_bf, k_bf = q_m.astype(BF16), k_m.astype(BF16)
        heads = []
        for hd in range(M_HEADS):
            idx = b * M_HEADS + hd
            lanes = slice(hd * M_HEAD_DIM, (hd + 1) * M_HEAD_DIM)
            bc, br = b_col[:, M_HEADS + hd:M_HEADS + hd + 1], b_row[M_HEADS + hd:M_HEADS + hd + 1, :]
            igc, igr = ig_col[:, hd:hd + 1], ig_row[hd:hd + 1, :]
            m_prev = m_ref[idx][0:1, 0:1]
            dmat = jnp.where(causal, bc - br + igr, neg_inf)
            m_inter = bc + m_prev
            m_t = jnp.maximum(m_inter, jnp.max(dmat, axis=-1, keepdims=True))
            b_end = bc[CHUNK - 1:CHUNK, :]
            m_new = jnp.maximum(b_end + m_prev, jnp.max(b_end - br + igr, axis=-1, keepdims=True))
            wk = jnp.exp(b_end - bc + igc - m_new)
            heads.append(dict(
                idx=idx, lanes=lanes, q=q_bf[:, lanes], k=k_bf[:, lanes],
                kw_t=(k_m[:, lanes] * wk).T.astype(BF16),
                v_aug=jnp.concatenate([v_m[:, lanes], ones_col], axis=1).astype(BF16),
                e=jnp.exp(dmat - m_t), w_inter=jnp.exp(m_inter - m_t), floor=jnp.exp(-m_t),
                decay=jnp.exp(b_end + m_prev - m_new), m_new=m_new))
        return dict(rows=rows, heads=heads, o_m=o_m)

    def mlstm_first_dots(st):
        for hd in st['heads']:
            cst = c_ref[hd['idx']]
            hd['qk'] = _dot_nt(hd['q'], hd['k'])
            hd['inter'] = _dot(hd['q'], cst)
            c_ref[hd['idx']] = hd['decay'] * cst + _dot(hd['kw_t'], hd['v_aug'])
            m_ref[hd['idx']] = jnp.broadcast_to(hd['m_new'], (SUBLANES, LANES))

    def mlstm_finish(st):
        rows = st['rows']
        for hd in st['heads']:
            lanes = hd['lanes']
            nd = _dot(hd['qk'] * hd['e'], hd['v_aug']) + hd['w_inter'] * hd['inter']
            num, den = nd[:, :M_HEAD_DIM], nd[:, M_HEAD_DIM:M_HEAD_DIM + 1]
            hh = num / jnp.maximum(jnp.abs(den), hd['floor'])
            y = _rms(hh, mg_ref[:, lanes]) * _sigmoid(st['o_m'][:, lanes])
            y_ref[rows, lanes] = y.astype(y_ref.dtype)

    def gla_batch(b):
        rows = slice(b * CHUNK, (b + 1) * CHUNK)
        q_g, k_g, v_g = _gla_qkv(z_ref, rows, valid_c)
        bg = bg_all[rows]
        bg_t = bg.T
        b_end = bg[CHUNK - 1:CHUNK, :]
        q_dec = (q_g * jnp.exp(bg)).astype(BF16)
        k_inv = (k_g * jnp.exp(-bg)).astype(BF16)
        k_end = k_g * jnp.exp(b_end - bg)
        v_bf = v_g.astype(BF16)
        heads = []
        for hd in range(G_HEADS):
            kl = slice(hd * G_K_DIM, (hd + 1) * G_K_DIM)
            vl = slice(hd * G_V_DIM, (hd + 1) * G_V_DIM)
            heads.append(dict(
                idx=b * G_HEADS + hd, vl=vl, yl=slice(M_WIDTH + hd * G_V_DIM, M_WIDTH + (hd + 1) * G_V_DIM),
                q_dec=q_dec[:, kl], k_inv=k_inv[:, kl], k_end_t=k_end[:, kl].T.astype(BF16), v=v_bf[:, vl],
                e_end=jnp.exp(bg_t[hd * G_K_DIM:(hd + 1) * G_K_DIM, CHUNK - 1:CHUNK])))
        return dict(rows=rows, heads=heads, unsafe=jnp.max(-b_end) > GLA_SAFE_EXP)

    def gla_first_dots(st):
        for hd in st['heads']:
            state = s_ref[hd['idx']]
            hd['a'] = _dot_nt(hd['q_dec'], hd['k_inv'])
            hd['o_inter'] = _dot(hd['q_dec'], state)
            s_ref[hd['idx']] = hd['e_end'] * state + _dot(hd['k_end_t'], hd['v'])

    def gla_finish(st):
        rows = st['rows']
        gate_g = z_ref[rows, Z_GG:Z_GG + G_WIDTH]
        for hd in st['heads']:
            vl = hd['vl']
            oi_ref[rows, vl] = hd['o_inter']
            o = _dot(jnp.where(causal, hd['a'], 0.0), hd['v']) + hd['o_inter']
            y_ref[rows, hd['yl']] = _gla_gated(o, gate_g[:, vl], gg_ref[:, vl]).astype(y_ref.dtype)

    def gla_redo_batch(b, carry):
        rows = pl.ds(pl.multiple_of(b * CHUNK, CHUNK), CHUNK)
        q_g, k_g, v_g = _gla_qkv(z_ref, rows, valid_c)
        bg = bg_ref[rows, :]
        gate_g = z_ref[rows, Z_GG:Z_GG + G_WIDTH]
        for hd in range(G_HEADS):
            kl = slice(hd * G_K_DIM, (hd + 1) * G_K_DIM)
            vl = slice(hd * G_V_DIM, (hd + 1) * G_V_DIM)
            yl = slice(M_WIDTH + hd * G_V_DIM, M_WIDTH + (hd + 1) * G_V_DIM)
            a = jnp.where(causal, _gla_intra_pairwise(q_g[:, kl], k_g[:, kl], bg[:, kl], kbuf, bbuf), 0.0)
            y = _gla_gated(_dot(a, v_g[:, vl]) + oi_ref[rows, vl], gate_g[:, vl], gg_ref[:, vl])
            y_ref[rows, yl] = y.astype(y_ref.dtype)
        return carry

    project_next(1)
    routed = route_logits()
    mlstm = [mlstm_batch(b) for b in range(bn)]
    gla = [gla_batch(b) for b in range(bn)]
    for b in range(bn):
        mlstm_first_dots(mlstm[b])
        gla_first_dots(gla[b])
    project_next(2)
    route_assign(*routed)
    for b in range(bn):
        mlstm_finish(mlstm[b])
        gla_finish(gla[b])
    unsafe = functools.reduce(jnp.logical_or, [st['unsafe'] for st in gla])

    @pl.when(unsafe)
    def _():
        lax.fori_loop(0, bn, gla_redo_batch, 0)


def _layer(stream, norm_g, w_in, conv_w, conv_b, gb_row, gb_col, m_norm_g, a_w2, a_b, g_norm_g,
           w_out, norm2_g, wr_hi, wr_lo, br, bn, n_chunks):
    rows = bn * CHUNK
    n = n_chunks * rows
    last = n_chunks - 1
    full = lambda shape: pl.BlockSpec(shape, lambda c: (0,) * len(shape))
    h_at = lambda m: pl.BlockSpec((rows, D_MODEL), m)
    prev = lambda c: (jnp.maximum(c - 1, 0), 0)
    embed = isinstance(stream, tuple)
    if embed:
        x4, meta = stream
        frames = lambda m: pl.BlockSpec((bn, 1, CHUNK, D_MODEL), m)
        stream_args = (meta, x4, x4)
        stream_specs = [
            full((N_META, D_MODEL)),
            frames(lambda c: (0, jnp.minimum(c, last - 1), 0, 0)),
            frames(lambda c: (0, jnp.clip(c - 2, 0, last - 1), 0, 0)),
        ]
    else:
        stream_args = (stream, stream, stream)
        stream_specs = [
            h_at(lambda c: (jnp.minimum(c, last), 0)),
            h_at(lambda c: (jnp.minimum(c + 1, last), 0)),
            h_at(prev),
        ]
    return pl.pallas_call(
        functools.partial(_layer_kernel, embed),
        grid=(n_chunks + 1,),
        in_specs=stream_specs + [
            full((1, D_MODEL)), full((D_MODEL, Z_WIDTH)),
            full((CONV_K, 2 * M_WIDTH)), full((1, 2 * M_WIDTH)),
            full((1, LANES)), full((2 * M_HEADS, rows)),
            full((1, M_WIDTH)), full((G_RANK, G_K_WIDTH)), full((1, G_K_WIDTH)), full((1, G_WIDTH)),
            full((D_MODEL, D_MODEL)), full((1, D_MODEL)), full((D_MODEL, LANES)), full((D_MODEL, LANES)),
            full((1, LANES)),
        ],
        out_specs=[
            h_at(prev),
            pl.BlockSpec((rows * SUBLANES, LANES), prev),
            pl.BlockSpec((1, SUBLANES, rows), lambda c: (jnp.maximum(c - 1, 0), 0, 0)),
            pl.BlockSpec((1, LANES), lambda c: (0, 0)),
        ],
        out_shape=[
            jax.ShapeDtypeStruct((n, D_MODEL), F32),
            jax.ShapeDtypeStruct((n * SUBLANES, LANES), jnp.uint32),
            jax.ShapeDtypeStruct((n_chunks, SUBLANES, rows), jnp.int32),
            jax.ShapeDtypeStruct((1, LANES), F32),
        ],
        scratch_shapes=[
            pltpu.VMEM((rows, Z_WIDTH), F32), pltpu.VMEM((rows, Z_WIDTH), F32),
            pltpu.VMEM((rows, D_MODEL), BF16), pltpu.VMEM((rows, D_MODEL), BF16),
            pltpu.VMEM((bn, SUBLANES, 2 * M_WIDTH), F32),
            pltpu.VMEM((bn * M_HEADS, M_HEAD_DIM, 2 * M_HEAD_DIM), F32),
            pltpu.VMEM((bn * M_HEADS, SUBLANES, LANES), F32),
            pltpu.VMEM((bn * G_HEADS, G_K_DIM, G_V_DIM), F32),
            pltpu.VMEM((rows, G_WIDTH), F32),
            pltpu.VMEM((rows, G_K_WIDTH), F32),
            pltpu.VMEM((CHUNK, G_K_DIM), F32), pltpu.VMEM((CHUNK, G_K_DIM), F32),
            pltpu.VMEM((1, LANES), F32),
        ],
        compiler_params=_params("arbitrary"),
        name="layer",
    )(*stream_args, norm_g, w_in, conv_w, conv_b, gb_row, gb_col, m_norm_g, a_w2, a_b, g_norm_g,
      w_out, norm2_g, wr_hi, wr_lo, br)


def _router_logits(read_h, y_ref, wo_ref, g_ref, wrh_ref, wrl_ref, br_ref, h1_ref):
    h1 = read_h() + jnp.dot(y_ref[...], wo_ref[...], preferred_element_type=F32)
    h1_ref[...] = h1
    xn = _rms(h1, g_ref[...])
    x_hi = xn.astype(BF16)
    x_lo = (xn - x_hi.astype(F32)).astype(BF16)
    dot = functools.partial(jnp.dot, preferred_element_type=F32)
    logits = dot(x_hi, wrh_ref[...]) + dot(x_hi, wrl_ref[...]) + dot(x_lo, wrh_ref[...]) + br_ref[...]
    return xn, logits


def _router_assign(live, xe_ref, route_ref, cnt_ref, carry_ref, xn, logits):
    rows = xn.shape[0]
    lane = lax.broadcasted_iota(jnp.int32, (rows, LANES), 1)
    neg_inf = -jnp.inf
    big = jnp.int32(LANES)

    def first_argmax(vals):
        top = jnp.max(vals, axis=-1, keepdims=True)
        return top, jnp.min(jnp.where(vals == top, lane, big), axis=-1, keepdims=True)

    lg = jnp.where(lane < N_GROUPS, logits, neg_inf)
    g_max, g_sel = first_argmax(lg)
    p_sel = 1.0 / jnp.sum(jnp.exp(lg - g_max), axis=-1, keepdims=True)
    in_group = jnp.logical_and(lane >= N_GROUPS + EXPERTS_PER_GROUP * g_sel,
                               lane < N_GROUPS + EXPERTS_PER_GROUP * (g_sel + 1))
    le = jnp.where(in_group, logits, neg_inf)
    v1, i1 = first_argmax(le)
    v2, i2 = first_argmax(jnp.where(lane == i1, neg_inf, le))
    e21 = jnp.exp(v2 - v1)
    gate1 = p_sel / (1.0 + e21)
    gate2 = p_sel * e21 / (1.0 + e21)
    j1 = i1 - N_GROUPS - EXPERTS_PER_GROUP * g_sel
    j2 = i2 - N_GROUPS - EXPERTS_PER_GROUP * g_sel
    ja, jb = jnp.minimum(j1, j2), jnp.maximum(j1, j2)
    ga = jnp.where(j1 < j2, gate1, gate2)
    gb = jnp.where(j1 < j2, gate2, gate1)
    pair = ((ja * (2 * EXPERTS_PER_GROUP - 1 - ja)) >> 1) + (jb - ja - 1)
    cls = g_sel * N_PAIRS + pair

    bits = lax.bitcast_convert_type(xn.astype(jnp.bfloat16).astype(F32), jnp.uint32)
    half = D_MODEL // 2
    packed = bits[:, half:] | (bits[:, :half] >> 16)
    gates = lax.bitcast_convert_type(jnp.where(lane == 0, ga, jnp.where(lane == 1, gb, 0.0)), jnp.uint32)
    for sg in range(SUBLANES):
        if sg < half // LANES:
            val = packed[:, sg * LANES:(sg + 1) * LANES]
        else:
            val = gates if sg == half // LANES else jnp.zeros((rows, LANES), jnp.uint32)
        xe_ref[pl.ds(sg, rows, stride=SUBLANES), :] = val

    onehot = jnp.where(jnp.logical_and(live, lane == cls), 1.0, 0.0).astype(F32)
    r_i = lax.broadcasted_iota(jnp.int32, (rows, rows), 0)
    c_i = lax.broadcasted_iota(jnp.int32, (rows, rows), 1)
    incl = jnp.where(c_i <= r_i, 1.0, 0.0).astype(BF16)
    prefix = jnp.dot(incl, onehot.astype(BF16), preferred_element_type=F32)
    rank = jnp.sum(onehot * (prefix - 1.0 + carry_ref[...]), axis=-1, keepdims=True)
    carry_ref[...] = carry_ref[...] + prefix[rows - 1:rows, :]
    cnt_ref[...] = carry_ref[...]
    route = jnp.where(lane == 0, cls, jnp.where(lane == 1, rank.astype(jnp.int32), 0))
    route_ref[0] = route.T[0:SUBLANES, :]


def _dispatch_kernel(off_ref, gap_ref, used_ref, cls_ref, rank_ref, x_ref, xs_hbm, stage, zeros, sem, fill_sem):
    i = pl.program_id(0)
    rows = cls_ref.shape[-1]
    s = lax.rem(i, 2)
    n_slots = xs_hbm.shape[0]

    def wait(slot):
        pltpu.make_async_copy(stage.at[slot], xs_hbm.at[pl.ds(0, rows)], sem.at[slot]).wait()

    def for_gaps(act):
        def class_gap(cl, carry):
            size = gap_ref[cl]

            @pl.when(size > 0)
            def _():
                act(pltpu.make_async_copy(zeros.at[pl.ds(0, size)],
                                          xs_hbm.at[pl.ds(off_ref[cl] + used_ref[cl], size)], fill_sem))
            return carry
        lax.fori_loop(0, N_CLASSES, class_gap, 0)

        def tail_block(b, carry):
            act(pltpu.make_async_copy(zeros, xs_hbm.at[pl.ds(b * EXPERT_ROWS, EXPERT_ROWS)], fill_sem))
            return carry
        lax.fori_loop(used_ref[N_CLASSES], n_slots // EXPERT_ROWS, tail_block, 0)

    @pl.when(i == 0)
    def _():
        zeros[...] = jnp.zeros_like(zeros)
        for_gaps(lambda copy: copy.start())

    @pl.when(i >= 2)
    def _():
        wait(s)

    stage[s] = x_ref[...]
    for k in range(rows):
        dst = off_ref[cls_ref[0, 0, k]] + rank_ref[0, 0, k]
        pltpu.make_async_copy(stage.at[s].at[k], xs_hbm.at[dst], sem.at[s]).start(priority=k % 2)

    @pl.when(i == pl.num_programs(0) - 1)
    def _():
        wait(s)

        @pl.when(i >= 1)
        def _():
            wait(1 - s)
        for_gaps(lambda copy: copy.wait())


def _dispatch(off, gap, used, cls, rank, xe, n_slots):
    n_steps, _, rows = cls.shape
    smem = pl.BlockSpec((1, 1, rows), lambda i, *_: (i, 0, 0), memory_space=pltpu.SMEM)
    grid_spec = pltpu.PrefetchScalarGridSpec(
        num_scalar_prefetch=3,
        grid=(n_steps,),
        in_specs=[smem, smem, pl.BlockSpec((rows, SUBLANES, LANES), lambda i, *_: (i, 0, 0))],
        out_specs=pl.BlockSpec(memory_space=pl.ANY),
        scratch_shapes=[pltpu.VMEM((2, rows, SUBLANES, LANES), xe.dtype),
                        pltpu.VMEM((EXPERT_ROWS, SUBLANES, LANES), xe.dtype),
                        pltpu.SemaphoreType.DMA((2,)), pltpu.SemaphoreType.DMA(())],
    )
    return pl.pallas_call(
        _dispatch_kernel,
        grid_spec=grid_spec,
        out_shape=jax.ShapeDtypeStruct((n_slots, SUBLANES, LANES), xe.dtype),
        compiler_params=_params("arbitrary"),
        name="dispatch",
    )(off, gap, used, cls, rank, xe)


BLOCKS_PER_STEP = 2


def _expert_kernel(ea_ref, eb_ref, nused_ref, xs_hbm, *refs):
    weights = refs[:6 * BLOCKS_PER_STEP]
    ys_hbm, xbuf, obuf, sem_in, sem_out = refs[6 * BLOCKS_PER_STEP:]
    i = pl.program_id(0)
    steps_used = (nused_ref[0] + BLOCKS_PER_STEP - 1) // BLOCKS_PER_STEP
    slot = lax.rem(i, 2)
    block = EXPERT_ROWS * SUBLANES
    tile = BLOCKS_PER_STEP * block

    def in_copy(step, s):
        return pltpu.make_async_copy(xs_hbm.at[pl.ds(pl.multiple_of(step * tile, tile), tile), :],
                                     xbuf.at[pl.ds(pl.multiple_of(s * tile, tile), tile), :], sem_in.at[s])

    def out_copy(step, s):
        return pltpu.make_async_copy(obuf.at[pl.ds(pl.multiple_of(s * tile, tile), tile), :],
                                     ys_hbm.at[pl.ds(pl.multiple_of(step * tile, tile), tile), :], sem_out.at[s])

    @pl.when(i < steps_used)
    def _():
        @pl.when(i == 0)
        def _():
            in_copy(0, 0).start()

        in_copy(i, slot).wait()

        @pl.when(i + 1 < steps_used)
        def _():
            in_copy(i + 1, 1 - slot).start()

        @pl.when(i >= 2)
        def _():
            out_copy(jnp.maximum(i - 2, 0), slot).wait()

        as_f32 = lambda w: lax.bitcast_convert_type(w, F32)
        for blk in range(BLOCKS_PER_STEP):
            w_a, w_b = weights[6 * blk:6 * blk + 3], weights[6 * blk + 3:6 * blk + 6]
            base = slot * tile + blk * block
            seg = lambda sg: xbuf[pl.ds(base + sg, EXPERT_ROWS, stride=SUBLANES), :]
            words = [seg(sg) for sg in range(D_MODEL // 2 // LANES)]
            xb = jnp.concatenate([as_f32(w << 16) for w in words]
                                 + [as_f32(w & jnp.uint32(0xFFFF0000)) for w in words], axis=1).astype(BF16)
            gates = as_f32(seg(D_MODEL // 2 // LANES))
            ga, gb = gates[:, 0:1], gates[:, 1:2]

            def mlp(wg, wu, wd):
                gate = jnp.dot(xb, wg[0, 0], preferred_element_type=F32)
                up = jnp.dot(xb, wu[0, 0], preferred_element_type=F32)
                return jnp.dot((gate * _sigmoid(gate) * up).astype(BF16), wd[0, 0], preferred_element_type=F32)

            y = ga * mlp(*w_a) + gb * mlp(*w_b)
            for sg in range(SUBLANES):
                obuf[pl.ds(base + sg, EXPERT_ROWS, stride=SUBLANES), :] = y[:, sg * LANES:(sg + 1) * LANES]
        out_copy(i, slot).start()

        @pl.when(i == steps_used - 1)
        def _():
            out_copy(i, slot).wait()

            @pl.when(i >= 1)
            def _():
                out_copy(jnp.maximum(i - 1, 0), 1 - slot).wait()

    @pl.when(i == steps_used)
    def _():
        obuf[0:tile, :] = jnp.zeros((tile, LANES), F32)
        n_steps = pl.num_programs(0)

        def each(act):
            def body(step, carry):
                act(out_copy(step, 0))
                return carry
            lax.fori_loop(steps_used, n_steps, body, 0)
        each(lambda copy: copy.start())
        each(lambda copy: copy.wait())


def _experts(blk_ea, blk_eb, n_used, xs, w_gate, w_up, w_down, layer):
    n_blocks = blk_ea.shape[0]
    tile = BLOCKS_PER_STEP * EXPERT_ROWS * SUBLANES
    weight_specs = []
    for blk in range(BLOCKS_PER_STEP):
        at = lambda tab, blk=blk: (lambda i, ea, eb, nu: (layer, (ea, eb)[tab][BLOCKS_PER_STEP * i + blk], 0, 0))
        for tab in (0, 1):
            weight_specs += [pl.BlockSpec((1, 1, D_MODEL, D_EXPERT), at(tab)),
                             pl.BlockSpec((1, 1, D_MODEL, D_EXPERT), at(tab)),
                             pl.BlockSpec((1, 1, D_EXPERT, D_MODEL), at(tab))]
    grid_spec = pltpu.PrefetchScalarGridSpec(
        num_scalar_prefetch=3,
        grid=(n_blocks // BLOCKS_PER_STEP,),
        in_specs=[pl.BlockSpec(memory_space=pl.ANY)] + weight_specs,
        out_specs=pl.BlockSpec(memory_space=pl.ANY),
        scratch_shapes=[
            pltpu.VMEM((2 * tile, LANES), jnp.uint32),
            pltpu.VMEM((2 * tile, LANES), F32),
            pltpu.SemaphoreType.DMA((2,)),
            pltpu.SemaphoreType.DMA((2,)),
        ],
    )
    return pl.pallas_call(
        _expert_kernel,
        grid_spec=grid_spec,
        out_shape=jax.ShapeDtypeStruct((n_blocks * EXPERT_ROWS * SUBLANES, LANES), F32),
        compiler_params=_params("arbitrary"),
        name="experts",
    )(blk_ea, blk_eb, n_used, xs, *([w_gate, w_up, w_down] * (2 * BLOCKS_PER_STEP)))


def _pair_tables():
    ea, eb = [], []
    for g in range(N_GROUPS):
        for a in range(EXPERTS_PER_GROUP):
            for b in range(a + 1, EXPERTS_PER_GROUP):
                ea.append(g * EXPERTS_PER_GROUP + a)
                eb.append(g * EXPERTS_PER_GROUP + b)
    return np.asarray(ea, np.int32), np.asarray(eb, np.int32)


def _plan_blocks(counts, n):
    cnt = counts[0, :N_CLASSES].astype(jnp.int32)
    padded = (cnt + EXPERT_ROWS - 1) // EXPERT_ROWS * EXPERT_ROWS
    pad_end = jnp.cumsum(padded)
    off = pad_end - padded
    n_blocks = -(-(n + N_CLASSES * (EXPERT_ROWS - 1)) // EXPERT_ROWS)
    n_blocks = -(-n_blocks // BLOCKS_PER_STEP) * BLOCKS_PER_STEP
    blk_start = jnp.arange(n_blocks, dtype=jnp.int32) * EXPERT_ROWS
    blk_cls = jnp.minimum(jnp.sum(pad_end[None, :] <= blk_start[:, None], axis=1), N_CLASSES - 1)
    ea_tab, eb_tab = _pair_tables()
    n_used = (pad_end[-1] // EXPERT_ROWS).astype(jnp.int32).reshape(1)
    used = jnp.concatenate([cnt, n_used])
    return (jnp.asarray(ea_tab)[blk_cls], jnp.asarray(eb_tab)[blk_cls], n_used, off.astype(jnp.int32),
            (padded - cnt).astype(jnp.int32), used.astype(jnp.int32), n_blocks)


def _final_kernel(off_ref, cls_ref, rank_ref, ncls_ref, nrank_ref, h_ref, ys_hbm, g_ref, o_ref, ybuf, sem):
    y = _moe_rows(pl.program_id(0), pl.num_programs(0),
                  _MoeStream(off_ref, (cls_ref, rank_ref), (ncls_ref, nrank_ref), ys_hbm, ybuf, sem))
    out = _rms(h_ref[...] + y, g_ref[...])
    o_ref[...] = out.reshape(o_ref.shape)


def _final(off, cls, rank, h, ys, g, bn):
    n_chunks, _, rows = cls.shape
    smem = lambda m: pl.BlockSpec((1, 1, rows), m, memory_space=pltpu.SMEM)
    cur = lambda c, off: (c + 1, 0, 0)
    nxt = lambda c, off: (jnp.minimum(c + 2, n_chunks - 1), 0, 0)
    grid_spec = pltpu.PrefetchScalarGridSpec(
        num_scalar_prefetch=1,
        grid=(n_chunks - 1,),
        in_specs=[
            smem(cur), smem(cur), smem(nxt), smem(nxt),
            pl.BlockSpec((rows, D_MODEL), lambda c, off: (c + 1, 0)),
            pl.BlockSpec(memory_space=pl.ANY),
            pl.BlockSpec((1, D_MODEL), lambda c, off: (0, 0)),
        ],
        out_specs=pl.BlockSpec((bn, 1, CHUNK, D_MODEL), lambda c, off: (0, c, 0, 0)),
        scratch_shapes=[pltpu.VMEM((2 * rows * SUBLANES, LANES), F32), pltpu.SemaphoreType.DMA((2,))],
    )
    return pl.pallas_call(
        _final_kernel,
        grid_spec=grid_spec,
        out_shape=jax.ShapeDtypeStruct((bn, n_chunks - 1, CHUNK, D_MODEL), F32),
        compiler_params=_params("arbitrary"),
        name="final_norm",
    )(off, cls, rank, cls, rank, h, ys, g)


def _repack_w_in(w):
    o = np.cumsum([0, 2 * M_WIDTH, M_WIDTH, M_WIDTH, M_HEADS, M_HEADS, G_K_WIDTH, G_K_WIDTH, G_WIDTH, G_WIDTH,
                   G_RANK])
    seg = lambda i: w[:, o[i]:o[i + 1]]
    small = jnp.concatenate([seg(3), seg(4), seg(9)], axis=1)
    small = jnp.pad(small, ((0, 0), (0, LANES - small.shape[1])))
    return jnp.concatenate([seg(0), seg(1), seg(2), seg(5), seg(6), seg(7), seg(8), small], axis=1).astype(BF16)


def kernel(x, meta_tokens, norm1_g, w_in, conv_w, conv_b, gate_b, m_norm_g, a_w2, a_b, g_norm_g, w_out, norm2_g,
           wr_g, br_g, wr_e, br_e, w_gate, w_up, w_down, final_norm_g):
    bn, seq, dm = x.shape
    depth = w_in.shape[0]
    assert dm == D_MODEL and seq % CHUNK == 0
    n_chunks = seq // CHUNK + 1
    rows = bn * CHUNK
    n = n_chunks * rows
    x4 = x.reshape(bn, seq // CHUNK, CHUNK, dm)
    row = lambda v: v.reshape(1, -1).astype(F32)
    w_gate_bf, w_up_bf, w_down_bf = w_gate.astype(BF16), w_up.astype(BF16), w_down.astype(BF16)

    h = ys = off = cls = rank = None
    for l in range(depth):
        stream = (x4, meta_tokens.astype(F32)) if l == 0 else _add_moe(off, cls, rank, h, ys)
        gb = gate_b[l].astype(F32).reshape(1, 2 * M_HEADS)
        wr = jnp.pad(jnp.concatenate([wr_g[l], wr_e[l]], axis=1).astype(F32),
                     ((0, 0), (0, LANES - N_GROUPS - N_EXPERTS)))
        wr_hi = wr.astype(BF16)
        wr_lo = (wr - wr_hi.astype(F32)).astype(BF16)
        br = jnp.pad(jnp.concatenate([br_g[l], br_e[l]]).astype(F32), (0, LANES - N_GROUPS - N_EXPERTS))
        h, xe, route, counts = _layer(
            stream, row(norm1_g[l]), _repack_w_in(w_in[l]), conv_w[l].astype(F32), row(conv_b[l]),
            jnp.pad(gb, ((0, 0), (0, LANES - 2 * M_HEADS))),
            jnp.broadcast_to(gb.reshape(2 * M_HEADS, 1), (2 * M_HEADS, rows)),
            row(m_norm_g[l]), a_w2[l].astype(F32), row(a_b[l]), row(g_norm_g[l]),
            w_out[l].astype(BF16), row(norm2_g[l]), wr_hi, wr_lo, br.reshape(1, LANES), bn, n_chunks)
        blk_ea, blk_eb, n_used, off, gap, used, n_blocks = _plan_blocks(counts, n)
        cls, rank = route[:, 0:1, :], route[:, 1:2, :]
        xs = _dispatch(off, gap, used, cls, rank, xe.reshape(n, SUBLANES, LANES), n_blocks * EXPERT_ROWS)
        ys = _experts(blk_ea, blk_eb, n_used, xs.reshape(n_blocks * EXPERT_ROWS * SUBLANES, LANES),
                      w_gate_bf, w_up_bf, w_down_bf, l)
    out = _final(off, cls, rank, h, ys, row(final_norm_g), bn)
    return out.reshape(bn, seq, dm)
```

```python
import functools

import numpy as np
import jax
import jax.numpy as jnp
from jax import lax
from jax.experimental import pallas as pl
from jax.experimental.pallas import tpu as pltpu

F32 = jnp.float32
BF16 = jnp.bfloat16

D_MODEL = 1024
CHUNK = 64
N_META = 16
LEAD_PAD = CHUNK - N_META
EPS = 1e-6
M_HEADS = 4
M_WIDTH = 512
M_HEAD_DIM = 128
CONV_K = 4
G_HEADS = 4
G_WIDTH = 512
G_V_DIM = 128
G_K_DIM = 64
G_K_WIDTH = 256
G_RANK = 16
G_TAU = 16.0
N_GROUPS = 4
EXPERTS_PER_GROUP = 8
N_EXPERTS = 32
D_EXPERT = 256
N_PAIRS = EXPERTS_PER_GROUP * (EXPERTS_PER_GROUP - 1) // 2
N_CLASSES = N_GROUPS * N_PAIRS

LANES = 128
SUBLANES = 8
EXPERT_ROWS = 128
VMEM_LIMIT = 56 * 1024 * 1024

Z_QM, Z_KM, Z_VM, Z_OM = 0, 512, 1024, 1536
Z_QG, Z_KG, Z_VG, Z_GG = 2048, 2304, 2560, 3072
Z_SMALL = 3584
Z_WIDTH = Z_SMALL + LANES
GLA_SAFE_EXP = 80.0


def _log_sigmoid(x):
    return jnp.minimum(x, 0.0) - jnp.log(1.0 + jnp.exp(-jnp.abs(x)))


def _sigmoid(x):
    return 0.5 * jnp.tanh(0.5 * x) + 0.5


def _rms(x, g):
    return x * lax.rsqrt(jnp.mean(x * x, axis=-1, keepdims=True) + EPS) * g


def _dot(a, b):
    return jnp.dot(a.astype(BF16), b.astype(BF16), preferred_element_type=F32)


def _dot_nt(a, b):
    return lax.dot_general(a.astype(BF16), b.astype(BF16), (((1,), (1,)), ((), ())),
                           preferred_element_type=F32)


def _params(*semantics):
    return pltpu.CompilerParams(dimension_semantics=semantics, vmem_limit_bytes=VMEM_LIMIT)


class _MoeStream:
    def __init__(self, off_ref, cur, nxt, ys_hbm, ybuf, sem):
        self.off_ref, self.cur, self.nxt, self.ys_hbm, self.ybuf, self.sem = off_ref, cur, nxt, ys_hbm, ybuf, sem
        self.rows = cur[0].shape[-1]
        self.tile = self.rows * SUBLANES

    def start(self, idx, slot):
        cls_ref, rank_ref = idx
        for k in range(self.rows):
            src = pl.multiple_of((self.off_ref[cls_ref[0, 0, k]] + rank_ref[0, 0, k]) * SUBLANES, SUBLANES)
            dst = pl.multiple_of(slot * self.tile + k * SUBLANES, SUBLANES)
            pltpu.make_async_copy(self.ys_hbm.at[pl.ds(src, SUBLANES), :], self.ybuf.at[pl.ds(dst, SUBLANES), :],
                                  self.sem.at[slot]).start(priority=k % 2)

    def wait(self, slot):
        half = self.ybuf.at[pl.ds(pl.multiple_of(slot * self.tile, self.tile), self.tile), :]
        pltpu.make_async_copy(self.ys_hbm.at[pl.ds(0, self.tile), :], half, self.sem.at[slot]).wait()

    def read(self, slot):
        base = slot * self.tile
        return jnp.concatenate([self.ybuf[pl.ds(base + sg, self.rows, stride=SUBLANES), :]
                                for sg in range(SUBLANES)], axis=1)

    def take(self, slot):
        self.wait(slot)
        self.start(self.nxt, 1 - slot)
        return self.read(slot)


def _moe_rows(i, n_steps, stream):
    s = lax.rem(i, 2)

    @pl.when(i == 0)
    def _():
        stream.start(stream.cur, 0)

    rows = stream.take(s)

    @pl.when(i == n_steps - 1)
    def _():
        stream.wait(1 - s)
    return rows


def _add_moe_kernel(off_ref, cls_ref, rank_ref, ncls_ref, nrank_ref, h_ref, ys_hbm, ho_ref, ybuf, sem):
    y = _moe_rows(pl.program_id(0), pl.num_programs(0),
                  _MoeStream(off_ref, (cls_ref, rank_ref), (ncls_ref, nrank_ref), ys_hbm, ybuf, sem))
    ho_ref[...] = h_ref[...] + y


def _add_moe(off, cls, rank, h, ys):
    n = h.shape[0]
    n_steps, _, rows = cls.shape
    smem = lambda m: pl.BlockSpec((1, 1, rows), m, memory_space=pltpu.SMEM)
    cur = lambda i, off: (i, 0, 0)
    nxt = lambda i, off: (jnp.minimum(i + 1, n_steps - 1), 0, 0)
    grid_spec = pltpu.PrefetchScalarGridSpec(
        num_scalar_prefetch=1,
        grid=(n_steps,),
        in_specs=[
            smem(cur), smem(cur), smem(nxt), smem(nxt),
            pl.BlockSpec((rows, D_MODEL), lambda i, off: (i, 0)),
            pl.BlockSpec(memory_space=pl.ANY),
        ],
        out_specs=pl.BlockSpec((rows, D_MODEL), lambda i, off: (i, 0)),
        scratch_shapes=[pltpu.VMEM((2 * rows * SUBLANES, LANES), F32), pltpu.SemaphoreType.DMA((2,))],
    )
    return pl.pallas_call(
        _add_moe_kernel,
        grid_spec=grid_spec,
        out_shape=jax.ShapeDtypeStruct((n, D_MODEL), F32),
        compiler_params=_params("arbitrary"),
        name="add_moe",
    )(off, cls, rank, cls, rank, h, ys)


def _gla_intra_pairwise(q, k, bg, kbuf, bbuf):
    kbuf[...] = k
    bbuf[...] = bg
    lane = lax.broadcasted_iota(jnp.int32, (CHUNK, CHUNK), 1)

    def column(s, a):
        ks = kbuf[pl.ds(s, 1), :]
        bs = bbuf[pl.ds(s, 1), :]
        col = jnp.sum(q * ks * jnp.exp(jnp.minimum(bg - bs, 0.0)), axis=-1, keepdims=True)
        return jnp.where(lane == s, col, a)

    return lax.fori_loop(0, CHUNK, column, jnp.zeros((CHUNK, CHUNK), F32))


def _gla_qkv(z_ref, rows, valid_c):
    q_g = jnp.where(valid_c, z_ref[rows, Z_QG:Z_QG + G_K_WIDTH], 0.0) * (G_K_DIM ** -0.5)
    k_g = jnp.where(valid_c, z_ref[rows, Z_KG:Z_KG + G_K_WIDTH], 0.0)
    v_g = jnp.where(valid_c, z_ref[rows, Z_VG:Z_VG + G_WIDTH], 0.0)
    return q_g, k_g, v_g


def _split3(x):
    hi = x.astype(BF16)
    r = x - hi.astype(F32)
    mid = r.astype(BF16)
    return hi, mid, (r - mid.astype(F32)).astype(BF16)


def _gla_gated(o, gate, g):
    return _rms(o, g) * (gate * _sigmoid(gate))


IN_PROJ_SPLITS = (0, 1280, 2560, Z_WIDTH)


def _layer_kernel(embed, *refs):
    c = pl.program_id(0)
    (s0_ref, s1_ref, s2_ref), refs = refs[:3], refs[3:]
    (g1_ref, w_ref, tri_ref, trit_ref, cw_ref, cb_ref, gbr_ref, gbc_ref, mg_ref, awh_ref, awl_ref, ab_ref, gg_ref,
     wo_ref, g2_ref, wrh_ref, wrl_ref, br_ref, h1_ref, xe_ref, route_ref, cnt_ref, za_ref, zb_ref, ya_ref, yb_ref, tail_ref, c_ref, m_ref, s_ref, oi_ref,
     bg_ref, kbuf, bbuf, carry_ref) = refs
    rows = za_ref.shape[0]
    if embed:
        lead = jnp.concatenate([jnp.zeros((LEAD_PAD, D_MODEL), F32), s0_ref[...]] * (rows // CHUNK), axis=0)
        read_first = lambda: lead
        read_next = lambda: s1_ref[...].reshape(rows, D_MODEL)
        read_prev = lambda: jnp.where(c <= 1, lead, s2_ref[...].reshape(rows, D_MODEL))
    else:
        read_first, read_next, read_prev = (lambda: s0_ref[...]), (lambda: s1_ref[...]), (lambda: s2_ref[...])

    @pl.when(c == 0)
    def _():
        tail_ref[...] = jnp.zeros_like(tail_ref)
        c_ref[...] = jnp.zeros_like(c_ref)
        m_ref[...] = jnp.zeros_like(m_ref)
        s_ref[...] = jnp.zeros_like(s_ref)
        carry_ref[...] = jnp.zeros_like(carry_ref)
        yb_ref[...] = jnp.zeros_like(yb_ref)
        za_ref[...] = _dot(_rms(read_first(), g1_ref[...]), w_ref[...])

    args = (g1_ref, w_ref, tri_ref, trit_ref, cw_ref, cb_ref, gbr_ref, gbc_ref, mg_ref, awh_ref, awl_ref, ab_ref,
            gg_ref)
    state = (tail_ref, c_ref, m_ref, s_ref, oi_ref, bg_ref, kbuf, bbuf)
    parity = lax.rem(c, 2)

    def step(z_ref, zn_ref, y_ref, y_prev_ref):
        first = functools.partial(_router_logits, read_prev, y_prev_ref, wo_ref, g2_ref, wrh_ref, wrl_ref, br_ref,
                                  h1_ref)
        second = functools.partial(_router_assign, c > 0, xe_ref, route_ref, cnt_ref, carry_ref)
        _mixer_step(c, z_ref, zn_ref, read_next, *args, y_ref, *state, (first, second))

    @pl.when(parity == 0)
    def _():
        step(za_ref, zb_ref, ya_ref, yb_ref)

    @pl.when(parity == 1)
    def _():
        step(zb_ref, za_ref, yb_ref, ya_ref)


def _mixer_step(c, z_ref, zn_ref, read_next, g1_ref, w_ref, tri_ref, trit_ref, cw_ref, cb_ref, gbr_ref, gbc_ref,
                mg_ref, awh_ref, awl_ref, ab_ref, gg_ref, y_ref, tail_ref, c_ref, m_ref, s_ref, oi_ref, bg_ref,
                kbuf, bbuf, router):
    route_logits, route_assign = router
    bn = z_ref.shape[0] // CHUNK
    xn_next = _rms(read_next(), g1_ref[...]).astype(BF16)

    def project_next(part):
        lo, hi = IN_PROJ_SPLITS[part], IN_PROJ_SPLITS[part + 1]
        zn_ref[:, lo:hi] = jnp.dot(xn_next, w_ref[:, lo:hi], preferred_element_type=F32)

    rows_all = bn * CHUNK
    row = lax.broadcasted_iota(jnp.int32, (CHUNK, CHUNK), 0)
    col = lax.broadcasted_iota(jnp.int32, (CHUNK, CHUNK), 1)
    causal = col <= row
    later = jnp.logical_not(c == 0)
    valid_c = jnp.logical_or(later, lax.broadcasted_iota(jnp.int32, (CHUNK, 1), 0) >= LEAD_PAD)
    pos_c = lax.broadcasted_iota(jnp.int32, (rows_all, 1), 0) & (CHUNK - 1)
    pos_r = lax.broadcasted_iota(jnp.int32, (1, rows_all), 1) & (CHUNK - 1)
    valid_ca = jnp.logical_or(later, pos_c >= LEAD_PAD)
    valid_ra = jnp.logical_or(later, pos_r >= LEAD_PAD)
    row8 = lax.broadcasted_iota(jnp.int32, (SUBLANES, 1), 0)
    ones_col = jnp.where(lax.broadcasted_iota(jnp.int32, (CHUNK, LANES), 1) == 0, 1.0, 0.0).astype(F32)
    neg_inf = -jnp.inf

    tri, tri_t = tri_ref[...], trit_ref[...]
    zs = z_ref[:, Z_SMALL:Z_SMALL + LANES]
    g_col = zs + gbr_ref[...]
    g_row = zs.T[0:2 * M_HEADS, :] + gbc_ref[...]
    code = zs[:, 2 * M_HEADS:2 * M_HEADS + G_RANK]
    code_hi = code.astype(BF16)
    code_lo = (code - code_hi.astype(F32)).astype(BF16)
    dot = functools.partial(jnp.dot, preferred_element_type=F32)
    u = dot(code_hi, awh_ref[...]) + dot(code_hi, awl_ref[...]) + dot(code_lo, awh_ref[...]) + ab_ref[...]
    project_next(0)
    la = jnp.where(valid_ca, _log_sigmoid(u) * (1.0 / G_TAU), 0.0)
    lf_col = jnp.where(valid_ca, _log_sigmoid(g_col), 0.0)
    cum = sum(jnp.dot(tri, p, preferred_element_type=F32) for p in _split3(jnp.concatenate([lf_col, la], axis=1)))
    b_col_all, bg_all = cum[:, :LANES], cum[:, LANES:]
    bg_ref[...] = bg_all
    lf_row = jnp.where(valid_ra, _log_sigmoid(g_row), 0.0)
    b_row_all = sum(jnp.dot(p, tri_t, preferred_element_type=F32) for p in _split3(lf_row))
    ig_col_all = jnp.where(valid_ca, g_col, neg_inf)
    ig_row_all = jnp.where(valid_ra, g_row, neg_inf)

    def mlstm_batch(b):
        rows = slice(b * CHUNK, (b + 1) * CHUNK)
        x = jnp.where(valid_c, z_ref[rows, Z_QM:Z_QM + 2 * M_WIDTH], 0.0)
        prev = tail_ref[b]
        acc = x * cw_ref[CONV_K - 1:CONV_K, :] + cb_ref[...]
        for k in range(1, CONV_K):
            cur = pltpu.roll(x, k, 0)
            fix = pltpu.roll(prev, k, 0)
            top = jnp.where(row8 < k, fix, cur[0:SUBLANES])
            shifted = jnp.concatenate([top, cur[SUBLANES:]], axis=0)
            acc = acc + shifted * cw_ref[CONV_K - 1 - k:CONV_K - k, :]
        tail_ref[b] = x[CHUNK - SUBLANES:]
        qk = acc * _sigmoid(acc)
        qk = jnp.where(valid_c, qk, 0.0)
        q_m = qk[:, :M_WIDTH]
        k_m = qk[:, M_WIDTH:] * (M_HEAD_DIM ** -0.5)
        v_m = jnp.where(valid_c, z_ref[rows, Z_VM:Z_VM + M_WIDTH], 0.0)
        o_m = z_ref[rows, Z_OM:Z_OM + M_WIDTH]

        b_col, ig_col = b_col_all[rows], ig_col_all[rows]
        b_row, ig_row = b_row_all[:, rows], ig_row_all[:, rows]

        q_bf, k_bf = q_m.astype(BF16), k_m.astype(BF16)
        heads = []
        for hd in range(M_HEADS):
            idx = b * M_HEADS + hd
            lanes = slice(hd * M_HEAD_DIM, (hd + 1) * M_HEAD_DIM)
            bc, br = b_col[:, M_HEADS + hd:M_HEADS + hd + 1], b_row[M_HEADS + hd:M_HEADS + hd + 1, :]
            igc, igr = ig_col[:, hd:hd + 1], ig_row[hd:hd + 1, :]
            m_prev = m_ref[idx][0:1, 0:1]
            dmat = jnp.where(causal, bc - br + igr, neg_inf)
            m_inter = bc + m_prev
            m_t = jnp.maximum(m_inter, jnp.max(dmat, axis=-1, keepdims=True))
            b_end = bc[CHUNK - 1:CHUNK, :]
            m_new = jnp.maximum(b_end + m_prev, jnp.max(b_end - br + igr, axis=-1, keepdims=True))
            wk = jnp.exp(b_end - bc + igc - m_new)
            heads.append(dict(
                idx=idx, lanes=lanes, q=q_bf[:, lanes], k=k_bf[:, lanes],
                kw_t=(k_m[:, lanes] * wk).T.astype(BF16),
                v_aug=jnp.concatenate([v_m[:, lanes], ones_col], axis=1).astype(BF16),
                e=jnp.exp(dmat - m_t), w_inter=jnp.exp(m_inter - m_t), floor=jnp.exp(-m_t),
                decay=jnp.exp(b_end + m_prev - m_new), m_new=m_new))
        return dict(rows=rows, heads=heads, o_m=o_m)

    def mlstm_first_dots(st):
        for hd in st['heads']:
            cst = c_ref[hd['idx']]
            hd['qk'] = _dot_nt(hd['q'], hd['k'])
            hd['inter'] = _dot(hd['q'], cst)
            c_ref[hd['idx']] = hd['decay'] * cst + _dot(hd['kw_t'], hd['v_aug'])
            m_ref[hd['idx']] = jnp.broadcast_to(hd['m_new'], (SUBLANES, LANES))

    def mlstm_finish(st):
        rows = st['rows']
        for hd in st['heads']:
            lanes = hd['lanes']
            nd = _dot(hd['qk'] * hd['e'], hd['v_aug']) + hd['w_inter'] * hd['inter']
            num, den = nd[:, :M_HEAD_DIM], nd[:, M_HEAD_DIM:M_HEAD_DIM + 1]
            hh = num / jnp.maximum(jnp.abs(den), hd['floor'])
            y = _rms(hh, mg_ref[:, lanes]) * _sigmoid(st['o_m'][:, lanes])
            y_ref[rows, lanes] = y.astype(y_ref.dtype)

    def gla_batch(b):
        rows = slice(b * CHUNK, (b + 1) * CHUNK)
        q_g, k_g, v_g = _gla_qkv(z_ref, rows, valid_c)
        bg = bg_all[rows]
        bg_t = bg.T
        b_end = bg[CHUNK - 1:CHUNK, :]
        q_dec = (q_g * jnp.exp(bg)).astype(BF16)
        k_inv = (k_g * jnp.exp(-bg)).astype(BF16)
        k_end = k_g * jnp.exp(b_end - bg)
        v_bf = v_g.astype(BF16)
        heads = []
        for hd in range(G_HEADS):
            kl = slice(hd * G_K_DIM, (hd + 1) * G_K_DIM)
            vl = slice(hd * G_V_DIM, (hd + 1) * G_V_DIM)
            heads.append(dict(
                idx=b * G_HEADS + hd, vl=vl, yl=slice(M_WIDTH + hd * G_V_DIM, M_WIDTH + (hd + 1) * G_V_DIM),
                q_dec=q_dec[:, kl], k_inv=k_inv[:, kl], k_end_t=k_end[:, kl].T.astype(BF16), v=v_bf[:, vl],
                e_end=jnp.exp(bg_t[hd * G_K_DIM:(hd + 1) * G_K_DIM, CHUNK - 1:CHUNK])))
        return dict(rows=rows, heads=heads, unsafe=jnp.max(-b_end) > GLA_SAFE_EXP)

    def gla_first_dots(st):
        for hd in st['heads']:
            state = s_ref[hd['idx']]
            hd['a'] = _dot_nt(hd['q_dec'], hd['k_inv'])
            hd['o_inter'] = _dot(hd['q_dec'], state)
            s_ref[hd['idx']] = hd['e_end'] * state + _dot(hd['k_end_t'], hd['v'])

    def gla_finish(st):
        rows = st['rows']
        gate_g = z_ref[rows, Z_GG:Z_GG + G_WIDTH]
        for hd in st['heads']:
            vl = hd['vl']
            oi_ref[rows, vl] = hd['o_inter']
            o = _dot(jnp.where(causal, hd['a'], 0.0), hd['v']) + hd['o_inter']
            y_ref[rows, hd['yl']] = _gla_gated(o, gate_g[:, vl], gg_ref[:, vl]).astype(y_ref.dtype)

    def gla_redo_batch(b, carry):
        rows = pl.ds(pl.multiple_of(b * CHUNK, CHUNK), CHUNK)
        q_g, k_g, v_g = _gla_qkv(z_ref, rows, valid_c)
        bg = bg_ref[rows, :]
        gate_g = z_ref[rows, Z_GG:Z_GG + G_WIDTH]
        for hd in range(G_HEADS):
            kl = slice(hd * G_K_DIM, (hd + 1) * G_K_DIM)
            vl = slice(hd * G_V_DIM, (hd + 1) * G_V_DIM)
            yl = slice(M_WIDTH + hd * G_V_DIM, M_WIDTH + (hd + 1) * G_V_DIM)
            a = jnp.where(causal, _gla_intra_pairwise(q_g[:, kl], k_g[:, kl], bg[:, kl], kbuf, bbuf), 0.0)
            y = _gla_gated(_dot(a, v_g[:, vl]) + oi_ref[rows, vl], gate_g[:, vl], gg_ref[:, vl])
            y_ref[rows, yl] = y.astype(y_ref.dtype)
        return carry

    project_next(1)
    routed = route_logits()
    mlstm = [mlstm_batch(b) for b in range(bn)]
    gla = [gla_batch(b) for b in range(bn)]
    for b in range(bn):
        mlstm_first_dots(mlstm[b])
        gla_first_dots(gla[b])
    project_next(2)
    route_assign(*routed)
    for b in range(bn):
        mlstm_finish(mlstm[b])
        gla_finish(gla[b])
    unsafe = functools.reduce(jnp.logical_or, [st['unsafe'] for st in gla])

    @pl.when(unsafe)
    def _():
        lax.fori_loop(0, bn, gla_redo_batch, 0)


def _layer(stream, norm_g, w_in, conv_w, conv_b, gb_row, gb_col, m_norm_g, a_w2, a_b, g_norm_g,
           w_out, norm2_g, wr_hi, wr_lo, br, bn, n_chunks):
    rows = bn * CHUNK
    n = n_chunks * rows
    last = n_chunks - 1
    full = lambda shape: pl.BlockSpec(shape, lambda c: (0,) * len(shape))
    h_at = lambda m: pl.BlockSpec((rows, D_MODEL), m)
    prev = lambda c: (jnp.maximum(c - 1, 0), 0)
    r_i, c_i = np.arange(rows)[:, None], np.arange(rows)[None, :]
    tri = jnp.asarray((r_i // CHUNK == c_i // CHUNK) & (c_i <= r_i), BF16)
    aw_hi = a_w2.astype(BF16)
    aw_lo = (a_w2 - aw_hi.astype(F32)).astype(BF16)
    embed = isinstance(stream, tuple)
    if embed:
        x4, meta = stream
        frames = lambda m: pl.BlockSpec((bn, 1, CHUNK, D_MODEL), m)
        stream_args = (meta, x4, x4)
        stream_specs = [
            full((N_META, D_MODEL)),
            frames(lambda c: (0, jnp.minimum(c, last - 1), 0, 0)),
            frames(lambda c: (0, jnp.clip(c - 2, 0, last - 1), 0, 0)),
        ]
    else:
        stream_args = (stream, stream, stream)
        stream_specs = [
            h_at(lambda c: (jnp.minimum(c, last), 0)),
            h_at(lambda c: (jnp.minimum(c + 1, last), 0)),
            h_at(prev),
        ]
    return pl.pallas_call(
        functools.partial(_layer_kernel, embed),
        grid=(n_chunks + 1,),
        in_specs=stream_specs + [
            full((1, D_MODEL)), full((D_MODEL, Z_WIDTH)), full((rows, rows)), full((rows, rows)),
            full((CONV_K, 2 * M_WIDTH)), full((1, 2 * M_WIDTH)),
            full((1, LANES)), full((2 * M_HEADS, rows)),
            full((1, M_WIDTH)), full((G_RANK, G_K_WIDTH)), full((G_RANK, G_K_WIDTH)), full((1, G_K_WIDTH)),
            full((1, G_WIDTH)),
            full((D_MODEL, D_MODEL)), full((1, D_MODEL)), full((D_MODEL, LANES)), full((D_MODEL, LANES)),
            full((1, LANES)),
        ],
        out_specs=[
            h_at(prev),
            pl.BlockSpec((rows * SUBLANES, LANES), prev),
            pl.BlockSpec((1, SUBLANES, rows), lambda c: (jnp.maximum(c - 1, 0), 0, 0)),
            pl.BlockSpec((1, LANES), lambda c: (0, 0)),
        ],
        out_shape=[
            jax.ShapeDtypeStruct((n, D_MODEL), F32),
            jax.ShapeDtypeStruct((n * SUBLANES, LANES), jnp.uint32),
            jax.ShapeDtypeStruct((n_chunks, SUBLANES, rows), jnp.int32),
            jax.ShapeDtypeStruct((1, LANES), F32),
        ],
        scratch_shapes=[
            pltpu.VMEM((rows, Z_WIDTH), F32), pltpu.VMEM((rows, Z_WIDTH), F32),
            pltpu.VMEM((rows, D_MODEL), BF16), pltpu.VMEM((rows, D_MODEL), BF16),
            pltpu.VMEM((bn, SUBLANES, 2 * M_WIDTH), F32),
            pltpu.VMEM((bn * M_HEADS, M_HEAD_DIM, 2 * M_HEAD_DIM), F32),
            pltpu.VMEM((bn * M_HEADS, SUBLANES, LANES), F32),
            pltpu.VMEM((bn * G_HEADS, G_K_DIM, G_V_DIM), F32),
            pltpu.VMEM((rows, G_WIDTH), F32),
            pltpu.VMEM((rows, G_K_WIDTH), F32),
            pltpu.VMEM((CHUNK, G_K_DIM), F32), pltpu.VMEM((CHUNK, G_K_DIM), F32),
            pltpu.VMEM((1, LANES), F32),
        ],
        compiler_params=_params("arbitrary"),
        name="layer",
    )(*stream_args, norm_g, w_in, tri, tri.T, conv_w, conv_b, gb_row, gb_col, m_norm_g, aw_hi, aw_lo, a_b, g_norm_g,
      w_out, norm2_g, wr_hi, wr_lo, br)


def _router_logits(read_h, y_ref, wo_ref, g_ref, wrh_ref, wrl_ref, br_ref, h1_ref):
    h1 = read_h() + jnp.dot(y_ref[...], wo_ref[...], preferred_element_type=F32)
    h1_ref[...] = h1
    xn = _rms(h1, g_ref[...])
    x_hi = xn.astype(BF16)
    x_lo = (xn - x_hi.astype(F32)).astype(BF16)
    dot = functools.partial(jnp.dot, preferred_element_type=F32)
    logits = dot(x_hi, wrh_ref[...]) + dot(x_hi, wrl_ref[...]) + dot(x_lo, wrh_ref[...]) + br_ref[...]
    return xn, logits


def _router_assign(live, xe_ref, route_ref, cnt_ref, carry_ref, xn, logits):
    rows = xn.shape[0]
    lane = lax.broadcasted_iota(jnp.int32, (rows, LANES), 1)
    neg_inf = -jnp.inf
    big = jnp.int32(LANES)

    def first_argmax(vals):
        top = jnp.max(vals, axis=-1, keepdims=True)
        return top, jnp.min(jnp.where(vals == top, lane, big), axis=-1, keepdims=True)

    lg = jnp.where(lane < N_GROUPS, logits, neg_inf)
    g_max, g_sel = first_argmax(lg)
    p_sel = 1.0 / jnp.sum(jnp.exp(lg - g_max), axis=-1, keepdims=True)
    in_group = jnp.logical_and(lane >= N_GROUPS + EXPERTS_PER_GROUP * g_sel,
                               lane < N_GROUPS + EXPERTS_PER_GROUP * (g_sel + 1))
    le = jnp.where(in_group, logits, neg_inf)
    v1, i1 = first_argmax(le)
    v2, i2 = first_argmax(jnp.where(lane == i1, neg_inf, le))
    e21 = jnp.exp(v2 - v1)
    gate1 = p_sel / (1.0 + e21)
    gate2 = p_sel * e21 / (1.0 + e21)
    j1 = i1 - N_GROUPS - EXPERTS_PER_GROUP * g_sel
    j2 = i2 - N_GROUPS - EXPERTS_PER_GROUP * g_sel
    ja, jb = jnp.minimum(j1, j2), jnp.maximum(j1, j2)
    ga = jnp.where(j1 < j2, gate1, gate2)
    gb = jnp.where(j1 < j2, gate2, gate1)
    pair = ((ja * (2 * EXPERTS_PER_GROUP - 1 - ja)) >> 1) + (jb - ja - 1)
    cls = g_sel * N_PAIRS + pair

    bits = lax.bitcast_convert_type(xn.astype(jnp.bfloat16).astype(F32), jnp.uint32)
    half = D_MODEL // 2
    packed = bits[:, half:] | (bits[:, :half] >> 16)
    gates = lax.bitcast_convert_type(jnp.where(lane == 0, ga, jnp.where(lane == 1, gb, 0.0)), jnp.uint32)
    for sg in range(SUBLANES):
        if sg < half // LANES:
            val = packed[:, sg * LANES:(sg + 1) * LANES]
        else:
            val = gates if sg == half // LANES else jnp.zeros((rows, LANES), jnp.uint32)
        xe_ref[pl.ds(sg, rows, stride=SUBLANES), :] = val

    onehot = jnp.where(jnp.logical_and(live, lane == cls), 1.0, 0.0).astype(F32)
    r_i = lax.broadcasted_iota(jnp.int32, (rows, rows), 0)
    c_i = lax.broadcasted_iota(jnp.int32, (rows, rows), 1)
    incl = jnp.where(c_i <= r_i, 1.0, 0.0).astype(BF16)
    prefix = jnp.dot(incl, onehot.astype(BF16), preferred_element_type=F32)
    rank = jnp.sum(onehot * (prefix - 1.0 + carry_ref[...]), axis=-1, keepdims=True)
    carry_ref[...] = carry_ref[...] + prefix[rows - 1:rows, :]
    cnt_ref[...] = carry_ref[...]
    route = jnp.where(lane == 0, cls, jnp.where(lane == 1, rank.astype(jnp.int32), 0))
    route_ref[0] = route.T[0:SUBLANES, :]


def _dispatch_kernel(off_ref, gap_ref, used_ref, cls_ref, rank_ref, x_ref, xs_hbm, stage, zeros, sem, fill_sem):
    i = pl.program_id(0)
    rows = cls_ref.shape[-1]
    s = lax.rem(i, 2)
    n_slots = xs_hbm.shape[0]

    def wait(slot):
        pltpu.make_async_copy(stage.at[slot], xs_hbm.at[pl.ds(0, rows)], sem.at[slot]).wait()

    def for_gaps(act):
        def class_gap(cl, carry):
            size = gap_ref[cl]

            @pl.when(size > 0)
            def _():
                act(pltpu.make_async_copy(zeros.at[pl.ds(0, size)],
                                          xs_hbm.at[pl.ds(off_ref[cl] + used_ref[cl], size)], fill_sem))
            return carry
        lax.fori_loop(0, N_CLASSES, class_gap, 0)

        def tail_block(b, carry):
            act(pltpu.make_async_copy(zeros, xs_hbm.at[pl.ds(b * EXPERT_ROWS, EXPERT_ROWS)], fill_sem))
            return carry
        lax.fori_loop(used_ref[N_CLASSES], n_slots // EXPERT_ROWS, tail_block, 0)

    @pl.when(i == 0)
    def _():
        zeros[...] = jnp.zeros_like(zeros)
        for_gaps(lambda copy: copy.start())

    @pl.when(i >= 2)
    def _():
        wait(s)

    stage[s] = x_ref[...]
    for k in range(rows):
        dst = off_ref[cls_ref[0, 0, k]] + rank_ref[0, 0, k]
        pltpu.make_async_copy(stage.at[s].at[k], xs_hbm.at[dst], sem.at[s]).start(priority=k % 2)

    @pl.when(i == pl.num_programs(0) - 1)
    def _():
        wait(s)

        @pl.when(i >= 1)
        def _():
            wait(1 - s)
        for_gaps(lambda copy: copy.wait())


def _dispatch(off, gap, used, cls, rank, xe, n_slots):
    n_steps, _, rows = cls.shape
    smem = pl.BlockSpec((1, 1, rows), lambda i, *_: (i, 0, 0), memory_space=pltpu.SMEM)
    grid_spec = pltpu.PrefetchScalarGridSpec(
        num_scalar_prefetch=3,
        grid=(n_steps,),
        in_specs=[smem, smem, pl.BlockSpec((rows, SUBLANES, LANES), lambda i, *_: (i, 0, 0))],
        out_specs=pl.BlockSpec(memory_space=pl.ANY),
        scratch_shapes=[pltpu.VMEM((2, rows, SUBLANES, LANES), xe.dtype),
                        pltpu.VMEM((EXPERT_ROWS, SUBLANES, LANES), xe.dtype),
                        pltpu.SemaphoreType.DMA((2,)), pltpu.SemaphoreType.DMA(())],
    )
    return pl.pallas_call(
        _dispatch_kernel,
        grid_spec=grid_spec,
        out_shape=jax.ShapeDtypeStruct((n_slots, SUBLANES, LANES), xe.dtype),
        compiler_params=_params("arbitrary"),
        name="dispatch",
    )(off, gap, used, cls, rank, xe)


BLOCKS_PER_STEP = 2


def _expert_kernel(ea_ref, eb_ref, nused_ref, xs_hbm, *refs):
    weights = refs[:6 * BLOCKS_PER_STEP]
    ys_hbm, xbuf, obuf, sem_in, sem_out = refs[6 * BLOCKS_PER_STEP:]
    i = pl.program_id(0)
    steps_used = (nused_ref[0] + BLOCKS_PER_STEP - 1) // BLOCKS_PER_STEP
    slot = lax.rem(i, 2)
    block = EXPERT_ROWS * SUBLANES
    tile = BLOCKS_PER_STEP * block

    def in_copy(step, s):
        return pltpu.make_async_copy(xs_hbm.at[pl.ds(pl.multiple_of(step * tile, tile), tile), :],
                                     xbuf.at[pl.ds(pl.multiple_of(s * tile, tile), tile), :], sem_in.at[s])

    def out_copy(step, s):
        return pltpu.make_async_copy(obuf.at[pl.ds(pl.multiple_of(s * tile, tile), tile), :],
                                     ys_hbm.at[pl.ds(pl.multiple_of(step * tile, tile), tile), :], sem_out.at[s])

    @pl.when(i < steps_used)
    def _():
        @pl.when(i == 0)
        def _():
            in_copy(0, 0).start()

        in_copy(i, slot).wait()

        @pl.when(i + 1 < steps_used)
        def _():
            in_copy(i + 1, 1 - slot).start()

        @pl.when(i >= 2)
        def _():
            out_copy(jnp.maximum(i - 2, 0), slot).wait()

        as_f32 = lambda w: lax.bitcast_convert_type(w, F32)
        for blk in range(BLOCKS_PER_STEP):
            w_a, w_b = weights[6 * blk:6 * blk + 3], weights[6 * blk + 3:6 * blk + 6]
            base = slot * tile + blk * block
            seg = lambda sg: xbuf[pl.ds(base + sg, EXPERT_ROWS, stride=SUBLANES), :]
            words = [seg(sg) for sg in range(D_MODEL // 2 // LANES)]
            xb = jnp.concatenate([as_f32(w << 16) for w in words]
                                 + [as_f32(w & jnp.uint32(0xFFFF0000)) for w in words], axis=1).astype(BF16)
            gates = as_f32(seg(D_MODEL // 2 // LANES))
            ga, gb = gates[:, 0:1], gates[:, 1:2]

            def mlp(wg, wu, wd):
                gate = jnp.dot(xb, wg[0, 0], preferred_element_type=F32)
                up = jnp.dot(xb, wu[0, 0], preferred_element_type=F32)
                return jnp.dot((gate * _sigmoid(gate) * up).astype(BF16), wd[0, 0], preferred_element_type=F32)

            y = ga * mlp(*w_a) + gb * mlp(*w_b)
            for sg in range(SUBLANES):
                obuf[pl.ds(base + sg, EXPERT_ROWS, stride=SUBLANES), :] = y[:, sg * LANES:(sg + 1) * LANES]
        out_copy(i, slot).start()

        @pl.when(i == steps_used - 1)
        def _():
            out_copy(i, slot).wait()

            @pl.when(i >= 1)
            def _():
                out_copy(jnp.maximum(i - 1, 0), 1 - slot).wait()

    @pl.when(i == steps_used)
    def _():
        obuf[0:tile, :] = jnp.zeros((tile, LANES), F32)
        n_steps = pl.num_programs(0)

        def each(act):
            def body(step, carry):
                act(out_copy(step, 0))
                return carry
            lax.fori_loop(steps_used, n_steps, body, 0)
        each(lambda copy: copy.start())
        each(lambda copy: copy.wait())


def _experts(blk_ea, blk_eb, n_used, xs, w_gate, w_up, w_down, layer):
    n_blocks = blk_ea.shape[0]
    tile = BLOCKS_PER_STEP * EXPERT_ROWS * SUBLANES
    weight_specs = []
    for blk in range(BLOCKS_PER_STEP):
        at = lambda tab, blk=blk: (lambda i, ea, eb, nu: (layer, (ea, eb)[tab][BLOCKS_PER_STEP * i + blk], 0, 0))
        for tab in (0, 1):
            weight_specs += [pl.BlockSpec((1, 1, D_MODEL, D_EXPERT), at(tab)),
                             pl.BlockSpec((1, 1, D_MODEL, D_EXPERT), at(tab)),
                             pl.BlockSpec((1, 1, D_EXPERT, D_MODEL), at(tab))]
    grid_spec = pltpu.PrefetchScalarGridSpec(
        num_scalar_prefetch=3,
        grid=(n_blocks // BLOCKS_PER_STEP,),
        in_specs=[pl.BlockSpec(memory_space=pl.ANY)] + weight_specs,
        out_specs=pl.BlockSpec(memory_space=pl.ANY),
        scratch_shapes=[
            pltpu.VMEM((2 * tile, LANES), jnp.uint32),
            pltpu.VMEM((2 * tile, LANES), F32),
            pltpu.SemaphoreType.DMA((2,)),
            pltpu.SemaphoreType.DMA((2,)),
        ],
    )
    return pl.pallas_call(
        _expert_kernel,
        grid_spec=grid_spec,
        out_shape=jax.ShapeDtypeStruct((n_blocks * EXPERT_ROWS * SUBLANES, LANES), F32),
        compiler_params=_params("arbitrary"),
        name="experts",
    )(blk_ea, blk_eb, n_used, xs, *([w_gate, w_up, w_down] * (2 * BLOCKS_PER_STEP)))


def _pair_tables():
    ea, eb = [], []
    for g in range(N_GROUPS):
        for a in range(EXPERTS_PER_GROUP):
            for b in range(a + 1, EXPERTS_PER_GROUP):
                ea.append(g * EXPERTS_PER_GROUP + a)
                eb.append(g * EXPERTS_PER_GROUP + b)
    return np.asarray(ea, np.int32), np.asarray(eb, np.int32)


def _plan_blocks(counts, n):
    cnt = counts[0, :N_CLASSES].astype(jnp.int32)
    padded = (cnt + EXPERT_ROWS - 1) // EXPERT_ROWS * EXPERT_ROWS
    pad_end = jnp.cumsum(padded)
    off = pad_end - padded
    n_blocks = -(-(n + N_CLASSES * (EXPERT_ROWS - 1)) // EXPERT_ROWS)
    n_blocks = -(-n_blocks // BLOCKS_PER_STEP) * BLOCKS_PER_STEP
    blk_start = jnp.arange(n_blocks, dtype=jnp.int32) * EXPERT_ROWS
    blk_cls = jnp.minimum(jnp.sum(pad_end[None, :] <= blk_start[:, None], axis=1), N_CLASSES - 1)
    ea_tab, eb_tab = _pair_tables()
    n_used = (pad_end[-1] // EXPERT_ROWS).astype(jnp.int32).reshape(1)
    used = jnp.concatenate([cnt, n_used])
    return (jnp.asarray(ea_tab)[blk_cls], jnp.asarray(eb_tab)[blk_cls], n_used, off.astype(jnp.int32),
            (padded - cnt).astype(jnp.int32), used.astype(jnp.int32), n_blocks)


def _final_kernel(off_ref, cls_ref, rank_ref, ncls_ref, nrank_ref, h_ref, ys_hbm, g_ref, o_ref, ybuf, sem):
    y = _moe_rows(pl.program_id(0), pl.num_programs(0),
                  _MoeStream(off_ref, (cls_ref, rank_ref), (ncls_ref, nrank_ref), ys_hbm, ybuf, sem))
    out = _rms(h_ref[...] + y, g_ref[...])
    o_ref[...] = out.reshape(o_ref.shape)


def _final(off, cls, rank, h, ys, g, bn):
    n_chunks, _, rows = cls.shape
    smem = lambda m: pl.BlockSpec((1, 1, rows), m, memory_space=pltpu.SMEM)
    cur = lambda c, off: (c + 1, 0, 0)
    nxt = lambda c, off: (jnp.minimum(c + 2, n_chunks - 1), 0, 0)
    grid_spec = pltpu.PrefetchScalarGridSpec(
        num_scalar_prefetch=1,
        grid=(n_chunks - 1,),
        in_specs=[
            smem(cur), smem(cur), smem(nxt), smem(nxt),
            pl.BlockSpec((rows, D_MODEL), lambda c, off: (c + 1, 0)),
            pl.BlockSpec(memory_space=pl.ANY),
            pl.BlockSpec((1, D_MODEL), lambda c, off: (0, 0)),
        ],
        out_specs=pl.BlockSpec((bn, 1, CHUNK, D_MODEL), lambda c, off: (0, c, 0, 0)),
        scratch_shapes=[pltpu.VMEM((2 * rows * SUBLANES, LANES), F32), pltpu.SemaphoreType.DMA((2,))],
    )
    return pl.pallas_call(
        _final_kernel,
        grid_spec=grid_spec,
        out_shape=jax.ShapeDtypeStruct((bn, n_chunks - 1, CHUNK, D_MODEL), F32),
        compiler_params=_params("arbitrary"),
        name="final_norm",
    )(off, cls, rank, cls, rank, h, ys, g)


def _repack_w_in(w):
    o = np.cumsum([0, 2 * M_WIDTH, M_WIDTH, M_WIDTH, M_HEADS, M_HEADS, G_K_WIDTH, G_K_WIDTH, G_WIDTH, G_WIDTH,
                   G_RANK])
    seg = lambda i: w[:, o[i]:o[i + 1]]
    small = jnp.concatenate([seg(3), seg(4), seg(9)], axis=1)
    small = jnp.pad(small, ((0, 0), (0, LANES - small.shape[1])))
    return jnp.concatenate([seg(0), seg(1), seg(2), seg(5), seg(6), seg(7), seg(8), small], axis=1).astype(BF16)


def kernel(x, meta_tokens, norm1_g, w_in, conv_w, conv_b, gate_b, m_norm_g, a_w2, a_b, g_norm_g, w_out, norm2_g,
           wr_g, br_g, wr_e, br_e, w_gate, w_up, w_down, final_norm_g):
    bn, seq, dm = x.shape
    depth = w_in.shape[0]
    assert dm == D_MODEL and seq % CHUNK == 0
    n_chunks = seq // CHUNK + 1
    rows = bn * CHUNK
    n = n_chunks * rows
    x4 = x.reshape(bn, seq // CHUNK, CHUNK, dm)
    row = lambda v: v.reshape(1, -1).astype(F32)
    w_gate_bf, w_up_bf, w_down_bf = w_gate.astype(BF16), w_up.astype(BF16), w_down.astype(BF16)

    h = ys = off = cls = rank = None
    for l in range(depth):
        stream = (x4, meta_tokens.astype(F32)) if l == 0 else _add_moe(off, cls, rank, h, ys)
        gb = gate_b[l].astype(F32).reshape(1, 2 * M_HEADS)
        wr = jnp.pad(jnp.concatenate([wr_g[l], wr_e[l]], axis=1).astype(F32),
                     ((0, 0), (0, LANES - N_GROUPS - N_EXPERTS)))
        wr_hi = wr.astype(BF16)
        wr_lo = (wr - wr_hi.astype(F32)).astype(BF16)
        br = jnp.pad(jnp.concatenate([br_g[l], br_e[l]]).astype(F32), (0, LANES - N_GROUPS - N_EXPERTS))
        h, xe, route, counts = _layer(
            stream, row(norm1_g[l]), _repack_w_in(w_in[l]), conv_w[l].astype(F32), row(conv_b[l]),
            jnp.pad(gb, ((0, 0), (0, LANES - 2 * M_HEADS))),
            jnp.broadcast_to(gb.reshape(2 * M_HEADS, 1), (2 * M_HEADS, rows)),
            row(m_norm_g[l]), a_w2[l].astype(F32), row(a_b[l]), row(g_norm_g[l]),
            w_out[l].astype(BF16), row(norm2_g[l]), wr_hi, wr_lo, br.reshape(1, LANES), bn, n_chunks)
        blk_ea, blk_eb, n_used, off, gap, used, n_blocks = _plan_blocks(counts, n)
        cls, rank = route[:, 0:1, :], route[:, 1:2, :]
        xs = _dispatch(off, gap, used, cls, rank, xe.reshape(n, SUBLANES, LANES), n_blocks * EXPERT_ROWS)
        ys = _experts(blk_ea, blk_eb, n_used, xs.reshape(n_blocks * EXPERT_ROWS * SUBLANES, LANES),
                      w_gate_bf, w_up_bf, w_down_bf, l)
    out = _final(off, cls, rank, h, ys, row(final_norm_g), bn)
    return out.reshape(bn, seq, dm)
```

```python
import functools

import numpy as np
import jax
import jax.numpy as jnp
from jax import lax
from jax.experimental import pallas as pl
from jax.experimental.pallas import tpu as pltpu

F32 = jnp.float32
BF16 = jnp.bfloat16

D_MODEL = 1024
CHUNK = 64
N_META = 16
LEAD_PAD = CHUNK - N_META
EPS = 1e-6
M_HEADS = 4
M_WIDTH = 512
M_HEAD_DIM = 128
CONV_K = 4
G_HEADS = 4
G_WIDTH = 512
G_V_DIM = 128
G_K_DIM = 64
G_K_WIDTH = 256
G_RANK = 16
G_TAU = 16.0
N_GROUPS = 4
EXPERTS_PER_GROUP = 8
N_EXPERTS = 32
D_EXPERT = 256
N_PAIRS = EXPERTS_PER_GROUP * (EXPERTS_PER_GROUP - 1) // 2
N_CLASSES = N_GROUPS * N_PAIRS

LANES = 128
SUBLANES = 8
EXPERT_ROWS = 128
VMEM_LIMIT = 56 * 1024 * 1024

Z_QM, Z_KM, Z_VM, Z_OM = 0, 512, 1024, 1536
Z_QG, Z_KG, Z_VG, Z_GG = 2048, 2304, 2560, 3072
Z_SMALL = 3584
Z_WIDTH = Z_SMALL + LANES
GLA_SAFE_EXP = 80.0


def _log_sigmoid(x):
    return jnp.minimum(x, 0.0) - jnp.log(1.0 + jnp.exp(-jnp.abs(x)))


def _sigmoid(x):
    return 0.5 * jnp.tanh(0.5 * x) + 0.5


def _rms(x, g):
    return x * lax.rsqrt(jnp.mean(x * x, axis=-1, keepdims=True) + EPS) * g


def _dot(a, b):
    return jnp.dot(a.astype(BF16), b.astype(BF16), preferred_element_type=F32)


def _dot_nt(a, b):
    return lax.dot_general(a.astype(BF16), b.astype(BF16), (((1,), (1,)), ((), ())),
                           preferred_element_type=F32)


def _params(*semantics):
    return pltpu.CompilerParams(dimension_semantics=semantics, vmem_limit_bytes=VMEM_LIMIT)


class _MoeStream:
    def __init__(self, off_ref, cur, nxt, ys_hbm, ybuf, sem):
        self.off_ref, self.cur, self.nxt, self.ys_hbm, self.ybuf, self.sem = off_ref, cur, nxt, ys_hbm, ybuf, sem
        self.rows = cur[0].shape[-1]
        self.tile = self.rows * SUBLANES

    def start(self, idx, slot):
        cls_ref, rank_ref = idx
        for k in range(self.rows):
            src = pl.multiple_of((self.off_ref[cls_ref[0, 0, k]] + rank_ref[0, 0, k]) * SUBLANES, SUBLANES)
            dst = pl.multiple_of(slot * self.tile + k * SUBLANES, SUBLANES)
            pltpu.make_async_copy(self.ys_hbm.at[pl.ds(src, SUBLANES), :], self.ybuf.at[pl.ds(dst, SUBLANES), :],
                                  self.sem.at[slot]).start(priority=k % 2)

    def wait(self, slot):
        half = self.ybuf.at[pl.ds(pl.multiple_of(slot * self.tile, self.tile), self.tile), :]
        pltpu.make_async_copy(self.ys_hbm.at[pl.ds(0, self.tile), :], half, self.sem.at[slot]).wait()

    def read(self, slot):
        base = slot * self.tile
        return jnp.concatenate([self.ybuf[pl.ds(base + sg, self.rows, stride=SUBLANES), :]
                                for sg in range(SUBLANES)], axis=1)

    def take(self, slot):
        self.wait(slot)
        self.start(self.nxt, 1 - slot)
        return self.read(slot)


def _moe_rows(i, n_steps, stream):
    s = lax.rem(i, 2)

    @pl.when(i == 0)
    def _():
        stream.start(stream.cur, 0)

    rows = stream.take(s)

    @pl.when(i == n_steps - 1)
    def _():
        stream.wait(1 - s)
    return rows


def _add_moe_kernel(off_ref, cls_ref, rank_ref, ncls_ref, nrank_ref, h_ref, ys_hbm, ho_ref, ybuf, sem):
    y = _moe_rows(pl.program_id(0), pl.num_programs(0),
                  _MoeStream(off_ref, (cls_ref, rank_ref), (ncls_ref, nrank_ref), ys_hbm, ybuf, sem))
    ho_ref[...] = h_ref[...] + y


def _add_moe(off, cls, rank, h, ys):
    n = h.shape[0]
    n_steps, _, rows = cls.shape
    smem = lambda m: pl.BlockSpec((1, 1, rows), m, memory_space=pltpu.SMEM)
    cur = lambda i, off: (i, 0, 0)
    nxt = lambda i, off: (jnp.minimum(i + 1, n_steps - 1), 0, 0)
    grid_spec = pltpu.PrefetchScalarGridSpec(
        num_scalar_prefetch=1,
        grid=(n_steps,),
        in_specs=[
            smem(cur), smem(cur), smem(nxt), smem(nxt),
            pl.BlockSpec((rows, D_MODEL), lambda i, off: (i, 0)),
            pl.BlockSpec(memory_space=pl.ANY),
        ],
        out_specs=pl.BlockSpec((rows, D_MODEL), lambda i, off: (i, 0)),
        scratch_shapes=[pltpu.VMEM((2 * rows * SUBLANES, LANES), F32), pltpu.SemaphoreType.DMA((2,))],
    )
    return pl.pallas_call(
        _add_moe_kernel,
        grid_spec=grid_spec,
        out_shape=jax.ShapeDtypeStruct((n, D_MODEL), F32),
        compiler_params=_params("arbitrary"),
        name="add_moe",
    )(off, cls, rank, cls, rank, h, ys)


def _gla_intra_pairwise(q, k, bg, kbuf, bbuf):
    kbuf[...] = k
    bbuf[...] = bg
    lane = lax.broadcasted_iota(jnp.int32, (CHUNK, CHUNK), 1)

    def column(s, a):
        ks = kbuf[pl.ds(s, 1), :]
        bs = bbuf[pl.ds(s, 1), :]
        col = jnp.sum(q * ks * jnp.exp(jnp.minimum(bg - bs, 0.0)), axis=-1, keepdims=True)
        return jnp.where(lane == s, col, a)

    return lax.fori_loop(0, CHUNK, column, jnp.zeros((CHUNK, CHUNK), F32))


def _gla_qkv(z_ref, rows, valid_c):
    q_g = jnp.where(valid_c, z_ref[rows, Z_QG:Z_QG + G_K_WIDTH], 0.0) * (G_K_DIM ** -0.5)
    k_g = jnp.where(valid_c, z_ref[rows, Z_KG:Z_KG + G_K_WIDTH], 0.0)
    v_g = jnp.where(valid_c, z_ref[rows, Z_VG:Z_VG + G_WIDTH], 0.0)
    return q_g, k_g, v_g


def _split3(x):
    hi = x.astype(BF16)
    r = x - hi.astype(F32)
    mid = r.astype(BF16)
    return hi, mid, (r - mid.astype(F32)).astype(BF16)


def _gla_gated(o, gate, g):
    return _rms(o, g) * (gate * _sigmoid(gate))


IN_PROJ_SPLITS = (0, 1280, 2560, Z_WIDTH)


def _layer_kernel(embed, *refs):
    c = pl.program_id(0)
    (s0_ref, s1_ref, s2_ref), refs = refs[:3], refs[3:]
    (g1_ref, w_ref, tri_ref, trit_ref, cw_ref, cb_ref, gbr_ref, gbc_ref, mg_ref, awh_ref, awl_ref, ab_ref, gg_ref,
     wo_ref, g2_ref, wrh_ref, wrl_ref, br_ref, h1_ref, xe_ref, route_ref, cnt_ref, za_ref, zb_ref, ya_ref, yb_ref, tail_ref, c_ref, m_ref, s_ref, oi_ref,
     bg_ref, kbuf, bbuf, carry_ref) = refs
    rows = za_ref.shape[0]
    if embed:
        lead = jnp.concatenate([jnp.zeros((LEAD_PAD, D_MODEL), F32), s0_ref[...]] * (rows // CHUNK), axis=0)
        read_first = lambda: lead
        read_next = lambda: s1_ref[...].reshape(rows, D_MODEL)
        read_prev = lambda: jnp.where(c <= 1, lead, s2_ref[...].reshape(rows, D_MODEL))
    else:
        read_first, read_next, read_prev = (lambda: s0_ref[...]), (lambda: s1_ref[...]), (lambda: s2_ref[...])

    @pl.when(c == 0)
    def _():
        tail_ref[...] = jnp.zeros_like(tail_ref)
        c_ref[...] = jnp.zeros_like(c_ref)
        m_ref[...] = jnp.zeros_like(m_ref)
        s_ref[...] = jnp.zeros_like(s_ref)
        carry_ref[...] = jnp.zeros_like(carry_ref)
        yb_ref[...] = jnp.zeros_like(yb_ref)
        za_ref[...] = _dot(_rms(read_first(), g1_ref[...]), w_ref[...])

    args = (g1_ref, w_ref, tri_ref, trit_ref, cw_ref, cb_ref, gbr_ref, gbc_ref, mg_ref, awh_ref, awl_ref, ab_ref,
            gg_ref)
    state = (tail_ref, c_ref, m_ref, s_ref, oi_ref, bg_ref, kbuf, bbuf)
    parity = lax.rem(c, 2)

    def step(z_ref, zn_ref, y_ref, y_prev_ref):
        first = functools.partial(_router_logits, read_prev, y_prev_ref, wo_ref, g2_ref, wrh_ref, wrl_ref, br_ref,
                                  h1_ref)
        second = functools.partial(_router_assign, c > 0, xe_ref, route_ref, cnt_ref, carry_ref)
        _mixer_step(c, z_ref, zn_ref, read_next, *args, y_ref, *state, (first, second))

    @pl.when(parity == 0)
    def _():
        step(za_ref, zb_ref, ya_ref, yb_ref)

    @pl.when(parity == 1)
    def _():
        step(zb_ref, za_ref, yb_ref, ya_ref)


def _mixer_step(c, z_ref, zn_ref, read_next, g1_ref, w_ref, tri_ref, trit_ref, cw_ref, cb_ref, gbr_ref, gbc_ref,
                mg_ref, awh_ref, awl_ref, ab_ref, gg_ref, y_ref, tail_ref, c_ref, m_ref, s_ref, oi_ref, bg_ref,
                kbuf, bbuf, router):
    route_logits, route_assign = router
    bn = z_ref.shape[0] // CHUNK
    xn_next = _rms(read_next(), g1_ref[...]).astype(BF16)

    def project_next(part):
        lo, hi = IN_PROJ_SPLITS[part], IN_PROJ_SPLITS[part + 1]
        zn_ref[:, lo:hi] = jnp.dot(xn_next, w_ref[:, lo:hi], preferred_element_type=F32)

    rows_all = bn * CHUNK
    row = lax.broadcasted_iota(jnp.int32, (CHUNK, CHUNK), 0)
    col = lax.broadcasted_iota(jnp.int32, (CHUNK, CHUNK), 1)
    causal = col <= row
    later = jnp.logical_not(c == 0)
    valid_c = jnp.logical_or(later, lax.broadcasted_iota(jnp.int32, (CHUNK, 1), 0) >= LEAD_PAD)
    pos_c = lax.broadcasted_iota(jnp.int32, (rows_all, 1), 0) & (CHUNK - 1)
    pos_r = lax.broadcasted_iota(jnp.int32, (1, rows_all), 1) & (CHUNK - 1)
    valid_ca = jnp.logical_or(later, pos_c >= LEAD_PAD)
    valid_ra = jnp.logical_or(later, pos_r >= LEAD_PAD)
    row8 = lax.broadcasted_iota(jnp.int32, (SUBLANES, 1), 0)
    ones_col = jnp.where(lax.broadcasted_iota(jnp.int32, (CHUNK, LANES), 1) == 0, 1.0, 0.0).astype(F32)
    neg_inf = -jnp.inf

    tri, tri_t = tri_ref[...], trit_ref[...]
    zs = z_ref[:, Z_SMALL:Z_SMALL + LANES]
    g_col = zs + gbr_ref[...]
    g_row = zs.T[0:2 * M_HEADS, :] + gbc_ref[...]
    code = zs[:, 2 * M_HEADS:2 * M_HEADS + G_RANK]
    code_hi = code.astype(BF16)
    code_lo = (code - code_hi.astype(F32)).astype(BF16)
    dot = functools.partial(jnp.dot, preferred_element_type=F32)
    u = dot(code_hi, awh_ref[...]) + dot(code_hi, awl_ref[...]) + dot(code_lo, awh_ref[...]) + ab_ref[...]
    project_next(0)
    la = jnp.where(valid_ca, _log_sigmoid(u) * (1.0 / G_TAU), 0.0)
    lf_col = jnp.where(valid_ca, _log_sigmoid(g_col), 0.0)
    cum = sum(jnp.dot(tri, p, preferred_element_type=F32) for p in _split3(jnp.concatenate([lf_col, la], axis=1)))
    b_col_all, bg_all = cum[:, :LANES], cum[:, LANES:]
    bg_ref[...] = bg_all
    lf_row = jnp.where(valid_ra, _log_sigmoid(g_row), 0.0)
    b_row_all = sum(jnp.dot(p, tri_t, preferred_element_type=F32) for p in _split3(lf_row))
    ig_col_all = jnp.where(valid_ca, g_col, neg_inf)
    ig_row_all = jnp.where(valid_ra, g_row, neg_inf)

    def mlstm_batch(b):
        rows = slice(b * CHUNK, (b + 1) * CHUNK)
        x = jnp.where(valid_c, z_ref[rows, Z_QM:Z_QM + 2 * M_WIDTH], 0.0)
        prev = tail_ref[b]
        acc = x * cw_ref[CONV_K - 1:CONV_K, :] + cb_ref[...]
        for k in range(1, CONV_K):
            cur = pltpu.roll(x, k, 0)
            fix = pltpu.roll(prev, k, 0)
            top = jnp.where(row8 < k, fix, cur[0:SUBLANES])
            shifted = jnp.concatenate([top, cur[SUBLANES:]], axis=0)
            acc = acc + shifted * cw_ref[CONV_K - 1 - k:CONV_K - k, :]
        tail_ref[b] = x[CHUNK - SUBLANES:]
        qk = acc * _sigmoid(acc)
        qk = jnp.where(valid_c, qk, 0.0)
        q_m = qk[:, :M_WIDTH]
        k_m = qk[:, M_WIDTH:] * (M_HEAD_DIM ** -0.5)
        v_m = jnp.where(valid_c, z_ref[rows, Z_VM:Z_VM + M_WIDTH], 0.0)
        o_m = z_ref[rows, Z_OM:Z_OM + M_WIDTH]

        b_col, ig_col = b_col_all[rows], ig_col_all[rows]
        b_row, ig_row = b_row_all[:, rows], ig_row_all[:, rows]

        q_bf, k_bf = q_m.astype(BF16), k_m.astype(BF16)
        heads = []
        for hd in range(M_HEADS):
            idx = b * M_HEADS + hd
            lanes = slice(hd * M_HEAD_DIM, (hd + 1) * M_HEAD_DIM)
            bc, br = b_col[:, M_HEADS + hd:M_HEADS + hd + 1], b_row[M_HEADS + hd:M_HEADS + hd + 1, :]
            igc, igr = ig_col[:, hd:hd + 1], ig_row[hd:hd + 1, :]
            m_prev = m_ref[idx][0:1, 0:1]
            dmat = jnp.where(causal, bc - br + igr, neg_inf)
            m_inter = bc + m_prev
            m_t = jnp.maximum(m_inter, jnp.max(dmat, axis=-1, keepdims=True))
            b_end = bc[CHUNK - 1:CHUNK, :]
            m_new = jnp.maximum(b_end + m_prev, jnp.max(b_end - br + igr, axis=-1, keepdims=True))
            wk = jnp.exp(b_end - bc + igc - m_new)
            heads.append(dict(
                idx=idx, lanes=lanes, q=q_bf[:, lanes], k=k_bf[:, lanes],
                kw_t=(k_m[:, lanes] * wk).T.astype(BF16),
                v_aug=jnp.concatenate([v_m[:, lanes], ones_col], axis=1).astype(BF16),
                e=jnp.exp(dmat - m_t), w_inter=jnp.exp(m_inter - m_t), floor=jnp.exp(-m_t),
                decay=jnp.exp(b_end + m_prev - m_new), m_new=m_new))
        return dict(rows=rows, heads=heads, o_m=o_m)

    def mlstm_first_dots(st):
        for hd in st['heads']:
            cst = c_ref[hd['idx']]
            hd['qk'] = _dot_nt(hd['q'], hd['k'])
            hd['inter'] = _dot(hd['q'], cst)
            c_ref[hd['idx']] = hd['decay'] * cst + _dot(hd['kw_t'], hd['v_aug'])
            m_ref[hd['idx']] = jnp.broadcast_to(hd['m_new'], (SUBLANES, LANES))

    def mlstm_finish(st):
        rows = st['rows']
        for hd in st['heads']:
            lanes = hd['lanes']
            nd = _dot(hd['qk'] * hd['e'], hd['v_aug']) + hd['w_inter'] * hd['inter']
            num, den = nd[:, :M_HEAD_DIM], nd[:, M_HEAD_DIM:M_HEAD_DIM + 1]
            hh = num / jnp.maximum(jnp.abs(den), hd['floor'])
            y = _rms(hh, mg_ref[:, lanes]) * _sigmoid(st['o_m'][:, lanes])
            y_ref[rows, lanes] = y.astype(y_ref.dtype)

    def gla_batch(b):
        rows = slice(b * CHUNK, (b + 1) * CHUNK)
        q_g, k_g, v_g = _gla_qkv(z_ref, rows, valid_c)
        bg = bg_all[rows]
        bg_t = bg.T
        b_end = bg[CHUNK - 1:CHUNK, :]
        q_dec = (q_g * jnp.exp(bg)).astype(BF16)
        k_inv = (k_g * jnp.exp(-bg)).astype(BF16)
        k_end = k_g * jnp.exp(b_end - bg)
        v_bf = v_g.astype(BF16)
        heads = []
        for hd in range(G_HEADS):
            kl = slice(hd * G_K_DIM, (hd + 1) * G_K_DIM)
            vl = slice(hd * G_V_DIM, (hd + 1) * G_V_DIM)
            heads.append(dict(
                idx=b * G_HEADS + hd, vl=vl, yl=slice(M_WIDTH + hd * G_V_DIM, M_WIDTH + (hd + 1) * G_V_DIM),
                q_dec=q_dec[:, kl], k_inv=k_inv[:, kl], k_end_t=k_end[:, kl].T.astype(BF16), v=v_bf[:, vl],
                e_end=jnp.exp(bg_t[hd * G_K_DIM:(hd + 1) * G_K_DIM, CHUNK - 1:CHUNK])))
        return dict(rows=rows, heads=heads, unsafe=jnp.max(-b_end) > GLA_SAFE_EXP)

    def gla_first_dots(st):
        for hd in st['heads']:
            state = s_ref[hd['idx']]
            hd['a'] = _dot_nt(hd['q_dec'], hd['k_inv'])
            hd['o_inter'] = _dot(hd['q_dec'], state)
            s_ref[hd['idx']] = hd['e_end'] * state + _dot(hd['k_end_t'], hd['v'])

    def gla_finish(st):
        rows = st['rows']
        gate_g = z_ref[rows, Z_GG:Z_GG + G_WIDTH]
        for hd in st['heads']:
            vl = hd['vl']
            oi_ref[rows, vl] = hd['o_inter']
            o = _dot(jnp.where(causal, hd['a'], 0.0), hd['v']) + hd['o_inter']
            y_ref[rows, hd['yl']] = _gla_gated(o, gate_g[:, vl], gg_ref[:, vl]).astype(y_ref.dtype)

    def gla_redo_batch(b, carry):
        rows = pl.ds(pl.multiple_of(b * CHUNK, CHUNK), CHUNK)
        q_g, k_g, v_g = _gla_qkv(z_ref, rows, valid_c)
        bg = bg_ref[rows, :]
        gate_g = z_ref[rows, Z_GG:Z_GG + G_WIDTH]
        for hd in range(G_HEADS):
            kl = slice(hd * G_K_DIM, (hd + 1) * G_K_DIM)
            vl = slice(hd * G_V_DIM, (hd + 1) * G_V_DIM)
            yl = slice(M_WIDTH + hd * G_V_DIM, M_WIDTH + (hd + 1) * G_V_DIM)
            a = jnp.where(causal, _gla_intra_pairwise(q_g[:, kl], k_g[:, kl], bg[:, kl], kbuf, bbuf), 0.0)
            y = _gla_gated(_dot(a, v_g[:, vl]) + oi_ref[rows, vl], gate_g[:, vl], gg_ref[:, vl])
            y_ref[rows, yl] = y.astype(y_ref.dtype)
        return carry

    project_next(1)
    routed = route_logits()
    mlstm = [mlstm_batch(b) for b in range(bn)]
    gla = [gla_batch(b) for b in range(bn)]
    for b in range(bn):
        mlstm_first_dots(mlstm[b])
        gla_first_dots(gla[b])
    project_next(2)
    route_assign(*routed)
    for b in range(bn):
        mlstm_finish(mlstm[b])
        gla_finish(gla[b])
    unsafe = functools.reduce(jnp.logical_or, [st['unsafe'] for st in gla])

    @pl.when(unsafe)
    def _():
        lax.fori_loop(0, bn, gla_redo_batch, 0)


def _layer(stream, norm_g, w_in, conv_w, conv_b, gb_row, gb_col, m_norm_g, a_w2, a_b, g_norm_g,
           w_out, norm2_g, wr_hi, wr_lo, br, bn, n_chunks):
    rows = bn * CHUNK
    n = n_chunks * rows
    last = n_chunks - 1
    full = lambda shape: pl.BlockSpec(shape, lambda c: (0,) * len(shape))
    h_at = lambda m: pl.BlockSpec((rows, D_MODEL), m)
    prev = lambda c: (jnp.maximum(c - 1, 0), 0)
    r_i, c_i = np.arange(rows)[:, None], np.arange(rows)[None, :]
    tri = jnp.asarray((r_i // CHUNK == c_i // CHUNK) & (c_i <= r_i), BF16)
    aw_hi = a_w2.astype(BF16)
    aw_lo = (a_w2 - aw_hi.astype(F32)).astype(BF16)
    embed = isinstance(stream, tuple)
    if embed:
        x4, meta = stream
        frames = lambda m: pl.BlockSpec((bn, 1, CHUNK, D_MODEL), m)
        stream_args = (meta, x4, x4)
        stream_specs = [
            full((N_META, D_MODEL)),
            frames(lambda c: (0, jnp.minimum(c, last - 1), 0, 0)),
            frames(lambda c: (0, jnp.clip(c - 2, 0, last - 1), 0, 0)),
        ]
    else:
        stream_args = (stream, stream, stream)
        stream_specs = [
            h_at(lambda c: (jnp.minimum(c, last), 0)),
            h_at(lambda c: (jnp.minimum(c + 1, last), 0)),
            h_at(prev),
        ]
    return pl.pallas_call(
        functools.partial(_layer_kernel, embed),
        grid=(n_chunks + 1,),
        in_specs=stream_specs + [
            full((1, D_MODEL)), full((D_MODEL, Z_WIDTH)), full((rows, rows)), full((rows, rows)),
            full((CONV_K, 2 * M_WIDTH)), full((1, 2 * M_WIDTH)),
            full((1, LANES)), full((2 * M_HEADS, rows)),
            full((1, M_WIDTH)), full((G_RANK, G_K_WIDTH)), full((G_RANK, G_K_WIDTH)), full((1, G_K_WIDTH)),
            full((1, G_WIDTH)),
            full((D_MODEL, D_MODEL)), full((1, D_MODEL)), full((D_MODEL, LANES)), full((D_MODEL, LANES)),
            full((1, LANES)),
        ],
        out_specs=[
            h_at(prev),
            pl.BlockSpec((rows * SUBLANES, LANES), prev),
            pl.BlockSpec((1, SUBLANES, rows), lambda c: (jnp.maximum(c - 1, 0), 0, 0)),
            pl.BlockSpec((1, LANES), lambda c: (0, 0)),
        ],
        out_shape=[
            jax.ShapeDtypeStruct((n, D_MODEL), F32),
            jax.ShapeDtypeStruct((n * SUBLANES, LANES), jnp.uint32),
            jax.ShapeDtypeStruct((n_chunks, SUBLANES, rows), jnp.int32),
            jax.ShapeDtypeStruct((1, LANES), F32),
        ],
        scratch_shapes=[
            pltpu.VMEM((rows, Z_WIDTH), F32), pltpu.VMEM((rows, Z_WIDTH), F32),
            pltpu.VMEM((rows, D_MODEL), BF16), pltpu.VMEM((rows, D_MODEL), BF16),
            pltpu.VMEM((bn, SUBLANES, 2 * M_WIDTH), F32),
            pltpu.VMEM((bn * M_HEADS, M_HEAD_DIM, 2 * M_HEAD_DIM), F32),
            pltpu.VMEM((bn * M_HEADS, SUBLANES, LANES), F32),
            pltpu.VMEM((bn * G_HEADS, G_K_DIM, G_V_DIM), F32),
            pltpu.VMEM((rows, G_WIDTH), F32),
            pltpu.VMEM((rows, G_K_WIDTH), F32),
            pltpu.VMEM((CHUNK, G_K_DIM), F32), pltpu.VMEM((CHUNK, G_K_DIM), F32),
            pltpu.VMEM((1, LANES), F32),
        ],
        compiler_params=_params("arbitrary"),
        name="layer",
    )(*stream_args, norm_g, w_in, tri, tri.T, conv_w, conv_b, gb_row, gb_col, m_norm_g, aw_hi, aw_lo, a_b, g_norm_g,
      w_out, norm2_g, wr_hi, wr_lo, br)


def _router_logits(read_h, y_ref, wo_ref, g_ref, wrh_ref, wrl_ref, br_ref, h1_ref):
    h1 = read_h() + jnp.dot(y_ref[...], wo_ref[...], preferred_element_type=F32)
    h1_ref[...] = h1
    xn = _rms(h1, g_ref[...])
    x_hi = xn.astype(BF16)
    x_lo = (xn - x_hi.astype(F32)).astype(BF16)
    dot = functools.partial(jnp.dot, preferred_element_type=F32)
    logits = dot(x_hi, wrh_ref[...]) + dot(x_hi, wrl_ref[...]) + dot(x_lo, wrh_ref[...]) + br_ref[...]
    return xn, logits


def _router_assign(live, xe_ref, route_ref, cnt_ref, carry_ref, xn, logits):
    rows = xn.shape[0]
    lane = lax.broadcasted_iota(jnp.int32, (rows, LANES), 1)
    neg_inf = -jnp.inf
    big = jnp.int32(LANES)

    def first_argmax(vals):
        top = jnp.max(vals, axis=-1, keepdims=True)
        return top, jnp.min(jnp.where(vals == top, lane, big), axis=-1, keepdims=True)

    lg = jnp.where(lane < N_GROUPS, logits, neg_inf)
    g_max, g_sel = first_argmax(lg)
    p_sel = 1.0 / jnp.sum(jnp.exp(lg - g_max), axis=-1, keepdims=True)
    in_group = jnp.logical_and(lane >= N_GROUPS + EXPERTS_PER_GROUP * g_sel,
                               lane < N_GROUPS + EXPERTS_PER_GROUP * (g_sel + 1))
    le = jnp.where(in_group, logits, neg_inf)
    v1, i1 = first_argmax(le)
    v2, i2 = first_argmax(jnp.where(lane == i1, neg_inf, le))
    e21 = jnp.exp(v2 - v1)
    gate1 = p_sel / (1.0 + e21)
    gate2 = p_sel * e21 / (1.0 + e21)
    j1 = i1 - N_GROUPS - EXPERTS_PER_GROUP * g_sel
    j2 = i2 - N_GROUPS - EXPERTS_PER_GROUP * g_sel
    ja, jb = jnp.minimum(j1, j2), jnp.maximum(j1, j2)
    ga = jnp.where(j1 < j2, gate1, gate2)
    gb = jnp.where(j1 < j2, gate2, gate1)
    pair = ((ja * (2 * EXPERTS_PER_GROUP - 1 - ja)) >> 1) + (jb - ja - 1)
    cls = g_sel * N_PAIRS + pair

    bits = lax.bitcast_convert_type(xn.astype(jnp.bfloat16).astype(F32), jnp.uint32)
    half = D_MODEL // 2
    packed = bits[:, half:] | (bits[:, :half] >> 16)
    gates = lax.bitcast_convert_type(jnp.where(lane == 0, ga, jnp.where(lane == 1, gb, 0.0)), jnp.uint32)
    for sg in range(SUBLANES):
        if sg < half // LANES:
            val = packed[:, sg * LANES:(sg + 1) * LANES]
        else:
            val = gates if sg == half // LANES else jnp.zeros((rows, LANES), jnp.uint32)
        xe_ref[pl.ds(sg, rows, stride=SUBLANES), :] = val

    onehot = jnp.where(jnp.logical_and(live, lane == cls), 1.0, 0.0).astype(F32)
    r_i = lax.broadcasted_iota(jnp.int32, (rows, rows), 0)
    c_i = lax.broadcasted_iota(jnp.int32, (rows, rows), 1)
    incl = jnp.where(c_i <= r_i, 1.0, 0.0).astype(BF16)
    prefix = jnp.dot(incl, onehot.astype(BF16), preferred_element_type=F32)
    rank = jnp.sum(onehot * (prefix - 1.0 + carry_ref[...]), axis=-1, keepdims=True)
    carry_ref[...] = carry_ref[...] + prefix[rows - 1:rows, :]
    cnt_ref[...] = carry_ref[...]
    route = jnp.where(lane == 0, cls, jnp.where(lane == 1, rank.astype(jnp.int32), 0))
    route_ref[0] = route.T[0:SUBLANES, :]


def _dispatch_kernel(off_ref, gap_ref, used_ref, cls_ref, rank_ref, x_ref, xs_hbm, stage, zeros, sem, fill_sem):
    i = pl.program_id(0)
    rows = cls_ref.shape[-1]
    s = lax.rem(i, 2)
    n_slots = xs_hbm.shape[0]

    def wait(slot):
        pltpu.make_async_copy(stage.at[slot], xs_hbm.at[pl.ds(0, rows)], sem.at[slot]).wait()

    def for_gaps(act):
        def class_gap(cl, carry):
            size = gap_ref[cl]

            @pl.when(size > 0)
            def _():
                act(pltpu.make_async_copy(zeros.at[pl.ds(0, size)],
                                          xs_hbm.at[pl.ds(off_ref[cl] + used_ref[cl], size)], fill_sem))
            return carry
        lax.fori_loop(0, N_CLASSES, class_gap, 0)

        def tail_block(b, carry):
            act(pltpu.make_async_copy(zeros, xs_hbm.at[pl.ds(b * EXPERT_ROWS, EXPERT_ROWS)], fill_sem))
            return carry
        lax.fori_loop(used_ref[N_CLASSES], n_slots // EXPERT_ROWS, tail_block, 0)

    @pl.when(i == 0)
    def _():
        zeros[...] = jnp.zeros_like(zeros)
        for_gaps(lambda copy: copy.start())

    @pl.when(i >= 2)
    def _():
        wait(s)

    stage[s] = x_ref[...]
    for k in range(rows):
        dst = off_ref[cls_ref[0, 0, k]] + rank_ref[0, 0, k]
        pltpu.make_async_copy(stage.at[s].at[k], xs_hbm.at[dst], sem.at[s]).start(priority=k % 2)

    @pl.when(i == pl.num_programs(0) - 1)
    def _():
        wait(s)

        @pl.when(i >= 1)
        def _():
            wait(1 - s)
        for_gaps(lambda copy: copy.wait())


def _dispatch(off, gap, used, cls, rank, xe, n_slots):
    n_steps, _, rows = cls.shape
    smem = pl.BlockSpec((1, 1, rows), lambda i, *_: (i, 0, 0), memory_space=pltpu.SMEM)
    grid_spec = pltpu.PrefetchScalarGridSpec(
        num_scalar_prefetch=3,
        grid=(n_steps,),
        in_specs=[smem, smem, pl.BlockSpec((rows, SUBLANES, LANES), lambda i, *_: (i, 0, 0))],
        out_specs=pl.BlockSpec(memory_space=pl.ANY),
        scratch_shapes=[pltpu.VMEM((2, rows, SUBLANES, LANES), xe.dtype),
                        pltpu.VMEM((EXPERT_ROWS, SUBLANES, LANES), xe.dtype),
                        pltpu.SemaphoreType.DMA((2,)), pltpu.SemaphoreType.DMA(())],
    )
    return pl.pallas_call(
        _dispatch_kernel,
        grid_spec=grid_spec,
        out_shape=jax.ShapeDtypeStruct((n_slots, SUBLANES, LANES), xe.dtype),
        compiler_params=_params("arbitrary"),
        name="dispatch",
    )(off, gap, used, cls, rank, xe)


BLOCKS_PER_STEP = 4


def _expert_kernel(ea_ref, eb_ref, nused_ref, xs_hbm, *refs):
    weights = refs[:6 * BLOCKS_PER_STEP]
    ys_hbm, xbuf, obuf, sem_in, sem_out = refs[6 * BLOCKS_PER_STEP:]
    i = pl.program_id(0)
    steps_used = (nused_ref[0] + BLOCKS_PER_STEP - 1) // BLOCKS_PER_STEP
    slot = lax.rem(i, 2)
    block = EXPERT_ROWS * SUBLANES
    tile = BLOCKS_PER_STEP * block

    def in_copy(step, s):
        return pltpu.make_async_copy(xs_hbm.at[pl.ds(pl.multiple_of(step * tile, tile), tile), :],
                                     xbuf.at[pl.ds(pl.multiple_of(s * tile, tile), tile), :], sem_in.at[s])

    def out_copy(step, s):
        return pltpu.make_async_copy(obuf.at[pl.ds(pl.multiple_of(s * tile, tile), tile), :],
                                     ys_hbm.at[pl.ds(pl.multiple_of(step * tile, tile), tile), :], sem_out.at[s])

    @pl.when(i < steps_used)
    def _():
        @pl.when(i == 0)
        def _():
            in_copy(0, 0).start()

        in_copy(i, slot).wait()

        @pl.when(i + 1 < steps_used)
        def _():
            in_copy(i + 1, 1 - slot).start()

        @pl.when(i >= 2)
        def _():
            out_copy(jnp.maximum(i - 2, 0), slot).wait()

        as_f32 = lambda w: lax.bitcast_convert_type(w, F32)
        for blk in range(BLOCKS_PER_STEP):
            w_a, w_b = weights[6 * blk:6 * blk + 3], weights[6 * blk + 3:6 * blk + 6]
            base = slot * tile + blk * block
            seg = lambda sg: xbuf[pl.ds(base + sg, EXPERT_ROWS, stride=SUBLANES), :]
            words = [seg(sg) for sg in range(D_MODEL // 2 // LANES)]
            xb = jnp.concatenate([as_f32(w << 16) for w in words]
                                 + [as_f32(w & jnp.uint32(0xFFFF0000)) for w in words], axis=1).astype(BF16)
            gates = as_f32(seg(D_MODEL // 2 // LANES))
            ga, gb = gates[:, 0:1], gates[:, 1:2]

            def mlp(wg, wu, wd):
                gate = jnp.dot(xb, wg[0, 0], preferred_element_type=F32)
                up = jnp.dot(xb, wu[0, 0], preferred_element_type=F32)
                return jnp.dot((gate * _sigmoid(gate) * up).astype(BF16), wd[0, 0], preferred_element_type=F32)

            y = ga * mlp(*w_a) + gb * mlp(*w_b)
            for sg in range(SUBLANES):
                obuf[pl.ds(base + sg, EXPERT_ROWS, stride=SUBLANES), :] = y[:, sg * LANES:(sg + 1) * LANES]
        out_copy(i, slot).start()

        @pl.when(i == steps_used - 1)
        def _():
            out_copy(i, slot).wait()

            @pl.when(i >= 1)
            def _():
                out_copy(jnp.maximum(i - 1, 0), 1 - slot).wait()

    @pl.when(i == steps_used)
    def _():
        obuf[0:tile, :] = jnp.zeros((tile, LANES), F32)
        n_steps = pl.num_programs(0)

        def each(act):
            def body(step, carry):
                act(out_copy(step, 0))
                return carry
            lax.fori_loop(steps_used, n_steps, body, 0)
        each(lambda copy: copy.start())
        each(lambda copy: copy.wait())


def _experts(blk_ea, blk_eb, n_used, xs, w_gate, w_up, w_down, layer):
    n_blocks = blk_ea.shape[0]
    tile = BLOCKS_PER_STEP * EXPERT_ROWS * SUBLANES
    weight_specs = []
    for blk in range(BLOCKS_PER_STEP):
        at = lambda tab, blk=blk: (lambda i, ea, eb, nu: (layer, (ea, eb)[tab][BLOCKS_PER_STEP * i + blk], 0, 0))
        for tab in (0, 1):
            weight_specs += [pl.BlockSpec((1, 1, D_MODEL, D_EXPERT), at(tab)),
                             pl.BlockSpec((1, 1, D_MODEL, D_EXPERT), at(tab)),
                             pl.BlockSpec((1, 1, D_EXPERT, D_MODEL), at(tab))]
    grid_spec = pltpu.PrefetchScalarGridSpec(
        num_scalar_prefetch=3,
        grid=(n_blocks // BLOCKS_PER_STEP,),
        in_specs=[pl.BlockSpec(memory_space=pl.ANY)] + weight_specs,
        out_specs=pl.BlockSpec(memory_space=pl.ANY),
        scratch_shapes=[
            pltpu.VMEM((2 * tile, LANES), jnp.uint32),
            pltpu.VMEM((2 * tile, LANES), F32),
            pltpu.SemaphoreType.DMA((2,)),
            pltpu.SemaphoreType.DMA((2,)),
        ],
    )
    return pl.pallas_call(
        _expert_kernel,
        grid_spec=grid_spec,
        out_shape=jax.ShapeDtypeStruct((n_blocks * EXPERT_ROWS * SUBLANES, LANES), F32),
        compiler_params=_params("arbitrary"),
        name="experts",
    )(blk_ea, blk_eb, n_used, xs, *([w_gate, w_up, w_down] * (2 * BLOCKS_PER_STEP)))


def _pair_tables():
    ea, eb = [], []
    for g in range(N_GROUPS):
        for a in range(EXPERTS_PER_GROUP):
            for b in range(a + 1, EXPERTS_PER_GROUP):
                ea.append(g * EXPERTS_PER_GROUP + a)
                eb.append(g * EXPERTS_PER_GROUP + b)
    return np.asarray(ea, np.int32), np.asarray(eb, np.int32)


def _plan_blocks(counts, n):
    cnt = counts[0, :N_CLASSES].astype(jnp.int32)
    padded = (cnt + EXPERT_ROWS - 1) // EXPERT_ROWS * EXPERT_ROWS
    pad_end = jnp.cumsum(padded)
    off = pad_end - padded
    n_blocks = -(-(n + N_CLASSES * (EXPERT_ROWS - 1)) // EXPERT_ROWS)
    n_blocks = -(-n_blocks // BLOCKS_PER_STEP) * BLOCKS_PER_STEP
    blk_start = jnp.arange(n_blocks, dtype=jnp.int32) * EXPERT_ROWS
    blk_cls = jnp.minimum(jnp.sum(pad_end[None, :] <= blk_start[:, None], axis=1), N_CLASSES - 1)
    ea_tab, eb_tab = _pair_tables()
    n_used = (pad_end[-1] // EXPERT_ROWS).astype(jnp.int32).reshape(1)
    used = jnp.concatenate([cnt, n_used])
    return (jnp.asarray(ea_tab)[blk_cls], jnp.asarray(eb_tab)[blk_cls], n_used, off.astype(jnp.int32),
            (padded - cnt).astype(jnp.int32), used.astype(jnp.int32), n_blocks)


def _final_kernel(off_ref, cls_ref, rank_ref, ncls_ref, nrank_ref, h_ref, ys_hbm, g_ref, o_ref, ybuf, sem):
    y = _moe_rows(pl.program_id(0), pl.num_programs(0),
                  _MoeStream(off_ref, (cls_ref, rank_ref), (ncls_ref, nrank_ref), ys_hbm, ybuf, sem))
    out = _rms(h_ref[...] + y, g_ref[...])
    o_ref[...] = out.reshape(o_ref.shape)


def _final(off, cls, rank, h, ys, g, bn):
    n_chunks, _, rows = cls.shape
    smem = lambda m: pl.BlockSpec((1, 1, rows), m, memory_space=pltpu.SMEM)
    cur = lambda c, off: (c + 1, 0, 0)
    nxt = lambda c, off: (jnp.minimum(c + 2, n_chunks - 1), 0, 0)
    grid_spec = pltpu.PrefetchScalarGridSpec(
        num_scalar_prefetch=1,
        grid=(n_chunks - 1,),
        in_specs=[
            smem(cur), smem(cur), smem(nxt), smem(nxt),
            pl.BlockSpec((rows, D_MODEL), lambda c, off: (c + 1, 0)),
            pl.BlockSpec(memory_space=pl.ANY),
            pl.BlockSpec((1, D_MODEL), lambda c, off: (0, 0)),
        ],
        out_specs=pl.BlockSpec((bn, 1, CHUNK, D_MODEL), lambda c, off: (0, c, 0, 0)),
        scratch_shapes=[pltpu.VMEM((2 * rows * SUBLANES, LANES), F32), pltpu.SemaphoreType.DMA((2,))],
    )
    return pl.pallas_call(
        _final_kernel,
        grid_spec=grid_spec,
        out_shape=jax.ShapeDtypeStruct((bn, n_chunks - 1, CHUNK, D_MODEL), F32),
        compiler_params=_params("arbitrary"),
        name="final_norm",
    )(off, cls, rank, cls, rank, h, ys, g)


def _repack_w_in(w):
    o = np.cumsum([0, 2 * M_WIDTH, M_WIDTH, M_WIDTH, M_HEADS, M_HEADS, G_K_WIDTH, G_K_WIDTH, G_WIDTH, G_WIDTH,
                   G_RANK])
    seg = lambda i: w[:, o[i]:o[i + 1]]
    small = jnp.concatenate([seg(3), seg(4), seg(9)], axis=1)
    small = jnp.pad(small, ((0, 0), (0, LANES - small.shape[1])))
    return jnp.concatenate([seg(0), seg(1), seg(2), seg(5), seg(6), seg(7), seg(8), small], axis=1).astype(BF16)


def kernel(x, meta_tokens, norm1_g, w_in, conv_w, conv_b, gate_b, m_norm_g, a_w2, a_b, g_norm_g, w_out, norm2_g,
           wr_g, br_g, wr_e, br_e, w_gate, w_up, w_down, final_norm_g):
    bn, seq, dm = x.shape
    depth = w_in.shape[0]
    assert dm == D_MODEL and seq % CHUNK == 0
    n_chunks = seq // CHUNK + 1
    rows = bn * CHUNK
    n = n_chunks * rows
    x4 = x.reshape(bn, seq // CHUNK, CHUNK, dm)
    row = lambda v: v.reshape(1, -1).astype(F32)
    w_gate_bf, w_up_bf, w_down_bf = w_gate.astype(BF16), w_up.astype(BF16), w_down.astype(BF16)

    h = ys = off = cls = rank = None
    for l in range(depth):
        stream = (x4, meta_tokens.astype(F32)) if l == 0 else _add_moe(off, cls, rank, h, ys)
        gb = gate_b[l].astype(F32).reshape(1, 2 * M_HEADS)
        wr = jnp.pad(jnp.concatenate([wr_g[l], wr_e[l]], axis=1).astype(F32),
                     ((0, 0), (0, LANES - N_GROUPS - N_EXPERTS)))
        wr_hi = wr.astype(BF16)
        wr_lo = (wr - wr_hi.astype(F32)).astype(BF16)
        br = jnp.pad(jnp.concatenate([br_g[l], br_e[l]]).astype(F32), (0, LANES - N_GROUPS - N_EXPERTS))
        h, xe, route, counts = _layer(
            stream, row(norm1_g[l]), _repack_w_in(w_in[l]), conv_w[l].astype(F32), row(conv_b[l]),
            jnp.pad(gb, ((0, 0), (0, LANES - 2 * M_HEADS))),
            jnp.broadcast_to(gb.reshape(2 * M_HEADS, 1), (2 * M_HEADS, rows)),
            row(m_norm_g[l]), a_w2[l].astype(F32), row(a_b[l]), row(g_norm_g[l]),
            w_out[l].astype(BF16), row(norm2_g[l]), wr_hi, wr_lo, br.reshape(1, LANES), bn, n_chunks)
        blk_ea, blk_eb, n_used, off, gap, used, n_blocks = _plan_blocks(counts, n)
        cls, rank = route[:, 0:1, :], route[:, 1:2, :]
        xs = _dispatch(off, gap, used, cls, rank, xe.reshape(n, SUBLANES, LANES), n_blocks * EXPERT_ROWS)
        ys = _experts(blk_ea, blk_eb, n_used, xs.reshape(n_blocks * EXPERT_ROWS * SUBLANES, LANES),
                      w_gate_bf, w_up_bf, w_down_bf, l)
    out = _final(off, cls, rank, h, ys, row(final_norm_g), bn)
    return out.reshape(bn, seq, dm)
```

```python
import functools

import numpy as np
import jax
import jax.numpy as jnp
from jax import lax
from jax.experimental import pallas as pl
from jax.experimental.pallas import tpu as pltpu

F32 = jnp.float32
BF16 = jnp.bfloat16

D_MODEL = 1024
CHUNK = 64
N_META = 16
LEAD_PAD = CHUNK - N_META
EPS = 1e-6
M_HEADS = 4
M_WIDTH = 512
M_HEAD_DIM = 128
CONV_K = 4
G_HEADS = 4
G_WIDTH = 512
G_V_DIM = 128
G_K_DIM = 64
G_K_WIDTH = 256
G_RANK = 16
G_TAU = 16.0
N_GROUPS = 4
EXPERTS_PER_GROUP = 8
N_EXPERTS = 32
D_EXPERT = 256
N_PAIRS = EXPERTS_PER_GROUP * (EXPERTS_PER_GROUP - 1) // 2
N_CLASSES = N_GROUPS * N_PAIRS

LANES = 128
SUBLANES = 8
EXPERT_ROWS = 128
VMEM_LIMIT = 56 * 1024 * 1024

Z_QM, Z_KM, Z_VM, Z_OM = 0, 512, 1024, 1536
Z_QG, Z_KG, Z_VG, Z_GG = 2048, 2304, 2560, 3072
Z_SMALL = 3584
Z_WIDTH = Z_SMALL + LANES
GLA_SAFE_EXP = 80.0


def _log_sigmoid(x):
    return jnp.minimum(x, 0.0) - jnp.log(1.0 + jnp.exp(-jnp.abs(x)))


def _sigmoid(x):
    return 0.5 * jnp.tanh(0.5 * x) + 0.5


def _rms(x, g):
    return x * lax.rsqrt(jnp.mean(x * x, axis=-1, keepdims=True) + EPS) * g


def _dot(a, b):
    return jnp.dot(a.astype(BF16), b.astype(BF16), preferred_element_type=F32)


def _dot_nt(a, b):
    return lax.dot_general(a.astype(BF16), b.astype(BF16), (((1,), (1,)), ((), ())),
                           preferred_element_type=F32)


def _params(*semantics):
    return pltpu.CompilerParams(dimension_semantics=semantics, vmem_limit_bytes=VMEM_LIMIT)


class _MoeStream:
    def __init__(self, off_ref, cur, nxt, ys_hbm, ybuf, sem):
        self.off_ref, self.cur, self.nxt, self.ys_hbm, self.ybuf, self.sem = off_ref, cur, nxt, ys_hbm, ybuf, sem
        self.rows = cur[0].shape[-1]
        self.tile = self.rows * SUBLANES

    def start(self, idx, slot):
        cls_ref, rank_ref = idx
        for k in range(self.rows):
            src = pl.multiple_of((self.off_ref[cls_ref[0, 0, k]] + rank_ref[0, 0, k]) * SUBLANES, SUBLANES)
            dst = pl.multiple_of(slot * self.tile + k * SUBLANES, SUBLANES)
            pltpu.make_async_copy(self.ys_hbm.at[pl.ds(src, SUBLANES), :], self.ybuf.at[pl.ds(dst, SUBLANES), :],
                                  self.sem.at[slot]).start(priority=k % 2)

    def wait(self, slot):
        half = self.ybuf.at[pl.ds(pl.multiple_of(slot * self.tile, self.tile), self.tile), :]
        pltpu.make_async_copy(self.ys_hbm.at[pl.ds(0, self.tile), :], half, self.sem.at[slot]).wait()

    def read(self, slot):
        base = slot * self.tile
        return jnp.concatenate([self.ybuf[pl.ds(base + sg, self.rows, stride=SUBLANES), :]
                                for sg in range(SUBLANES)], axis=1)

    def take(self, slot):
        self.wait(slot)
        self.start(self.nxt, 1 - slot)
        return self.read(slot)


def _moe_rows(i, n_steps, stream):
    s = lax.rem(i, 2)

    @pl.when(i == 0)
    def _():
        stream.start(stream.cur, 0)

    rows = stream.take(s)

    @pl.when(i == n_steps - 1)
    def _():
        stream.wait(1 - s)
    return rows


def _add_moe_kernel(off_ref, cls_ref, rank_ref, ncls_ref, nrank_ref, h_ref, ys_hbm, ho_ref, ybuf, sem):
    y = _moe_rows(pl.program_id(0), pl.num_programs(0),
                  _MoeStream(off_ref, (cls_ref, rank_ref), (ncls_ref, nrank_ref), ys_hbm, ybuf, sem))
    ho_ref[...] = h_ref[...] + y


def _add_moe(off, cls, rank, h, ys):
    n = h.shape[0]
    n_steps, _, rows = cls.shape
    smem = lambda m: pl.BlockSpec((1, 1, rows), m, memory_space=pltpu.SMEM)
    cur = lambda i, off: (i, 0, 0)
    nxt = lambda i, off: (jnp.minimum(i + 1, n_steps - 1), 0, 0)
    grid_spec = pltpu.PrefetchScalarGridSpec(
        num_scalar_prefetch=1,
        grid=(n_steps,),
        in_specs=[
            smem(cur), smem(cur), smem(nxt), smem(nxt),
            pl.BlockSpec((rows, D_MODEL), lambda i, off: (i, 0)),
            pl.BlockSpec(memory_space=pl.ANY),
        ],
        out_specs=pl.BlockSpec((rows, D_MODEL), lambda i, off: (i, 0)),
        scratch_shapes=[pltpu.VMEM((2 * rows * SUBLANES, LANES), F32), pltpu.SemaphoreType.DMA((2,))],
    )
    return pl.pallas_call(
        _add_moe_kernel,
        grid_spec=grid_spec,
        out_shape=jax.ShapeDtypeStruct((n, D_MODEL), F32),
        compiler_params=_params("arbitrary"),
        name="add_moe",
    )(off, cls, rank, cls, rank, h, ys)


def _gla_intra_pairwise(q, k, bg, kbuf, bbuf):
    kbuf[...] = k
    bbuf[...] = bg
    lane = lax.broadcasted_iota(jnp.int32, (CHUNK, CHUNK), 1)

    def column(s, a):
        ks = kbuf[pl.ds(s, 1), :]
        bs = bbuf[pl.ds(s, 1), :]
        col = jnp.sum(q * ks * jnp.exp(jnp.minimum(bg - bs, 0.0)), axis=-1, keepdims=True)
        return jnp.where(lane == s, col, a)

    return lax.fori_loop(0, CHUNK, column, jnp.zeros((CHUNK, CHUNK), F32))


def _gla_qkv(z_ref, rows, valid_c):
    q_g = jnp.where(valid_c, z_ref[rows, Z_QG:Z_QG + G_K_WIDTH], 0.0) * (G_K_DIM ** -0.5)
    k_g = jnp.where(valid_c, z_ref[rows, Z_KG:Z_KG + G_K_WIDTH], 0.0)
    v_g = jnp.where(valid_c, z_ref[rows, Z_VG:Z_VG + G_WIDTH], 0.0)
    return q_g, k_g, v_g


def _split3(x):
    hi = x.astype(BF16)
    r = x - hi.astype(F32)
    mid = r.astype(BF16)
    return hi, mid, (r - mid.astype(F32)).astype(BF16)


def _gla_gated(o, gate, g):
    return _rms(o, g) * (gate * _sigmoid(gate))


IN_PROJ_SPLITS = (0, 1280, 2560, Z_WIDTH)


def _layer_kernel(embed, *refs):
    c = pl.program_id(0)
    (s0_ref, s1_ref, s2_ref), refs = refs[:3], refs[3:]
    (g1_ref, w_ref, tri_ref, trit_ref, cw_ref, cb_ref, gbr_ref, gbc_ref, mg_ref, awh_ref, awl_ref, ab_ref, gg_ref,
     wo_ref, g2_ref, wrh_ref, wrl_ref, br_ref, h1_ref, xe_ref, route_ref, cnt_ref, za_ref, zb_ref, ya_ref, yb_ref, tail_ref, c_ref, m_ref, s_ref, oi_ref,
     bg_ref, kbuf, bbuf, carry_ref) = refs
    rows = za_ref.shape[0]
    if embed:
        lead = jnp.concatenate([jnp.zeros((LEAD_PAD, D_MODEL), F32), s0_ref[...]] * (rows // CHUNK), axis=0)
        read_first = lambda: lead
        read_next = lambda: s1_ref[...].reshape(rows, D_MODEL)
        read_prev = lambda: jnp.where(c <= 1, lead, s2_ref[...].reshape(rows, D_MODEL))
    else:
        read_first, read_next, read_prev = (lambda: s0_ref[...]), (lambda: s1_ref[...]), (lambda: s2_ref[...])

    @pl.when(c == 0)
    def _():
        tail_ref[...] = jnp.zeros_like(tail_ref)
        c_ref[...] = jnp.zeros_like(c_ref)
        m_ref[...] = jnp.zeros_like(m_ref)
        s_ref[...] = jnp.zeros_like(s_ref)
        carry_ref[...] = jnp.zeros_like(carry_ref)
        yb_ref[...] = jnp.zeros_like(yb_ref)
        za_ref[...] = _dot(_rms(read_first(), g1_ref[...]), w_ref[...])

    args = (g1_ref, w_ref, tri_ref, trit_ref, cw_ref, cb_ref, gbr_ref, gbc_ref, mg_ref, awh_ref, awl_ref, ab_ref,
            gg_ref)
    state = (tail_ref, c_ref, m_ref, s_ref, oi_ref, bg_ref, kbuf, bbuf)
    parity = lax.rem(c, 2)

    def step(z_ref, zn_ref, y_ref, y_prev_ref):
        first = functools.partial(_router_logits, read_prev, y_prev_ref, wo_ref, g2_ref, wrh_ref, wrl_ref, br_ref,
                                  h1_ref)
        second = functools.partial(_router_assign, c > 0, xe_ref, route_ref, cnt_ref, carry_ref)
        _mixer_step(c, z_ref, zn_ref, read_next, *args, y_ref, *state, (first, second))

    @pl.when(parity == 0)
    def _():
        step(za_ref, zb_ref, ya_ref, yb_ref)

    @pl.when(parity == 1)
    def _():
        step(zb_ref, za_ref, yb_ref, ya_ref)


def _mixer_step(c, z_ref, zn_ref, read_next, g1_ref, w_ref, tri_ref, trit_ref, cw_ref, cb_ref, gbr_ref, gbc_ref,
                mg_ref, awh_ref, awl_ref, ab_ref, gg_ref, y_ref, tail_ref, c_ref, m_ref, s_ref, oi_ref, bg_ref,
                kbuf, bbuf, router):
    route_logits, route_assign = router
    bn = z_ref.shape[0] // CHUNK
    xn_next = _rms(read_next(), g1_ref[...]).astype(BF16)

    def project_next(part):
        lo, hi = IN_PROJ_SPLITS[part], IN_PROJ_SPLITS[part + 1]
        zn_ref[:, lo:hi] = jnp.dot(xn_next, w_ref[:, lo:hi], preferred_element_type=F32)

    rows_all = bn * CHUNK
    row = lax.broadcasted_iota(jnp.int32, (CHUNK, CHUNK), 0)
    col = lax.broadcasted_iota(jnp.int32, (CHUNK, CHUNK), 1)
    causal = col <= row
    later = jnp.logical_not(c == 0)
    valid_c = jnp.logical_or(later, lax.broadcasted_iota(jnp.int32, (CHUNK, 1), 0) >= LEAD_PAD)
    pos_c = lax.broadcasted_iota(jnp.int32, (rows_all, 1), 0) & (CHUNK - 1)
    pos_r = lax.broadcasted_iota(jnp.int32, (1, rows_all), 1) & (CHUNK - 1)
    valid_ca = jnp.logical_or(later, pos_c >= LEAD_PAD)
    valid_ra = jnp.logical_or(later, pos_r >= LEAD_PAD)
    row8 = lax.broadcasted_iota(jnp.int32, (SUBLANES, 1), 0)
    ones_col = jnp.where(lax.broadcasted_iota(jnp.int32, (CHUNK, LANES), 1) == 0, 1.0, 0.0).astype(F32)
    neg_inf = -jnp.inf

    tri, tri_t = tri_ref[...], trit_ref[...]
    zs = z_ref[:, Z_SMALL:Z_SMALL + LANES]
    g_col = zs + gbr_ref[...]
    g_row = zs.T[0:2 * M_HEADS, :] + gbc_ref[...]
    code = zs[:, 2 * M_HEADS:2 * M_HEADS + G_RANK]
    code_hi = code.astype(BF16)
    code_lo = (code - code_hi.astype(F32)).astype(BF16)
    dot = functools.partial(jnp.dot, preferred_element_type=F32)
    u = dot(code_hi, awh_ref[...]) + dot(code_hi, awl_ref[...]) + dot(code_lo, awh_ref[...]) + ab_ref[...]
    project_next(0)
    la = jnp.where(valid_ca, _log_sigmoid(u) * (1.0 / G_TAU), 0.0)
    lf_col = jnp.where(valid_ca, _log_sigmoid(g_col), 0.0)
    cum = sum(jnp.dot(tri, p, preferred_element_type=F32) for p in _split3(jnp.concatenate([lf_col, la], axis=1)))
    b_col_all, bg_all = cum[:, :LANES], cum[:, LANES:]
    bg_ref[...] = bg_all
    lf_row = jnp.where(valid_ra, _log_sigmoid(g_row), 0.0)
    b_row_all = sum(jnp.dot(p, tri_t, preferred_element_type=F32) for p in _split3(lf_row))
    ig_col_all = jnp.where(valid_ca, g_col, neg_inf)
    ig_row_all = jnp.where(valid_ra, g_row, neg_inf)

    def mlstm_batch(b):
        rows = slice(b * CHUNK, (b + 1) * CHUNK)
        x = jnp.where(valid_c, z_ref[rows, Z_QM:Z_QM + 2 * M_WIDTH], 0.0)
        prev = tail_ref[b]
        acc = x * cw_ref[CONV_K - 1:CONV_K, :] + cb_ref[...]
        for k in range(1, CONV_K):
            cur = pltpu.roll(x, k, 0)
            fix = pltpu.roll(prev, k, 0)
            top = jnp.where(row8 < k, fix, cur[0:SUBLANES])
            shifted = jnp.concatenate([top, cur[SUBLANES:]], axis=0)
            acc = acc + shifted * cw_ref[CONV_K - 1 - k:CONV_K - k, :]
        tail_ref[b] = x[CHUNK - SUBLANES:]
        qk = acc * _sigmoid(acc)
        qk = jnp.where(valid_c, qk, 0.0)
        q_m = qk[:, :M_WIDTH]
        k_m = qk[:, M_WIDTH:] * (M_HEAD_DIM ** -0.5)
        v_m = jnp.where(valid_c, z_ref[rows, Z_VM:Z_VM + M_WIDTH], 0.0)
        o_m = z_ref[rows, Z_OM:Z_OM + M_WIDTH]

        b_col, ig_col = b_col_all[rows], ig_col_all[rows]
        b_row, ig_row = b_row_all[:, rows], ig_row_all[:, rows]

        q_bf, k_bf = q_m.astype(BF16), k_m.astype(BF16)
        heads = []
        for hd in range(M_HEADS):
            idx = b * M_HEADS + hd
            lanes = slice(hd * M_HEAD_DIM, (hd + 1) * M_HEAD_DIM)
            bc, br = b_col[:, M_HEADS + hd:M_HEADS + hd + 1], b_row[M_HEADS + hd:M_HEADS + hd + 1, :]
            igc, igr = ig_col[:, hd:hd + 1], ig_row[hd:hd + 1, :]
            m_prev = m_ref[idx][0:1, 0:1]
            dmat = jnp.where(causal, bc - br + igr, neg_inf)
            m_inter = bc + m_prev
            m_t = jnp.maximum(m_inter, jnp.max(dmat, axis=-1, keepdims=True))
            b_end = bc[CHUNK - 1:CHUNK, :]
            m_new = jnp.maximum(b_end + m_prev, jnp.max(b_end - br + igr, axis=-1, keepdims=True))
            wk = jnp.exp(b_end - bc + igc - m_new)
            heads.append(dict(
                idx=idx, lanes=lanes, q=q_bf[:, lanes], k=k_bf[:, lanes],
                kw_t=(k_m[:, lanes] * wk).T.astype(BF16),
                v_aug=jnp.concatenate([v_m[:, lanes], ones_col], axis=1).astype(BF16),
                e=jnp.exp(dmat - m_t), w_inter=jnp.exp(m_inter - m_t), floor=jnp.exp(-m_t),
                decay=jnp.exp(b_end + m_prev - m_new), m_new=m_new))
        return dict(rows=rows, heads=heads, o_m=o_m)

    def mlstm_first_dots(st):
        for hd in st['heads']:
            cst = c_ref[hd['idx']]
            hd['qk'] = _dot_nt(hd['q'], hd['k'])
            hd['inter'] = _dot(hd['q'], cst)
            c_ref[hd['idx']] = hd['decay'] * cst + _dot(hd['kw_t'], hd['v_aug'])
            m_ref[hd['idx']] = jnp.broadcast_to(hd['m_new'], (SUBLANES, LANES))

    def mlstm_finish(st):
        rows = st['rows']
        for hd in st['heads']:
            lanes = hd['lanes']
            nd = _dot(hd['qk'] * hd['e'], hd['v_aug']) + hd['w_inter'] * hd['inter']
            num, den = nd[:, :M_HEAD_DIM], nd[:, M_HEAD_DIM:M_HEAD_DIM + 1]
            hh = num / jnp.maximum(jnp.abs(den), hd['floor'])
            y = _rms(hh, mg_ref[:, lanes]) * _sigmoid(st['o_m'][:, lanes])
            y_ref[rows, lanes] = y.astype(y_ref.dtype)

    def gla_batch(b):
        rows = slice(b * CHUNK, (b + 1) * CHUNK)
        q_g, k_g, v_g = _gla_qkv(z_ref, rows, valid_c)
        bg = bg_all[rows]
        bg_t = bg.T
        b_end = bg[CHUNK - 1:CHUNK, :]
        q_dec = (q_g * jnp.exp(bg)).astype(BF16)
        k_inv = (k_g * jnp.exp(-bg)).astype(BF16)
        k_end = k_g * jnp.exp(b_end - bg)
        v_bf = v_g.astype(BF16)
        heads = []
        for hd in range(G_HEADS):
            kl = slice(hd * G_K_DIM, (hd + 1) * G_K_DIM)
            vl = slice(hd * G_V_DIM, (hd + 1) * G_V_DIM)
            heads.append(dict(
                idx=b * G_HEADS + hd, vl=vl, yl=slice(M_WIDTH + hd * G_V_DIM, M_WIDTH + (hd + 1) * G_V_DIM),
                q_dec=q_dec[:, kl], k_inv=k_inv[:, kl], k_end_t=k_end[:, kl].T.astype(BF16), v=v_bf[:, vl],
                e_end=jnp.exp(bg_t[hd * G_K_DIM:(hd + 1) * G_K_DIM, CHUNK - 1:CHUNK])))
        return dict(rows=rows, heads=heads, unsafe=jnp.max(-b_end) > GLA_SAFE_EXP)

    def gla_first_dots(st):
        for hd in st['heads']:
            state = s_ref[hd['idx']]
            hd['a'] = _dot_nt(hd['q_dec'], hd['k_inv'])
            hd['o_inter'] = _dot(hd['q_dec'], state)
            s_ref[hd['idx']] = hd['e_end'] * state + _dot(hd['k_end_t'], hd['v'])

    def gla_finish(st):
        rows = st['rows']
        gate_g = z_ref[rows, Z_GG:Z_GG + G_WIDTH]
        for hd in st['heads']:
            vl = hd['vl']
            oi_ref[rows, vl] = hd['o_inter']
            o = _dot(jnp.where(causal, hd['a'], 0.0), hd['v']) + hd['o_inter']
            y_ref[rows, hd['yl']] = _gla_gated(o, gate_g[:, vl], gg_ref[:, vl]).astype(y_ref.dtype)

    def gla_redo_batch(b, carry):
        rows = pl.ds(pl.multiple_of(b * CHUNK, CHUNK), CHUNK)
        q_g, k_g, v_g = _gla_qkv(z_ref, rows, valid_c)
        bg = bg_ref[rows, :]
        gate_g = z_ref[rows, Z_GG:Z_GG + G_WIDTH]
        for hd in range(G_HEADS):
            kl = slice(hd * G_K_DIM, (hd + 1) * G_K_DIM)
            vl = slice(hd * G_V_DIM, (hd + 1) * G_V_DIM)
            yl = slice(M_WIDTH + hd * G_V_DIM, M_WIDTH + (hd + 1) * G_V_DIM)
            a = jnp.where(causal, _gla_intra_pairwise(q_g[:, kl], k_g[:, kl], bg[:, kl], kbuf, bbuf), 0.0)
            y = _gla_gated(_dot(a, v_g[:, vl]) + oi_ref[rows, vl], gate_g[:, vl], gg_ref[:, vl])
            y_ref[rows, yl] = y.astype(y_ref.dtype)
        return carry

    project_next(1)
    routed = route_logits()
    mlstm = [mlstm_batch(b) for b in range(bn)]
    gla = [gla_batch(b) for b in range(bn)]
    for b in range(bn):
        mlstm_first_dots(mlstm[b])
        gla_first_dots(gla[b])
    project_next(2)
    route_assign(*routed)
    for b in range(bn):
        mlstm_finish(mlstm[b])
        gla_finish(gla[b])
    unsafe = functools.reduce(jnp.logical_or, [st['unsafe'] for st in gla])

    @pl.when(unsafe)
    def _():
        lax.fori_loop(0, bn, gla_redo_batch, 0)


def _layer(stream, norm_g, w_in, conv_w, conv_b, gb_row, gb_col, m_norm_g, a_w2, a_b, g_norm_g,
           w_out, norm2_g, wr_hi, wr_lo, br, bn, n_chunks):
    rows = bn * CHUNK
    n = n_chunks * rows
    last = n_chunks - 1
    full = lambda shape: pl.BlockSpec(shape, lambda c: (0,) * len(shape))
    h_at = lambda m: pl.BlockSpec((rows, D_MODEL), m)
    prev = lambda c: (jnp.maximum(c - 1, 0), 0)
    r_i, c_i = np.arange(rows)[:, None], np.arange(rows)[None, :]
    tri = jnp.asarray((r_i // CHUNK == c_i // CHUNK) & (c_i <= r_i), BF16)
    aw_hi = a_w2.astype(BF16)
    aw_lo = (a_w2 - aw_hi.astype(F32)).astype(BF16)
    embed = isinstance(stream, tuple)
    if embed:
        x4, meta = stream
        frames = lambda m: pl.BlockSpec((bn, 1, CHUNK, D_MODEL), m)
        stream_args = (meta, x4, x4)
        stream_specs = [
            full((N_META, D_MODEL)),
            frames(lambda c: (0, jnp.minimum(c, last - 1), 0, 0)),
            frames(lambda c: (0, jnp.clip(c - 2, 0, last - 1), 0, 0)),
        ]
    else:
        stream_args = (stream, stream, stream)
        stream_specs = [
            h_at(lambda c: (jnp.minimum(c, last), 0)),
            h_at(lambda c: (jnp.minimum(c + 1, last), 0)),
            h_at(prev),
        ]
    return pl.pallas_call(
        functools.partial(_layer_kernel, embed),
        grid=(n_chunks + 1,),
        in_specs=stream_specs + [
            full((1, D_MODEL)), full((D_MODEL, Z_WIDTH)), full((rows, rows)), full((rows, rows)),
            full((CONV_K, 2 * M_WIDTH)), full((1, 2 * M_WIDTH)),
            full((1, LANES)), full((2 * M_HEADS, rows)),
            full((1, M_WIDTH)), full((G_RANK, G_K_WIDTH)), full((G_RANK, G_K_WIDTH)), full((1, G_K_WIDTH)),
            full((1, G_WIDTH)),
            full((D_MODEL, D_MODEL)), full((1, D_MODEL)), full((D_MODEL, LANES)), full((D_MODEL, LANES)),
            full((1, LANES)),
        ],
        out_specs=[
            h_at(prev),
            pl.BlockSpec((rows * SUBLANES, LANES), prev),
            pl.BlockSpec((1, SUBLANES, rows), lambda c: (jnp.maximum(c - 1, 0), 0, 0)),
            pl.BlockSpec((1, LANES), lambda c: (0, 0)),
        ],
        out_shape=[
            jax.ShapeDtypeStruct((n, D_MODEL), F32),
            jax.ShapeDtypeStruct((n * SUBLANES, LANES), jnp.uint32),
            jax.ShapeDtypeStruct((n_chunks, SUBLANES, rows), jnp.int32),
            jax.ShapeDtypeStruct((1, LANES), F32),
        ],
        scratch_shapes=[
            pltpu.VMEM((rows, Z_WIDTH), F32), pltpu.VMEM((rows, Z_WIDTH), F32),
            pltpu.VMEM((rows, D_MODEL), BF16), pltpu.VMEM((rows, D_MODEL), BF16),
            pltpu.VMEM((bn, SUBLANES, 2 * M_WIDTH), F32),
            pltpu.VMEM((bn * M_HEADS, M_HEAD_DIM, 2 * M_HEAD_DIM), F32),
            pltpu.VMEM((bn * M_HEADS, SUBLANES, LANES), F32),
            pltpu.VMEM((bn * G_HEADS, G_K_DIM, G_V_DIM), F32),
            pltpu.VMEM((rows, G_WIDTH), F32),
            pltpu.VMEM((rows, G_K_WIDTH), F32),
            pltpu.VMEM((CHUNK, G_K_DIM), F32), pltpu.VMEM((CHUNK, G_K_DIM), F32),
            pltpu.VMEM((1, LANES), F32),
        ],
        compiler_params=_params("arbitrary"),
        name="layer",
    )(*stream_args, norm_g, w_in, tri, tri.T, conv_w, conv_b, gb_row, gb_col, m_norm_g, aw_hi, aw_lo, a_b, g_norm_g,
      w_out, norm2_g, wr_hi, wr_lo, br)


def _router_logits(read_h, y_ref, wo_ref, g_ref, wrh_ref, wrl_ref, br_ref, h1_ref):
    h1 = read_h() + jnp.dot(y_ref[...], wo_ref[...], preferred_element_type=F32)
    h1_ref[...] = h1
    xn = _rms(h1, g_ref[...])
    x_hi = xn.astype(BF16)
    x_lo = (xn - x_hi.astype(F32)).astype(BF16)
    dot = functools.partial(jnp.dot, preferred_element_type=F32)
    logits = dot(x_hi, wrh_ref[...]) + dot(x_hi, wrl_ref[...]) + dot(x_lo, wrh_ref[...]) + br_ref[...]
    return xn, logits


def _router_assign(live, xe_ref, route_ref, cnt_ref, carry_ref, xn, logits):
    rows = xn.shape[0]
    lane = lax.broadcasted_iota(jnp.int32, (rows, LANES), 1)
    neg_inf = -jnp.inf
    big = jnp.int32(LANES)

    def first_argmax(vals):
        top = jnp.max(vals, axis=-1, keepdims=True)
        return top, jnp.min(jnp.where(vals == top, lane, big), axis=-1, keepdims=True)

    lg = jnp.where(lane < N_GROUPS, logits, neg_inf)
    g_max, g_sel = first_argmax(lg)
    p_sel = 1.0 / jnp.sum(jnp.exp(lg - g_max), axis=-1, keepdims=True)
    in_group = jnp.logical_and(lane >= N_GROUPS + EXPERTS_PER_GROUP * g_sel,
                               lane < N_GROUPS + EXPERTS_PER_GROUP * (g_sel + 1))
    le = jnp.where(in_group, logits, neg_inf)
    v1, i1 = first_argmax(le)
    v2, i2 = first_argmax(jnp.where(lane == i1, neg_inf, le))
    e21 = jnp.exp(v2 - v1)
    gate1 = p_sel / (1.0 + e21)
    gate2 = p_sel * e21 / (1.0 + e21)
    j1 = i1 - N_GROUPS - EXPERTS_PER_GROUP * g_sel
    j2 = i2 - N_GROUPS - EXPERTS_PER_GROUP * g_sel
    ja, jb = jnp.minimum(j1, j2), jnp.maximum(j1, j2)
    ga = jnp.where(j1 < j2, gate1, gate2)
    gb = jnp.where(j1 < j2, gate2, gate1)
    pair = ((ja * (2 * EXPERTS_PER_GROUP - 1 - ja)) >> 1) + (jb - ja - 1)
    cls = g_sel * N_PAIRS + pair

    bits = lax.bitcast_convert_type(xn.astype(jnp.bfloat16).astype(F32), jnp.uint32)
    half = D_MODEL // 2
    packed = bits[:, half:] | (bits[:, :half] >> 16)
    gates = lax.bitcast_convert_type(jnp.where(lane == 0, ga, jnp.where(lane == 1, gb, 0.0)), jnp.uint32)
    for sg in range(SUBLANES):
        if sg < half // LANES:
            val = packed[:, sg * LANES:(sg + 1) * LANES]
        else:
            val = gates if sg == half // LANES else jnp.zeros((rows, LANES), jnp.uint32)
        xe_ref[pl.ds(sg, rows, stride=SUBLANES), :] = val

    onehot = jnp.where(jnp.logical_and(live, lane == cls), 1.0, 0.0).astype(F32)
    r_i = lax.broadcasted_iota(jnp.int32, (rows, rows), 0)
    c_i = lax.broadcasted_iota(jnp.int32, (rows, rows), 1)
    incl = jnp.where(c_i <= r_i, 1.0, 0.0).astype(BF16)
    prefix = jnp.dot(incl, onehot.astype(BF16), preferred_element_type=F32)
    rank = jnp.sum(onehot * (prefix - 1.0 + carry_ref[...]), axis=-1, keepdims=True)
    carry_ref[...] = carry_ref[...] + prefix[rows - 1:rows, :]
    cnt_ref[...] = carry_ref[...]
    route = jnp.where(lane == 0, cls, jnp.where(lane == 1, rank.astype(jnp.int32), 0))
    route_ref[0] = route.T[0:SUBLANES, :]


def _dispatch_kernel(off_ref, gap_ref, used_ref, cls_ref, rank_ref, x_ref, xs_hbm, stage, zeros, sem, fill_sem):
    i = pl.program_id(0)
    rows = cls_ref.shape[-1]
    s = lax.rem(i, 2)
    n_slots = xs_hbm.shape[0]

    def wait(slot):
        pltpu.make_async_copy(stage.at[slot], xs_hbm.at[pl.ds(0, rows)], sem.at[slot]).wait()

    def for_gaps(act):
        def class_gap(cl, carry):
            size = gap_ref[cl]

            @pl.when(size > 0)
            def _():
                act(pltpu.make_async_copy(zeros.at[pl.ds(0, size)],
                                          xs_hbm.at[pl.ds(off_ref[cl] + used_ref[cl], size)], fill_sem))
            return carry
        lax.fori_loop(0, N_CLASSES, class_gap, 0)

        def tail_block(b, carry):
            act(pltpu.make_async_copy(zeros, xs_hbm.at[pl.ds(b * EXPERT_ROWS, EXPERT_ROWS)], fill_sem))
            return carry
        lax.fori_loop(used_ref[N_CLASSES], n_slots // EXPERT_ROWS, tail_block, 0)

    @pl.when(i == 0)
    def _():
        zeros[...] = jnp.zeros_like(zeros)
        for_gaps(lambda copy: copy.start())

    @pl.when(i >= 2)
    def _():
        wait(s)

    stage[s] = x_ref[...]
    for k in range(rows):
        dst = off_ref[cls_ref[0, 0, k]] + rank_ref[0, 0, k]
        pltpu.make_async_copy(stage.at[s].at[k], xs_hbm.at[dst], sem.at[s]).start(priority=k % 2)

    @pl.when(i == pl.num_programs(0) - 1)
    def _():
        wait(s)

        @pl.when(i >= 1)
        def _():
            wait(1 - s)
        for_gaps(lambda copy: copy.wait())


def _dispatch(off, gap, used, cls, rank, xe, n_slots):
    n_steps, _, rows = cls.shape
    smem = pl.BlockSpec((1, 1, rows), lambda i, *_: (i, 0, 0), memory_space=pltpu.SMEM)
    grid_spec = pltpu.PrefetchScalarGridSpec(
        num_scalar_prefetch=3,
        grid=(n_steps,),
        in_specs=[smem, smem, pl.BlockSpec((rows, SUBLANES, LANES), lambda i, *_: (i, 0, 0))],
        out_specs=pl.BlockSpec(memory_space=pl.ANY),
        scratch_shapes=[pltpu.VMEM((2, rows, SUBLANES, LANES), xe.dtype),
                        pltpu.VMEM((EXPERT_ROWS, SUBLANES, LANES), xe.dtype),
                        pltpu.SemaphoreType.DMA((2,)), pltpu.SemaphoreType.DMA(())],
    )
    return pl.pallas_call(
        _dispatch_kernel,
        grid_spec=grid_spec,
        out_shape=jax.ShapeDtypeStruct((n_slots, SUBLANES, LANES), xe.dtype),
        compiler_params=_params("arbitrary"),
        name="dispatch",
    )(off, gap, used, cls, rank, xe)


BLOCKS_PER_STEP = 2


def _expert_kernel(ea_ref, eb_ref, nused_ref, xs_hbm, *refs):
    weights = refs[:2 * BLOCKS_PER_STEP]
    ys_hbm, xbuf, obuf, sem_in, sem_out = refs[2 * BLOCKS_PER_STEP:]
    i = pl.program_id(0)
    steps_used = (nused_ref[0] + BLOCKS_PER_STEP - 1) // BLOCKS_PER_STEP
    slot = lax.rem(i, 2)
    block = EXPERT_ROWS * SUBLANES
    tile = BLOCKS_PER_STEP * block

    def in_copy(step, s):
        return pltpu.make_async_copy(xs_hbm.at[pl.ds(pl.multiple_of(step * tile, tile), tile), :],
                                     xbuf.at[pl.ds(pl.multiple_of(s * tile, tile), tile), :], sem_in.at[s])

    def out_copy(step, s):
        return pltpu.make_async_copy(obuf.at[pl.ds(pl.multiple_of(s * tile, tile), tile), :],
                                     ys_hbm.at[pl.ds(pl.multiple_of(step * tile, tile), tile), :], sem_out.at[s])

    @pl.when(i < steps_used)
    def _():
        @pl.when(i == 0)
        def _():
            in_copy(0, 0).start()

        in_copy(i, slot).wait()

        @pl.when(i + 1 < steps_used)
        def _():
            in_copy(i + 1, 1 - slot).start()

        @pl.when(i >= 2)
        def _():
            out_copy(jnp.maximum(i - 2, 0), slot).wait()

        as_f32 = lambda w: lax.bitcast_convert_type(w, F32)
        for blk in range(BLOCKS_PER_STEP):
            w_a, w_b = weights[2 * blk], weights[2 * blk + 1]
            base = slot * tile + blk * block
            seg = lambda sg: xbuf[pl.ds(base + sg, EXPERT_ROWS, stride=SUBLANES), :]
            words = [seg(sg) for sg in range(D_MODEL // 2 // LANES)]
            xb = jnp.concatenate([as_f32(w << 16) for w in words]
                                 + [as_f32(w & jnp.uint32(0xFFFF0000)) for w in words], axis=1).astype(BF16)
            gates = as_f32(seg(D_MODEL // 2 // LANES))
            ga, gb = gates[:, 0:1], gates[:, 1:2]

            def mlp(w):
                gu = jnp.dot(xb, w[0, 0, :, :2 * D_EXPERT], preferred_element_type=F32)
                gate, up = gu[:, :D_EXPERT], gu[:, D_EXPERT:]
                return _dot_nt(gate * _sigmoid(gate) * up, w[0, 0, :, 2 * D_EXPERT:])

            y = ga * mlp(w_a) + gb * mlp(w_b)
            for sg in range(SUBLANES):
                obuf[pl.ds(base + sg, EXPERT_ROWS, stride=SUBLANES), :] = y[:, sg * LANES:(sg + 1) * LANES]
        out_copy(i, slot).start()

        @pl.when(i == steps_used - 1)
        def _():
            out_copy(i, slot).wait()

            @pl.when(i >= 1)
            def _():
                out_copy(jnp.maximum(i - 1, 0), 1 - slot).wait()

    @pl.when(i == steps_used)
    def _():
        obuf[0:tile, :] = jnp.zeros((tile, LANES), F32)
        n_steps = pl.num_programs(0)

        def each(act):
            def body(step, carry):
                act(out_copy(step, 0))
                return carry
            lax.fori_loop(steps_used, n_steps, body, 0)
        each(lambda copy: copy.start())
        each(lambda copy: copy.wait())


def _experts(blk_ea, blk_eb, n_used, xs, w_all, layer):
    n_blocks = blk_ea.shape[0]
    tile = BLOCKS_PER_STEP * EXPERT_ROWS * SUBLANES
    weight_specs = []
    for blk in range(BLOCKS_PER_STEP):
        at = lambda tab, blk=blk: (lambda i, ea, eb, nu: (layer, (ea, eb)[tab][BLOCKS_PER_STEP * i + blk], 0, 0))
        weight_specs += [pl.BlockSpec((1, 1, D_MODEL, 3 * D_EXPERT), at(tab)) for tab in (0, 1)]
    grid_spec = pltpu.PrefetchScalarGridSpec(
        num_scalar_prefetch=3,
        grid=(n_blocks // BLOCKS_PER_STEP,),
        in_specs=[pl.BlockSpec(memory_space=pl.ANY)] + weight_specs,
        out_specs=pl.BlockSpec(memory_space=pl.ANY),
        scratch_shapes=[
            pltpu.VMEM((2 * tile, LANES), jnp.uint32),
            pltpu.VMEM((2 * tile, LANES), F32),
            pltpu.SemaphoreType.DMA((2,)),
            pltpu.SemaphoreType.DMA((2,)),
        ],
    )
    return pl.pallas_call(
        _expert_kernel,
        grid_spec=grid_spec,
        out_shape=jax.ShapeDtypeStruct((n_blocks * EXPERT_ROWS * SUBLANES, LANES), F32),
        compiler_params=_params("arbitrary"),
        name="experts",
    )(blk_ea, blk_eb, n_used, xs, *([w_all] * (2 * BLOCKS_PER_STEP)))


def _pair_tables():
    ea, eb = [], []
    for g in range(N_GROUPS):
        for a in range(EXPERTS_PER_GROUP):
            for b in range(a + 1, EXPERTS_PER_GROUP):
                ea.append(g * EXPERTS_PER_GROUP + a)
                eb.append(g * EXPERTS_PER_GROUP + b)
    return np.asarray(ea, np.int32), np.asarray(eb, np.int32)


def _plan_blocks(counts, n):
    cnt = counts[0, :N_CLASSES].astype(jnp.int32)
    padded = (cnt + EXPERT_ROWS - 1) // EXPERT_ROWS * EXPERT_ROWS
    pad_end = jnp.cumsum(padded)
    off = pad_end - padded
    n_blocks = -(-(n + N_CLASSES * (EXPERT_ROWS - 1)) // EXPERT_ROWS)
    n_blocks = -(-n_blocks // BLOCKS_PER_STEP) * BLOCKS_PER_STEP
    blk_start = jnp.arange(n_blocks, dtype=jnp.int32) * EXPERT_ROWS
    blk_cls = jnp.minimum(jnp.sum(pad_end[None, :] <= blk_start[:, None], axis=1), N_CLASSES - 1)
    ea_tab, eb_tab = _pair_tables()
    n_used = (pad_end[-1] // EXPERT_ROWS).astype(jnp.int32).reshape(1)
    used = jnp.concatenate([cnt, n_used])
    return (jnp.asarray(ea_tab)[blk_cls], jnp.asarray(eb_tab)[blk_cls], n_used, off.astype(jnp.int32),
            (padded - cnt).astype(jnp.int32), used.astype(jnp.int32), n_blocks)


def _final_kernel(off_ref, cls_ref, rank_ref, ncls_ref, nrank_ref, h_ref, ys_hbm, g_ref, o_ref, ybuf, sem):
    y = _moe_rows(pl.program_id(0), pl.num_programs(0),
                  _MoeStream(off_ref, (cls_ref, rank_ref), (ncls_ref, nrank_ref), ys_hbm, ybuf, sem))
    out = _rms(h_ref[...] + y, g_ref[...])
    o_ref[...] = out.reshape(o_ref.shape)


def _final(off, cls, rank, h, ys, g, bn):
    n_chunks, _, rows = cls.shape
    smem = lambda m: pl.BlockSpec((1, 1, rows), m, memory_space=pltpu.SMEM)
    cur = lambda c, off: (c + 1, 0, 0)
    nxt = lambda c, off: (jnp.minimum(c + 2, n_chunks - 1), 0, 0)
    grid_spec = pltpu.PrefetchScalarGridSpec(
        num_scalar_prefetch=1,
        grid=(n_chunks - 1,),
        in_specs=[
            smem(cur), smem(cur), smem(nxt), smem(nxt),
            pl.BlockSpec((rows, D_MODEL), lambda c, off: (c + 1, 0)),
            pl.BlockSpec(memory_space=pl.ANY),
            pl.BlockSpec((1, D_MODEL), lambda c, off: (0, 0)),
        ],
        out_specs=pl.BlockSpec((bn, 1, CHUNK, D_MODEL), lambda c, off: (0, c, 0, 0)),
        scratch_shapes=[pltpu.VMEM((2 * rows * SUBLANES, LANES), F32), pltpu.SemaphoreType.DMA((2,))],
    )
    return pl.pallas_call(
        _final_kernel,
        grid_spec=grid_spec,
        out_shape=jax.ShapeDtypeStruct((bn, n_chunks - 1, CHUNK, D_MODEL), F32),
        compiler_params=_params("arbitrary"),
        name="final_norm",
    )(off, cls, rank, cls, rank, h, ys, g)


def _repack_w_in(w):
    o = np.cumsum([0, 2 * M_WIDTH, M_WIDTH, M_WIDTH, M_HEADS, M_HEADS, G_K_WIDTH, G_K_WIDTH, G_WIDTH, G_WIDTH,
                   G_RANK])
    seg = lambda i: w[:, o[i]:o[i + 1]]
    small = jnp.concatenate([seg(3), seg(4), seg(9)], axis=1)
    small = jnp.pad(small, ((0, 0), (0, LANES - small.shape[1])))
    return jnp.concatenate([seg(0), seg(1), seg(2), seg(5), seg(6), seg(7), seg(8), small], axis=1).astype(BF16)


def kernel(x, meta_tokens, norm1_g, w_in, conv_w, conv_b, gate_b, m_norm_g, a_w2, a_b, g_norm_g, w_out, norm2_g,
           wr_g, br_g, wr_e, br_e, w_gate, w_up, w_down, final_norm_g):
    bn, seq, dm = x.shape
    depth = w_in.shape[0]
    assert dm == D_MODEL and seq % CHUNK == 0
    n_chunks = seq // CHUNK + 1
    rows = bn * CHUNK
    n = n_chunks * rows
    x4 = x.reshape(bn, seq // CHUNK, CHUNK, dm)
    row = lambda v: v.reshape(1, -1).astype(F32)
    w_experts = jnp.concatenate([w_gate, w_up, jnp.swapaxes(w_down, -1, -2)], axis=-1).astype(BF16)

    h = ys = off = cls = rank = None
    for l in range(depth):
        stream = (x4, meta_tokens.astype(F32)) if l == 0 else _add_moe(off, cls, rank, h, ys)
        gb = gate_b[l].astype(F32).reshape(1, 2 * M_HEADS)
        wr = jnp.pad(jnp.concatenate([wr_g[l], wr_e[l]], axis=1).astype(F32),
                     ((0, 0), (0, LANES - N_GROUPS - N_EXPERTS)))
        wr_hi = wr.astype(BF16)
        wr_lo = (wr - wr_hi.astype(F32)).astype(BF16)
        br = jnp.pad(jnp.concatenate([br_g[l], br_e[l]]).astype(F32), (0, LANES - N_GROUPS - N_EXPERTS))
        h, xe, route, counts = _layer(
            stream, row(norm1_g[l]), _repack_w_in(w_in[l]), conv_w[l].astype(F32), row(conv_b[l]),
            jnp.pad(gb, ((0, 0), (0, LANES - 2 * M_HEADS))),
            jnp.broadcast_to(gb.reshape(2 * M_HEADS, 1), (2 * M_HEADS, rows)),
            row(m_norm_g[l]), a_w2[l].astype(F32), row(a_b[l]), row(g_norm_g[l]),
            w_out[l].astype(BF16), row(norm2_g[l]), wr_hi, wr_lo, br.reshape(1, LANES), bn, n_chunks)
        blk_ea, blk_eb, n_used, off, gap, used, n_blocks = _plan_blocks(counts, n)
        cls, rank = route[:, 0:1, :], route[:, 1:2, :]
        xs = _dispatch(off, gap, used, cls, rank, xe.reshape(n, SUBLANES, LANES), n_blocks * EXPERT_ROWS)
        ys = _experts(blk_ea, blk_eb, n_used, xs.reshape(n_blocks * EXPERT_ROWS * SUBLANES, LANES),
                      w_experts, l)
    out = _final(off, cls, rank, h, ys, row(final_norm_g), bn)
    return out.reshape(bn, seq, dm)
```

```python
import functools

import numpy as np
import jax
import jax.numpy as jnp
from jax import lax
from jax.experimental import pallas as pl
from jax.experimental.pallas import tpu as pltpu

F32 = jnp.float32
BF16 = jnp.bfloat16

D_MODEL = 1024
CHUNK = 64
N_META = 16
LEAD_PAD = CHUNK - N_META
EPS = 1e-6
M_HEADS = 4
M_WIDTH = 512
M_HEAD_DIM = 128
CONV_K = 4
G_HEADS = 4
G_WIDTH = 512
G_V_DIM = 128
G_K_DIM = 64
G_K_WIDTH = 256
G_RANK = 16
G_TAU = 16.0
N_GROUPS = 4
EXPERTS_PER_GROUP = 8
N_EXPERTS = 32
D_EXPERT = 256
N_PAIRS = EXPERTS_PER_GROUP * (EXPERTS_PER_GROUP - 1) // 2
N_CLASSES = N_GROUPS * N_PAIRS

LANES = 128
SUBLANES = 8
EXPERT_ROWS = 128
VMEM_LIMIT = 56 * 1024 * 1024

Z_QM, Z_KM, Z_VM, Z_OM = 0, 512, 1024, 1536
Z_QG, Z_KG, Z_VG, Z_GG = 2048, 2304, 2560, 3072
Z_SMALL = 3584
Z_WIDTH = Z_SMALL + LANES
GLA_SAFE_EXP = 80.0


def _log_sigmoid(x):
    return jnp.minimum(x, 0.0) - jnp.log(1.0 + jnp.exp(-jnp.abs(x)))


def _sigmoid(x):
    return 0.5 * jnp.tanh(0.5 * x) + 0.5


def _rms(x, g):
    return x * lax.rsqrt(jnp.mean(x * x, axis=-1, keepdims=True) + EPS) * g


def _dot(a, b):
    return jnp.dot(a.astype(BF16), b.astype(BF16), preferred_element_type=F32)


def _dot_nt(a, b):
    return lax.dot_general(a.astype(BF16), b.astype(BF16), (((1,), (1,)), ((), ())),
                           preferred_element_type=F32)


def _params(*semantics):
    return pltpu.CompilerParams(dimension_semantics=semantics, vmem_limit_bytes=VMEM_LIMIT)


class _MoeStream:
    def __init__(self, off_ref, cur, nxt, ys_hbm, ybuf, sem):
        self.off_ref, self.cur, self.nxt, self.ys_hbm, self.ybuf, self.sem = off_ref, cur, nxt, ys_hbm, ybuf, sem
        self.rows = cur[0].shape[-1]
        self.tile = self.rows * SUBLANES

    def start(self, idx, slot):
        cls_ref, rank_ref = idx
        for k in range(self.rows):
            src = pl.multiple_of((self.off_ref[cls_ref[0, 0, k]] + rank_ref[0, 0, k]) * SUBLANES, SUBLANES)
            dst = pl.multiple_of(slot * self.tile + k * SUBLANES, SUBLANES)
            pltpu.make_async_copy(self.ys_hbm.at[pl.ds(src, SUBLANES), :], self.ybuf.at[pl.ds(dst, SUBLANES), :],
                                  self.sem.at[slot]).start(priority=k % 2)

    def wait(self, slot):
        half = self.ybuf.at[pl.ds(pl.multiple_of(slot * self.tile, self.tile), self.tile), :]
        pltpu.make_async_copy(self.ys_hbm.at[pl.ds(0, self.tile), :], half, self.sem.at[slot]).wait()

    def read(self, slot):
        base = slot * self.tile
        return jnp.concatenate([self.ybuf[pl.ds(base + sg, self.rows, stride=SUBLANES), :]
                                for sg in range(SUBLANES)], axis=1)

    def take(self, slot):
        self.wait(slot)
        self.start(self.nxt, 1 - slot)
        return self.read(slot)


def _moe_rows(i, n_steps, stream):
    s = lax.rem(i, 2)

    @pl.when(i == 0)
    def _():
        stream.start(stream.cur, 0)

    rows = stream.take(s)

    @pl.when(i == n_steps - 1)
    def _():
        stream.wait(1 - s)
    return rows


def _add_moe_kernel(off_ref, cls_ref, rank_ref, ncls_ref, nrank_ref, h_ref, ys_hbm, ho_ref, ybuf, sem):
    y = _moe_rows(pl.program_id(0), pl.num_programs(0),
                  _MoeStream(off_ref, (cls_ref, rank_ref), (ncls_ref, nrank_ref), ys_hbm, ybuf, sem))
    ho_ref[...] = h_ref[...] + y


def _add_moe(off, cls, rank, h, ys):
    n = h.shape[0]
    n_steps, _, rows = cls.shape
    smem = lambda m: pl.BlockSpec((1, 1, rows), m, memory_space=pltpu.SMEM)
    cur = lambda i, off: (i, 0, 0)
    nxt = lambda i, off: (jnp.minimum(i + 1, n_steps - 1), 0, 0)
    grid_spec = pltpu.PrefetchScalarGridSpec(
        num_scalar_prefetch=1,
        grid=(n_steps,),
        in_specs=[
            smem(cur), smem(cur), smem(nxt), smem(nxt),
            pl.BlockSpec((rows, D_MODEL), lambda i, off: (i, 0)),
            pl.BlockSpec(memory_space=pl.ANY),
        ],
        out_specs=pl.BlockSpec((rows, D_MODEL), lambda i, off: (i, 0)),
        scratch_shapes=[pltpu.VMEM((2 * rows * SUBLANES, LANES), F32), pltpu.SemaphoreType.DMA((2,))],
    )
    return pl.pallas_call(
        _add_moe_kernel,
        grid_spec=grid_spec,
        out_shape=jax.ShapeDtypeStruct((n, D_MODEL), F32),
        compiler_params=_params("arbitrary"),
        name="add_moe",
    )(off, cls, rank, cls, rank, h, ys)


def _gla_intra_pairwise(q, k, bg, kbuf, bbuf):
    kbuf[...] = k
    bbuf[...] = bg
    lane = lax.broadcasted_iota(jnp.int32, (CHUNK, CHUNK), 1)

    def column(s, a):
        ks = kbuf[pl.ds(s, 1), :]
        bs = bbuf[pl.ds(s, 1), :]
        col = jnp.sum(q * ks * jnp.exp(jnp.minimum(bg - bs, 0.0)), axis=-1, keepdims=True)
        return jnp.where(lane == s, col, a)

    return lax.fori_loop(0, CHUNK, column, jnp.zeros((CHUNK, CHUNK), F32))


def _gla_qkv(z_ref, rows, valid_c):
    q_g = jnp.where(valid_c, z_ref[rows, Z_QG:Z_QG + G_K_WIDTH], 0.0) * (G_K_DIM ** -0.5)
    k_g = jnp.where(valid_c, z_ref[rows, Z_KG:Z_KG + G_K_WIDTH], 0.0)
    v_g = jnp.where(valid_c, z_ref[rows, Z_VG:Z_VG + G_WIDTH], 0.0)
    return q_g, k_g, v_g


def _split3(x):
    hi = x.astype(BF16)
    r = x - hi.astype(F32)
    mid = r.astype(BF16)
    return hi, mid, (r - mid.astype(F32)).astype(BF16)


def _gla_gated(o, gate, g):
    return _rms(o, g) * (gate * _sigmoid(gate))


IN_PROJ_SPLITS = (0, 1280, 2560, Z_WIDTH)


def _layer_kernel(embed, *refs):
    c = pl.program_id(0)
    (s0_ref, s1_ref, s2_ref), refs = refs[:3], refs[3:]
    (g1_ref, w_ref, tri_ref, trit_ref, cw_ref, cb_ref, gbr_ref, gbc_ref, mg_ref, awh_ref, awl_ref, ab_ref, gg_ref,
     wo_ref, g2_ref, wrh_ref, wrl_ref, br_ref, h1_ref, xe_ref, route_ref, cnt_ref, za_ref, zb_ref, ya_ref, yb_ref, tail_ref, c_ref, m_ref, s_ref, oi_ref,
     bg_ref, kbuf, bbuf, carry_ref) = refs
    rows = za_ref.shape[0]
    if embed:
        lead = jnp.concatenate([jnp.zeros((LEAD_PAD, D_MODEL), F32), s0_ref[...]] * (rows // CHUNK), axis=0)
        read_first = lambda: lead
        read_next = lambda: s1_ref[...].reshape(rows, D_MODEL)
        read_prev = lambda: jnp.where(c <= 1, lead, s2_ref[...].reshape(rows, D_MODEL))
    else:
        read_first, read_next, read_prev = (lambda: s0_ref[...]), (lambda: s1_ref[...]), (lambda: s2_ref[...])

    @pl.when(c == 0)
    def _():
        tail_ref[...] = jnp.zeros_like(tail_ref)
        c_ref[...] = jnp.zeros_like(c_ref)
        m_ref[...] = jnp.zeros_like(m_ref)
        s_ref[...] = jnp.zeros_like(s_ref)
        carry_ref[...] = jnp.zeros_like(carry_ref)
        yb_ref[...] = jnp.zeros_like(yb_ref)
        za_ref[...] = _dot(_rms(read_first(), g1_ref[...]), w_ref[...])

    args = (g1_ref, w_ref, tri_ref, trit_ref, cw_ref, cb_ref, gbr_ref, gbc_ref, mg_ref, awh_ref, awl_ref, ab_ref,
            gg_ref)
    state = (tail_ref, c_ref, m_ref, s_ref, oi_ref, bg_ref, kbuf, bbuf)
    parity = lax.rem(c, 2)

    def step(z_ref, zn_ref, y_ref, y_prev_ref):
        first = functools.partial(_router_logits, read_prev, y_prev_ref, wo_ref, g2_ref, wrh_ref, wrl_ref, br_ref,
                                  h1_ref)
        second = functools.partial(_router_assign, c > 0, xe_ref, route_ref, cnt_ref, carry_ref)
        _mixer_step(c, z_ref, zn_ref, read_next, *args, y_ref, *state, (first, second))

    @pl.when(parity == 0)
    def _():
        step(za_ref, zb_ref, ya_ref, yb_ref)

    @pl.when(parity == 1)
    def _():
        step(zb_ref, za_ref, yb_ref, ya_ref)


def _mixer_step(c, z_ref, zn_ref, read_next, g1_ref, w_ref, tri_ref, trit_ref, cw_ref, cb_ref, gbr_ref, gbc_ref,
                mg_ref, awh_ref, awl_ref, ab_ref, gg_ref, y_ref, tail_ref, c_ref, m_ref, s_ref, oi_ref, bg_ref,
                kbuf, bbuf, router):
    route_logits, route_assign = router
    bn = z_ref.shape[0] // CHUNK
    xn_next = _rms(read_next(), g1_ref[...]).astype(BF16)

    def project_next(part):
        lo, hi = IN_PROJ_SPLITS[part], IN_PROJ_SPLITS[part + 1]
        zn_ref[:, lo:hi] = jnp.dot(xn_next, w_ref[:, lo:hi], preferred_element_type=F32)

    rows_all = bn * CHUNK
    row = lax.broadcasted_iota(jnp.int32, (CHUNK, CHUNK), 0)
    col = lax.broadcasted_iota(jnp.int32, (CHUNK, CHUNK), 1)
    causal = col <= row
    later = jnp.logical_not(c == 0)
    valid_c = jnp.logical_or(later, lax.broadcasted_iota(jnp.int32, (CHUNK, 1), 0) >= LEAD_PAD)
    pos_c = lax.broadcasted_iota(jnp.int32, (rows_all, 1), 0) & (CHUNK - 1)
    pos_r = lax.broadcasted_iota(jnp.int32, (1, rows_all), 1) & (CHUNK - 1)
    valid_ca = jnp.logical_or(later, pos_c >= LEAD_PAD)
    valid_ra = jnp.logical_or(later, pos_r >= LEAD_PAD)
    row8 = lax.broadcasted_iota(jnp.int32, (SUBLANES, 1), 0)
    ones_col = jnp.where(lax.broadcasted_iota(jnp.int32, (CHUNK, LANES), 1) == 0, 1.0, 0.0).astype(F32)
    neg_inf = -jnp.inf

    tri, tri_t = tri_ref[...], trit_ref[...]
    zs = z_ref[:, Z_SMALL:Z_SMALL + LANES]
    g_col = zs + gbr_ref[...]
    g_row = zs.T[0:2 * M_HEADS, :] + gbc_ref[...]
    code = zs[:, 2 * M_HEADS:2 * M_HEADS + G_RANK]
    code_hi = code.astype(BF16)
    code_lo = (code - code_hi.astype(F32)).astype(BF16)
    dot = functools.partial(jnp.dot, preferred_element_type=F32)
    u = dot(code_hi, awh_ref[...]) + dot(code_hi, awl_ref[...]) + dot(code_lo, awh_ref[...]) + ab_ref[...]
    project_next(0)
    la = jnp.where(valid_ca, _log_sigmoid(u) * (1.0 / G_TAU), 0.0)
    lf_col = jnp.where(valid_ca, _log_sigmoid(g_col), 0.0)
    cum = sum(jnp.dot(tri, p, preferred_element_type=F32) for p in _split3(jnp.concatenate([lf_col, la], axis=1)))
    b_col_all, bg_all = cum[:, :LANES], cum[:, LANES:]
    bg_ref[...] = bg_all
    lf_row = jnp.where(valid_ra, _log_sigmoid(g_row), 0.0)
    b_row_all = sum(jnp.dot(p, tri_t, preferred_element_type=F32) for p in _split3(lf_row))
    ig_col_all = jnp.where(valid_ca, g_col, neg_inf)
    ig_row_all = jnp.where(valid_ra, g_row, neg_inf)

    def mlstm_batch(b):
        rows = slice(b * CHUNK, (b + 1) * CHUNK)
        x = jnp.where(valid_c, z_ref[rows, Z_QM:Z_QM + 2 * M_WIDTH], 0.0)
        prev = tail_ref[b]
        acc = x * cw_ref[CONV_K - 1:CONV_K, :] + cb_ref[...]
        for k in range(1, CONV_K):
            cur = pltpu.roll(x, k, 0)
            fix = pltpu.roll(prev, k, 0)
            top = jnp.where(row8 < k, fix, cur[0:SUBLANES])
            shifted = jnp.concatenate([top, cur[SUBLANES:]], axis=0)
            acc = acc + shifted * cw_ref[CONV_K - 1 - k:CONV_K - k, :]
        tail_ref[b] = x[CHUNK - SUBLANES:]
        qk = acc * _sigmoid(acc)
        qk = jnp.where(valid_c, qk, 0.0)
        q_m = qk[:, :M_WIDTH]
        k_m = qk[:, M_WIDTH:] * (M_HEAD_DIM ** -0.5)
        v_m = jnp.where(valid_c, z_ref[rows, Z_VM:Z_VM + M_WIDTH], 0.0)
        o_m = z_ref[rows, Z_OM:Z_OM + M_WIDTH]

        b_col, ig_col = b_col_all[rows], ig_col_all[rows]
        b_row, ig_row = b_row_all[:, rows], ig_row_all[:, rows]

        q_bf, k_bf = q_m.astype(BF16), k_m.astype(BF16)
        heads = []
        for hd in range(M_HEADS):
            idx = b * M_HEADS + hd
            lanes = slice(hd * M_HEAD_DIM, (hd + 1) * M_HEAD_DIM)
            bc, br = b_col[:, M_HEADS + hd:M_HEADS + hd + 1], b_row[M_HEADS + hd:M_HEADS + hd + 1, :]
            igc, igr = ig_col[:, hd:hd + 1], ig_row[hd:hd + 1, :]
            m_prev = m_ref[idx][0:1, 0:1]
            dmat = jnp.where(causal, bc - br + igr, neg_inf)
            m_inter = bc + m_prev
            m_t = jnp.maximum(m_inter, jnp.max(dmat, axis=-1, keepdims=True))
            b_end = bc[CHUNK - 1:CHUNK, :]
            m_new = jnp.maximum(b_end + m_prev, jnp.max(b_end - br + igr, axis=-1, keepdims=True))
            wk = jnp.exp(b_end - bc + igc - m_new)
            heads.append(dict(
                idx=idx, lanes=lanes, q=q_bf[:, lanes], k=k_bf[:, lanes],
                kw_t=(k_m[:, lanes] * wk).T.astype(BF16),
                v_aug=jnp.concatenate([v_m[:, lanes], ones_col], axis=1).astype(BF16),
                e=jnp.exp(dmat - m_t), w_inter=jnp.exp(m_inter - m_t), floor=jnp.exp(-m_t),
                decay=jnp.exp(b_end + m_prev - m_new), m_new=m_new))
        return dict(rows=rows, heads=heads, o_m=o_m)

    def mlstm_first_dots(st):
        for hd in st['heads']:
            cst = c_ref[hd['idx']]
            hd['qk'] = _dot_nt(hd['q'], hd['k'])
            hd['inter'] = _dot(hd['q'], cst)
            c_ref[hd['idx']] = hd['decay'] * cst + _dot(hd['kw_t'], hd['v_aug'])
            m_ref[hd['idx']] = jnp.broadcast_to(hd['m_new'], (SUBLANES, LANES))

    def mlstm_finish(st):
        rows = st['rows']
        for hd in st['heads']:
            lanes = hd['lanes']
            nd = _dot(hd['qk'] * hd['e'], hd['v_aug']) + hd['w_inter'] * hd['inter']
            num, den = nd[:, :M_HEAD_DIM], nd[:, M_HEAD_DIM:M_HEAD_DIM + 1]
            hh = num / jnp.maximum(jnp.abs(den), hd['floor'])
            y = _rms(hh, mg_ref[:, lanes]) * _sigmoid(st['o_m'][:, lanes])
            y_ref[rows, lanes] = y.astype(y_ref.dtype)

    def gla_batch(b):
        rows = slice(b * CHUNK, (b + 1) * CHUNK)
        q_g, k_g, v_g = _gla_qkv(z_ref, rows, valid_c)
        bg = bg_all[rows]
        bg_t = bg.T
        b_end = bg[CHUNK - 1:CHUNK, :]
        q_dec = (q_g * jnp.exp(bg)).astype(BF16)
        k_inv = (k_g * jnp.exp(-bg)).astype(BF16)
        k_end = k_g * jnp.exp(b_end - bg)
        v_bf = v_g.astype(BF16)
        heads = []
        for hd in range(G_HEADS):
            kl = slice(hd * G_K_DIM, (hd + 1) * G_K_DIM)
            vl = slice(hd * G_V_DIM, (hd + 1) * G_V_DIM)
            heads.append(dict(
                idx=b * G_HEADS + hd, vl=vl, yl=slice(M_WIDTH + hd * G_V_DIM, M_WIDTH + (hd + 1) * G_V_DIM),
                q_dec=q_dec[:, kl], k_inv=k_inv[:, kl], k_end_t=k_end[:, kl].T.astype(BF16), v=v_bf[:, vl],
                e_end=jnp.exp(bg_t[hd * G_K_DIM:(hd + 1) * G_K_DIM, CHUNK - 1:CHUNK])))
        return dict(rows=rows, heads=heads, unsafe=jnp.max(-b_end) > GLA_SAFE_EXP)

    def gla_first_dots(st):
        for hd in st['heads']:
            state = s_ref[hd['idx']]
            hd['a'] = _dot_nt(hd['q_dec'], hd['k_inv'])
            hd['o_inter'] = _dot(hd['q_dec'], state)
            s_ref[hd['idx']] = hd['e_end'] * state + _dot(hd['k_end_t'], hd['v'])

    def gla_finish(st):
        rows = st['rows']
        gate_g = z_ref[rows, Z_GG:Z_GG + G_WIDTH]
        for hd in st['heads']:
            vl = hd['vl']
            oi_ref[rows, vl] = hd['o_inter']
            o = _dot(jnp.where(causal, hd['a'], 0.0), hd['v']) + hd['o_inter']
            y_ref[rows, hd['yl']] = _gla_gated(o, gate_g[:, vl], gg_ref[:, vl]).astype(y_ref.dtype)

    def gla_redo_batch(b, carry):
        rows = pl.ds(pl.multiple_of(b * CHUNK, CHUNK), CHUNK)
        q_g, k_g, v_g = _gla_qkv(z_ref, rows, valid_c)
        bg = bg_ref[rows, :]
        gate_g = z_ref[rows, Z_GG:Z_GG + G_WIDTH]
        for hd in range(G_HEADS):
            kl = slice(hd * G_K_DIM, (hd + 1) * G_K_DIM)
            vl = slice(hd * G_V_DIM, (hd + 1) * G_V_DIM)
            yl = slice(M_WIDTH + hd * G_V_DIM, M_WIDTH + (hd + 1) * G_V_DIM)
            a = jnp.where(causal, _gla_intra_pairwise(q_g[:, kl], k_g[:, kl], bg[:, kl], kbuf, bbuf), 0.0)
            y = _gla_gated(_dot(a, v_g[:, vl]) + oi_ref[rows, vl], gate_g[:, vl], gg_ref[:, vl])
            y_ref[rows, yl] = y.astype(y_ref.dtype)
        return carry

    project_next(1)
    routed = route_logits()
    mlstm = [mlstm_batch(b) for b in range(bn)]
    gla = [gla_batch(b) for b in range(bn)]
    for b in range(bn):
        mlstm_first_dots(mlstm[b])
        gla_first_dots(gla[b])
    project_next(2)
    route_assign(*routed)
    for b in range(bn):
        mlstm_finish(mlstm[b])
        gla_finish(gla[b])
    unsafe = functools.reduce(jnp.logical_or, [st['unsafe'] for st in gla])

    @pl.when(unsafe)
    def _():
        lax.fori_loop(0, bn, gla_redo_batch, 0)


def _layer(stream, norm_g, w_in, conv_w, conv_b, gb_row, gb_col, m_norm_g, a_w2, a_b, g_norm_g,
           w_out, norm2_g, wr_hi, wr_lo, br, bn, n_chunks):
    rows = bn * CHUNK
    n = n_chunks * rows
    last = n_chunks - 1
    full = lambda shape: pl.BlockSpec(shape, lambda c: (0,) * len(shape))
    h_at = lambda m: pl.BlockSpec((rows, D_MODEL), m)
    prev = lambda c: (jnp.maximum(c - 1, 0), 0)
    r_i, c_i = np.arange(rows)[:, None], np.arange(rows)[None, :]
    tri = jnp.asarray((r_i // CHUNK == c_i // CHUNK) & (c_i <= r_i), BF16)
    aw_hi = a_w2.astype(BF16)
    aw_lo = (a_w2 - aw_hi.astype(F32)).astype(BF16)
    embed = isinstance(stream, tuple)
    if embed:
        x4, meta = stream
        frames = lambda m: pl.BlockSpec((bn, 1, CHUNK, D_MODEL), m)
        stream_args = (meta, x4, x4)
        stream_specs = [
            full((N_META, D_MODEL)),
            frames(lambda c: (0, jnp.minimum(c, last - 1), 0, 0)),
            frames(lambda c: (0, jnp.clip(c - 2, 0, last - 1), 0, 0)),
        ]
    else:
        stream_args = (stream, stream, stream)
        stream_specs = [
            h_at(lambda c: (jnp.minimum(c, last), 0)),
            h_at(lambda c: (jnp.minimum(c + 1, last), 0)),
            h_at(prev),
        ]
    return pl.pallas_call(
        functools.partial(_layer_kernel, embed),
        grid=(n_chunks + 1,),
        in_specs=stream_specs + [
            full((1, D_MODEL)), full((D_MODEL, Z_WIDTH)), full((rows, rows)), full((rows, rows)),
            full((CONV_K, 2 * M_WIDTH)), full((1, 2 * M_WIDTH)),
            full((1, LANES)), full((2 * M_HEADS, rows)),
            full((1, M_WIDTH)), full((G_RANK, G_K_WIDTH)), full((G_RANK, G_K_WIDTH)), full((1, G_K_WIDTH)),
            full((1, G_WIDTH)),
            full((D_MODEL, D_MODEL)), full((1, D_MODEL)), full((D_MODEL, LANES)), full((D_MODEL, LANES)),
            full((1, LANES)),
        ],
        out_specs=[
            h_at(prev),
            pl.BlockSpec((rows * SUBLANES, LANES), prev),
            pl.BlockSpec((1, SUBLANES, rows), lambda c: (jnp.maximum(c - 1, 0), 0, 0)),
            pl.BlockSpec((1, LANES), lambda c: (0, 0)),
        ],
        out_shape=[
            jax.ShapeDtypeStruct((n, D_MODEL), F32),
            jax.ShapeDtypeStruct((n * SUBLANES, LANES), jnp.uint32),
            jax.ShapeDtypeStruct((n_chunks, SUBLANES, rows), jnp.int32),
            jax.ShapeDtypeStruct((1, LANES), F32),
        ],
        scratch_shapes=[
            pltpu.VMEM((rows, Z_WIDTH), F32), pltpu.VMEM((rows, Z_WIDTH), F32),
            pltpu.VMEM((rows, D_MODEL), BF16), pltpu.VMEM((rows, D_MODEL), BF16),
            pltpu.VMEM((bn, SUBLANES, 2 * M_WIDTH), F32),
            pltpu.VMEM((bn * M_HEADS, M_HEAD_DIM, 2 * M_HEAD_DIM), F32),
            pltpu.VMEM((bn * M_HEADS, SUBLANES, LANES), F32),
            pltpu.VMEM((bn * G_HEADS, G_K_DIM, G_V_DIM), F32),
            pltpu.VMEM((rows, G_WIDTH), F32),
            pltpu.VMEM((rows, G_K_WIDTH), F32),
            pltpu.VMEM((CHUNK, G_K_DIM), F32), pltpu.VMEM((CHUNK, G_K_DIM), F32),
            pltpu.VMEM((1, LANES), F32),
        ],
        compiler_params=_params("arbitrary"),
        name="layer",
    )(*stream_args, norm_g, w_in, tri, tri.T, conv_w, conv_b, gb_row, gb_col, m_norm_g, aw_hi, aw_lo, a_b, g_norm_g,
      w_out, norm2_g, wr_hi, wr_lo, br)


def _router_logits(read_h, y_ref, wo_ref, g_ref, wrh_ref, wrl_ref, br_ref, h1_ref):
    h1 = read_h() + jnp.dot(y_ref[...], wo_ref[...], preferred_element_type=F32)
    h1_ref[...] = h1
    xn = _rms(h1, g_ref[...])
    x_hi = xn.astype(BF16)
    x_lo = (xn - x_hi.astype(F32)).astype(BF16)
    dot = functools.partial(jnp.dot, preferred_element_type=F32)
    logits = dot(x_hi, wrh_ref[...]) + dot(x_hi, wrl_ref[...]) + dot(x_lo, wrh_ref[...]) + br_ref[...]
    return xn, logits


def _router_assign(live, xe_ref, route_ref, cnt_ref, carry_ref, xn, logits):
    rows = xn.shape[0]
    lane = lax.broadcasted_iota(jnp.int32, (rows, LANES), 1)
    neg_inf = -jnp.inf
    big = jnp.int32(LANES)

    def first_argmax(vals):
        top = jnp.max(vals, axis=-1, keepdims=True)
        return top, jnp.min(jnp.where(vals == top, lane, big), axis=-1, keepdims=True)

    lg = jnp.where(lane < N_GROUPS, logits, neg_inf)
    g_max, g_sel = first_argmax(lg)
    p_sel = 1.0 / jnp.sum(jnp.exp(lg - g_max), axis=-1, keepdims=True)
    in_group = jnp.logical_and(lane >= N_GROUPS + EXPERTS_PER_GROUP * g_sel,
                               lane < N_GROUPS + EXPERTS_PER_GROUP * (g_sel + 1))
    le = jnp.where(in_group, logits, neg_inf)
    v1, i1 = first_argmax(le)
    v2, i2 = first_argmax(jnp.where(lane == i1, neg_inf, le))
    e21 = jnp.exp(v2 - v1)
    gate1 = p_sel / (1.0 + e21)
    gate2 = p_sel * e21 / (1.0 + e21)
    j1 = i1 - N_GROUPS - EXPERTS_PER_GROUP * g_sel
    j2 = i2 - N_GROUPS - EXPERTS_PER_GROUP * g_sel
    ja, jb = jnp.minimum(j1, j2), jnp.maximum(j1, j2)
    ga = jnp.where(j1 < j2, gate1, gate2)
    gb = jnp.where(j1 < j2, gate2, gate1)
    pair = ((ja * (2 * EXPERTS_PER_GROUP - 1 - ja)) >> 1) + (jb - ja - 1)
    cls = g_sel * N_PAIRS + pair

    bits = lax.bitcast_convert_type(xn.astype(jnp.bfloat16).astype(F32), jnp.uint32)
    half = D_MODEL // 2
    packed = bits[:, half:] | (bits[:, :half] >> 16)
    gates = lax.bitcast_convert_type(jnp.where(lane == 0, ga, jnp.where(lane == 1, gb, 0.0)), jnp.uint32)
    for sg in range(SUBLANES):
        if sg < half // LANES:
            val = packed[:, sg * LANES:(sg + 1) * LANES]
        else:
            val = gates if sg == half // LANES else jnp.zeros((rows, LANES), jnp.uint32)
        xe_ref[pl.ds(sg, rows, stride=SUBLANES), :] = val

    onehot = jnp.where(jnp.logical_and(live, lane == cls), 1.0, 0.0).astype(F32)
    r_i = lax.broadcasted_iota(jnp.int32, (rows, rows), 0)
    c_i = lax.broadcasted_iota(jnp.int32, (rows, rows), 1)
    incl = jnp.where(c_i <= r_i, 1.0, 0.0).astype(BF16)
    prefix = jnp.dot(incl, onehot.astype(BF16), preferred_element_type=F32)
    rank = jnp.sum(onehot * (prefix - 1.0 + carry_ref[...]), axis=-1, keepdims=True)
    carry_ref[...] = carry_ref[...] + prefix[rows - 1:rows, :]
    cnt_ref[...] = carry_ref[...]
    route = jnp.where(lane == 0, cls, jnp.where(lane == 1, rank.astype(jnp.int32), 0))
    route_ref[0] = route.T[0:SUBLANES, :]


def _dispatch_kernel(off_ref, gap_ref, used_ref, cls_ref, rank_ref, x_ref, xs_hbm, stage, zeros, sem, fill_sem):
    i = pl.program_id(0)
    rows = cls_ref.shape[-1]
    s = lax.rem(i, 2)
    n_slots = xs_hbm.shape[0]

    def wait(slot):
        pltpu.make_async_copy(stage.at[slot], xs_hbm.at[pl.ds(0, rows)], sem.at[slot]).wait()

    def for_gaps(act):
        def class_gap(cl, carry):
            size = gap_ref[cl]

            @pl.when(size > 0)
            def _():
                act(pltpu.make_async_copy(zeros.at[pl.ds(0, size)],
                                          xs_hbm.at[pl.ds(off_ref[cl] + used_ref[cl], size)], fill_sem))
            return carry
        lax.fori_loop(0, N_CLASSES, class_gap, 0)

        def tail_block(b, carry):
            act(pltpu.make_async_copy(zeros, xs_hbm.at[pl.ds(b * EXPERT_ROWS, EXPERT_ROWS)], fill_sem))
            return carry
        lax.fori_loop(used_ref[N_CLASSES], n_slots // EXPERT_ROWS, tail_block, 0)

    @pl.when(i == 0)
    def _():
        zeros[...] = jnp.zeros_like(zeros)
        for_gaps(lambda copy: copy.start())

    @pl.when(i >= 2)
    def _():
        wait(s)

    stage[s] = x_ref[...]
    for k in range(rows):
        dst = off_ref[cls_ref[0, 0, k]] + rank_ref[0, 0, k]
        pltpu.make_async_copy(stage.at[s].at[k], xs_hbm.at[dst], sem.at[s]).start(priority=k % 2)

    @pl.when(i == pl.num_programs(0) - 1)
    def _():
        wait(s)

        @pl.when(i >= 1)
        def _():
            wait(1 - s)
        for_gaps(lambda copy: copy.wait())


def _dispatch(off, gap, used, cls, rank, xe, n_slots):
    n_steps, _, rows = cls.shape
    smem = pl.BlockSpec((1, 1, rows), lambda i, *_: (i, 0, 0), memory_space=pltpu.SMEM)
    grid_spec = pltpu.PrefetchScalarGridSpec(
        num_scalar_prefetch=3,
        grid=(n_steps,),
        in_specs=[smem, smem, pl.BlockSpec((rows, SUBLANES, LANES), lambda i, *_: (i, 0, 0))],
        out_specs=pl.BlockSpec(memory_space=pl.ANY),
        scratch_shapes=[pltpu.VMEM((2, rows, SUBLANES, LANES), xe.dtype),
                        pltpu.VMEM((EXPERT_ROWS, SUBLANES, LANES), xe.dtype),
                        pltpu.SemaphoreType.DMA((2,)), pltpu.SemaphoreType.DMA(())],
    )
    return pl.pallas_call(
        _dispatch_kernel,
        grid_spec=grid_spec,
        out_shape=jax.ShapeDtypeStruct((n_slots, SUBLANES, LANES), xe.dtype),
        compiler_params=_params("arbitrary"),
        name="dispatch",
    )(off, gap, used, cls, rank, xe)


BLOCKS_PER_STEP = 2


def _expert_kernel(ea_ref, eb_ref, nused_ref, xs_hbm, *refs):
    weights = refs[:6 * BLOCKS_PER_STEP]
    ys_hbm, xbuf, obuf, sem_in, sem_out = refs[6 * BLOCKS_PER_STEP:]
    i = pl.program_id(0)
    steps_used = (nused_ref[0] + BLOCKS_PER_STEP - 1) // BLOCKS_PER_STEP
    slot = lax.rem(i, 2)
    block = EXPERT_ROWS * SUBLANES
    tile = BLOCKS_PER_STEP * block

    def in_copy(step, s):
        return pltpu.make_async_copy(xs_hbm.at[pl.ds(pl.multiple_of(step * tile, tile), tile), :],
                                     xbuf.at[pl.ds(pl.multiple_of(s * tile, tile), tile), :], sem_in.at[s])

    def out_copy(step, s):
        return pltpu.make_async_copy(obuf.at[pl.ds(pl.multiple_of(s * tile, tile), tile), :],
                                     ys_hbm.at[pl.ds(pl.multiple_of(step * tile, tile), tile), :], sem_out.at[s])

    @pl.when(i < steps_used)
    def _():
        @pl.when(i == 0)
        def _():
            in_copy(0, 0).start(priority=1)

        in_copy(i, slot).wait()

        @pl.when(i + 1 < steps_used)
        def _():
            in_copy(i + 1, 1 - slot).start(priority=1)

        @pl.when(i >= 2)
        def _():
            out_copy(jnp.maximum(i - 2, 0), slot).wait()

        as_f32 = lambda w: lax.bitcast_convert_type(w, F32)
        for blk in range(BLOCKS_PER_STEP):
            w_a, w_b = weights[6 * blk:6 * blk + 3], weights[6 * blk + 3:6 * blk + 6]
            base = slot * tile + blk * block
            seg = lambda sg: xbuf[pl.ds(base + sg, EXPERT_ROWS, stride=SUBLANES), :]
            words = [seg(sg) for sg in range(D_MODEL // 2 // LANES)]
            xb = jnp.concatenate([as_f32(w << 16) for w in words]
                                 + [as_f32(w & jnp.uint32(0xFFFF0000)) for w in words], axis=1).astype(BF16)
            gates = as_f32(seg(D_MODEL // 2 // LANES))
            ga, gb = gates[:, 0:1], gates[:, 1:2]

            def mlp(wg, wu, wd):
                gate = jnp.dot(xb, wg[0, 0], preferred_element_type=F32)
                up = jnp.dot(xb, wu[0, 0], preferred_element_type=F32)
                return jnp.dot((gate * _sigmoid(gate) * up).astype(BF16), wd[0, 0], preferred_element_type=F32)

            y = ga * mlp(*w_a) + gb * mlp(*w_b)
            for sg in range(SUBLANES):
                obuf[pl.ds(base + sg, EXPERT_ROWS, stride=SUBLANES), :] = y[:, sg * LANES:(sg + 1) * LANES]
        out_copy(i, slot).start(priority=1)

        @pl.when(i == steps_used - 1)
        def _():
            out_copy(i, slot).wait()

            @pl.when(i >= 1)
            def _():
                out_copy(jnp.maximum(i - 1, 0), 1 - slot).wait()

    @pl.when(i == steps_used)
    def _():
        obuf[0:tile, :] = jnp.zeros((tile, LANES), F32)
        n_steps = pl.num_programs(0)

        def each(act):
            def body(step, carry):
                act(out_copy(step, 0))
                return carry
            lax.fori_loop(steps_used, n_steps, body, 0)
        each(lambda copy: copy.start())
        each(lambda copy: copy.wait())


def _experts(blk_ea, blk_eb, n_used, xs, w_gate, w_up, w_down, layer):
    n_blocks = blk_ea.shape[0]
    tile = BLOCKS_PER_STEP * EXPERT_ROWS * SUBLANES
    weight_specs = []
    for blk in range(BLOCKS_PER_STEP):
        at = lambda tab, blk=blk: (lambda i, ea, eb, nu: (layer, (ea, eb)[tab][BLOCKS_PER_STEP * i + blk], 0, 0))
        for tab in (0, 1):
            weight_specs += [pl.BlockSpec((1, 1, D_MODEL, D_EXPERT), at(tab)),
                             pl.BlockSpec((1, 1, D_MODEL, D_EXPERT), at(tab)),
                             pl.BlockSpec((1, 1, D_EXPERT, D_MODEL), at(tab))]
    grid_spec = pltpu.PrefetchScalarGridSpec(
        num_scalar_prefetch=3,
        grid=(n_blocks // BLOCKS_PER_STEP,),
        in_specs=[pl.BlockSpec(memory_space=pl.ANY)] + weight_specs,
        out_specs=pl.BlockSpec(memory_space=pl.ANY),
        scratch_shapes=[
            pltpu.VMEM((2 * tile, LANES), jnp.uint32),
            pltpu.VMEM((2 * tile, LANES), F32),
            pltpu.SemaphoreType.DMA((2,)),
            pltpu.SemaphoreType.DMA((2,)),
        ],
    )
    return pl.pallas_call(
        _expert_kernel,
        grid_spec=grid_spec,
        out_shape=jax.ShapeDtypeStruct((n_blocks * EXPERT_ROWS * SUBLANES, LANES), F32),
        compiler_params=_params("arbitrary"),
        name="experts",
    )(blk_ea, blk_eb, n_used, xs, *([w_gate, w_up, w_down] * (2 * BLOCKS_PER_STEP)))


def _pair_tables():
    ea, eb = [], []
    for g in range(N_GROUPS):
        for a in range(EXPERTS_PER_GROUP):
            for b in range(a + 1, EXPERTS_PER_GROUP):
                ea.append(g * EXPERTS_PER_GROUP + a)
                eb.append(g * EXPERTS_PER_GROUP + b)
    return np.asarray(ea, np.int32), np.asarray(eb, np.int32)


def _plan_blocks(counts, n):
    cnt = counts[0, :N_CLASSES].astype(jnp.int32)
    padded = (cnt + EXPERT_ROWS - 1) // EXPERT_ROWS * EXPERT_ROWS
    pad_end = jnp.cumsum(padded)
    off = pad_end - padded
    n_blocks = -(-(n + N_CLASSES * (EXPERT_ROWS - 1)) // EXPERT_ROWS)
    n_blocks = -(-n_blocks // BLOCKS_PER_STEP) * BLOCKS_PER_STEP
    blk_start = jnp.arange(n_blocks, dtype=jnp.int32) * EXPERT_ROWS
    blk_cls = jnp.minimum(jnp.sum(pad_end[None, :] <= blk_start[:, None], axis=1), N_CLASSES - 1)
    ea_tab, eb_tab = _pair_tables()
    n_used = (pad_end[-1] // EXPERT_ROWS).astype(jnp.int32).reshape(1)
    used = jnp.concatenate([cnt, n_used])
    return (jnp.asarray(ea_tab)[blk_cls], jnp.asarray(eb_tab)[blk_cls], n_used, off.astype(jnp.int32),
            (padded - cnt).astype(jnp.int32), used.astype(jnp.int32), n_blocks)


def _final_kernel(off_ref, cls_ref, rank_ref, ncls_ref, nrank_ref, h_ref, ys_hbm, g_ref, o_ref, ybuf, sem):
    y = _moe_rows(pl.program_id(0), pl.num_programs(0),
                  _MoeStream(off_ref, (cls_ref, rank_ref), (ncls_ref, nrank_ref), ys_hbm, ybuf, sem))
    out = _rms(h_ref[...] + y, g_ref[...])
    o_ref[...] = out.reshape(o_ref.shape)


def _final(off, cls, rank, h, ys, g, bn):
    n_chunks, _, rows = cls.shape
    smem = lambda m: pl.BlockSpec((1, 1, rows), m, memory_space=pltpu.SMEM)
    cur = lambda c, off: (c + 1, 0, 0)
    nxt = lambda c, off: (jnp.minimum(c + 2, n_chunks - 1), 0, 0)
    grid_spec = pltpu.PrefetchScalarGridSpec(
        num_scalar_prefetch=1,
        grid=(n_chunks - 1,),
        in_specs=[
            smem(cur), smem(cur), smem(nxt), smem(nxt),
            pl.BlockSpec((rows, D_MODEL), lambda c, off: (c + 1, 0)),
            pl.BlockSpec(memory_space=pl.ANY),
            pl.BlockSpec((1, D_MODEL), lambda c, off: (0, 0)),
        ],
        out_specs=pl.BlockSpec((bn, 1, CHUNK, D_MODEL), lambda c, off: (0, c, 0, 0)),
        scratch_shapes=[pltpu.VMEM((2 * rows * SUBLANES, LANES), F32), pltpu.SemaphoreType.DMA((2,))],
    )
    return pl.pallas_call(
        _final_kernel,
        grid_spec=grid_spec,
        out_shape=jax.ShapeDtypeStruct((bn, n_chunks - 1, CHUNK, D_MODEL), F32),
        compiler_params=_params("arbitrary"),
        name="final_norm",
    )(off, cls, rank, cls, rank, h, ys, g)


def _repack_w_in(w):
    o = np.cumsum([0, 2 * M_WIDTH, M_WIDTH, M_WIDTH, M_HEADS, M_HEADS, G_K_WIDTH, G_K_WIDTH, G_WIDTH, G_WIDTH,
                   G_RANK])
    seg = lambda i: w[:, o[i]:o[i + 1]]
    small = jnp.concatenate([seg(3), seg(4), seg(9)], axis=1)
    small = jnp.pad(small, ((0, 0), (0, LANES - small.shape[1])))
    return jnp.concatenate([seg(0), seg(1), seg(2), seg(5), seg(6), seg(7), seg(8), small], axis=1).astype(BF16)


def kernel(x, meta_tokens, norm1_g, w_in, conv_w, conv_b, gate_b, m_norm_g, a_w2, a_b, g_norm_g, w_out, norm2_g,
           wr_g, br_g, wr_e, br_e, w_gate, w_up, w_down, final_norm_g):
    bn, seq, dm = x.shape
    depth = w_in.shape[0]
    assert dm == D_MODEL and seq % CHUNK == 0
    n_chunks = seq // CHUNK + 1
    rows = bn * CHUNK
    n = n_chunks * rows
    x4 = x.reshape(bn, seq // CHUNK, CHUNK, dm)
    row = lambda v: v.reshape(1, -1).astype(F32)
    w_gate_bf, w_up_bf, w_down_bf = w_gate.astype(BF16), w_up.astype(BF16), w_down.astype(BF16)

    h = ys = off = cls = rank = None
    for l in range(depth):
        stream = (x4, meta_tokens.astype(F32)) if l == 0 else _add_moe(off, cls, rank, h, ys)
        gb = gate_b[l].astype(F32).reshape(1, 2 * M_HEADS)
        wr = jnp.pad(jnp.concatenate([wr_g[l], wr_e[l]], axis=1).astype(F32),
                     ((0, 0), (0, LANES - N_GROUPS - N_EXPERTS)))
        wr_hi = wr.astype(BF16)
        wr_lo = (wr - wr_hi.astype(F32)).astype(BF16)
        br = jnp.pad(jnp.concatenate([br_g[l], br_e[l]]).astype(F32), (0, LANES - N_GROUPS - N_EXPERTS))
        h, xe, route, counts = _layer(
            stream, row(norm1_g[l]), _repack_w_in(w_in[l]), conv_w[l].astype(F32), row(conv_b[l]),
            jnp.pad(gb, ((0, 0), (0, LANES - 2 * M_HEADS))),
            jnp.broadcast_to(gb.reshape(2 * M_HEADS, 1), (2 * M_HEADS, rows)),
            row(m_norm_g[l]), a_w2[l].astype(F32), row(a_b[l]), row(g_norm_g[l]),
            w_out[l].astype(BF16), row(norm2_g[l]), wr_hi, wr_lo, br.reshape(1, LANES), bn, n_chunks)
        blk_ea, blk_eb, n_used, off, gap, used, n_blocks = _plan_blocks(counts, n)
        cls, rank = route[:, 0:1, :], route[:, 1:2, :]
        xs = _dispatch(off, gap, used, cls, rank, xe.reshape(n, SUBLANES, LANES), n_blocks * EXPERT_ROWS)
        ys = _experts(blk_ea, blk_eb, n_used, xs.reshape(n_blocks * EXPERT_ROWS * SUBLANES, LANES),
                      w_gate_bf, w_up_bf, w_down_bf, l)
    out = _final(off, cls, rank, h, ys, row(final_norm_g), bn)
    return out.reshape(bn, seq, dm)
```
